```python
import jax, jax.numpy as jnp
from jax import lax
import numpy as np

D_MODEL = 1024
BATCH = 8
SEQ = 8192
DEPTH = 4

CTX_LEN = 256
GRID_W = 64
EPS = 1e-6

F_GROUPS = 4
F_GROUP_DIM = D_MODEL // 16
F_WIDTH = F_GROUPS * F_GROUP_DIM
M_HEADS = 4
M_HEAD_DIM = D_MODEL // 16
M_WIDTH = M_HEADS * M_HEAD_DIM
M_CONV_W = 3
M_CHUNK = 128
M_GATES = 2 * 2 * M_HEADS
A_HEADS = 8
A_NOPE = D_MODEL // 16
A_ROPE = D_MODEL // 32
A_V = D_MODEL // 16
A_WIDTH = A_HEADS * A_V
Q_LORA = 3 * D_MODEL // 8
KV_LORA = D_MODEL // 4
ROPE_BASE = 10000.0
Q_BLOCK = 128
D_MIX = F_WIDTH + M_WIDTH + A_WIDTH
IN_SIZES = (F_WIDTH, 2 * M_WIDTH, M_WIDTH, M_WIDTH, M_GATES, Q_LORA, KV_LORA, A_ROPE)
IN_COLS = sum(IN_SIZES)
N_EXPERTS = 16
EC_FACTOR = 2
EXPERT_FF = D_MODEL

kernel_name = "hybrid_fnet_mlstm_mla_ec_diffusion"


def rms_norm(x, g):
    xf = x.astype(jnp.float32)
    y = xf * lax.rsqrt(jnp.mean(xf * xf, -1, keepdims=True) + EPS)
    return (y * g.astype(jnp.float32)).astype(x.dtype)


def modulated_rms(x, g, shift, scale):
    return rms_norm(x, g) * (1 + scale) + shift


def split_in(u):
    outs = []
    start = 0
    for size in IN_SIZES:
        outs.append(u[..., start:start + size])
        start += size
    return outs


def axial_rope(n, dtype):
    rows = n // GRID_W
    row_id = jnp.repeat(jnp.arange(rows, dtype=jnp.float32), GRID_W)
    col_id = jnp.tile(jnp.arange(GRID_W, dtype=jnp.float32), rows)
    n_freq = A_ROPE // 4
    inv = ROPE_BASE ** (-jnp.arange(n_freq, dtype=jnp.float32) / n_freq)
    ang = jnp.concatenate([row_id[:, None] * inv, col_id[:, None] * inv], -1)
    return jnp.cos(ang).astype(dtype), jnp.sin(ang).astype(dtype)


def apply_rope(t, cos, sin):
    half = A_ROPE // 2
    t1, t2 = t[..., :half], t[..., half:]
    cs, sn = cos[:, None, :], sin[:, None, :]
    return jnp.concatenate([t1 * cs - t2 * sn, t1 * sn + t2 * cs], -1)


def fourier_mix(u):
    b, t, _ = u.shape
    ug = u.astype(jnp.float32).reshape(b, t, F_GROUPS, F_GROUP_DIM)
    y = jnp.fft.fft2(ug, axes=(1, 3), norm="ortho").real
    return y.reshape(b, t, F_WIDTH).astype(u.dtype)


def centred_dwconv(u, w, bias):
    pad = (M_CONV_W - 1) // 2
    y = lax.conv_general_dilated(u, w[:, None, :], window_strides=(1,), padding=[(pad, pad)],
                                 dimension_numbers=("NWC", "WIO", "NWC"), feature_group_count=u.shape[-1])
    return y + bias


def mlstm_prepare(qk, v, gates, conv_w, conv_b, ib, fb):
    b, t, _ = v.shape
    qk = jax.nn.silu(centred_dwconv(qk, conv_w, conv_b))
    q, k = qk[..., :M_WIDTH], qk[..., M_WIDTH:]
    heads = lambda a: a.reshape(b, t, M_HEADS, M_HEAD_DIM).transpose(0, 2, 1, 3)
    g = gates.astype(jnp.float32).reshape(b, t, 2, 2, M_HEADS)
    logi = (g[:, :, :, 0] + ib.astype(jnp.float32)).transpose(2, 0, 3, 1)
    logf = jax.nn.log_sigmoid(g[:, :, :, 1] + fb.astype(jnp.float32)).transpose(2, 0, 3, 1)
    return heads(q) * (M_HEAD_DIM ** -0.5), heads(k), heads(v), logi, logf


def mlstm_scan(q, k, v, logi, logf, state):
    b, h, t, d = q.shape
    nc = t // M_CHUNK
    to_chunks = lambda a: jnp.moveaxis(a.astype(jnp.float32).reshape(b, h, nc, M_CHUNK, *a.shape[3:]), 2, 0)
    xs = (to_chunks(q), to_chunks(k), to_chunks(v), to_chunks(logi), to_chunks(logf))
    tri = jnp.tril(jnp.ones((M_CHUNK, M_CHUNK), bool))

    def step(carry, inp):
        C, n, m = carry
        qc, kc, vc, ic, fc = inp
        bcum = jnp.cumsum(fc, -1)
        dmat = bcum[..., :, None] - bcum[..., None, :] + ic[..., None, :]
        dmat = jnp.where(tri, dmat, -jnp.inf)
        inter = bcum + m[..., None]
        m_t = jnp.maximum(inter, dmat.max(-1))
        w = jnp.exp(dmat - m_t[..., None])
        a = jnp.exp(inter - m_t)
        s = jnp.einsum("bhtd,bhsd->bhts", qc, kc) * w
        num = a[..., None] * jnp.einsum("bhtd,bhde->bhte", qc, C) + jnp.einsum("bhts,bhse->bhte", s, vc)
        den = a * jnp.einsum("bhtd,bhd->bht", qc, n) + s.sum(-1)
        h_out = num / jnp.maximum(jnp.abs(den), jnp.exp(-m_t))[..., None]
        btot = bcum[..., -1]
        dec = btot[..., None] - bcum + ic
        m_new = jnp.maximum(btot + m, dec.max(-1))
        ws = jnp.exp(dec - m_new[..., None])
        a_st = jnp.exp(btot + m - m_new)
        C_new = a_st[..., None, None] * C + jnp.einsum("bhs,bhsd,bhse->bhde", ws, kc, vc)
        n_new = a_st[..., None] * n + jnp.einsum("bhs,bhsd->bhd", ws, kc)
        return (C_new, n_new, m_new), h_out

    state, hs = lax.scan(step, state, xs)
    return state, jnp.moveaxis(hs, 0, 2).reshape(b, h, t, d)


def mlstm_bidirectional(ctx_m, lat_m):
    qc, kc, vc, ic, fc = ctx_m
    qx, kx, vx, ix, fx = lat_m
    b = qx.shape[0]
    zero = (jnp.zeros((b, M_HEADS, M_HEAD_DIM, M_HEAD_DIM), jnp.float32),
            jnp.zeros((b, M_HEADS, M_HEAD_DIM), jnp.float32),
            jnp.zeros((b, M_HEADS), jnp.float32))
    rev = lambda a: jnp.flip(a, axis=2)
    st_f, hc_f = mlstm_scan(qc, kc, vc, ic[0], fc[0], zero)
    _, hx_f = mlstm_scan(qx, kx, vx, ix[0], fx[0], st_f)
    st_b, hc_b = mlstm_scan(rev(qc), rev(kc), rev(vc), rev(ic[1]), rev(fc[1]), zero)
    _, hx_b = mlstm_scan(rev(qx), rev(kx), rev(vx), rev(ix[1]), rev(fx[1]), st_b)
    return hx_f + rev(hx_b), hc_f + rev(hc_b)


def mlstm_output(h, o, g):
    b, _, t, _ = h.shape
    hn = h * lax.rsqrt(jnp.mean(h * h, -1, keepdims=True) + EPS)
    hn = hn.transpose(0, 2, 1, 3).reshape(b, t, M_WIDTH) * g.astype(jnp.float32)
    return hn.astype(o.dtype) * jax.nn.sigmoid(o)


def mla_project(cq, ckv, kr, qn_g, wq_up, kvn_g, wkv_up, rope):
    b, t, _ = cq.shape
    q = (rms_norm(cq, qn_g) @ wq_up).reshape(b, t, A_HEADS, A_NOPE + A_ROPE)
    kv = (rms_norm(ckv, kvn_g) @ wkv_up).reshape(b, t, A_HEADS, A_NOPE + A_V)
    q_nope, q_rope = q[..., :A_NOPE], q[..., A_NOPE:]
    k_nope, v = kv[..., :A_NOPE], kv[..., A_NOPE:]
    k_rope = kr[:, :, None, :]
    if rope is not None:
        q_rope = apply_rope(q_rope, *rope)
        k_rope = apply_rope(k_rope, *rope)
    q = jnp.concatenate([q_nope, q_rope], -1)
    k = jnp.concatenate([k_nope, jnp.broadcast_to(k_rope, (b, t, A_HEADS, A_ROPE))], -1)
    return q, k, v


def block_attention(q, k, v):
    b, tq, h, dk = q.shape
    nb = tq // Q_BLOCK
    qb = jnp.moveaxis(q.reshape(b, nb, Q_BLOCK, h, dk), 1, 0)
    scale = dk ** -0.5

    def one(qi):
        s = jnp.einsum("bqhd,bkhd->bhqk", qi, k).astype(jnp.float32) * scale
        p = jax.nn.softmax(s, -1).astype(v.dtype)
        return jnp.einsum("bhqk,bkhd->bqhd", p, v)

    out = lax.map(one, qb)
    return jnp.moveaxis(out, 0, 1).reshape(b, tq, h * v.shape[-1])


def expert_choice_moe(h, router_w, w_gate, w_up, w_down):
    b, n, _ = h.shape
    cap = EC_FACTOR * n // N_EXPERTS
    aff = jax.nn.softmax((h @ router_w).astype(jnp.float32), -1)
    g, idx = lax.top_k(jnp.swapaxes(aff, 1, 2), cap)
    xe = jax.vmap(lambda hb, ib: hb[ib])(h, idx)
    hid = jax.nn.silu(jnp.einsum("becd,edf->becf", xe, w_gate)) * jnp.einsum("becd,edf->becf", xe, w_up)
    ye = jnp.einsum("becf,efd->becd", hid, w_down) * g[..., None].astype(h.dtype)
    bidx = jnp.arange(b)[:, None, None]
    return jnp.zeros_like(h).at[bidx, idx].add(ye)


def setup_inputs(seed: int = 0) -> dict:
    key = jax.random.key(seed)
    ks = jax.random.split(key, 24)
    nrm = lambda k, shape, s: jax.random.normal(k, shape, jnp.float32) * s
    L, D = DEPTH, D_MODEL
    return {
        "x": nrm(ks[0], (BATCH, SEQ, D), 1.0),
        "c": nrm(ks[1], (BATCH, D), 1.0),
        "ctx": nrm(ks[2], (BATCH, CTX_LEN, D), 1.0),
        "c_ctx": nrm(ks[3], (D,), 1.0),
        "ada_w": nrm(ks[4], (L, D, 6 * D), 0.5 * D ** -0.5),
        "ada_b": nrm(ks[5], (L, 6 * D), 0.01),
        "norm1_g": 1.0 + nrm(ks[6], (L, D), 0.05),
        "norm2_g": 1.0 + nrm(ks[7], (L, D), 0.05),
        "w_in": nrm(ks[8], (L, D, IN_COLS), D ** -0.5),
        "m_conv_w": nrm(ks[9], (L, M_CONV_W, 2 * M_WIDTH), M_CONV_W ** -0.5),
        "m_conv_b": nrm(ks[10], (L, 2 * M_WIDTH), 0.01),
        "m_ib": nrm(ks[11], (L, 2, M_HEADS), 0.1),
        "m_fb": 3.0 + nrm(ks[12], (L, 2, M_HEADS), 0.5),
        "m_norm_g": 1.0 + nrm(ks[13], (L, M_WIDTH), 0.05),
        "a_qnorm_g": 1.0 + nrm(ks[14], (L, Q_LORA), 0.05),
        "a_wq_up": nrm(ks[15], (L, Q_LORA, A_HEADS * (A_NOPE + A_ROPE)), Q_LORA ** -0.5),
        "a_kvnorm_g": 1.0 + nrm(ks[16], (L, KV_LORA), 0.05),
        "a_wkv_up": nrm(ks[17], (L, KV_LORA, A_HEADS * (A_NOPE + A_V)), KV_LORA ** -0.5),
        "w_out": nrm(ks[18], (L, D_MIX, D), D_MIX ** -0.5),
        "router_w": nrm(ks[19], (L, D, N_EXPERTS), D ** -0.5),
        "e_w_gate": nrm(ks[20], (L, N_EXPERTS, D, EXPERT_FF), D ** -0.5),
        "e_w_up": nrm(ks[21], (L, N_EXPERTS, D, EXPERT_FF), D ** -0.5),
        "e_w_down": nrm(ks[22], (L, N_EXPERTS, EXPERT_FF, D), EXPERT_FF ** -0.5),
        "final_g": 1.0 + nrm(ks[23], (D,), 0.05),
    }


def reference(x, c, ctx, c_ctx, ada_w, ada_b, norm1_g, norm2_g, w_in, m_conv_w, m_conv_b, m_ib, m_fb,
              m_norm_g, a_qnorm_g, a_wq_up, a_kvnorm_g, a_wkv_up, w_out, router_w, e_w_gate, e_w_up,
              e_w_down, final_g):
    b, n, _ = x.shape
    rope = axial_rope(n, x.dtype)
    for layer in range(DEPTH):
        with_ctx_out = layer < DEPTH - 1
        mod_x = (jax.nn.silu(c) @ ada_w[layer] + ada_b[layer])[:, None, :]
        mod_c = (jax.nn.silu(c_ctx) @ ada_w[layer] + ada_b[layer])[None, None, :]
        sh1x, sc1x, g1x, sh2x, sc2x, g2x = jnp.split(mod_x, 6, -1)
        sh1c, sc1c, g1c, sh2c, sc2c, g2c = jnp.split(mod_c, 6, -1)

        ux = modulated_rms(x, norm1_g[layer], sh1x, sc1x) @ w_in[layer]
        uc = modulated_rms(ctx, norm1_g[layer], sh1c, sc1c) @ w_in[layer]
        fx, qkx, vx, ox, gx, cqx, ckvx, krx = split_in(ux)
        fc, qkc, vc, oc, gc, cqc, ckvc, krc = split_in(uc)

        ya_x = fourier_mix(fx)
        lat_m = mlstm_prepare(qkx, vx, gx, m_conv_w[layer], m_conv_b[layer], m_ib[layer], m_fb[layer])
        ctx_m = mlstm_prepare(qkc, vc, gc, m_conv_w[layer], m_conv_b[layer], m_ib[layer], m_fb[layer])
        hb_x, hb_c = mlstm_bidirectional(ctx_m, lat_m)
        yb_x = mlstm_output(hb_x, ox, m_norm_g[layer])
        q_x, k_x, v_x = mla_project(cqx, ckvx, krx, a_qnorm_g[layer], a_wq_up[layer], a_kvnorm_g[layer], a_wkv_up[layer], rope)
        q_c, k_c, v_c = mla_project(cqc, ckvc, krc, a_qnorm_g[layer], a_wq_up[layer], a_kvnorm_g[layer], a_wkv_up[layer], None)
        yc_x = block_attention(q_x, jnp.concatenate([k_x, k_c], 1), jnp.concatenate([v_x, v_c], 1))

        x = x + g1x * (jnp.concatenate([ya_x, yb_x, yc_x], -1) @ w_out[layer])
        if with_ctx_out:
            ya_c = fourier_mix(fc)
            yb_c = mlstm_output(hb_c, oc, m_norm_g[layer])
            yc_c = block_attention(q_c, k_c, v_c)
            ctx = ctx + g1c * (jnp.concatenate([ya_c, yb_c, yc_c], -1) @ w_out[layer])

        hx = modulated_rms(x, norm2_g[layer], sh2x, sc2x)
        x = x + g2x * expert_choice_moe(hx, router_w[layer], e_w_gate[layer], e_w_up[layer], e_w_down[layer])
        if with_ctx_out:
            hc = modulated_rms(ctx, norm2_g[layer], sh2c, sc2c)
            ctx = ctx + g2c * expert_choice_moe(hc, router_w[layer], e_w_gate[layer], e_w_up[layer], e_w_down[layer])
    return rms_norm(x, final_g)
```

```python
import functools
import math

import jax
import jax.numpy as jnp
from jax import lax
from jax.experimental import pallas as pl
from jax.experimental.pallas import tpu as pltpu

F32 = jnp.float32
BF16 = jnp.bfloat16
I32 = jnp.int32

EPS = 1e-6
GRID_W = 64
ROPE_BASE = 10000.0
LANE = 128
HEAD = 64
M_HEADS = 4
A_HEADS = 8
A_ROPE = 32
N_EXPERTS = 16
EC_FACTOR = 2
TOK_TILE = 256
CHUNK = 128
GATHER_ALIGN_F32 = 8
GATHER_ALIGN_BF16 = 16
VMEM_LIMIT = 56 * 1024 * 1024


def _cparams(n_axes, vmem=None):
    return pltpu.CompilerParams(dimension_semantics=("arbitrary",) * n_axes,
                                vmem_limit_bytes=vmem)


def _dot(a, b):
    return jnp.dot(a, b, preferred_element_type=F32)


def _dot_nt(a, b):
    return lax.dot_general(a, b, (((1,), (1,)), ((), ())), preferred_element_type=F32)


def _split_bf16(a):
    hi = a.astype(BF16)
    lo = (a - hi.astype(F32)).astype(BF16)
    return hi, lo


def _sigmoid(x):
    return 1.0 / (1.0 + jnp.exp(-x))


def _rms(x, g):
    return x * lax.rsqrt(jnp.mean(x * x, axis=-1, keepdims=True) + EPS) * g


def _ada_kernel(c_ref, w_ref, b_ref, o_ref):
    a = c_ref[...]
    a = a * _sigmoid(a)
    a_hi, a_lo = _split_bf16(a)
    w_hi, w_lo = _split_bf16(w_ref[0])
    o_ref[0] = _dot(a_hi, w_hi) + _dot(a_lo, w_hi) + _dot(a_hi, w_lo) + b_ref[0]


def _ada_mods(cvec, ada_w, ada_b):
    L, D, D6 = ada_w.shape
    rows = cvec.shape[0]
    return pl.pallas_call(
        _ada_kernel,
        grid=(L, D6 // D),
        in_specs=[pl.BlockSpec((rows, D), lambda l, j: (0, 0)),
                  pl.BlockSpec((1, D, D), lambda l, j: (l, 0, j)),
                  pl.BlockSpec((1, 1, D), lambda l, j: (l, 0, j))],
        out_specs=pl.BlockSpec((1, rows, D), lambda l, j: (l, 0, j)),
        out_shape=jax.ShapeDtypeStruct((L, rows, D6), F32),
        compiler_params=_cparams(2),
        name="ada_mods",
    )(cvec, ada_w, ada_b.reshape(L, 1, D6))


_C_F = (0, 256)
_C_Q = (256, 768)
_C_K = (768, 1280)
_C_V = (1280, 1792)
_C_O = (1792, 2304)
_C_CQ = (2304, 2688)
_C_CKV = (2688, 2944)
_C_KR = (2944, 3072)
_C_G = (3072, 3200)
_N_COLS = 3200


def _inproj_kernel(x_ref, mod_ref, n1_ref, w1_ref, wgt_ref, qn_ref, kvn_ref, wq_ref, wqs_ref,
                   wk_ref, wv_ref, sel_ref, cos_ref, sin_ref,
                   f_ref, qk_ref, vm_ref, o_ref, gat_ref, gatt_ref, q_ref, k_ref, va_ref, *, d_model,
                   q_scale):
    D = d_model
    x = x_ref[0]
    mod = mod_ref[0]
    sh1, sc1 = mod[:, 0:D], mod[:, D:2 * D]
    xm = _rms(x, n1_ref[...]) * (1.0 + sc1) + sh1
    xb = xm.astype(BF16)
    u = _dot(xb, w1_ref[...])
    f_ref[0] = u[:, _C_F[0]:_C_F[1]].astype(BF16)
    qk_ref[0] = u[:, _C_Q[0]:_C_K[1]].astype(BF16)
    lane512 = lax.broadcasted_iota(I32, (1, 4 * LANE), 1)
    ones_m = jnp.where(lane512 % LANE == HEAD, 1.0, 0.0)
    vm_ref[0] = (u[:, _C_V[0]:_C_V[1]] + ones_m).astype(BF16)
    o_ref[0] = u[:, _C_O[0]:_C_O[1]].astype(BF16)
    gat_ref[0] = u[:, _C_G[0]:_C_G[0] + 16]
    gatt_ref[0] = _dot_nt(wgt_ref[...], xb)

    cosw = jnp.tile(cos_ref[...], (1, A_HEADS))
    sinw = jnp.tile(sin_ref[...], (1, A_HEADS))
    cqn = _rms(u[:, _C_CQ[0]:_C_CQ[1]], qn_ref[...]).astype(BF16)
    q = (_dot(cqn, wq_ref[...]) * cosw + _dot(cqn, wqs_ref[...]) * sinw) * q_scale
    q_ref[0] = q.astype(BF16)
    ckvn = _rms(u[:, _C_CKV[0]:_C_CKV[1]], kvn_ref[...]).astype(BF16)
    krw = _dot(u[:, _C_KR[0]:_C_KR[1]].astype(BF16), sel_ref[...])
    kw = A_HEADS * LANE
    k = _dot(ckvn, wk_ref[...]) + krw[:, :kw] * cosw + krw[:, kw:] * sinw
    k_ref[0] = k.astype(BF16)
    lane1024 = lax.broadcasted_iota(I32, (1, kw), 1)
    ones_a = jnp.where(lane1024 % LANE == HEAD, 1.0, 0.0)
    va_ref[0] = (_dot(ckvn, wv_ref[...]) + ones_a).astype(BF16)


def _inproj(xs, mods_l, n1, w1, wgt, qn, kvn, wq, wqs, wk, wv, sel, cos128, sin128, *, n_ctx_tiles):
    B, S, D = xs.shape
    nt = S // TOK_TILE
    tok = lambda w: pl.BlockSpec((1, TOK_TILE, w), lambda b, i: (b, i, 0))
    full = lambda a: pl.BlockSpec(a.shape, lambda b, i: (0,) * a.ndim)
    mod_spec = pl.BlockSpec((1, 1, 6 * D), lambda b, i: (jnp.where(i < n_ctx_tiles, B, b), 0, 0))
    tab_spec = pl.BlockSpec((TOK_TILE, LANE), lambda b, i: (i, 0))
    sd = lambda w, dt: jax.ShapeDtypeStruct((B, S, w), dt)
    kern = functools.partial(_inproj_kernel, d_model=D, q_scale=(HEAD + A_ROPE) ** -0.5)
    return pl.pallas_call(
        kern,
        grid=(B, nt),
        in_specs=[tok(D), mod_spec, full(n1), full(w1), full(wgt), full(qn), full(kvn), full(wq),
                  full(wqs), full(wk), full(wv), full(sel), tab_spec, tab_spec],
        out_specs=[tok(256), tok(1024), tok(512), tok(512), tok(16),
                   pl.BlockSpec((1, 16, TOK_TILE), lambda b, i: (b, 0, i)),
                   tok(1024), tok(1024), tok(1024)],
        out_shape=[sd(256, BF16), sd(1024, BF16), sd(512, BF16), sd(512, BF16), sd(16, F32),
                   jax.ShapeDtypeStruct((B, 16, S), F32),
                   sd(1024, BF16), sd(1024, BF16), sd(1024, BF16)],
        compiler_params=_cparams(2, VMEM_LIMIT),
        name="inproj",
    )(xs, mods_l, n1, w1, wgt, qn, kvn, wq, wqs, wk, wv, sel, cos128, sin128)


def _dft_mats(n):
    idx = jnp.arange(n, dtype=F32)
    ang = 2.0 * math.pi * jnp.mod(idx[:, None] * idx[None, :], n) / n
    return jnp.cos(ang), jnp.sin(ang)


def _fft_chan_mats():
    c, s = _dft_mats(HEAD)
    eye = jnp.eye(4, dtype=F32)
    return jnp.concatenate([jnp.kron(eye, c), jnp.kron(eye, s)], 0).astype(BF16)


def _fft_ctx_kernel(f_ref, ft_ref, cs_ref, o_ref, *, n, scale):
    xst = _dot(ft_ref[...], f_ref[0])
    xr = xst[:n].astype(BF16)
    xi = xst[n:].astype(BF16)
    y = _dot(xr, cs_ref[0:256, :]) + _dot(xi, cs_ref[256:512, :])
    o_ref[0] = (y * scale).astype(BF16)


def _fft_ctx(f, n_ctx, cs):
    B = f.shape[0]
    c, s = _dft_mats(n_ctx)
    ft = jnp.concatenate([c, -s], 0).astype(BF16)
    kern = functools.partial(_fft_ctx_kernel, n=n_ctx, scale=(n_ctx * HEAD) ** -0.5)
    return pl.pallas_call(
        kern,
        grid=(B,),
        in_specs=[pl.BlockSpec((1, n_ctx, 256), lambda b: (b, 0, 0)),
                  pl.BlockSpec(ft.shape, lambda b: (0, 0)),
                  pl.BlockSpec(cs.shape, lambda b: (0, 0))],
        out_specs=pl.BlockSpec((1, n_ctx, 256), lambda b: (b, 0, 0)),
        out_shape=jax.ShapeDtypeStruct((B, n_ctx, 256), BF16),
        compiler_params=_cparams(1),
        name="fft_ctx",
    )(f, ft, cs)


def _fft_stage1_kernel(x_ref, f1_ref, cw_ref, sw_ref, o_ref, *, n1):
    z = _dot(f1_ref[...], x_ref[0])
    zr, zi = z[:n1], z[n1:]
    cw, sw = cw_ref[...], sw_ref[...]
    o_ref[0, 0] = (zr * cw + zi * sw).astype(BF16)
    o_ref[0, 1] = (zi * cw - zr * sw).astype(BF16)


def _fft_stage2_kernel(z_ref, f2_ref, cs_ref, o_ref, *, tb, scale):
    for j in range(tb):
        zcat = jnp.concatenate([z_ref[0, 0, j], z_ref[0, 1, j]], axis=0)
        xst = _dot(f2_ref[...], zcat)
        xr = xst[:HEAD].astype(BF16)
        xi = xst[HEAD:].astype(BF16)
        y = _dot(xr, cs_ref[0:256, :]) + _dot(xi, cs_ref[256:512, :])
        o_ref[0, j] = (y * scale).astype(BF16)


def _fft_latent(f_lat, cs):
    B, T, W = f_lat.shape
    n2 = HEAD
    n1 = T // n2
    cols = n2 * W
    cb = 2048
    c1, s1 = _dft_mats(n1)
    f1 = jnp.concatenate([c1, -s1], 0).astype(BF16)
    t1 = jnp.arange(n1, dtype=F32)[:, None]
    s2 = jnp.arange(n2, dtype=F32)[None, :]
    ang = 2.0 * math.pi * (t1 * s2) / T
    cw = jnp.repeat(jnp.cos(ang), W, axis=1)
    sw = jnp.repeat(jnp.sin(ang), W, axis=1)
    z = pl.pallas_call(
        functools.partial(_fft_stage1_kernel, n1=n1),
        grid=(B, cols // cb),
        in_specs=[pl.BlockSpec((1, n1, cb), lambda b, j: (b, 0, j)),
                  pl.BlockSpec(f1.shape, lambda b, j: (0, 0)),
                  pl.BlockSpec((n1, cb), lambda b, j: (0, j)),
                  pl.BlockSpec((n1, cb), lambda b, j: (0, j))],
        out_specs=pl.BlockSpec((1, 2, n1, cb), lambda b, j: (b, 0, 0, j)),
        out_shape=jax.ShapeDtypeStruct((B, 2, n1, cols), BF16),
        compiler_params=_cparams(2),
        name="fft_stage1",
    )(f_lat.reshape(B, n1, cols), f1, cw, sw)
    z = z.reshape(B, 2, n1, n2, W)
    c2, s2m = _dft_mats(n2)
    f2 = jnp.concatenate([jnp.concatenate([c2, s2m], 1),
                          jnp.concatenate([-s2m, c2], 1)], 0).astype(BF16)
    tb = 8
    y = pl.pallas_call(
        functools.partial(_fft_stage2_kernel, tb=tb, scale=(T * HEAD) ** -0.5),
        grid=(B, n1 // tb),
        in_specs=[pl.BlockSpec((1, 2, tb, n2, W), lambda b, j: (b, 0, j, 0, 0)),
                  pl.BlockSpec(f2.shape, lambda b, j: (0, 0)),
                  pl.BlockSpec(cs.shape, lambda b, j: (0, 0))],
        out_specs=pl.BlockSpec((1, tb, n2, W), lambda b, j: (b, j, 0, 0)),
        out_shape=jax.ShapeDtypeStruct((B, n1, n2, W), BF16),
        compiler_params=_cparams(2),
        name="fft_stage2",
    )(z, f2, cs)
    return jnp.transpose(y, (0, 2, 1, 3)).reshape(B, T, W)


def _mlstm_prep_kernel(cur_ref, prev_ref, next_ref, cw_ref, cb_ref, gat_ref, gatt_ref, br_ref, bc_ref,
                       q_ref, kt_ref, col_ref, row_ref, *, n_chunks, n_ctx_chunks):
    c = pl.program_id(1)
    cur = cur_ref[0].astype(F32)
    first = jnp.logical_or(c == 0, c == n_ctx_chunks)
    last = jnp.logical_or(c == n_ctx_chunks - 1, c == n_chunks - 1)
    prev_row = prev_ref[0].astype(F32)[15:16, :]
    next_row = next_ref[0].astype(F32)[0:1, :]
    prev_row = jnp.where(first, 0.0, prev_row)
    next_row = jnp.where(last, 0.0, next_row)
    rows = lax.broadcasted_iota(I32, (CHUNK, 1), 0)
    up = jnp.where(rows == 0, prev_row, pltpu.roll(cur, 1, axis=0))
    dn = jnp.where(rows == CHUNK - 1, next_row, pltpu.roll(cur, CHUNK - 1, axis=0))
    y = cw_ref[0:1, :] * up + cw_ref[1:2, :] * cur + cw_ref[2:3, :] * dn + cb_ref[...]
    y = y * _sigmoid(y)
    hw = M_HEADS * LANE
    q_ref[0] = (y[:, :hw] * HEAD ** -0.5).astype(BF16)
    kt_ref[0] = y[:, hw:].T.astype(BF16)

    def logsig(v):
        return jnp.minimum(v, 0.0) - jnp.log(1.0 + jnp.exp(-jnp.abs(v)))

    r_i = lax.broadcasted_iota(I32, (CHUNK, CHUNK), 0)
    c_i = lax.broadcasted_iota(I32, (CHUNK, CHUNK), 1)
    lower = jnp.where(c_i <= r_i, 1.0, 0.0).astype(BF16)
    upper = jnp.where(c_i >= r_i, 1.0, 0.0).astype(BF16)

    g = gat_ref[0] + br_ref[...]
    lane = lax.broadcasted_iota(I32, (1, 16), 1)
    lg = jnp.where(lane % 8 >= 4, logsig(g), g)
    hi, lo = _split_bf16(lg)
    cum_f = _dot(lower, hi) + _dot(lower, lo)
    cum_b = _dot(upper, hi) + _dot(upper, lo)
    col_ref[0] = jnp.where(lane < 8, cum_f, cum_b)

    gt = gatt_ref[0] + bc_ref[...]
    row = lax.broadcasted_iota(I32, (16, 1), 0)
    lgt = jnp.where(row % 8 >= 4, logsig(gt), gt)
    hit, lot = _split_bf16(lgt)
    cum_ft = _dot(hit, upper) + _dot(lot, upper)
    cum_bt = _dot(hit, lower) + _dot(lot, lower)
    cumt = jnp.where(row < 8, cum_ft, cum_bt)
    rterm = lgt - pltpu.roll(cumt, 12, axis=0)
    btot = jnp.broadcast_to(jnp.sum(lgt, axis=1, keepdims=True), (16, CHUNK))
    row_ref[0] = jnp.where(row % 8 < 4, rterm, btot)


def _mlstm_prep(qk, gat, gatt, conv_w, conv_b, bias_row, bias_col, *, n_ctx_chunks):
    B, S, W = qk.shape
    nc = S // CHUNK
    n16 = S // 16
    hw = M_HEADS * LANE
    kern = functools.partial(_mlstm_prep_kernel, n_chunks=nc, n_ctx_chunks=n_ctx_chunks)
    full = lambda a: pl.BlockSpec(a.shape, lambda b, c: (0,) * a.ndim)
    return pl.pallas_call(
        kern,
        grid=(B, nc),
        in_specs=[pl.BlockSpec((1, CHUNK, W), lambda b, c: (b, c, 0)),
                  pl.BlockSpec((1, 16, W), lambda b, c: (b, jnp.maximum(c * 8 - 1, 0), 0)),
                  pl.BlockSpec((1, 16, W), lambda b, c: (b, jnp.minimum((c + 1) * 8, n16 - 1), 0)),
                  full(conv_w), full(conv_b),
                  pl.BlockSpec((1, CHUNK, 16), lambda b, c: (b, c, 0)),
                  pl.BlockSpec((1, 16, CHUNK), lambda b, c: (b, 0, c)),
                  full(bias_row), full(bias_col)],
        out_specs=[pl.BlockSpec((1, CHUNK, hw), lambda b, c: (b, c, 0)),
                   pl.BlockSpec((1, hw, CHUNK), lambda b, c: (b, 0, c)),
                   pl.BlockSpec((1, CHUNK, 16), lambda b, c: (b, c, 0)),
                   pl.BlockSpec((1, 16, CHUNK), lambda b, c: (b, 0, c))],
        out_shape=[jax.ShapeDtypeStruct((B, S, hw), BF16),
                   jax.ShapeDtypeStruct((B, hw, S), BF16),
                   jax.ShapeDtypeStruct((B, S, 16), F32),
                   jax.ShapeDtypeStruct((B, 16, S), F32)],
        compiler_params=_cparams(2),
        name="mlstm_prep",
    )(qk, qk, qk, conv_w, conv_b, gat, gatt, bias_row, bias_col)


def _mlstm_scan_kernel(q_ref, kt_ref, v_ref, col_ref, row_ref, o_ref, c_scr, m_scr):
    d = pl.program_id(1)
    j = pl.program_id(2)

    @pl.when(j == 0)
    def _():
        c_scr[...] = jnp.zeros_like(c_scr)
        m_scr[...] = jnp.zeros_like(m_scr)

    fwd = d == 0
    colp = col_ref[0]
    rowp = row_ref[0]
    t_i = lax.broadcasted_iota(I32, (CHUNK, CHUNK), 0)
    s_i = lax.broadcasted_iota(I32, (CHUNK, CHUNK), 1)
    mask = jnp.where(fwd, s_i - t_i, t_i - s_i) <= 0
    outs = []
    for h in range(M_HEADS):
        bcol = jnp.where(fwd, colp[:, 4 + h:5 + h], colp[:, 12 + h:13 + h])
        rrow = jnp.where(fwd, rowp[h:h + 1, :], rowp[8 + h:9 + h, :])
        btot = jnp.where(fwd, rowp[4 + h:5 + h, :], rowp[12 + h:13 + h, :])
        m = m_scr[h:h + 1, 0:1]
        dm = jnp.where(mask, bcol + rrow, -jnp.inf)
        inter = bcol + m
        m_t = jnp.maximum(inter, jnp.max(dm, axis=1, keepdims=True))
        w = jnp.exp(dm - m_t)
        a = jnp.exp(inter - m_t)
        qh = q_ref[0, :, h * LANE:(h + 1) * LANE]
        kth = kt_ref[0, h * LANE:(h + 1) * LANE, :]
        vh = v_ref[0, :, h * LANE:(h + 1) * LANE]
        s = _dot(qh, kth) * w
        cst = c_scr[h]
        tot = a * _dot(qh, cst.astype(BF16)) + _dot(s.astype(BF16), vh)
        den = tot[:, HEAD:HEAD + 1]
        outs.append(tot / jnp.maximum(jnp.abs(den), jnp.exp(-m_t)))
        dec = btot + rrow
        bm = btot[:, 0:1] + m
        m_new = jnp.maximum(bm, jnp.max(dec, axis=1, keepdims=True))
        ws = jnp.exp(dec - m_new)
        kw = (kth.astype(F32) * ws).astype(BF16)
        c_scr[h] = jnp.exp(bm - m_new) * cst + _dot(kw, vh)
        m_scr[h:h + 1, :] = jnp.broadcast_to(m_new, (1, LANE))
    o_ref[0, 0] = jnp.concatenate(outs, axis=1)


def _mlstm_scan(q, kt, v, colp, rowp, *, n_ctx_chunks):
    B, S, hw = q.shape
    nc = S // CHUNK

    def cid(d, j):
        rev = jnp.where(j < n_ctx_chunks, n_ctx_chunks - 1 - j, nc + n_ctx_chunks - 1 - j)
        return jnp.where(d == 0, j, rev)

    return pl.pallas_call(
        _mlstm_scan_kernel,
        grid=(B, 2, nc),
        in_specs=[pl.BlockSpec((1, CHUNK, hw), lambda b, d, j: (b, cid(d, j), 0)),
                  pl.BlockSpec((1, hw, CHUNK), lambda b, d, j: (b, 0, cid(d, j))),
                  pl.BlockSpec((1, CHUNK, hw), lambda b, d, j: (b, cid(d, j), 0)),
                  pl.BlockSpec((1, CHUNK, 16), lambda b, d, j: (b, cid(d, j), 0)),
                  pl.BlockSpec((1, 16, CHUNK), lambda b, d, j: (b, 0, cid(d, j)))],
        out_specs=pl.BlockSpec((1, 1, CHUNK, hw), lambda b, d, j: (d, b, cid(d, j), 0)),
        out_shape=jax.ShapeDtypeStruct((2, B, S, hw), F32),
        scratch_shapes=[pltpu.VMEM((M_HEADS, LANE, LANE), F32), pltpu.VMEM((8, LANE), F32)],
        compiler_params=_cparams(3),
        name="mlstm_scan",
    )(q, kt, v, colp, rowp)


def _attn_kernel(q_ref, k_ref, v_ref, o_ref, *, n_ctx_tiles, n_ctx_kc, n_all_kc, tk):
    i = pl.program_id(2)
    n_kc = jnp.where(i < n_ctx_tiles, n_ctx_kc, n_all_kc)
    outs = []
    for hh in range(2):
        lanes = slice(hh * LANE, (hh + 1) * LANE)
        qh = q_ref[0, :, lanes]

        def body(c, carry):
            m, acc = carry
            start = pl.multiple_of(c * tk, tk)
            kh = k_ref[0, pl.ds(start, tk), lanes]
            vh = v_ref[0, pl.ds(start, tk), lanes]
            s = _dot_nt(qh, kh)
            m_new = jnp.maximum(m, jnp.max(s, axis=1, keepdims=True))
            p = jnp.exp(s - m_new)
            acc = jnp.exp(m - m_new) * acc + _dot(p.astype(BF16), vh)
            return m_new, acc

        m0 = jnp.full((TOK_TILE, 1), -jnp.inf, F32)
        acc0 = jnp.zeros((TOK_TILE, LANE), F32)
        _, acc = lax.fori_loop(0, n_kc, body, (m0, acc0))
        outs.append(acc[:, :HEAD] / acc[:, HEAD:HEAD + 1])
    o_ref[0] = jnp.concatenate(outs, axis=1).astype(BF16)


def _attention(q, k, v, *, n_ctx):
    B, S, W = q.shape
    tk = 256
    kern = functools.partial(_attn_kernel, n_ctx_tiles=n_ctx // TOK_TILE, n_ctx_kc=n_ctx // tk,
                             n_all_kc=S // tk, tk=tk)
    return pl.pallas_call(
        kern,
        grid=(B, A_HEADS // 2, S // TOK_TILE),
        in_specs=[pl.BlockSpec((1, TOK_TILE, 2 * LANE), lambda b, hp, i: (b, i, hp)),
                  pl.BlockSpec((1, S, 2 * LANE), lambda b, hp, i: (b, 0, hp)),
                  pl.BlockSpec((1, S, 2 * LANE), lambda b, hp, i: (b, 0, hp))],
        out_specs=pl.BlockSpec((1, TOK_TILE, LANE), lambda b, hp, i: (b, i, hp)),
        out_shape=jax.ShapeDtypeStruct((B, S, A_HEADS * HEAD), BF16),
        compiler_params=_cparams(3, VMEM_LIMIT),
        name="attention",
    )(q, k, v)


def _outproj_kernel(yac_ref, yal_ref, hf_ref, hb_ref, o_ref, yc_ref, x_ref, mod_ref, wa_ref, wb_ref,
                    wc_ref, mg_ref, blk_ref, n2_ref, rw_ref, xo_ref, hx_ref, aff_ref, *, d_model,
                    n_ctx_tiles):
    D = d_model
    i = pl.program_id(1)
    mod = mod_ref[0]
    g1, sh2, sc2 = mod[:, 2 * D:3 * D], mod[:, 3 * D:4 * D], mod[:, 4 * D:5 * D]
    ya = jnp.where(i < n_ctx_tiles, yac_ref[0], yal_ref[0])
    lane = lax.broadcasted_iota(I32, (1, M_HEADS * LANE), 1)
    h = jnp.where(lane % LANE < HEAD, hf_ref[0, 0] + hb_ref[0, 0], 0.0)
    hi, lo = _split_bf16(h * h)
    ms = _dot(hi, blk_ref[...]) + _dot(lo, blk_ref[...])
    hn = h * lax.rsqrt(ms + EPS) * mg_ref[...]
    yb = (hn * _sigmoid(o_ref[0].astype(F32))).astype(BF16)
    mix = _dot(ya, wa_ref[...]) + _dot(yb, wb_ref[...]) + _dot(yc_ref[0], wc_ref[...])
    x = x_ref[0] + g1 * mix
    xo_ref[0] = x
    hx = _rms(x, n2_ref[...]) * (1.0 + sc2) + sh2
    hx_ref[0] = hx.astype(BF16)
    h_hi, h_lo = _split_bf16(hx)
    r_hi, r_lo = _split_bf16(rw_ref[...])
    logits = _dot(h_hi, r_hi) + _dot(h_lo, r_hi) + _dot(h_hi, r_lo)
    e = jnp.exp(logits - jnp.max(logits, axis=1, keepdims=True))
    aff_ref[0] = e / jnp.sum(e, axis=1, keepdims=True)


def _outproj(ya_ctx, ya_lat, hdir, o, yc, xs, mods_l, wa, wb, wc, mg, blk, n2, rw, *, n_ctx_tiles):
    B, S, D = xs.shape
    nt = S // TOK_TILE
    tok = lambda w: pl.BlockSpec((1, TOK_TILE, w), lambda b, i: (b, i, 0))
    full = lambda a: pl.BlockSpec(a.shape, lambda b, i: (0,) * a.ndim)
    mod_spec = pl.BlockSpec((1, 1, 6 * D), lambda b, i: (jnp.where(i < n_ctx_tiles, B, b), 0, 0))
    hw = M_HEADS * LANE
    n_lat_tiles = nt - n_ctx_tiles
    kern = functools.partial(_outproj_kernel, d_model=D, n_ctx_tiles=n_ctx_tiles)
    return pl.pallas_call(
        kern,
        grid=(B, nt),
        in_specs=[pl.BlockSpec((1, TOK_TILE, 256), lambda b, i: (b, jnp.minimum(i, n_ctx_tiles - 1), 0)),
                  pl.BlockSpec((1, TOK_TILE, 256),
                               lambda b, i: (b, jnp.clip(i - n_ctx_tiles, 0, n_lat_tiles - 1), 0)),
                  pl.BlockSpec((1, 1, TOK_TILE, hw), lambda b, i: (0, b, i, 0)),
                  pl.BlockSpec((1, 1, TOK_TILE, hw), lambda b, i: (1, b, i, 0)),
                  tok(hw), tok(A_HEADS * HEAD), tok(D), mod_spec,
                  full(wa), full(wb), full(wc), full(mg), full(blk), full(n2), full(rw)],
        out_specs=[tok(D), tok(D), tok(N_EXPERTS)],
        out_shape=[jax.ShapeDtypeStruct((B, S, D), F32), jax.ShapeDtypeStruct((B, S, D), BF16),
                   jax.ShapeDtypeStruct((B, S, N_EXPERTS), F32)],
        compiler_params=_cparams(2, VMEM_LIMIT),
        name="outproj",
    )(ya_ctx, ya_lat, hdir, hdir, o, yc, xs, mods_l, wa, wb, wc, mg, blk, n2, rw)


def _route_part(a, cap, slot0, tri, pos_ref, base_ref, lane0):
    n = a.shape[1]
    capf = float(cap)
    bits = pltpu.bitcast(a, I32)
    v = jnp.zeros((N_EXPERTS, 1), I32)
    for bit in range(30, -1, -1):
        cand = v | (1 << bit)
        cnt = jnp.sum(jnp.where(bits >= cand, 1.0, 0.0), axis=1, keepdims=True)
        v = jnp.where(cnt >= capf, cand, v)
    gt = bits > v
    eq = bits == v
    need = capf - jnp.sum(jnp.where(gt, 1.0, 0.0), axis=1, keepdims=True)
    idx = lax.broadcasted_iota(I32, (N_EXPERTS, n), 1)
    x = jnp.zeros((N_EXPERTS, 1), I32)
    for bit in range(max(n - 1, 1).bit_length() - 1, -1, -1):
        cand = x | (1 << bit)
        cnt = jnp.sum(jnp.where(eq, jnp.where(idx < cand, 1.0, 0.0), 0.0), axis=1, keepdims=True)
        x = jnp.where(cnt < need, cand, x)
    sel = jnp.where(gt, 1.0, jnp.where(eq, jnp.where(idx <= x, 1.0, 0.0), 0.0))
    running = jnp.zeros((N_EXPERTS, 1), F32)
    for c in range(n // CHUNK):
        blk = sel[:, c * CHUNK:(c + 1) * CHUNK]
        incl = _dot(blk.astype(BF16), tri)
        pos = running + incl - blk + float(slot0)
        cg = lane0 // CHUNK + c
        pos_ref[0, :, lane0 + c * CHUNK:lane0 + (c + 1) * CHUNK] = jnp.where(
            blk > 0.5, pos, -1.0).astype(I32)
        base_ref[0, :, cg:cg + 1] = (running + float(slot0)).astype(I32)
        running = running + incl[:, CHUNK - 1:CHUNK]


def _route_kernel(aff_ref, pos_ref, base_ref, *, n_ctx, cap_ctx, cap_lat):
    r_i = lax.broadcasted_iota(I32, (CHUNK, CHUNK), 0)
    c_i = lax.broadcasted_iota(I32, (CHUNK, CHUNK), 1)
    tri = jnp.where(r_i <= c_i, 1.0, 0.0).astype(BF16)
    base_ref[...] = jnp.zeros_like(base_ref)
    a = aff_ref[0]
    _route_part(a[:, :n_ctx], cap_ctx, 0, tri, pos_ref, base_ref, 0)
    _route_part(a[:, n_ctx:], cap_lat, cap_ctx, tri, pos_ref, base_ref, n_ctx)


def _route(aff_t, *, n_ctx, cap_ctx, cap_lat):
    B, E, S = aff_t.shape
    kern = functools.partial(_route_kernel, n_ctx=n_ctx, cap_ctx=cap_ctx, cap_lat=cap_lat)
    return pl.pallas_call(
        kern,
        grid=(B,),
        in_specs=[pl.BlockSpec((1, E, S), lambda b: (b, 0, 0))],
        out_specs=[pl.BlockSpec((1, E, S), lambda b: (b, 0, 0)),
                   pl.BlockSpec((1, E, LANE), lambda b: (b, 0, 0))],
        out_shape=[jax.ShapeDtypeStruct((B, E, S), I32), jax.ShapeDtypeStruct((B, E, LANE), I32)],
        compiler_params=_cparams(1),
        name="route",
    )(aff_t)


def _window_start(base, align, rows, win):
    w0 = lax.shift_left(lax.shift_right_logical(base, int(math.log2(align))), int(math.log2(align)))
    return pl.multiple_of(jnp.minimum(w0, rows - win), align)


def _gather_kernel(base_ref, h_ref, pos_ref, xe_ref, acc_ref, *, n_chunks, rows, win):
    b = pl.program_id(0)
    e = pl.program_id(2)
    acc_ref[...] = jnp.zeros_like(acc_ref)

    def chunk(c, carry):
        w0 = _window_start(base_ref[b, e, c], GATHER_ALIGN_F32, rows, win)
        posr = pos_ref[0, 0, pl.ds(c, 1), :]
        slot = lax.broadcasted_iota(I32, (win, CHUNK), 0) + w0
        onehot = jnp.where(posr == slot, 1.0, 0.0).astype(BF16)
        hc = h_ref[0, pl.ds(pl.multiple_of(c * CHUNK, CHUNK), CHUNK), :]
        acc_ref[pl.ds(w0, win), :] += _dot(onehot, hc)
        return carry

    lax.fori_loop(0, n_chunks, chunk, 0)
    xe_ref[0, 0] = acc_ref[...].astype(BF16)


def _gather(hx, pos4, bases, *, rows):
    B, S, D = hx.shape
    nc = S // CHUNK
    dh = D // 2
    win = min(CHUNK + GATHER_ALIGN_F32, rows)
    kern = functools.partial(_gather_kernel, n_chunks=nc, rows=rows, win=win)
    return pl.pallas_call(
        kern,
        grid_spec=pltpu.PrefetchScalarGridSpec(
            num_scalar_prefetch=1,
            grid=(B, 2, N_EXPERTS),
            in_specs=[pl.BlockSpec((1, S, dh), lambda b, j, e, bs: (b, 0, j)),
                      pl.BlockSpec((1, 1, nc, CHUNK), lambda b, j, e, bs: (b, e, 0, 0))],
            out_specs=pl.BlockSpec((1, 1, rows, dh), lambda b, j, e, bs: (b, e, 0, j)),
            scratch_shapes=[pltpu.VMEM((rows, dh), F32)]),
        out_shape=jax.ShapeDtypeStruct((B, N_EXPERTS, rows, D), BF16),
        compiler_params=_cparams(3, VMEM_LIMIT),
        name="moe_gather",
    )(bases, hx, pos4)


def _ffn_kernel(x_ref, wg_ref, wu_ref, wd_ref, y_ref, *, row_tile, n_tiles):
    def tile(i, carry):
        r0 = pl.multiple_of(i * row_tile, 16)
        x = x_ref[0, 0, pl.ds(r0, row_tile), :]
        g = _dot(x, wg_ref[0])
        u = _dot(x, wu_ref[0])
        hid = (g * _sigmoid(g) * u).astype(BF16)
        y_ref[0, 0, pl.ds(r0, row_tile), :] = _dot(hid, wd_ref[0]).astype(BF16)
        return carry

    lax.fori_loop(0, n_tiles, tile, 0)


def _ffn_row_tile(rows):
    for t in (352, 256, 176, 128, 96, 64, 32, 16):
        if rows % t == 0:
            return t
    raise ValueError(f"expert slot rows {rows} must be a multiple of 16")


def _ffn(xe, wg, wu, wd):
    B, E, R, D = xe.shape
    F = wg.shape[-1]
    rt = _ffn_row_tile(R)
    kern = functools.partial(_ffn_kernel, row_tile=rt, n_tiles=R // rt)
    return pl.pallas_call(
        kern,
        grid=(E, B),
        in_specs=[pl.BlockSpec((1, 1, R, D), lambda e, b: (b, e, 0, 0)),
                  pl.BlockSpec((1, D, F), lambda e, b: (e, 0, 0)),
                  pl.BlockSpec((1, D, F), lambda e, b: (e, 0, 0)),
                  pl.BlockSpec((1, F, D), lambda e, b: (e, 0, 0))],
        out_specs=pl.BlockSpec((1, 1, R, D), lambda e, b: (b, e, 0, 0)),
        out_shape=jax.ShapeDtypeStruct((B, E, R, D), BF16),
        compiler_params=_cparams(2, VMEM_LIMIT),
        name="moe_ffn",
    )(xe, wg, wu, wd)


def _combine_kernel(base_ref, x_ref, ye_ref, posc_ref, aff_ref, modc_ref, modx_ref, o_ref, *, d_model,
                    blk_chunks, n_ctx_chunks, rows, win):
    D = d_model
    b = pl.program_id(0)
    tb = pl.program_id(1)
    e = pl.program_id(2)

    @pl.when(e == 0)
    def _():
        o_ref[0] = x_ref[0]

    g2c = modc_ref[0][:, 5 * D:6 * D]
    g2x = modx_ref[0][:, 5 * D:6 * D]
    lane = lax.broadcasted_iota(I32, (1, N_EXPERTS), 1)
    for cc in range(blk_chunks):
        rs = slice(cc * CHUNK, (cc + 1) * CHUNK)
        chunk = tb * blk_chunks + cc
        g2 = jnp.where(chunk < n_ctx_chunks, g2c, g2x)
        w0 = _window_start(base_ref[b, e, chunk], GATHER_ALIGN_BF16, rows, win)
        pcol = jnp.sum(jnp.where(lane == e, posc_ref[0, rs, :].astype(F32), 0.0), axis=1, keepdims=True)
        gcol = jnp.sum(jnp.where(lane == e, aff_ref[0, rs, :], 0.0), axis=1, keepdims=True)
        slot = (lax.broadcasted_iota(I32, (CHUNK, win), 1) + w0).astype(F32)
        onehot = jnp.where(pcol == slot, 1.0, 0.0).astype(BF16)
        yw = ye_ref[0, 0, pl.ds(w0, win), :]
        o_ref[0, rs, :] += (g2 * gcol) * _dot(onehot, yw)


def _combine_blk_chunks(nc):
    for k in (11, 8, 6, 4, 3, 2, 1):
        if nc % k == 0:
            return k
    return 1


def _combine(x, ye, posc, aff, mods_l, bases, *, n_ctx):
    B, S, D = x.shape
    R = ye.shape[2]
    nc = S // CHUNK
    k = _combine_blk_chunks(nc)
    tb = k * CHUNK
    win = min(CHUNK + GATHER_ALIGN_BF16, R)
    kern = functools.partial(_combine_kernel, d_model=D, blk_chunks=k, n_ctx_chunks=n_ctx // CHUNK,
                             rows=R, win=win)
    modc_spec = pl.BlockSpec((1, 1, 6 * D), lambda b, t, e, bs: (B, 0, 0))
    modx_spec = pl.BlockSpec((1, 1, 6 * D), lambda b, t, e, bs: (b, 0, 0))
    return pl.pallas_call(
        kern,
        grid_spec=pltpu.PrefetchScalarGridSpec(
            num_scalar_prefetch=1,
            grid=(B, S // tb, N_EXPERTS),
            in_specs=[pl.BlockSpec((1, tb, D), lambda b, t, e, bs: (b, t, 0)),
                      pl.BlockSpec((1, 1, R, D), lambda b, t, e, bs: (b, e, 0, 0)),
                      pl.BlockSpec((1, tb, N_EXPERTS), lambda b, t, e, bs: (b, t, 0)),
                      pl.BlockSpec((1, tb, N_EXPERTS), lambda b, t, e, bs: (b, t, 0)),
                      modc_spec, modx_spec],
            out_specs=pl.BlockSpec((1, tb, D), lambda b, t, e, bs: (b, t, 0))),
        out_shape=jax.ShapeDtypeStruct((B, S, D), F32),
        compiler_params=_cparams(3, VMEM_LIMIT),
        name="moe_combine",
    )(bases, x, ye, posc, aff, mods_l, mods_l)


def _final_kernel(x_ref, g_ref, o_ref):
    o_ref[0] = _rms(x_ref[0], g_ref[...])


def _final_norm(xs, g, *, n_ctx):
    B, S, D = xs.shape
    T = S - n_ctx
    off = n_ctx // TOK_TILE
    return pl.pallas_call(
        _final_kernel,
        grid=(B, T // TOK_TILE),
        in_specs=[pl.BlockSpec((1, TOK_TILE, D), lambda b, i: (b, i + off, 0)),
                  pl.BlockSpec((1, D), lambda b, i: (0, 0))],
        out_specs=pl.BlockSpec((1, TOK_TILE, D), lambda b, i: (b, i, 0)),
        out_shape=jax.ShapeDtypeStruct((B, T, D), F32),
        compiler_params=_cparams(2),
        name="final_norm",
    )(xs, g)


def _pad_heads(w, n_heads):
    lead = w.shape[:-1]
    w = w.reshape(*lead, n_heads, HEAD)
    w = jnp.pad(w, [(0, 0)] * len(lead) + [(0, 0), (0, LANE - HEAD)])
    return w.reshape(*lead, n_heads * LANE)


def _rot_swap(w):
    half = A_ROPE // 2
    return jnp.concatenate([-w[..., half:], w[..., :half]], -1)


def _rope_tables(n_ctx, n_lat):
    rows = n_lat // GRID_W
    row_id = jnp.repeat(jnp.arange(rows, dtype=F32), GRID_W)
    col_id = jnp.tile(jnp.arange(GRID_W, dtype=F32), rows)
    n_freq = A_ROPE // 4
    inv = ROPE_BASE ** (-jnp.arange(n_freq, dtype=F32) / n_freq)
    ang = jnp.concatenate([row_id[:, None] * inv, col_id[:, None] * inv], -1)
    ang = jnp.concatenate([jnp.zeros((n_ctx, A_ROPE // 2), F32), ang], 0)
    S = n_ctx + n_lat
    cos, sin = jnp.cos(ang), jnp.sin(ang)
    pad = jnp.zeros((S, LANE - HEAD - A_ROPE), F32)
    cos128 = jnp.concatenate([jnp.ones((S, HEAD), F32), cos, cos, pad], -1)
    sin128 = jnp.concatenate([jnp.zeros((S, HEAD), F32), sin, sin, pad], -1)
    return cos128, sin128


def kernel(x, c, ctx, c_ctx, ada_w, ada_b, norm1_g, norm2_g, w_in, m_conv_w, m_conv_b, m_ib, m_fb, m_norm_g, a_qnorm_g, a_wq_up, a_kvnorm_g, a_wkv_up, w_out, router_w, e_w_gate, e_w_up, e_w_down, final_g):
    B, T, D = x.shape
    n_ctx = ctx.shape[1]
    L = ada_w.shape[0]
    S = n_ctx + T
    assert D == 16 * HEAD and n_ctx % TOK_TILE == 0 and T % TOK_TILE == 0 and T % (GRID_W * 8) == 0
    n_ctx_tiles = n_ctx // TOK_TILE
    n_ctx_chunks = n_ctx // CHUNK
    cap_ctx = EC_FACTOR * n_ctx // N_EXPERTS
    cap_lat = EC_FACTOR * T // N_EXPERTS
    slot_rows = cap_ctx + cap_lat

    o_f, o_qk, o_v, o_o, o_g, o_cq, o_ckv, o_kr = (0, 256, 768, 1024, 1280, 1296, 1680, 1936)
    w_kr = w_in[:, :, o_kr:o_kr + A_ROPE]
    kr3 = jnp.concatenate([w_kr, _rot_swap(w_kr), jnp.zeros((L, D, LANE - 2 * A_ROPE), F32)], -1)
    w_g = w_in[:, :, o_g:o_g + 16]
    w1 = jnp.concatenate([
        w_in[:, :, o_f:o_f + 256],
        _pad_heads(w_in[:, :, o_qk:o_qk + 256], M_HEADS),
        _pad_heads(w_in[:, :, o_qk + 256:o_qk + 512], M_HEADS),
        _pad_heads(w_in[:, :, o_v:o_v + 256], M_HEADS),
        _pad_heads(w_in[:, :, o_o:o_o + 256], M_HEADS),
        w_in[:, :, o_cq:o_cq + 384],
        w_in[:, :, o_ckv:o_ckv + 256],
        kr3,
        jnp.pad(w_g, ((0, 0), (0, 0), (0, LANE - 16))),
    ], -1).astype(BF16)
    assert w1.shape[-1] == _N_COLS
    wgt = jnp.swapaxes(w_g, 1, 2).astype(BF16)
    conv_w = jnp.concatenate([_pad_heads(m_conv_w[..., :256], M_HEADS),
                              _pad_heads(m_conv_w[..., 256:], M_HEADS)], -1)
    conv_b = jnp.concatenate([_pad_heads(m_conv_b[..., :256], M_HEADS),
                              _pad_heads(m_conv_b[..., 256:], M_HEADS)], -1)[:, None, :]
    gate_bias = jnp.stack([m_ib, m_fb], 2).reshape(L, 16)
    wq3 = a_wq_up.reshape(L, -1, A_HEADS, HEAD + A_ROPE)
    zq = jnp.zeros(wq3.shape[:-1] + (LANE - HEAD - A_ROPE,), F32)
    wq = jnp.concatenate([wq3, zq], -1).reshape(L, -1, A_HEADS * LANE).astype(BF16)
    wqs = jnp.concatenate([jnp.zeros_like(wq3[..., :HEAD]), _rot_swap(wq3[..., HEAD:]), zq],
                          -1).reshape(L, -1, A_HEADS * LANE).astype(BF16)
    wkv3 = a_wkv_up.reshape(L, -1, A_HEADS, 2 * HEAD)
    zk = jnp.zeros(wkv3.shape[:-1] + (HEAD,), F32)
    wk = jnp.concatenate([wkv3[..., :HEAD], zk], -1).reshape(L, -1, A_HEADS * LANE).astype(BF16)
    wv = jnp.concatenate([wkv3[..., HEAD:], zk], -1).reshape(L, -1, A_HEADS * LANE).astype(BF16)
    j32 = jnp.arange(A_ROPE)
    heads = jnp.arange(A_HEADS)
    sel = jnp.zeros((LANE, 2 * A_HEADS * LANE), F32)
    dst = (heads[None, :] * LANE + HEAD + j32[:, None]).reshape(-1)
    sel = sel.at[jnp.repeat(j32, A_HEADS), dst].set(1.0)
    sel = sel.at[jnp.repeat(j32, A_HEADS) + A_ROPE, dst + A_HEADS * LANE].set(1.0)
    sel = sel.astype(BF16)
    wa = w_out[:, 0:256].astype(BF16)
    wb = jnp.swapaxes(_pad_heads(jnp.swapaxes(w_out[:, 256:512], 1, 2), M_HEADS), 1, 2).astype(BF16)
    wc = w_out[:, 512:1024].astype(BF16)
    mg = _pad_heads(m_norm_g, M_HEADS)[:, None, :]
    lane = jnp.arange(M_HEADS * LANE)
    blk = jnp.where((lane[:, None] // LANE == lane[None, :] // LANE) & (lane[:, None] % LANE < HEAD),
                    1.0 / HEAD, 0.0).astype(BF16)
    cos128, sin128 = _rope_tables(n_ctx, T)
    cs = _fft_chan_mats()
    wg_e = e_w_gate.astype(BF16)
    wu_e = e_w_up.astype(BF16)
    wd_e = e_w_down.astype(BF16)

    rows16 = 16
    cvec = jnp.zeros((rows16, D), F32).at[:B].set(c).at[B].set(c_ctx)
    mods = _ada_mods(cvec, ada_w, ada_b).reshape(L, rows16, 1, 6 * D)

    xs = jnp.concatenate([ctx, x], axis=1)
    for l in range(L):
        mods_l = mods[l]
        f, qk, vm, og, gat, gatt, q, k, va = _inproj(
            xs, mods_l, norm1_g[l][None], w1[l], wgt[l], a_qnorm_g[l][None], a_kvnorm_g[l][None],
            wq[l], wqs[l], wk[l], wv[l], sel, cos128, sin128, n_ctx_tiles=n_ctx_tiles)
        ya_ctx = _fft_ctx(f, n_ctx, cs)
        ya_lat = _fft_latent(f[:, n_ctx:], cs)
        qm, ktm, colp, rowp = _mlstm_prep(qk, gat, gatt, conv_w[l], conv_b[l], gate_bias[l][None, :],
                                          gate_bias[l][:, None], n_ctx_chunks=n_ctx_chunks)
        hdir = _mlstm_scan(qm, ktm, vm, colp, rowp, n_ctx_chunks=n_ctx_chunks)
        yc = _attention(q, k, va, n_ctx=n_ctx)
        xs, hx, aff = _outproj(ya_ctx, ya_lat, hdir, og, yc, xs, mods_l, wa[l], wb[l], wc[l], mg[l],
                               blk, norm2_g[l][None], router_w[l], n_ctx_tiles=n_ctx_tiles)
        posm, bases = _route(jnp.swapaxes(aff, 1, 2), n_ctx=n_ctx, cap_ctx=cap_ctx, cap_lat=cap_lat)
        xe = _gather(hx, posm.reshape(B, N_EXPERTS, S // CHUNK, CHUNK), bases, rows=slot_rows)
        ye = _ffn(xe, wg_e[l], wu_e[l], wd_e[l])
        xs = _combine(xs, ye, jnp.swapaxes(posm, 1, 2), aff, mods_l, bases, n_ctx=n_ctx)
    return _final_norm(xs, final_g[None], n_ctx=n_ctx)
```

```python
import functools
import math

import jax
import jax.numpy as jnp
from jax import lax
from jax.experimental import pallas as pl
from jax.experimental.pallas import tpu as pltpu

F32 = jnp.float32
BF16 = jnp.bfloat16
I32 = jnp.int32

EPS = 1e-6
GRID_W = 64
ROPE_BASE = 10000.0
LANE = 128
HEAD = 64
M_HEADS = 4
A_HEADS = 8
A_ROPE = 32
N_EXPERTS = 16
EC_FACTOR = 2
TOK_TILE = 256
CHUNK = 128
GATHER_ALIGN_F32 = 8
GATHER_ALIGN_BF16 = 16
VMEM_LIMIT = 56 * 1024 * 1024


def _cparams(n_axes, vmem=None):
    return pltpu.CompilerParams(dimension_semantics=("arbitrary",) * n_axes,
                                vmem_limit_bytes=vmem)


def _dot(a, b):
    return jnp.dot(a, b, preferred_element_type=F32)


def _dot_nt(a, b):
    return lax.dot_general(a, b, (((1,), (1,)), ((), ())), preferred_element_type=F32)


def _split_bf16(a):
    hi = a.astype(BF16)
    lo = (a - hi.astype(F32)).astype(BF16)
    return hi, lo


def _sigmoid(x):
    return 1.0 / (1.0 + jnp.exp(-x))


def _rms(x, g):
    return x * lax.rsqrt(jnp.mean(x * x, axis=-1, keepdims=True) + EPS) * g


def _ada_kernel(c_ref, w_ref, b_ref, o_ref):
    a = c_ref[...]
    a = a * _sigmoid(a)
    a_hi, a_lo = _split_bf16(a)
    w_hi, w_lo = _split_bf16(w_ref[0])
    o_ref[0] = _dot(a_hi, w_hi) + _dot(a_lo, w_hi) + _dot(a_hi, w_lo) + b_ref[0]


def _ada_mods(cvec, ada_w, ada_b):
    L, D, D6 = ada_w.shape
    rows = cvec.shape[0]
    return pl.pallas_call(
        _ada_kernel,
        grid=(L, D6 // D),
        in_specs=[pl.BlockSpec((rows, D), lambda l, j: (0, 0)),
                  pl.BlockSpec((1, D, D), lambda l, j: (l, 0, j)),
                  pl.BlockSpec((1, 1, D), lambda l, j: (l, 0, j))],
        out_specs=pl.BlockSpec((1, rows, D), lambda l, j: (l, 0, j)),
        out_shape=jax.ShapeDtypeStruct((L, rows, D6), F32),
        compiler_params=_cparams(2),
        name="ada_mods",
    )(cvec, ada_w, ada_b.reshape(L, 1, D6))


_C_F = (0, 256)
_C_Q = (256, 768)
_C_K = (768, 1280)
_C_V = (1280, 1792)
_C_O = (1792, 2304)
_C_CQ = (2304, 2688)
_C_CKV = (2688, 2944)
_C_KR = (2944, 3072)
_C_G = (3072, 3200)
_N_COLS = 3200


def _inproj_kernel(x_ref, mod_ref, n1_ref, w1_ref, wgt_ref, qn_ref, kvn_ref, wqt_ref, wqst_ref,
                   wk_ref, wvt_ref, sel_ref, cos_ref, sin_ref, cost_ref, sint_ref,
                   f_ref, qk_ref, vm_ref, o_ref, gat_ref, gatt_ref, qtc_ref, qtl_ref, k_ref, vt_ref, *,
                   d_model, q_scale, n_ctx_tiles):
    D = d_model
    i = pl.program_id(1)
    x = x_ref[0]
    mod = mod_ref[0]
    sh1, sc1 = mod[:, 0:D], mod[:, D:2 * D]
    xm = _rms(x, n1_ref[...]) * (1.0 + sc1) + sh1
    xb = xm.astype(BF16)
    u = _dot(xb, w1_ref[...])
    f_ref[0] = u[:, _C_F[0]:_C_F[1]].astype(BF16)
    qk_ref[0] = u[:, _C_Q[0]:_C_K[1]].astype(BF16)
    lane512 = lax.broadcasted_iota(I32, (1, 4 * LANE), 1)
    ones_m = jnp.where(lane512 % LANE == HEAD, 1.0, 0.0)
    vm_ref[0] = (u[:, _C_V[0]:_C_V[1]] + ones_m).astype(BF16)
    o_ref[0] = u[:, _C_O[0]:_C_O[1]].astype(BF16)
    gat_ref[0] = u[:, _C_G[0]:_C_G[0] + 16]
    gatt_ref[0] = _dot_nt(wgt_ref[...], xb)

    cqn = _rms(u[:, _C_CQ[0]:_C_CQ[1]], qn_ref[...]).astype(BF16)
    cost = jnp.tile(cost_ref[...], (A_HEADS, 1))
    sint = jnp.tile(sint_ref[...], (A_HEADS, 1))
    qt = ((_dot_nt(wqt_ref[...], cqn) * cost + _dot_nt(wqst_ref[...], cqn) * sint) * q_scale).astype(BF16)

    @pl.when(i < n_ctx_tiles)
    def _():
        qtc_ref[0] = qt

    @pl.when(i >= n_ctx_tiles)
    def _():
        qtl_ref[0] = qt

    ckvn = _rms(u[:, _C_CKV[0]:_C_CKV[1]], kvn_ref[...]).astype(BF16)
    cosw = jnp.tile(cos_ref[...], (1, A_HEADS))
    sinw = jnp.tile(sin_ref[...], (1, A_HEADS))
    krw = _dot(u[:, _C_KR[0]:_C_KR[1]].astype(BF16), sel_ref[...])
    kw = A_HEADS * LANE
    k = _dot(ckvn, wk_ref[...]) + krw[:, :kw] * cosw + krw[:, kw:] * sinw
    k_ref[0] = k.astype(BF16)
    row1024 = lax.broadcasted_iota(I32, (kw, 1), 0)
    ones_a = jnp.where(row1024 % LANE == HEAD, 1.0, 0.0)
    vt_ref[0, 0] = (_dot_nt(wvt_ref[...], ckvn) + ones_a).astype(BF16)


def _inproj(xs, mods_l, n1, w1, wgt, qn, kvn, wqt, wqst, wk, wvt, sel, cos128, sin128, cost128, sint128,
            *, n_ctx_tiles):
    B, S, D = xs.shape
    nt = S // TOK_TILE
    n_ctx = n_ctx_tiles * TOK_TILE
    n_lat_tiles = nt - n_ctx_tiles
    kw = A_HEADS * LANE
    tok = lambda w: pl.BlockSpec((1, TOK_TILE, w), lambda b, i: (b, i, 0))
    full = lambda a: pl.BlockSpec(a.shape, lambda b, i: (0,) * a.ndim)
    mod_spec = pl.BlockSpec((1, 1, 6 * D), lambda b, i: (jnp.where(i < n_ctx_tiles, B, b), 0, 0))
    tab_spec = pl.BlockSpec((TOK_TILE, LANE), lambda b, i: (i, 0))
    tabt_spec = pl.BlockSpec((LANE, TOK_TILE), lambda b, i: (0, i))
    sd = lambda w, dt: jax.ShapeDtypeStruct((B, S, w), dt)
    kern = functools.partial(_inproj_kernel, d_model=D, n_ctx_tiles=n_ctx_tiles,
                             q_scale=(HEAD + A_ROPE) ** -0.5 * math.log2(math.e))
    qtc_spec = pl.BlockSpec((1, kw, TOK_TILE), lambda b, i: (b, 0, jnp.minimum(i, n_ctx_tiles - 1)))
    qtl_spec = pl.BlockSpec((1, kw, TOK_TILE),
                            lambda b, i: (b, 0, jnp.clip(i - n_ctx_tiles, 0, n_lat_tiles - 1)))
    return pl.pallas_call(
        kern,
        grid=(B, nt),
        in_specs=[tok(D), mod_spec, full(n1), full(w1), full(wgt), full(qn), full(kvn), full(wqt),
                  full(wqst), full(wk), full(wvt), full(sel), tab_spec, tab_spec, tabt_spec, tabt_spec],
        out_specs=[tok(256), tok(1024), tok(512), tok(512), tok(16),
                   pl.BlockSpec((1, 16, TOK_TILE), lambda b, i: (b, 0, i)),
                   qtc_spec, qtl_spec, tok(kw),
                   pl.BlockSpec((1, 1, kw, TOK_TILE), lambda b, i: (b, i, 0, 0))],
        out_shape=[sd(256, BF16), sd(1024, BF16), sd(512, BF16), sd(512, BF16), sd(16, F32),
                   jax.ShapeDtypeStruct((B, 16, S), F32),
                   jax.ShapeDtypeStruct((B, kw, n_ctx), BF16),
                   jax.ShapeDtypeStruct((B, kw, S - n_ctx), BF16),
                   sd(kw, BF16),
                   jax.ShapeDtypeStruct((B, nt, kw, TOK_TILE), BF16)],
        compiler_params=_cparams(2, VMEM_LIMIT),
        name="inproj",
    )(xs, mods_l, n1, w1, wgt, qn, kvn, wqt, wqst, wk, wvt, sel, cos128, sin128, cost128, sint128)


def _dft_mats(n):
    idx = jnp.arange(n, dtype=F32)
    ang = 2.0 * math.pi * jnp.mod(idx[:, None] * idx[None, :], n) / n
    return jnp.cos(ang), jnp.sin(ang)


def _fft_chan_mats():
    c, s = _dft_mats(HEAD)
    eye = jnp.eye(4, dtype=F32)
    return jnp.concatenate([jnp.kron(eye, c), jnp.kron(eye, s)], 0).astype(BF16)


def _fft_ctx_kernel(f_ref, ft_ref, cs_ref, o_ref, *, n, scale):
    xst = _dot(ft_ref[...], f_ref[0])
    xr = xst[:n].astype(BF16)
    xi = xst[n:].astype(BF16)
    y = _dot(xr, cs_ref[0:256, :]) + _dot(xi, cs_ref[256:512, :])
    o_ref[0] = (y * scale).astype(BF16)


def _fft_ctx(f, n_ctx, cs):
    B = f.shape[0]
    c, s = _dft_mats(n_ctx)
    ft = jnp.concatenate([c, -s], 0).astype(BF16)
    kern = functools.partial(_fft_ctx_kernel, n=n_ctx, scale=(n_ctx * HEAD) ** -0.5)
    return pl.pallas_call(
        kern,
        grid=(B,),
        in_specs=[pl.BlockSpec((1, n_ctx, 256), lambda b: (b, 0, 0)),
                  pl.BlockSpec(ft.shape, lambda b: (0, 0)),
                  pl.BlockSpec(cs.shape, lambda b: (0, 0))],
        out_specs=pl.BlockSpec((1, n_ctx, 256), lambda b: (b, 0, 0)),
        out_shape=jax.ShapeDtypeStruct((B, n_ctx, 256), BF16),
        compiler_params=_cparams(1),
        name="fft_ctx",
    )(f, ft, cs)


def _fft_stage1_kernel(x_ref, f1_ref, cw_ref, sw_ref, o_ref, *, n1):
    z = _dot(f1_ref[...], x_ref[0])
    zr, zi = z[:n1], z[n1:]
    cw, sw = cw_ref[...], sw_ref[...]
    o_ref[0, 0] = (zr * cw + zi * sw).astype(BF16)
    o_ref[0, 1] = (zi * cw - zr * sw).astype(BF16)


def _fft_stage2_kernel(z_ref, f2_ref, cs_ref, o_ref, *, tb, scale):
    for j in range(tb):
        zcat = jnp.concatenate([z_ref[0, 0, j], z_ref[0, 1, j]], axis=0)
        xst = _dot(f2_ref[...], zcat)
        xr = xst[:HEAD].astype(BF16)
        xi = xst[HEAD:].astype(BF16)
        y = _dot(xr, cs_ref[0:256, :]) + _dot(xi, cs_ref[256:512, :])
        o_ref[0, j] = (y * scale).astype(BF16)


def _fft_latent(f_lat, cs):
    B, T, W = f_lat.shape
    n2 = HEAD
    n1 = T // n2
    cols = n2 * W
    cb = 2048
    c1, s1 = _dft_mats(n1)
    f1 = jnp.concatenate([c1, -s1], 0).astype(BF16)
    t1 = jnp.arange(n1, dtype=F32)[:, None]
    s2 = jnp.arange(n2, dtype=F32)[None, :]
    ang = 2.0 * math.pi * (t1 * s2) / T
    cw = jnp.repeat(jnp.cos(ang), W, axis=1)
    sw = jnp.repeat(jnp.sin(ang), W, axis=1)
    z = pl.pallas_call(
        functools.partial(_fft_stage1_kernel, n1=n1),
        grid=(B, cols // cb),
        in_specs=[pl.BlockSpec((1, n1, cb), lambda b, j: (b, 0, j)),
                  pl.BlockSpec(f1.shape, lambda b, j: (0, 0)),
                  pl.BlockSpec((n1, cb), lambda b, j: (0, j)),
                  pl.BlockSpec((n1, cb), lambda b, j: (0, j))],
        out_specs=pl.BlockSpec((1, 2, n1, cb), lambda b, j: (b, 0, 0, j)),
        out_shape=jax.ShapeDtypeStruct((B, 2, n1, cols), BF16),
        compiler_params=_cparams(2),
        name="fft_stage1",
    )(f_lat.reshape(B, n1, cols), f1, cw, sw)
    z = z.reshape(B, 2, n1, n2, W)
    c2, s2m = _dft_mats(n2)
    f2 = jnp.concatenate([jnp.concatenate([c2, s2m], 1),
                          jnp.concatenate([-s2m, c2], 1)], 0).astype(BF16)
    tb = 8
    y = pl.pallas_call(
        functools.partial(_fft_stage2_kernel, tb=tb, scale=(T * HEAD) ** -0.5),
        grid=(B, n1 // tb),
        in_specs=[pl.BlockSpec((1, 2, tb, n2, W), lambda b, j: (b, 0, j, 0, 0)),
                  pl.BlockSpec(f2.shape, lambda b, j: (0, 0)),
                  pl.BlockSpec(cs.shape, lambda b, j: (0, 0))],
        out_specs=pl.BlockSpec((1, tb, n2, W), lambda b, j: (b, j, 0, 0)),
        out_shape=jax.ShapeDtypeStruct((B, n1, n2, W), BF16),
        compiler_params=_cparams(2),
        name="fft_stage2",
    )(z, f2, cs)
    return jnp.transpose(y, (0, 2, 1, 3)).reshape(B, T, W)


def _mlstm_prep_kernel(cur_ref, prev_ref, next_ref, cw_ref, cb_ref, gat_ref, gatt_ref, br_ref, bc_ref,
                       q_ref, kt_ref, col_ref, row_ref, *, n_chunks, n_ctx_chunks):
    c = pl.program_id(1)
    cur = cur_ref[0].astype(F32)
    first = jnp.logical_or(c == 0, c == n_ctx_chunks)
    last = jnp.logical_or(c == n_ctx_chunks - 1, c == n_chunks - 1)
    prev_row = prev_ref[0].astype(F32)[15:16, :]
    next_row = next_ref[0].astype(F32)[0:1, :]
    prev_row = jnp.where(first, 0.0, prev_row)
    next_row = jnp.where(last, 0.0, next_row)
    rows = lax.broadcasted_iota(I32, (CHUNK, 1), 0)
    up = jnp.where(rows == 0, prev_row, pltpu.roll(cur, 1, axis=0))
    dn = jnp.where(rows == CHUNK - 1, next_row, pltpu.roll(cur, CHUNK - 1, axis=0))
    y = cw_ref[0:1, :] * up + cw_ref[1:2, :] * cur + cw_ref[2:3, :] * dn + cb_ref[...]
    y = y * _sigmoid(y)
    hw = M_HEADS * LANE
    q_ref[0] = (y[:, :hw] * HEAD ** -0.5).astype(BF16)
    kt_ref[0] = y[:, hw:].T.astype(BF16)

    def logsig(v):
        return jnp.minimum(v, 0.0) - jnp.log(1.0 + jnp.exp(-jnp.abs(v)))

    r_i = lax.broadcasted_iota(I32, (CHUNK, CHUNK), 0)
    c_i = lax.broadcasted_iota(I32, (CHUNK, CHUNK), 1)
    lower = jnp.where(c_i <= r_i, 1.0, 0.0).astype(BF16)
    upper = jnp.where(c_i >= r_i, 1.0, 0.0).astype(BF16)

    g = gat_ref[0] + br_ref[...]
    lane = lax.broadcasted_iota(I32, (1, 16), 1)
    lg = jnp.where(lane % 8 >= 4, logsig(g), g)
    hi, lo = _split_bf16(lg)
    cum_f = _dot(lower, hi) + _dot(lower, lo)
    cum_b = _dot(upper, hi) + _dot(upper, lo)
    col_ref[0] = jnp.where(lane < 8, cum_f, cum_b)

    gt = gatt_ref[0] + bc_ref[...]
    row = lax.broadcasted_iota(I32, (16, 1), 0)
    lgt = jnp.where(row % 8 >= 4, logsig(gt), gt)
    hit, lot = _split_bf16(lgt)
    cum_ft = _dot(hit, upper) + _dot(lot, upper)
    cum_bt = _dot(hit, lower) + _dot(lot, lower)
    cumt = jnp.where(row < 8, cum_ft, cum_bt)
    rterm = lgt - pltpu.roll(cumt, 12, axis=0)
    btot = jnp.broadcast_to(jnp.sum(lgt, axis=1, keepdims=True), (16, CHUNK))
    row_ref[0] = jnp.where(row % 8 < 4, rterm, btot)


def _mlstm_prep(qk, gat, gatt, conv_w, conv_b, bias_row, bias_col, *, n_ctx_chunks):
    B, S, W = qk.shape
    nc = S // CHUNK
    n16 = S // 16
    hw = M_HEADS * LANE
    kern = functools.partial(_mlstm_prep_kernel, n_chunks=nc, n_ctx_chunks=n_ctx_chunks)
    full = lambda a: pl.BlockSpec(a.shape, lambda b, c: (0,) * a.ndim)
    return pl.pallas_call(
        kern,
        grid=(B, nc),
        in_specs=[pl.BlockSpec((1, CHUNK, W), lambda b, c: (b, c, 0)),
                  pl.BlockSpec((1, 16, W), lambda b, c: (b, jnp.maximum(c * 8 - 1, 0), 0)),
                  pl.BlockSpec((1, 16, W), lambda b, c: (b, jnp.minimum((c + 1) * 8, n16 - 1), 0)),
                  full(conv_w), full(conv_b),
                  pl.BlockSpec((1, CHUNK, 16), lambda b, c: (b, c, 0)),
                  pl.BlockSpec((1, 16, CHUNK), lambda b, c: (b, 0, c)),
                  full(bias_row), full(bias_col)],
        out_specs=[pl.BlockSpec((1, CHUNK, hw), lambda b, c: (b, c, 0)),
                   pl.BlockSpec((1, hw, CHUNK), lambda b, c: (b, 0, c)),
                   pl.BlockSpec((1, CHUNK, 16), lambda b, c: (b, c, 0)),
                   pl.BlockSpec((1, 16, CHUNK), lambda b, c: (b, 0, c))],
        out_shape=[jax.ShapeDtypeStruct((B, S, hw), BF16),
                   jax.ShapeDtypeStruct((B, hw, S), BF16),
                   jax.ShapeDtypeStruct((B, S, 16), F32),
                   jax.ShapeDtypeStruct((B, 16, S), F32)],
        compiler_params=_cparams(2),
        name="mlstm_prep",
    )(qk, qk, qk, conv_w, conv_b, gat, gatt, bias_row, bias_col)


def _mlstm_scan_kernel(q_ref, kt_ref, v_ref, col_ref, row_ref, o_ref, c_scr, m_scr):
    d = pl.program_id(1)
    j = pl.program_id(2)

    @pl.when(j == 0)
    def _():
        c_scr[...] = jnp.zeros_like(c_scr)
        m_scr[...] = jnp.zeros_like(m_scr)

    fwd = d == 0
    colp = col_ref[0]
    rowp = row_ref[0]
    t_i = lax.broadcasted_iota(I32, (CHUNK, CHUNK), 0)
    s_i = lax.broadcasted_iota(I32, (CHUNK, CHUNK), 1)
    mask = jnp.where(fwd, s_i - t_i, t_i - s_i) <= 0
    outs = []
    for h in range(M_HEADS):
        bcol = jnp.where(fwd, colp[:, 4 + h:5 + h], colp[:, 12 + h:13 + h])
        rrow = jnp.where(fwd, rowp[h:h + 1, :], rowp[8 + h:9 + h, :])
        btot = jnp.where(fwd, rowp[4 + h:5 + h, :], rowp[12 + h:13 + h, :])
        m = m_scr[h:h + 1, 0:1]
        dm = jnp.where(mask, bcol + rrow, -jnp.inf)
        inter = bcol + m
        m_t = jnp.maximum(inter, jnp.max(dm, axis=1, keepdims=True))
        w = jnp.exp(dm - m_t)
        a = jnp.exp(inter - m_t)
        qh = q_ref[0, :, h * LANE:(h + 1) * LANE]
        kth = kt_ref[0, h * LANE:(h + 1) * LANE, :]
        vh = v_ref[0, :, h * LANE:(h + 1) * LANE]
        s = _dot(qh, kth) * w
        cst = c_scr[h]
        tot = a * _dot(qh, cst.astype(BF16)) + _dot(s.astype(BF16), vh)
        den = tot[:, HEAD:HEAD + 1]
        outs.append(tot / jnp.maximum(jnp.abs(den), jnp.exp(-m_t)))
        dec = btot + rrow
        bm = btot[:, 0:1] + m
        m_new = jnp.maximum(bm, jnp.max(dec, axis=1, keepdims=True))
        ws = jnp.exp(dec - m_new)
        kw = (kth.astype(F32) * ws).astype(BF16)
        c_scr[h] = jnp.exp(bm - m_new) * cst + _dot(kw, vh)
        m_scr[h:h + 1, :] = jnp.broadcast_to(m_new, (1, LANE))
    o_ref[0, 0] = jnp.concatenate(outs, axis=1)


def _mlstm_scan(q, kt, v, colp, rowp, *, n_ctx_chunks):
    B, S, hw = q.shape
    nc = S // CHUNK

    def cid(d, j):
        rev = jnp.where(j < n_ctx_chunks, n_ctx_chunks - 1 - j, nc + n_ctx_chunks - 1 - j)
        return jnp.where(d == 0, j, rev)

    return pl.pallas_call(
        _mlstm_scan_kernel,
        grid=(B, 2, nc),
        in_specs=[pl.BlockSpec((1, CHUNK, hw), lambda b, d, j: (b, cid(d, j), 0)),
                  pl.BlockSpec((1, hw, CHUNK), lambda b, d, j: (b, 0, cid(d, j))),
                  pl.BlockSpec((1, CHUNK, hw), lambda b, d, j: (b, cid(d, j), 0)),
                  pl.BlockSpec((1, CHUNK, 16), lambda b, d, j: (b, cid(d, j), 0)),
                  pl.BlockSpec((1, 16, CHUNK), lambda b, d, j: (b, 0, cid(d, j)))],
        out_specs=pl.BlockSpec((1, 1, CHUNK, hw), lambda b, d, j: (d, b, cid(d, j), 0)),
        out_shape=jax.ShapeDtypeStruct((2, B, S, hw), F32),
        scratch_shapes=[pltpu.VMEM((M_HEADS, LANE, LANE), F32), pltpu.VMEM((8, LANE), F32)],
        compiler_params=_cparams(3),
        name="mlstm_scan",
    )(q, kt, v, colp, rowp)


def _attn_kernel(qt_ref, k_ref, vt_ref, o_ref, *, tq, n_kc, sub):
    tk = sub * TOK_TILE
    qts = [qt_ref[0, hh * LANE:(hh + 1) * LANE, :] for hh in range(2)]

    def body(c, carry):
        new = []
        for hh in range(2):
            m, acc = carry[hh]
            lanes = slice(hh * LANE, (hh + 1) * LANE)
            kc = k_ref[0, pl.ds(pl.multiple_of(c * tk, tk), tk), lanes]
            st = _dot(kc, qts[hh])
            m_new = jnp.maximum(m, jnp.max(st, axis=0, keepdims=True))
            pt = jnp.exp2(st - m_new).astype(BF16)
            pv = _dot(vt_ref[0, c * sub, lanes, :], pt[0:TOK_TILE])
            for j in range(1, sub):
                pv += _dot(vt_ref[0, c * sub + j, lanes, :], pt[j * TOK_TILE:(j + 1) * TOK_TILE])
            new.append((m_new, jnp.exp2(m - m_new) * acc + pv))
        return tuple(new)

    init = tuple((jnp.full((1, tq), -jnp.inf, F32), jnp.zeros((LANE, tq), F32)) for _ in range(2))
    res = lax.fori_loop(0, n_kc, body, init)
    o = jnp.concatenate([acc[:HEAD] / acc[HEAD:HEAD + 1] for _, acc in res], axis=0)
    o_ref[0] = o.T.astype(BF16)


def _attention(qt, k, vt4, *, n_keys, tq, sub):
    B, kw, Tq = qt.shape
    n_kb = n_keys // TOK_TILE
    kern = functools.partial(_attn_kernel, tq=tq, n_kc=n_kb // sub, sub=sub)
    return pl.pallas_call(
        kern,
        grid=(B, A_HEADS // 2, Tq // tq),
        in_specs=[pl.BlockSpec((1, 2 * LANE, tq), lambda b, hp, i: (b, hp, i)),
                  pl.BlockSpec((1, n_keys, 2 * LANE), lambda b, hp, i: (b, 0, hp)),
                  pl.BlockSpec((1, n_kb, 2 * LANE, TOK_TILE), lambda b, hp, i: (b, 0, hp, 0))],
        out_specs=pl.BlockSpec((1, tq, LANE), lambda b, hp, i: (b, i, hp)),
        out_shape=jax.ShapeDtypeStruct((B, Tq, A_HEADS * HEAD), BF16),
        compiler_params=_cparams(3, VMEM_LIMIT),
        name="attention",
    )(qt, k, vt4)


def _outproj_kernel(yac_ref, yal_ref, hf_ref, hb_ref, o_ref, ycc_ref, ycl_ref, x_ref, mod_ref, wa_ref, wb_ref,
                    wc_ref, mg_ref, blk_ref, n2_ref, rw_ref, xo_ref, hx_ref, aff_ref, *, d_model,
                    n_ctx_tiles):
    D = d_model
    i = pl.program_id(1)
    mod = mod_ref[0]
    g1, sh2, sc2 = mod[:, 2 * D:3 * D], mod[:, 3 * D:4 * D], mod[:, 4 * D:5 * D]
    ya = jnp.where(i < n_ctx_tiles, yac_ref[0], yal_ref[0])
    lane = lax.broadcasted_iota(I32, (1, M_HEADS * LANE), 1)
    h = jnp.where(lane % LANE < HEAD, hf_ref[0, 0] + hb_ref[0, 0], 0.0)
    hi, lo = _split_bf16(h * h)
    ms = _dot(hi, blk_ref[...]) + _dot(lo, blk_ref[...])
    hn = h * lax.rsqrt(ms + EPS) * mg_ref[...]
    yb = (hn * _sigmoid(o_ref[0].astype(F32))).astype(BF16)
    yc = jnp.where(i < n_ctx_tiles, ycc_ref[0], ycl_ref[0])
    mix = _dot(ya, wa_ref[...]) + _dot(yb, wb_ref[...]) + _dot(yc, wc_ref[...])
    x = x_ref[0] + g1 * mix
    xo_ref[0] = x
    hx = _rms(x, n2_ref[...]) * (1.0 + sc2) + sh2
    hx_ref[0] = hx.astype(BF16)
    h_hi, h_lo = _split_bf16(hx)
    r_hi, r_lo = _split_bf16(rw_ref[...])
    logits = _dot(h_hi, r_hi) + _dot(h_lo, r_hi) + _dot(h_hi, r_lo)
    e = jnp.exp(logits - jnp.max(logits, axis=1, keepdims=True))
    aff_ref[0] = e / jnp.sum(e, axis=1, keepdims=True)


def _outproj(ya_ctx, ya_lat, hdir, o, yc_ctx, yc_lat, xs, mods_l, wa, wb, wc, mg, blk, n2, rw, *,
             n_ctx_tiles):
    B, S, D = xs.shape
    nt = S // TOK_TILE
    tok = lambda w: pl.BlockSpec((1, TOK_TILE, w), lambda b, i: (b, i, 0))
    full = lambda a: pl.BlockSpec(a.shape, lambda b, i: (0,) * a.ndim)
    mod_spec = pl.BlockSpec((1, 1, 6 * D), lambda b, i: (jnp.where(i < n_ctx_tiles, B, b), 0, 0))
    hw = M_HEADS * LANE
    n_lat_tiles = nt - n_ctx_tiles
    ctx_tok = lambda w: pl.BlockSpec((1, TOK_TILE, w), lambda b, i: (b, jnp.minimum(i, n_ctx_tiles - 1), 0))
    lat_tok = lambda w: pl.BlockSpec(
        (1, TOK_TILE, w), lambda b, i: (b, jnp.clip(i - n_ctx_tiles, 0, n_lat_tiles - 1), 0))
    kern = functools.partial(_outproj_kernel, d_model=D, n_ctx_tiles=n_ctx_tiles)
    return pl.pallas_call(
        kern,
        grid=(B, nt),
        in_specs=[ctx_tok(256), lat_tok(256),
                  pl.BlockSpec((1, 1, TOK_TILE, hw), lambda b, i: (0, b, i, 0)),
                  pl.BlockSpec((1, 1, TOK_TILE, hw), lambda b, i: (1, b, i, 0)),
                  tok(hw), ctx_tok(A_HEADS * HEAD), lat_tok(A_HEADS * HEAD), tok(D), mod_spec,
                  full(wa), full(wb), full(wc), full(mg), full(blk), full(n2), full(rw)],
        out_specs=[tok(D), tok(D), tok(N_EXPERTS)],
        out_shape=[jax.ShapeDtypeStruct((B, S, D), F32), jax.ShapeDtypeStruct((B, S, D), BF16),
                   jax.ShapeDtypeStruct((B, S, N_EXPERTS), F32)],
        compiler_params=_cparams(2, VMEM_LIMIT),
        name="outproj",
    )(ya_ctx, ya_lat, hdir, hdir, o, yc_ctx, yc_lat, xs, mods_l, wa, wb, wc, mg, blk, n2, rw)


def _route_part(a, cap, slot0, tri, pos_ref, base_ref, lane0):
    n = a.shape[1]
    capf = float(cap)
    bits = pltpu.bitcast(a, I32)
    v = jnp.zeros((N_EXPERTS, 1), I32)
    for bit in range(30, -1, -1):
        cand = v | (1 << bit)
        cnt = jnp.sum(jnp.where(bits >= cand, 1.0, 0.0), axis=1, keepdims=True)
        v = jnp.where(cnt >= capf, cand, v)
    gt = bits > v
    eq = bits == v
    need = capf - jnp.sum(jnp.where(gt, 1.0, 0.0), axis=1, keepdims=True)
    idx = lax.broadcasted_iota(I32, (N_EXPERTS, n), 1)
    x = jnp.zeros((N_EXPERTS, 1), I32)
    for bit in range(max(n - 1, 1).bit_length() - 1, -1, -1):
        cand = x | (1 << bit)
        cnt = jnp.sum(jnp.where(eq, jnp.where(idx < cand, 1.0, 0.0), 0.0), axis=1, keepdims=True)
        x = jnp.where(cnt < need, cand, x)
    sel = jnp.where(gt, 1.0, jnp.where(eq, jnp.where(idx <= x, 1.0, 0.0), 0.0))
    running = jnp.zeros((N_EXPERTS, 1), F32)
    for c in range(n // CHUNK):
        blk = sel[:, c * CHUNK:(c + 1) * CHUNK]
        incl = _dot(blk.astype(BF16), tri)
        pos = running + incl - blk + float(slot0)
        cg = lane0 // CHUNK + c
        pos_ref[0, :, lane0 + c * CHUNK:lane0 + (c + 1) * CHUNK] = jnp.where(
            blk > 0.5, pos, -1.0).astype(I32)
        base_ref[0, :, cg:cg + 1] = (running + float(slot0)).astype(I32)
        running = running + incl[:, CHUNK - 1:CHUNK]


def _route_kernel(aff_ref, pos_ref, base_ref, *, n_ctx, cap_ctx, cap_lat):
    r_i = lax.broadcasted_iota(I32, (CHUNK, CHUNK), 0)
    c_i = lax.broadcasted_iota(I32, (CHUNK, CHUNK), 1)
    tri = jnp.where(r_i <= c_i, 1.0, 0.0).astype(BF16)
    base_ref[...] = jnp.zeros_like(base_ref)
    a = aff_ref[0]
    _route_part(a[:, :n_ctx], cap_ctx, 0, tri, pos_ref, base_ref, 0)
    _route_part(a[:, n_ctx:], cap_lat, cap_ctx, tri, pos_ref, base_ref, n_ctx)


def _route(aff_t, *, n_ctx, cap_ctx, cap_lat):
    B, E, S = aff_t.shape
    kern = functools.partial(_route_kernel, n_ctx=n_ctx, cap_ctx=cap_ctx, cap_lat=cap_lat)
    return pl.pallas_call(
        kern,
        grid=(B,),
        in_specs=[pl.BlockSpec((1, E, S), lambda b: (b, 0, 0))],
        out_specs=[pl.BlockSpec((1, E, S), lambda b: (b, 0, 0)),
                   pl.BlockSpec((1, E, LANE), lambda b: (b, 0, 0))],
        out_shape=[jax.ShapeDtypeStruct((B, E, S), I32), jax.ShapeDtypeStruct((B, E, LANE), I32)],
        compiler_params=_cparams(1),
        name="route",
    )(aff_t)


def _window_start(base, align, rows, win):
    w0 = lax.shift_left(lax.shift_right_logical(base, int(math.log2(align))), int(math.log2(align)))
    return pl.multiple_of(jnp.minimum(w0, rows - win), align)


def _gather_kernel(base_ref, h_ref, pos_ref, xe_ref, acc_ref, *, n_chunks, rows, win):
    b = pl.program_id(0)
    e = pl.program_id(2)
    acc_ref[...] = jnp.zeros_like(acc_ref)

    def chunk(c, carry):
        w0 = _window_start(base_ref[b, e, c], GATHER_ALIGN_F32, rows, win)
        posr = pos_ref[0, 0, pl.ds(c, 1), :]
        slot = lax.broadcasted_iota(I32, (win, CHUNK), 0) + w0
        onehot = jnp.where(posr == slot, 1.0, 0.0).astype(BF16)
        hc = h_ref[0, pl.ds(pl.multiple_of(c * CHUNK, CHUNK), CHUNK), :]
        acc_ref[pl.ds(w0, win), :] += _dot(onehot, hc)
        return carry

    lax.fori_loop(0, n_chunks, chunk, 0)
    xe_ref[0, 0] = acc_ref[...].astype(BF16)


def _gather(hx, pos4, bases, *, rows):
    B, S, D = hx.shape
    nc = S // CHUNK
    dh = D // 2
    win = min(CHUNK + GATHER_ALIGN_F32, rows)
    kern = functools.partial(_gather_kernel, n_chunks=nc, rows=rows, win=win)
    return pl.pallas_call(
        kern,
        grid_spec=pltpu.PrefetchScalarGridSpec(
            num_scalar_prefetch=1,
            grid=(B, 2, N_EXPERTS),
            in_specs=[pl.BlockSpec((1, S, dh), lambda b, j, e, bs: (b, 0, j)),
                      pl.BlockSpec((1, 1, nc, CHUNK), lambda b, j, e, bs: (b, e, 0, 0))],
            out_specs=pl.BlockSpec((1, 1, rows, dh), lambda b, j, e, bs: (b, e, 0, j)),
            scratch_shapes=[pltpu.VMEM((rows, dh), F32)]),
        out_shape=jax.ShapeDtypeStruct((B, N_EXPERTS, rows, D), BF16),
        compiler_params=_cparams(3, VMEM_LIMIT),
        name="moe_gather",
    )(bases, hx, pos4)


def _ffn_kernel(x_ref, wg_ref, wu_ref, wd_ref, y_ref, *, row_tile, n_tiles):
    def tile(i, carry):
        r0 = pl.multiple_of(i * row_tile, 16)
        x = x_ref[0, 0, pl.ds(r0, row_tile), :]
        g = _dot(x, wg_ref[0])
        u = _dot(x, wu_ref[0])
        hid = (g * _sigmoid(g) * u).astype(BF16)
        y_ref[0, 0, pl.ds(r0, row_tile), :] = _dot(hid, wd_ref[0]).astype(BF16)
        return carry

    lax.fori_loop(0, n_tiles, tile, 0)


def _ffn_row_tile(rows):
    for t in (352, 256, 176, 128, 96, 64, 32, 16):
        if rows % t == 0:
            return t
    raise ValueError(f"expert slot rows {rows} must be a multiple of 16")


def _ffn(xe, wg, wu, wd):
    B, E, R, D = xe.shape
    F = wg.shape[-1]
    rt = _ffn_row_tile(R)
    kern = functools.partial(_ffn_kernel, row_tile=rt, n_tiles=R // rt)
    return pl.pallas_call(
        kern,
        grid=(E, B),
        in_specs=[pl.BlockSpec((1, 1, R, D), lambda e, b: (b, e, 0, 0)),
                  pl.BlockSpec((1, D, F), lambda e, b: (e, 0, 0)),
                  pl.BlockSpec((1, D, F), lambda e, b: (e, 0, 0)),
                  pl.BlockSpec((1, F, D), lambda e, b: (e, 0, 0))],
        out_specs=pl.BlockSpec((1, 1, R, D), lambda e, b: (b, e, 0, 0)),
        out_shape=jax.ShapeDtypeStruct((B, E, R, D), BF16),
        compiler_params=_cparams(2, VMEM_LIMIT),
        name="moe_ffn",
    )(xe, wg, wu, wd)


def _combine_kernel(base_ref, x_ref, ye_ref, posc_ref, aff_ref, modc_ref, modx_ref, o_ref, *, d_model,
                    blk_chunks, n_ctx_chunks, rows, win):
    D = d_model
    b = pl.program_id(0)
    tb = pl.program_id(1)
    e = pl.program_id(2)

    @pl.when(e == 0)
    def _():
        o_ref[0] = x_ref[0]

    g2c = modc_ref[0][:, 5 * D:6 * D]
    g2x = modx_ref[0][:, 5 * D:6 * D]
    lane = lax.broadcasted_iota(I32, (1, N_EXPERTS), 1)
    for cc in range(blk_chunks):
        rs = slice(cc * CHUNK, (cc + 1) * CHUNK)
        chunk = tb * blk_chunks + cc
        g2 = jnp.where(chunk < n_ctx_chunks, g2c, g2x)
        w0 = _window_start(base_ref[b, e, chunk], GATHER_ALIGN_BF16, rows, win)
        pcol = jnp.sum(jnp.where(lane == e, posc_ref[0, rs, :].astype(F32), 0.0), axis=1, keepdims=True)
        gcol = jnp.sum(jnp.where(lane == e, aff_ref[0, rs, :], 0.0), axis=1, keepdims=True)
        slot = (lax.broadcasted_iota(I32, (CHUNK, win), 1) + w0).astype(F32)
        onehot = jnp.where(pcol == slot, 1.0, 0.0).astype(BF16)
        yw = ye_ref[0, 0, pl.ds(w0, win), :]
        o_ref[0, rs, :] += (g2 * gcol) * _dot(onehot, yw)


def _combine_blk_chunks(nc):
    for k in (11, 8, 6, 4, 3, 2, 1):
        if nc % k == 0:
            return k
    return 1


def _combine(x, ye, posc, aff, mods_l, bases, *, n_ctx):
    B, S, D = x.shape
    R = ye.shape[2]
    nc = S // CHUNK
    k = _combine_blk_chunks(nc)
    tb = k * CHUNK
    win = min(CHUNK + GATHER_ALIGN_BF16, R)
    kern = functools.partial(_combine_kernel, d_model=D, blk_chunks=k, n_ctx_chunks=n_ctx // CHUNK,
                             rows=R, win=win)
    modc_spec = pl.BlockSpec((1, 1, 6 * D), lambda b, t, e, bs: (B, 0, 0))
    modx_spec = pl.BlockSpec((1, 1, 6 * D), lambda b, t, e, bs: (b, 0, 0))
    return pl.pallas_call(
        kern,
        grid_spec=pltpu.PrefetchScalarGridSpec(
            num_scalar_prefetch=1,
            grid=(B, S // tb, N_EXPERTS),
            in_specs=[pl.BlockSpec((1, tb, D), lambda b, t, e, bs: (b, t, 0)),
                      pl.BlockSpec((1, 1, R, D), lambda b, t, e, bs: (b, e, 0, 0)),
                      pl.BlockSpec((1, tb, N_EXPERTS), lambda b, t, e, bs: (b, t, 0)),
                      pl.BlockSpec((1, tb, N_EXPERTS), lambda b, t, e, bs: (b, t, 0)),
                      modc_spec, modx_spec],
            out_specs=pl.BlockSpec((1, tb, D), lambda b, t, e, bs: (b, t, 0))),
        out_shape=jax.ShapeDtypeStruct((B, S, D), F32),
        compiler_params=_cparams(3, VMEM_LIMIT),
        name="moe_combine",
    )(bases, x, ye, posc, aff, mods_l, mods_l)


def _final_kernel(x_ref, g_ref, o_ref):
    o_ref[0] = _rms(x_ref[0], g_ref[...])


def _final_norm(xs, g, *, n_ctx):
    B, S, D = xs.shape
    T = S - n_ctx
    off = n_ctx // TOK_TILE
    return pl.pallas_call(
        _final_kernel,
        grid=(B, T // TOK_TILE),
        in_specs=[pl.BlockSpec((1, TOK_TILE, D), lambda b, i: (b, i + off, 0)),
                  pl.BlockSpec((1, D), lambda b, i: (0, 0))],
        out_specs=pl.BlockSpec((1, TOK_TILE, D), lambda b, i: (b, i, 0)),
        out_shape=jax.ShapeDtypeStruct((B, T, D), F32),
        compiler_params=_cparams(2),
        name="final_norm",
    )(xs, g)


def _pad_heads(w, n_heads):
    lead = w.shape[:-1]
    w = w.reshape(*lead, n_heads, HEAD)
    w = jnp.pad(w, [(0, 0)] * len(lead) + [(0, 0), (0, LANE - HEAD)])
    return w.reshape(*lead, n_heads * LANE)


def _rot_swap(w):
    half = A_ROPE // 2
    return jnp.concatenate([-w[..., half:], w[..., :half]], -1)


def _rope_tables(n_ctx, n_lat):
    rows = n_lat // GRID_W
    row_id = jnp.repeat(jnp.arange(rows, dtype=F32), GRID_W)
    col_id = jnp.tile(jnp.arange(GRID_W, dtype=F32), rows)
    n_freq = A_ROPE // 4
    inv = ROPE_BASE ** (-jnp.arange(n_freq, dtype=F32) / n_freq)
    ang = jnp.concatenate([row_id[:, None] * inv, col_id[:, None] * inv], -1)
    ang = jnp.concatenate([jnp.zeros((n_ctx, A_ROPE // 2), F32), ang], 0)
    S = n_ctx + n_lat
    cos, sin = jnp.cos(ang), jnp.sin(ang)
    pad = jnp.zeros((S, LANE - HEAD - A_ROPE), F32)
    cos128 = jnp.concatenate([jnp.ones((S, HEAD), F32), cos, cos, pad], -1)
    sin128 = jnp.concatenate([jnp.zeros((S, HEAD), F32), sin, sin, pad], -1)
    return cos128, sin128


def kernel(x, c, ctx, c_ctx, ada_w, ada_b, norm1_g, norm2_g, w_in, m_conv_w, m_conv_b, m_ib, m_fb, m_norm_g, a_qnorm_g, a_wq_up, a_kvnorm_g, a_wkv_up, w_out, router_w, e_w_gate, e_w_up, e_w_down, final_g):
    B, T, D = x.shape
    n_ctx = ctx.shape[1]
    L = ada_w.shape[0]
    S = n_ctx + T
    assert D == 16 * HEAD and n_ctx % TOK_TILE == 0 and T % TOK_TILE == 0 and T % (GRID_W * 8) == 0
    n_ctx_tiles = n_ctx // TOK_TILE
    n_ctx_chunks = n_ctx // CHUNK
    cap_ctx = EC_FACTOR * n_ctx // N_EXPERTS
    cap_lat = EC_FACTOR * T // N_EXPERTS
    slot_rows = cap_ctx + cap_lat

    o_f, o_qk, o_v, o_o, o_g, o_cq, o_ckv, o_kr = (0, 256, 768, 1024, 1280, 1296, 1680, 1936)
    w_kr = w_in[:, :, o_kr:o_kr + A_ROPE]
    kr3 = jnp.concatenate([w_kr, _rot_swap(w_kr), jnp.zeros((L, D, LANE - 2 * A_ROPE), F32)], -1)
    w_g = w_in[:, :, o_g:o_g + 16]
    w1 = jnp.concatenate([
        w_in[:, :, o_f:o_f + 256],
        _pad_heads(w_in[:, :, o_qk:o_qk + 256], M_HEADS),
        _pad_heads(w_in[:, :, o_qk + 256:o_qk + 512], M_HEADS),
        _pad_heads(w_in[:, :, o_v:o_v + 256], M_HEADS),
        _pad_heads(w_in[:, :, o_o:o_o + 256], M_HEADS),
        w_in[:, :, o_cq:o_cq + 384],
        w_in[:, :, o_ckv:o_ckv + 256],
        kr3,
        jnp.pad(w_g, ((0, 0), (0, 0), (0, LANE - 16))),
    ], -1).astype(BF16)
    assert w1.shape[-1] == _N_COLS
    wgt = jnp.swapaxes(w_g, 1, 2).astype(BF16)
    conv_w = jnp.concatenate([_pad_heads(m_conv_w[..., :256], M_HEADS),
                              _pad_heads(m_conv_w[..., 256:], M_HEADS)], -1)
    conv_b = jnp.concatenate([_pad_heads(m_conv_b[..., :256], M_HEADS),
                              _pad_heads(m_conv_b[..., 256:], M_HEADS)], -1)[:, None, :]
    gate_bias = jnp.stack([m_ib, m_fb], 2).reshape(L, 16)
    wq3 = a_wq_up.reshape(L, -1, A_HEADS, HEAD + A_ROPE)
    zq = jnp.zeros(wq3.shape[:-1] + (LANE - HEAD - A_ROPE,), F32)
    wq = jnp.concatenate([wq3, zq], -1).reshape(L, -1, A_HEADS * LANE)
    wqs = jnp.concatenate([jnp.zeros_like(wq3[..., :HEAD]), _rot_swap(wq3[..., HEAD:]), zq],
                          -1).reshape(L, -1, A_HEADS * LANE)
    wqt = jnp.swapaxes(wq, 1, 2).astype(BF16)
    wqst = jnp.swapaxes(wqs, 1, 2).astype(BF16)
    wkv3 = a_wkv_up.reshape(L, -1, A_HEADS, 2 * HEAD)
    zk = jnp.zeros(wkv3.shape[:-1] + (HEAD,), F32)
    wk = jnp.concatenate([wkv3[..., :HEAD], zk], -1).reshape(L, -1, A_HEADS * LANE).astype(BF16)
    wvt = jnp.swapaxes(jnp.concatenate([wkv3[..., HEAD:], zk], -1).reshape(L, -1, A_HEADS * LANE),
                       1, 2).astype(BF16)
    j32 = jnp.arange(A_ROPE)
    heads = jnp.arange(A_HEADS)
    sel = jnp.zeros((LANE, 2 * A_HEADS * LANE), F32)
    dst = (heads[None, :] * LANE + HEAD + j32[:, None]).reshape(-1)
    sel = sel.at[jnp.repeat(j32, A_HEADS), dst].set(1.0)
    sel = sel.at[jnp.repeat(j32, A_HEADS) + A_ROPE, dst + A_HEADS * LANE].set(1.0)
    sel = sel.astype(BF16)
    wa = w_out[:, 0:256].astype(BF16)
    wb = jnp.swapaxes(_pad_heads(jnp.swapaxes(w_out[:, 256:512], 1, 2), M_HEADS), 1, 2).astype(BF16)
    wc = w_out[:, 512:1024].astype(BF16)
    mg = _pad_heads(m_norm_g, M_HEADS)[:, None, :]
    lane = jnp.arange(M_HEADS * LANE)
    blk = jnp.where((lane[:, None] // LANE == lane[None, :] // LANE) & (lane[:, None] % LANE < HEAD),
                    1.0 / HEAD, 0.0).astype(BF16)
    cos128, sin128 = _rope_tables(n_ctx, T)
    cost128, sint128 = cos128.T, sin128.T
    cs = _fft_chan_mats()
    attn_sub = 3 if (S // TOK_TILE) % 3 == 0 else 1
    attn_tq = 512 if T % 512 == 0 else TOK_TILE
    wg_e = e_w_gate.astype(BF16)
    wu_e = e_w_up.astype(BF16)
    wd_e = e_w_down.astype(BF16)

    rows16 = 16
    cvec = jnp.zeros((rows16, D), F32).at[:B].set(c).at[B].set(c_ctx)
    mods = _ada_mods(cvec, ada_w, ada_b).reshape(L, rows16, 1, 6 * D)

    xs = jnp.concatenate([ctx, x], axis=1)
    for l in range(L):
        mods_l = mods[l]
        f, qk, vm, og, gat, gatt, qt_ctx, qt_lat, k, vt4 = _inproj(
            xs, mods_l, norm1_g[l][None], w1[l], wgt[l], a_qnorm_g[l][None], a_kvnorm_g[l][None],
            wqt[l], wqst[l], wk[l], wvt[l], sel, cos128, sin128, cost128, sint128,
            n_ctx_tiles=n_ctx_tiles)
        ya_ctx = _fft_ctx(f, n_ctx, cs)
        ya_lat = _fft_latent(f[:, n_ctx:], cs)
        qm, ktm, colp, rowp = _mlstm_prep(qk, gat, gatt, conv_w[l], conv_b[l], gate_bias[l][None, :],
                                          gate_bias[l][:, None], n_ctx_chunks=n_ctx_chunks)
        hdir = _mlstm_scan(qm, ktm, vm, colp, rowp, n_ctx_chunks=n_ctx_chunks)
        yc_ctx = _attention(qt_ctx, k, vt4, n_keys=n_ctx, tq=TOK_TILE, sub=1)
        yc_lat = _attention(qt_lat, k, vt4, n_keys=S, tq=attn_tq, sub=attn_sub)
        xs, hx, aff = _outproj(ya_ctx, ya_lat, hdir, og, yc_ctx, yc_lat, xs, mods_l, wa[l], wb[l], wc[l],
                               mg[l], blk, norm2_g[l][None], router_w[l], n_ctx_tiles=n_ctx_tiles)
        posm, bases = _route(jnp.swapaxes(aff, 1, 2), n_ctx=n_ctx, cap_ctx=cap_ctx, cap_lat=cap_lat)
        xe = _gather(hx, posm.reshape(B, N_EXPERTS, S // CHUNK, CHUNK), bases, rows=slot_rows)
        ye = _ffn(xe, wg_e[l], wu_e[l], wd_e[l])
        xs = _combine(xs, ye, jnp.swapaxes(posm, 1, 2), aff, mods_l, bases, n_ctx=n_ctx)
    return _final_norm(xs, final_g[None], n_ctx=n_ctx)
```

```python
import functools
import math

import jax
import jax.numpy as jnp
from jax import lax
from jax.experimental import pallas as pl
from jax.experimental.pallas import tpu as pltpu

F32 = jnp.float32
BF16 = jnp.bfloat16
I32 = jnp.int32

EPS = 1e-6
GRID_W = 64
ROPE_BASE = 10000.0
LANE = 128
HEAD = 64
M_HEADS = 4
A_HEADS = 8
A_ROPE = 32
N_EXPERTS = 16
EC_FACTOR = 2
TOK_TILE = 256
CHUNK = 128
GATHER_ALIGN_F32 = 8
GATHER_ALIGN_BF16 = 16
VMEM_LIMIT = 56 * 1024 * 1024


def _cparams(n_axes, vmem=None):
    return pltpu.CompilerParams(dimension_semantics=("arbitrary",) * n_axes,
                                vmem_limit_bytes=vmem)


def _dot(a, b):
    return jnp.dot(a, b, preferred_element_type=F32)


def _dot_nt(a, b):
    return lax.dot_general(a, b, (((1,), (1,)), ((), ())), preferred_element_type=F32)


def _split_bf16(a):
    hi = a.astype(BF16)
    lo = (a - hi.astype(F32)).astype(BF16)
    return hi, lo


def _sigmoid(x):
    return 1.0 / (1.0 + jnp.exp(-x))


def _rms(x, g):
    return x * lax.rsqrt(jnp.mean(x * x, axis=-1, keepdims=True) + EPS) * g


def _ada_kernel(c_ref, w_ref, b_ref, o_ref):
    a = c_ref[...]
    a = a * _sigmoid(a)
    a_hi, a_lo = _split_bf16(a)
    w_hi, w_lo = _split_bf16(w_ref[0])
    o_ref[0] = _dot(a_hi, w_hi) + _dot(a_lo, w_hi) + _dot(a_hi, w_lo) + b_ref[0]


def _ada_mods(cvec, ada_w, ada_b):
    L, D, D6 = ada_w.shape
    rows = cvec.shape[0]
    return pl.pallas_call(
        _ada_kernel,
        grid=(L, D6 // D),
        in_specs=[pl.BlockSpec((rows, D), lambda l, j: (0, 0)),
                  pl.BlockSpec((1, D, D), lambda l, j: (l, 0, j)),
                  pl.BlockSpec((1, 1, D), lambda l, j: (l, 0, j))],
        out_specs=pl.BlockSpec((1, rows, D), lambda l, j: (l, 0, j)),
        out_shape=jax.ShapeDtypeStruct((L, rows, D6), F32),
        compiler_params=_cparams(2),
        name="ada_mods",
    )(cvec, ada_w, ada_b.reshape(L, 1, D6))


_C_F = (0, 256)
_C_Q = (256, 768)
_C_K = (768, 1280)
_C_V = (1280, 1792)
_C_O = (1792, 2304)
_C_CQ = (2304, 2688)
_C_CKV = (2688, 2944)
_C_KR = (2944, 3072)
_C_KRS = (3072, 3200)
_C_G = (3200, 3328)
_N_COLS = 3328


def _inproj_kernel(x_ref, mod_ref, n1_ref, w1_ref, wgt_ref, qn_ref, kvn_ref, wqt_ref,
                   wk_ref, wvt_ref, cos_ref, sin_ref, cost_ref, sint_ref,
                   f_ref, qk_ref, vm_ref, o_ref, gat_ref, gatt_ref, qtc_ref, qtl_ref, k_ref, vt_ref, *,
                   d_model, q_scale, n_ctx_tiles):
    D = d_model
    i = pl.program_id(1)
    x = x_ref[0]
    mod = mod_ref[0]
    sh1, sc1 = mod[:, 0:D], mod[:, D:2 * D]
    xm = _rms(x, n1_ref[...]) * (1.0 + sc1) + sh1
    xb = xm.astype(BF16)
    u = _dot(xb, w1_ref[...])
    f_ref[0] = u[:, _C_F[0]:_C_F[1]].astype(BF16)
    qk_ref[0] = u[:, _C_Q[0]:_C_K[1]].astype(BF16)
    lane512 = lax.broadcasted_iota(I32, (1, 4 * LANE), 1)
    ones_m = jnp.where(lane512 % LANE == HEAD, 1.0, 0.0)
    vm_ref[0] = (u[:, _C_V[0]:_C_V[1]] + ones_m).astype(BF16)
    o_ref[0] = u[:, _C_O[0]:_C_O[1]].astype(BF16)
    gat_ref[0] = u[:, _C_G[0]:_C_G[0] + 16]
    gatt_ref[0] = _dot_nt(wgt_ref[...], xb)

    cqn = _rms(u[:, _C_CQ[0]:_C_CQ[1]], qn_ref[...]).astype(BF16)
    cost = jnp.tile(cost_ref[...], (A_HEADS, 1))
    sint = jnp.tile(sint_ref[...], (A_HEADS, 1))
    qa = _dot_nt(wqt_ref[...], cqn)
    half = A_ROPE // 2
    parts = []
    for h in range(A_HEADS):
        r0 = h * LANE
        parts += [qa[r0:r0 + HEAD], qa[r0 + HEAD + half:r0 + HEAD + A_ROPE],
                  qa[r0 + HEAD:r0 + HEAD + half], qa[r0 + HEAD + A_ROPE:r0 + LANE]]
    qt = ((qa * cost + jnp.concatenate(parts, axis=0) * sint) * q_scale).astype(BF16)

    @pl.when(i < n_ctx_tiles)
    def _():
        qtc_ref[0] = qt

    @pl.when(i >= n_ctx_tiles)
    def _():
        qtl_ref[0] = qt

    ckvn = _rms(u[:, _C_CKV[0]:_C_CKV[1]], kvn_ref[...]).astype(BF16)
    cosw = jnp.tile(cos_ref[...], (1, A_HEADS))
    sinw = jnp.tile(sin_ref[...], (1, A_HEADS))
    kw = A_HEADS * LANE
    kr = jnp.tile(u[:, _C_KR[0]:_C_KR[1]], (1, A_HEADS))
    krs = jnp.tile(u[:, _C_KRS[0]:_C_KRS[1]], (1, A_HEADS))
    k = _dot(ckvn, wk_ref[...]) + kr * cosw + krs * sinw
    k_ref[0] = k.astype(BF16)
    row1024 = lax.broadcasted_iota(I32, (kw, 1), 0)
    ones_a = jnp.where(row1024 % LANE == HEAD, 1.0, 0.0)
    vt_ref[0, 0] = (_dot_nt(wvt_ref[...], ckvn) + ones_a).astype(BF16)


def _inproj(xs, mods_l, n1, w1, wgt, qn, kvn, wqt, wk, wvt, cos128, sin128, cost128, sint128,
            *, n_ctx_tiles):
    B, S, D = xs.shape
    nt = S // TOK_TILE
    n_ctx = n_ctx_tiles * TOK_TILE
    n_lat_tiles = nt - n_ctx_tiles
    kw = A_HEADS * LANE
    tok = lambda w: pl.BlockSpec((1, TOK_TILE, w), lambda b, i: (b, i, 0))
    full = lambda a: pl.BlockSpec(a.shape, lambda b, i: (0,) * a.ndim)
    mod_spec = pl.BlockSpec((1, 1, 6 * D), lambda b, i: (jnp.where(i < n_ctx_tiles, B, b), 0, 0))
    tab_spec = pl.BlockSpec((TOK_TILE, LANE), lambda b, i: (i, 0))
    tabt_spec = pl.BlockSpec((LANE, TOK_TILE), lambda b, i: (0, i))
    sd = lambda w, dt: jax.ShapeDtypeStruct((B, S, w), dt)
    kern = functools.partial(_inproj_kernel, d_model=D, n_ctx_tiles=n_ctx_tiles,
                             q_scale=(HEAD + A_ROPE) ** -0.5 * math.log2(math.e))
    qtc_spec = pl.BlockSpec((1, kw, TOK_TILE), lambda b, i: (b, 0, jnp.minimum(i, n_ctx_tiles - 1)))
    qtl_spec = pl.BlockSpec((1, kw, TOK_TILE),
                            lambda b, i: (b, 0, jnp.clip(i - n_ctx_tiles, 0, n_lat_tiles - 1)))
    return pl.pallas_call(
        kern,
        grid=(B, nt),
        in_specs=[tok(D), mod_spec, full(n1), full(w1), full(wgt), full(qn), full(kvn), full(wqt),
                  full(wk), full(wvt), tab_spec, tab_spec, tabt_spec, tabt_spec],
        out_specs=[tok(256), tok(1024), tok(512), tok(512), tok(16),
                   pl.BlockSpec((1, 16, TOK_TILE), lambda b, i: (b, 0, i)),
                   qtc_spec, qtl_spec, tok(kw),
                   pl.BlockSpec((1, 1, kw, TOK_TILE), lambda b, i: (b, i, 0, 0))],
        out_shape=[sd(256, BF16), sd(1024, BF16), sd(512, BF16), sd(512, BF16), sd(16, F32),
                   jax.ShapeDtypeStruct((B, 16, S), F32),
                   jax.ShapeDtypeStruct((B, kw, n_ctx), BF16),
                   jax.ShapeDtypeStruct((B, kw, S - n_ctx), BF16),
                   sd(kw, BF16),
                   jax.ShapeDtypeStruct((B, nt, kw, TOK_TILE), BF16)],
        compiler_params=_cparams(2, VMEM_LIMIT),
        name="inproj",
    )(xs, mods_l, n1, w1, wgt, qn, kvn, wqt, wk, wvt, cos128, sin128, cost128, sint128)


def _dft_mats(n):
    idx = jnp.arange(n, dtype=F32)
    ang = 2.0 * math.pi * jnp.mod(idx[:, None] * idx[None, :], n) / n
    return jnp.cos(ang), jnp.sin(ang)


def _fft_chan_mats():
    c, s = _dft_mats(HEAD)
    eye = jnp.eye(4, dtype=F32)
    return jnp.concatenate([jnp.kron(eye, c), jnp.kron(eye, s)], 0).astype(BF16)


def _fft_ctx_kernel(f_ref, ft_ref, cs_ref, o_ref, *, n, scale):
    xst = _dot(ft_ref[...], f_ref[0])
    xr = xst[:n].astype(BF16)
    xi = xst[n:].astype(BF16)
    y = _dot(xr, cs_ref[0:256, :]) + _dot(xi, cs_ref[256:512, :])
    o_ref[0] = (y * scale).astype(BF16)


def _fft_ctx(f, n_ctx, cs):
    B = f.shape[0]
    c, s = _dft_mats(n_ctx)
    ft = jnp.concatenate([c, -s], 0).astype(BF16)
    kern = functools.partial(_fft_ctx_kernel, n=n_ctx, scale=(n_ctx * HEAD) ** -0.5)
    return pl.pallas_call(
        kern,
        grid=(B,),
        in_specs=[pl.BlockSpec((1, n_ctx, 256), lambda b: (b, 0, 0)),
                  pl.BlockSpec(ft.shape, lambda b: (0, 0)),
                  pl.BlockSpec(cs.shape, lambda b: (0, 0))],
        out_specs=pl.BlockSpec((1, n_ctx, 256), lambda b: (b, 0, 0)),
        out_shape=jax.ShapeDtypeStruct((B, n_ctx, 256), BF16),
        compiler_params=_cparams(1),
        name="fft_ctx",
    )(f, ft, cs)


def _fft_stage1_kernel(x_ref, f1_ref, cw_ref, sw_ref, o_ref, *, n1):
    z = _dot(f1_ref[...], x_ref[0])
    zr, zi = z[:n1], z[n1:]
    cw, sw = cw_ref[...], sw_ref[...]
    o_ref[0, 0] = (zr * cw + zi * sw).astype(BF16)
    o_ref[0, 1] = (zi * cw - zr * sw).astype(BF16)


def _fft_stage2_kernel(z_ref, f2_ref, cs_ref, o_ref, *, tb, scale):
    for j in range(tb):
        zcat = jnp.concatenate([z_ref[0, 0, j], z_ref[0, 1, j]], axis=0)
        xst = _dot(f2_ref[...], zcat)
        xr = xst[:HEAD].astype(BF16)
        xi = xst[HEAD:].astype(BF16)
        y = _dot(xr, cs_ref[0:256, :]) + _dot(xi, cs_ref[256:512, :])
        o_ref[0, j] = (y * scale).astype(BF16)


def _fft_latent(f_lat, cs):
    B, T, W = f_lat.shape
    n2 = HEAD
    n1 = T // n2
    cols = n2 * W
    cb = 2048
    c1, s1 = _dft_mats(n1)
    f1 = jnp.concatenate([c1, -s1], 0).astype(BF16)
    t1 = jnp.arange(n1, dtype=F32)[:, None]
    s2 = jnp.arange(n2, dtype=F32)[None, :]
    ang = 2.0 * math.pi * (t1 * s2) / T
    cw = jnp.repeat(jnp.cos(ang), W, axis=1)
    sw = jnp.repeat(jnp.sin(ang), W, axis=1)
    z = pl.pallas_call(
        functools.partial(_fft_stage1_kernel, n1=n1),
        grid=(B, cols // cb),
        in_specs=[pl.BlockSpec((1, n1, cb), lambda b, j: (b, 0, j)),
                  pl.BlockSpec(f1.shape, lambda b, j: (0, 0)),
                  pl.BlockSpec((n1, cb), lambda b, j: (0, j)),
                  pl.BlockSpec((n1, cb), lambda b, j: (0, j))],
        out_specs=pl.BlockSpec((1, 2, n1, cb), lambda b, j: (b, 0, 0, j)),
        out_shape=jax.ShapeDtypeStruct((B, 2, n1, cols), BF16),
        compiler_params=_cparams(2),
        name="fft_stage1",
    )(f_lat.reshape(B, n1, cols), f1, cw, sw)
    z = z.reshape(B, 2, n1, n2, W)
    c2, s2m = _dft_mats(n2)
    f2 = jnp.concatenate([jnp.concatenate([c2, s2m], 1),
                          jnp.concatenate([-s2m, c2], 1)], 0).astype(BF16)
    tb = 8
    y = pl.pallas_call(
        functools.partial(_fft_stage2_kernel, tb=tb, scale=(T * HEAD) ** -0.5),
        grid=(B, n1 // tb),
        in_specs=[pl.BlockSpec((1, 2, tb, n2, W), lambda b, j: (b, 0, j, 0, 0)),
                  pl.BlockSpec(f2.shape, lambda b, j: (0, 0)),
                  pl.BlockSpec(cs.shape, lambda b, j: (0, 0))],
        out_specs=pl.BlockSpec((1, tb, n2, W), lambda b, j: (b, j, 0, 0)),
        out_shape=jax.ShapeDtypeStruct((B, n1, n2, W), BF16),
        compiler_params=_cparams(2),
        name="fft_stage2",
    )(z, f2, cs)
    return jnp.transpose(y, (0, 2, 1, 3)).reshape(B, T, W)


def _mlstm_prep_kernel(cur_ref, prev_ref, next_ref, cw_ref, cb_ref, gat_ref, gatt_ref, br_ref, bc_ref,
                       q_ref, kt_ref, col_ref, row_ref, *, n_chunks, n_ctx_chunks):
    c = pl.program_id(1)
    cur = cur_ref[0].astype(F32)
    first = jnp.logical_or(c == 0, c == n_ctx_chunks)
    last = jnp.logical_or(c == n_ctx_chunks - 1, c == n_chunks - 1)
    prev_row = prev_ref[0].astype(F32)[15:16, :]
    next_row = next_ref[0].astype(F32)[0:1, :]
    prev_row = jnp.where(first, 0.0, prev_row)
    next_row = jnp.where(last, 0.0, next_row)
    rows = lax.broadcasted_iota(I32, (CHUNK, 1), 0)
    up = jnp.where(rows == 0, prev_row, pltpu.roll(cur, 1, axis=0))
    dn = jnp.where(rows == CHUNK - 1, next_row, pltpu.roll(cur, CHUNK - 1, axis=0))
    y = cw_ref[0:1, :] * up + cw_ref[1:2, :] * cur + cw_ref[2:3, :] * dn + cb_ref[...]
    y = y * _sigmoid(y)
    hw = M_HEADS * LANE
    q_ref[0] = (y[:, :hw] * HEAD ** -0.5).astype(BF16)
    kt_ref[0] = y[:, hw:].T.astype(BF16)

    def logsig(v):
        return jnp.minimum(v, 0.0) - jnp.log(1.0 + jnp.exp(-jnp.abs(v)))

    r_i = lax.broadcasted_iota(I32, (CHUNK, CHUNK), 0)
    c_i = lax.broadcasted_iota(I32, (CHUNK, CHUNK), 1)
    lower = jnp.where(c_i <= r_i, 1.0, 0.0).astype(BF16)
    upper = jnp.where(c_i >= r_i, 1.0, 0.0).astype(BF16)

    g = gat_ref[0] + br_ref[...]
    lane = lax.broadcasted_iota(I32, (1, 16), 1)
    lg = jnp.where(lane % 8 >= 4, logsig(g), g)
    hi, lo = _split_bf16(lg)
    cum_f = _dot(lower, hi) + _dot(lower, lo)
    cum_b = _dot(upper, hi) + _dot(upper, lo)
    col_ref[0] = jnp.where(lane < 8, cum_f, cum_b)

    gt = gatt_ref[0] + bc_ref[...]
    row = lax.broadcasted_iota(I32, (16, 1), 0)
    lgt = jnp.where(row % 8 >= 4, logsig(gt), gt)
    hit, lot = _split_bf16(lgt)
    cum_ft = _dot(hit, upper) + _dot(lot, upper)
    cum_bt = _dot(hit, lower) + _dot(lot, lower)
    cumt = jnp.where(row < 8, cum_ft, cum_bt)
    rterm = lgt - pltpu.roll(cumt, 12, axis=0)
    btot = jnp.broadcast_to(jnp.sum(lgt, axis=1, keepdims=True), (16, CHUNK))
    row_ref[0] = jnp.where(row % 8 < 4, rterm, btot)


def _mlstm_prep(qk, gat, gatt, conv_w, conv_b, bias_row, bias_col, *, n_ctx_chunks):
    B, S, W = qk.shape
    nc = S // CHUNK
    n16 = S // 16
    hw = M_HEADS * LANE
    kern = functools.partial(_mlstm_prep_kernel, n_chunks=nc, n_ctx_chunks=n_ctx_chunks)
    full = lambda a: pl.BlockSpec(a.shape, lambda b, c: (0,) * a.ndim)
    return pl.pallas_call(
        kern,
        grid=(B, nc),
        in_specs=[pl.BlockSpec((1, CHUNK, W), lambda b, c: (b, c, 0)),
                  pl.BlockSpec((1, 16, W), lambda b, c: (b, jnp.maximum(c * 8 - 1, 0), 0)),
                  pl.BlockSpec((1, 16, W), lambda b, c: (b, jnp.minimum((c + 1) * 8, n16 - 1), 0)),
                  full(conv_w), full(conv_b),
                  pl.BlockSpec((1, CHUNK, 16), lambda b, c: (b, c, 0)),
                  pl.BlockSpec((1, 16, CHUNK), lambda b, c: (b, 0, c)),
                  full(bias_row), full(bias_col)],
        out_specs=[pl.BlockSpec((1, CHUNK, hw), lambda b, c: (b, c, 0)),
                   pl.BlockSpec((1, hw, CHUNK), lambda b, c: (b, 0, c)),
                   pl.BlockSpec((1, CHUNK, 16), lambda b, c: (b, c, 0)),
                   pl.BlockSpec((1, 16, CHUNK), lambda b, c: (b, 0, c))],
        out_shape=[jax.ShapeDtypeStruct((B, S, hw), BF16),
                   jax.ShapeDtypeStruct((B, hw, S), BF16),
                   jax.ShapeDtypeStruct((B, S, 16), F32),
                   jax.ShapeDtypeStruct((B, 16, S), F32)],
        compiler_params=_cparams(2),
        name="mlstm_prep",
    )(qk, qk, qk, conv_w, conv_b, gat, gatt, bias_row, bias_col)


def _mlstm_scan_kernel(*refs):
    ins, (of_ref, ob_ref, c_scr, m_scr) = refs[:10], refs[10:]
    j = pl.program_id(1)

    @pl.when(j == 0)
    def _():
        c_scr[...] = jnp.zeros_like(c_scr)
        m_scr[...] = jnp.zeros_like(m_scr)

    t_i = lax.broadcasted_iota(I32, (CHUNK, CHUNK), 0)
    s_i = lax.broadcasted_iota(I32, (CHUNK, CHUNK), 1)
    for d, o_ref in enumerate((of_ref, ob_ref)):
        q_ref, kt_ref, v_ref, col_ref, row_ref = ins[5 * d:5 * d + 5]
        colp = col_ref[0]
        rowp = row_ref[0]
        mask = (s_i <= t_i) if d == 0 else (s_i >= t_i)
        outs = []
        for h in range(M_HEADS):
            st = d * M_HEADS + h
            bcol = colp[:, 8 * d + 4 + h:8 * d + 5 + h]
            rrow = rowp[8 * d + h:8 * d + h + 1, :]
            btot = rowp[8 * d + 4 + h:8 * d + 5 + h, :]
            m = m_scr[st:st + 1, 0:1]
            dm = jnp.where(mask, bcol + rrow, -jnp.inf)
            inter = bcol + m
            m_t = jnp.maximum(inter, jnp.max(dm, axis=1, keepdims=True))
            w = jnp.exp(dm - m_t)
            a = jnp.exp(inter - m_t)
            qh = q_ref[0, :, h * LANE:(h + 1) * LANE]
            kth = kt_ref[0, h * LANE:(h + 1) * LANE, :]
            vh = v_ref[0, :, h * LANE:(h + 1) * LANE]
            s = _dot(qh, kth) * w
            cst = c_scr[st]
            tot = a * _dot(qh, cst.astype(BF16)) + _dot(s.astype(BF16), vh)
            den = tot[:, HEAD:HEAD + 1]
            outs.append(tot / jnp.maximum(jnp.abs(den), jnp.exp(-m_t)))
            dec = btot + rrow
            bm = btot[:, 0:1] + m
            m_new = jnp.maximum(bm, jnp.max(dec, axis=1, keepdims=True))
            ws = jnp.exp(dec - m_new)
            kw = (kth.astype(F32) * ws).astype(BF16)
            c_scr[st] = jnp.exp(bm - m_new) * cst + _dot(kw, vh)
            m_scr[st:st + 1, :] = jnp.broadcast_to(m_new, (1, LANE))
        o_ref[0] = jnp.concatenate(outs, axis=1)


def _mlstm_scan(q, kt, v, colp, rowp, *, n_ctx_chunks):
    B, S, hw = q.shape
    nc = S // CHUNK

    def rev(j):
        return jnp.where(j < n_ctx_chunks, n_ctx_chunks - 1 - j, nc + n_ctx_chunks - 1 - j)

    def specs(cid):
        return [pl.BlockSpec((1, CHUNK, hw), lambda b, j: (b, cid(j), 0)),
                pl.BlockSpec((1, hw, CHUNK), lambda b, j: (b, 0, cid(j))),
                pl.BlockSpec((1, CHUNK, hw), lambda b, j: (b, cid(j), 0)),
                pl.BlockSpec((1, CHUNK, 16), lambda b, j: (b, cid(j), 0)),
                pl.BlockSpec((1, 16, CHUNK), lambda b, j: (b, 0, cid(j)))]

    fwd = lambda j: j
    return pl.pallas_call(
        _mlstm_scan_kernel,
        grid=(B, nc),
        in_specs=specs(fwd) + specs(rev),
        out_specs=[pl.BlockSpec((1, CHUNK, hw), lambda b, j: (b, j, 0)),
                   pl.BlockSpec((1, CHUNK, hw), lambda b, j: (b, rev(j), 0))],
        out_shape=[jax.ShapeDtypeStruct((B, S, hw), F32), jax.ShapeDtypeStruct((B, S, hw), F32)],
        scratch_shapes=[pltpu.VMEM((2 * M_HEADS, LANE, LANE), F32), pltpu.VMEM((8, LANE), F32)],
        compiler_params=_cparams(2),
        name="mlstm_scan",
    )(q, kt, v, colp, rowp, q, kt, v, colp, rowp)


def _attn_kernel(qt_ref, k_ref, vt_ref, o_ref, sa_scr, sb_scr, acc_scr, m_scr, *, n_kc, sub):
    tk = sub * TOK_TILE
    heads = [slice(hh * LANE, (hh + 1) * LANE) for hh in range(2)]

    def scores(c, dst):
        for hh in range(2):
            kc = k_ref[0, pl.ds(pl.multiple_of(c * tk, tk), tk), heads[hh]]
            dst[hh] = _dot(kc, qt_ref[0, heads[hh], :])

    def consume(c, src):
        for hh in range(2):
            st = src[hh]
            m = m_scr[hh]
            m_new = jnp.maximum(m, jnp.max(st, axis=0, keepdims=True))
            pt = jnp.exp2(st - m_new).astype(BF16)
            pv = _dot(vt_ref[0, c * sub, heads[hh], :], pt[0:TOK_TILE])
            for j in range(1, sub):
                pv += _dot(vt_ref[0, c * sub + j, heads[hh], :], pt[j * TOK_TILE:(j + 1) * TOK_TILE])
            acc_scr[hh] = jnp.exp2(m - m_new) * acc_scr[hh] + pv
            m_scr[hh] = m_new

    m_scr[...] = jnp.full(m_scr.shape, -jnp.inf, F32)
    acc_scr[...] = jnp.zeros_like(acc_scr)
    scores(0, sa_scr)

    def pair(i, carry):
        scores(2 * i + 1, sb_scr)
        consume(2 * i, sa_scr)
        scores(2 * i + 2, sa_scr)
        consume(2 * i + 1, sb_scr)
        return carry

    lax.fori_loop(0, (n_kc - 1) // 2, pair, 0)
    if n_kc % 2 == 0:
        scores(n_kc - 1, sb_scr)
        consume(n_kc - 2, sa_scr)
        consume(n_kc - 1, sb_scr)
    else:
        consume(n_kc - 1, sa_scr)
    o = jnp.concatenate([acc_scr[hh, :HEAD] / acc_scr[hh, HEAD:HEAD + 1] for hh in range(2)], axis=0)
    o_ref[0] = o.T.astype(BF16)


def _attention(qt, k, vt4, *, n_keys, tq, sub):
    B, kw, Tq = qt.shape
    n_kb = n_keys // TOK_TILE
    tk = sub * TOK_TILE
    kern = functools.partial(_attn_kernel, n_kc=n_kb // sub, sub=sub)
    return pl.pallas_call(
        kern,
        grid=(B, A_HEADS // 2, Tq // tq),
        in_specs=[pl.BlockSpec((1, 2 * LANE, tq), lambda b, hp, i: (b, hp, i)),
                  pl.BlockSpec((1, n_keys, 2 * LANE), lambda b, hp, i: (b, 0, hp)),
                  pl.BlockSpec((1, n_kb, 2 * LANE, TOK_TILE), lambda b, hp, i: (b, 0, hp, 0))],
        out_specs=pl.BlockSpec((1, tq, LANE), lambda b, hp, i: (b, i, hp)),
        out_shape=jax.ShapeDtypeStruct((B, Tq, A_HEADS * HEAD), BF16),
        scratch_shapes=[pltpu.VMEM((2, tk, tq), F32), pltpu.VMEM((2, tk, tq), F32),
                        pltpu.VMEM((2, LANE, tq), F32), pltpu.VMEM((2, 1, tq), F32)],
        compiler_params=_cparams(3, VMEM_LIMIT),
        name="attention",
    )(qt, k, vt4)


def _outproj_kernel(yac_ref, yal_ref, hf_ref, hb_ref, o_ref, ycc_ref, ycl_ref, x_ref, mod_ref, wa_ref, wb_ref,
                    wc_ref, mg_ref, blk_ref, n2_ref, rw_ref, xo_ref, hx_ref, aff_ref, *, d_model,
                    n_ctx_tiles):
    D = d_model
    i = pl.program_id(1)
    mod = mod_ref[0]
    g1, sh2, sc2 = mod[:, 2 * D:3 * D], mod[:, 3 * D:4 * D], mod[:, 4 * D:5 * D]
    ya = jnp.where(i < n_ctx_tiles, yac_ref[0], yal_ref[0])
    lane = lax.broadcasted_iota(I32, (1, M_HEADS * LANE), 1)
    h = jnp.where(lane % LANE < HEAD, hf_ref[0] + hb_ref[0], 0.0)
    hi, lo = _split_bf16(h * h)
    ms = _dot(hi, blk_ref[...]) + _dot(lo, blk_ref[...])
    hn = h * lax.rsqrt(ms + EPS) * mg_ref[...]
    yb = (hn * _sigmoid(o_ref[0].astype(F32))).astype(BF16)
    yc = jnp.where(i < n_ctx_tiles, ycc_ref[0], ycl_ref[0])
    mix = _dot(ya, wa_ref[...]) + _dot(yb, wb_ref[...]) + _dot(yc, wc_ref[...])
    x = x_ref[0] + g1 * mix
    xo_ref[0] = x
    hx = _rms(x, n2_ref[...]) * (1.0 + sc2) + sh2
    hx_ref[0] = hx.astype(BF16)
    h_hi, h_lo = _split_bf16(hx)
    r_hi, r_lo = _split_bf16(rw_ref[...])
    logits = _dot(h_hi, r_hi) + _dot(h_lo, r_hi) + _dot(h_hi, r_lo)
    e = jnp.exp(logits - jnp.max(logits, axis=1, keepdims=True))
    aff_ref[0] = e / jnp.sum(e, axis=1, keepdims=True)


def _outproj(ya_ctx, ya_lat, hf, hb, o, yc_ctx, yc_lat, xs, mods_l, wa, wb, wc, mg, blk, n2, rw, *,
             n_ctx_tiles):
    B, S, D = xs.shape
    nt = S // TOK_TILE
    tok = lambda w: pl.BlockSpec((1, TOK_TILE, w), lambda b, i: (b, i, 0))
    full = lambda a: pl.BlockSpec(a.shape, lambda b, i: (0,) * a.ndim)
    mod_spec = pl.BlockSpec((1, 1, 6 * D), lambda b, i: (jnp.where(i < n_ctx_tiles, B, b), 0, 0))
    hw = M_HEADS * LANE
    n_lat_tiles = nt - n_ctx_tiles
    ctx_tok = lambda w: pl.BlockSpec((1, TOK_TILE, w), lambda b, i: (b, jnp.minimum(i, n_ctx_tiles - 1), 0))
    lat_tok = lambda w: pl.BlockSpec(
        (1, TOK_TILE, w), lambda b, i: (b, jnp.clip(i - n_ctx_tiles, 0, n_lat_tiles - 1), 0))
    kern = functools.partial(_outproj_kernel, d_model=D, n_ctx_tiles=n_ctx_tiles)
    return pl.pallas_call(
        kern,
        grid=(B, nt),
        in_specs=[ctx_tok(256), lat_tok(256),
                  tok(hw), tok(hw),
                  tok(hw), ctx_tok(A_HEADS * HEAD), lat_tok(A_HEADS * HEAD), tok(D), mod_spec,
                  full(wa), full(wb), full(wc), full(mg), full(blk), full(n2), full(rw)],
        out_specs=[tok(D), tok(D), tok(N_EXPERTS)],
        out_shape=[jax.ShapeDtypeStruct((B, S, D), F32), jax.ShapeDtypeStruct((B, S, D), BF16),
                   jax.ShapeDtypeStruct((B, S, N_EXPERTS), F32)],
        compiler_params=_cparams(2, VMEM_LIMIT),
        name="outproj",
    )(ya_ctx, ya_lat, hf, hb, o, yc_ctx, yc_lat, xs, mods_l, wa, wb, wc, mg, blk, n2, rw)


def _route_part(a, cap, slot0, tri, pos_ref, base_ref, lane0):
    n = a.shape[1]
    capf = float(cap)
    bits = pltpu.bitcast(a, I32)
    v = jnp.zeros((N_EXPERTS, 1), I32)
    for bit in range(30, -1, -1):
        cand = v | (1 << bit)
        cnt = jnp.sum(jnp.where(bits >= cand, 1.0, 0.0), axis=1, keepdims=True)
        v = jnp.where(cnt >= capf, cand, v)
    gt = bits > v
    eq = bits == v
    need = capf - jnp.sum(jnp.where(gt, 1.0, 0.0), axis=1, keepdims=True)
    idx = lax.broadcasted_iota(I32, (N_EXPERTS, n), 1)
    x = jnp.zeros((N_EXPERTS, 1), I32)
    for bit in range(max(n - 1, 1).bit_length() - 1, -1, -1):
        cand = x | (1 << bit)
        cnt = jnp.sum(jnp.where(eq, jnp.where(idx < cand, 1.0, 0.0), 0.0), axis=1, keepdims=True)
        x = jnp.where(cnt < need, cand, x)
    sel = jnp.where(gt, 1.0, jnp.where(eq, jnp.where(idx <= x, 1.0, 0.0), 0.0))
    running = jnp.zeros((N_EXPERTS, 1), F32)
    for c in range(n // CHUNK):
        blk = sel[:, c * CHUNK:(c + 1) * CHUNK]
        incl = _dot(blk.astype(BF16), tri)
        pos = running + incl - blk + float(slot0)
        cg = lane0 // CHUNK + c
        pos_ref[0, :, lane0 + c * CHUNK:lane0 + (c + 1) * CHUNK] = jnp.where(
            blk > 0.5, pos, -1.0).astype(I32)
        base_ref[0, :, cg:cg + 1] = (running + float(slot0)).astype(I32)
        running = running + incl[:, CHUNK - 1:CHUNK]


def _route_kernel(aff_ref, pos_ref, base_ref, *, n_ctx, cap_ctx, cap_lat):
    r_i = lax.broadcasted_iota(I32, (CHUNK, CHUNK), 0)
    c_i = lax.broadcasted_iota(I32, (CHUNK, CHUNK), 1)
    tri = jnp.where(r_i <= c_i, 1.0, 0.0).astype(BF16)
    base_ref[...] = jnp.zeros_like(base_ref)
    a = aff_ref[0]
    _route_part(a[:, :n_ctx], cap_ctx, 0, tri, pos_ref, base_ref, 0)
    _route_part(a[:, n_ctx:], cap_lat, cap_ctx, tri, pos_ref, base_ref, n_ctx)


def _route(aff_t, *, n_ctx, cap_ctx, cap_lat):
    B, E, S = aff_t.shape
    kern = functools.partial(_route_kernel, n_ctx=n_ctx, cap_ctx=cap_ctx, cap_lat=cap_lat)
    return pl.pallas_call(
        kern,
        grid=(B,),
        in_specs=[pl.BlockSpec((1, E, S), lambda b: (b, 0, 0))],
        out_specs=[pl.BlockSpec((1, E, S), lambda b: (b, 0, 0)),
                   pl.BlockSpec((1, E, LANE), lambda b: (b, 0, 0))],
        out_shape=[jax.ShapeDtypeStruct((B, E, S), I32), jax.ShapeDtypeStruct((B, E, LANE), I32)],
        compiler_params=_cparams(1),
        name="route",
    )(aff_t)


def _window_start(base, align, rows, win):
    w0 = lax.shift_left(lax.shift_right_logical(base, int(math.log2(align))), int(math.log2(align)))
    return pl.multiple_of(jnp.minimum(w0, rows - win), align)


def _gather_kernel(base_ref, h_ref, pos_ref, xe_ref, acc_ref, *, n_chunks, rows, win, unroll):
    b = pl.program_id(0)
    e = pl.program_id(2)
    acc_ref[...] = jnp.zeros_like(acc_ref)

    def group(g, carry):
        picked = []
        for u in range(unroll):
            c = g * unroll + u
            w0 = _window_start(base_ref[b, e, c], GATHER_ALIGN_F32, rows, win)
            posr = pos_ref[0, 0, pl.ds(c, 1), :]
            slot = lax.broadcasted_iota(I32, (win, CHUNK), 0) + w0
            onehot = jnp.where(posr == slot, 1.0, 0.0).astype(BF16)
            hc = h_ref[0, pl.ds(pl.multiple_of(c * CHUNK, CHUNK), CHUNK), :]
            picked.append((w0, _dot(onehot, hc)))
        for w0, rows_c in picked:
            acc_ref[pl.ds(w0, win), :] += rows_c
        return carry

    lax.fori_loop(0, n_chunks // unroll, group, 0)
    xe_ref[0, 0] = acc_ref[...].astype(BF16)


def _gather(hx, pos4, bases, *, rows):
    B, S, D = hx.shape
    nc = S // CHUNK
    dh = D // 2
    win = min(CHUNK + GATHER_ALIGN_F32, rows)
    unroll = next(u for u in (6, 4, 3, 2, 1) if nc % u == 0)
    kern = functools.partial(_gather_kernel, n_chunks=nc, rows=rows, win=win, unroll=unroll)
    return pl.pallas_call(
        kern,
        grid_spec=pltpu.PrefetchScalarGridSpec(
            num_scalar_prefetch=1,
            grid=(B, 2, N_EXPERTS),
            in_specs=[pl.BlockSpec((1, S, dh), lambda b, j, e, bs: (b, 0, j)),
                      pl.BlockSpec((1, 1, nc, CHUNK), lambda b, j, e, bs: (b, e, 0, 0))],
            out_specs=pl.BlockSpec((1, 1, rows, dh), lambda b, j, e, bs: (b, e, 0, j)),
            scratch_shapes=[pltpu.VMEM((rows, dh), F32)]),
        out_shape=jax.ShapeDtypeStruct((B, N_EXPERTS, rows, D), BF16),
        compiler_params=_cparams(3, VMEM_LIMIT),
        name="moe_gather",
    )(bases, hx, pos4)


def _ffn_kernel(x_ref, wg_ref, wu_ref, wd_ref, y_ref, *, row_tile, n_tiles):
    def tile(i, carry):
        r0 = pl.multiple_of(i * row_tile, 16)
        x = x_ref[0, 0, pl.ds(r0, row_tile), :]
        g = _dot(x, wg_ref[0])
        u = _dot(x, wu_ref[0])
        hid = (g * _sigmoid(g) * u).astype(BF16)
        y_ref[0, 0, pl.ds(r0, row_tile), :] = _dot(hid, wd_ref[0]).astype(BF16)
        return carry

    lax.fori_loop(0, n_tiles, tile, 0)


def _ffn_row_tile(rows):
    for t in (352, 256, 176, 128, 96, 64, 32, 16):
        if rows % t == 0:
            return t
    raise ValueError(f"expert slot rows {rows} must be a multiple of 16")


def _ffn(xe, wg, wu, wd):
    B, E, R, D = xe.shape
    F = wg.shape[-1]
    rt = _ffn_row_tile(R)
    kern = functools.partial(_ffn_kernel, row_tile=rt, n_tiles=R // rt)
    return pl.pallas_call(
        kern,
        grid=(E, B),
        in_specs=[pl.BlockSpec((1, 1, R, D), lambda e, b: (b, e, 0, 0)),
                  pl.BlockSpec((1, D, F), lambda e, b: (e, 0, 0)),
                  pl.BlockSpec((1, D, F), lambda e, b: (e, 0, 0)),
                  pl.BlockSpec((1, F, D), lambda e, b: (e, 0, 0))],
        out_specs=pl.BlockSpec((1, 1, R, D), lambda e, b: (b, e, 0, 0)),
        out_shape=jax.ShapeDtypeStruct((B, E, R, D), BF16),
        compiler_params=_cparams(2, VMEM_LIMIT),
        name="moe_ffn",
    )(xe, wg, wu, wd)


def _combine_kernel(base_ref, x_ref, ye_ref, posc_ref, aff_ref, modc_ref, modx_ref, o_ref, *, d_model,
                    blk_chunks, n_ctx_chunks, rows, win):
    D = d_model
    b = pl.program_id(0)
    tb = pl.program_id(1)
    e = pl.program_id(2)

    @pl.when(e == 0)
    def _():
        o_ref[0] = jnp.zeros(o_ref.shape[1:], F32)

    lane = lax.broadcasted_iota(I32, (1, N_EXPERTS), 1)
    for cc in range(blk_chunks):
        rs = slice(cc * CHUNK, (cc + 1) * CHUNK)
        chunk = tb * blk_chunks + cc
        w0 = _window_start(base_ref[b, e, chunk], GATHER_ALIGN_BF16, rows, win)
        pcol = jnp.sum(jnp.where(lane == e, posc_ref[0, rs, :].astype(F32), 0.0), axis=1, keepdims=True)
        gcol = jnp.sum(jnp.where(lane == e, aff_ref[0, rs, :], 0.0), axis=1, keepdims=True)
        slot = (lax.broadcasted_iota(I32, (CHUNK, win), 1) + w0).astype(F32)
        onehot = jnp.where(pcol == slot, 1.0, 0.0).astype(BF16)
        yw = ye_ref[0, 0, pl.ds(w0, win), :]
        o_ref[0, rs, :] += gcol * _dot(onehot, yw)

    @pl.when(e == N_EXPERTS - 1)
    def _():
        g2c = modc_ref[0][:, 5 * D:6 * D]
        g2x = modx_ref[0][:, 5 * D:6 * D]
        for cc in range(blk_chunks):
            rs = slice(cc * CHUNK, (cc + 1) * CHUNK)
            g2 = jnp.where(tb * blk_chunks + cc < n_ctx_chunks, g2c, g2x)
            o_ref[0, rs, :] = x_ref[0, rs, :] + g2 * o_ref[0, rs, :]


def _combine_blk_chunks(nc):
    for k in (11, 8, 6, 4, 3, 2, 1):
        if nc % k == 0:
            return k
    return 1


def _combine(x, ye, posc, aff, mods_l, bases, *, n_ctx):
    B, S, D = x.shape
    R = ye.shape[2]
    nc = S // CHUNK
    k = _combine_blk_chunks(nc)
    tb = k * CHUNK
    win = min(CHUNK + GATHER_ALIGN_BF16, R)
    kern = functools.partial(_combine_kernel, d_model=D, blk_chunks=k, n_ctx_chunks=n_ctx // CHUNK,
                             rows=R, win=win)
    modc_spec = pl.BlockSpec((1, 1, 6 * D), lambda b, t, e, bs: (B, 0, 0))
    modx_spec = pl.BlockSpec((1, 1, 6 * D), lambda b, t, e, bs: (b, 0, 0))
    return pl.pallas_call(
        kern,
        grid_spec=pltpu.PrefetchScalarGridSpec(
            num_scalar_prefetch=1,
            grid=(B, S // tb, N_EXPERTS),
            in_specs=[pl.BlockSpec((1, tb, D), lambda b, t, e, bs: (b, t, 0)),
                      pl.BlockSpec((1, 1, R, D), lambda b, t, e, bs: (b, e, 0, 0)),
                      pl.BlockSpec((1, tb, N_EXPERTS), lambda b, t, e, bs: (b, t, 0)),
                      pl.BlockSpec((1, tb, N_EXPERTS), lambda b, t, e, bs: (b, t, 0)),
                      modc_spec, modx_spec],
            out_specs=pl.BlockSpec((1, tb, D), lambda b, t, e, bs: (b, t, 0))),
        out_shape=jax.ShapeDtypeStruct((B, S, D), F32),
        compiler_params=_cparams(3, VMEM_LIMIT),
        name="moe_combine",
    )(bases, x, ye, posc, aff, mods_l, mods_l)


def _final_kernel(x_ref, g_ref, o_ref):
    o_ref[0] = _rms(x_ref[0], g_ref[...])


def _final_norm(xs, g, *, n_ctx):
    B, S, D = xs.shape
    T = S - n_ctx
    off = n_ctx // TOK_TILE
    return pl.pallas_call(
        _final_kernel,
        grid=(B, T // TOK_TILE),
        in_specs=[pl.BlockSpec((1, TOK_TILE, D), lambda b, i: (b, i + off, 0)),
                  pl.BlockSpec((1, D), lambda b, i: (0, 0))],
        out_specs=pl.BlockSpec((1, TOK_TILE, D), lambda b, i: (b, i, 0)),
        out_shape=jax.ShapeDtypeStruct((B, T, D), F32),
        compiler_params=_cparams(2),
        name="final_norm",
    )(xs, g)


def _pad_heads(w, n_heads):
    lead = w.shape[:-1]
    w = w.reshape(*lead, n_heads, HEAD)
    w = jnp.pad(w, [(0, 0)] * len(lead) + [(0, 0), (0, LANE - HEAD)])
    return w.reshape(*lead, n_heads * LANE)


def _rot_swap(w):
    half = A_ROPE // 2
    return jnp.concatenate([-w[..., half:], w[..., :half]], -1)


def _rope_tables(n_ctx, n_lat):
    rows = n_lat // GRID_W
    row_id = jnp.repeat(jnp.arange(rows, dtype=F32), GRID_W)
    col_id = jnp.tile(jnp.arange(GRID_W, dtype=F32), rows)
    n_freq = A_ROPE // 4
    inv = ROPE_BASE ** (-jnp.arange(n_freq, dtype=F32) / n_freq)
    ang = jnp.concatenate([row_id[:, None] * inv, col_id[:, None] * inv], -1)
    ang = jnp.concatenate([jnp.zeros((n_ctx, A_ROPE // 2), F32), ang], 0)
    S = n_ctx + n_lat
    cos, sin = jnp.cos(ang), jnp.sin(ang)
    pad = jnp.zeros((S, LANE - HEAD - A_ROPE), F32)
    cos128 = jnp.concatenate([jnp.ones((S, HEAD), F32), cos, cos, pad], -1)
    sin128 = jnp.concatenate([jnp.zeros((S, HEAD), F32), sin, sin, pad], -1)
    return cos128, sin128


def kernel(x, c, ctx, c_ctx, ada_w, ada_b, norm1_g, norm2_g, w_in, m_conv_w, m_conv_b, m_ib, m_fb, m_norm_g, a_qnorm_g, a_wq_up, a_kvnorm_g, a_wkv_up, w_out, router_w, e_w_gate, e_w_up, e_w_down, final_g):
    B, T, D = x.shape
    n_ctx = ctx.shape[1]
    L = ada_w.shape[0]
    S = n_ctx + T
    assert D == 16 * HEAD and n_ctx % TOK_TILE == 0 and T % TOK_TILE == 0 and T % (GRID_W * 8) == 0
    n_ctx_tiles = n_ctx // TOK_TILE
    n_ctx_chunks = n_ctx // CHUNK
    cap_ctx = EC_FACTOR * n_ctx // N_EXPERTS
    cap_lat = EC_FACTOR * T // N_EXPERTS
    slot_rows = cap_ctx + cap_lat

    o_f, o_qk, o_v, o_o, o_g, o_cq, o_ckv, o_kr = (0, 256, 768, 1024, 1280, 1296, 1680, 1936)
    w_kr = w_in[:, :, o_kr:o_kr + A_ROPE]
    slot = lambda w: jnp.pad(w, ((0, 0), (0, 0), (HEAD, LANE - HEAD - A_ROPE)))
    w_g = w_in[:, :, o_g:o_g + 16]
    w1 = jnp.concatenate([
        w_in[:, :, o_f:o_f + 256],
        _pad_heads(w_in[:, :, o_qk:o_qk + 256], M_HEADS),
        _pad_heads(w_in[:, :, o_qk + 256:o_qk + 512], M_HEADS),
        _pad_heads(w_in[:, :, o_v:o_v + 256], M_HEADS),
        _pad_heads(w_in[:, :, o_o:o_o + 256], M_HEADS),
        w_in[:, :, o_cq:o_cq + 384],
        w_in[:, :, o_ckv:o_ckv + 256],
        slot(w_kr),
        slot(_rot_swap(w_kr)),
        jnp.pad(w_g, ((0, 0), (0, 0), (0, LANE - 16))),
    ], -1).astype(BF16)
    assert w1.shape[-1] == _N_COLS
    wgt = jnp.swapaxes(w_g, 1, 2).astype(BF16)
    conv_w = jnp.concatenate([_pad_heads(m_conv_w[..., :256], M_HEADS),
                              _pad_heads(m_conv_w[..., 256:], M_HEADS)], -1)
    conv_b = jnp.concatenate([_pad_heads(m_conv_b[..., :256], M_HEADS),
                              _pad_heads(m_conv_b[..., 256:], M_HEADS)], -1)[:, None, :]
    gate_bias = jnp.stack([m_ib, m_fb], 2).reshape(L, 16)
    wq3 = a_wq_up.reshape(L, -1, A_HEADS, HEAD + A_ROPE)
    zq = jnp.zeros(wq3.shape[:-1] + (LANE - HEAD - A_ROPE,), F32)
    wq = jnp.concatenate([wq3, zq], -1).reshape(L, -1, A_HEADS * LANE)
    wqt = jnp.swapaxes(wq, 1, 2).astype(BF16)
    wkv3 = a_wkv_up.reshape(L, -1, A_HEADS, 2 * HEAD)
    zk = jnp.zeros(wkv3.shape[:-1] + (HEAD,), F32)
    wk = jnp.concatenate([wkv3[..., :HEAD], zk], -1).reshape(L, -1, A_HEADS * LANE).astype(BF16)
    wvt = jnp.swapaxes(jnp.concatenate([wkv3[..., HEAD:], zk], -1).reshape(L, -1, A_HEADS * LANE),
                       1, 2).astype(BF16)
    wa = w_out[:, 0:256].astype(BF16)
    wb = jnp.swapaxes(_pad_heads(jnp.swapaxes(w_out[:, 256:512], 1, 2), M_HEADS), 1, 2).astype(BF16)
    wc = w_out[:, 512:1024].astype(BF16)
    mg = _pad_heads(m_norm_g, M_HEADS)[:, None, :]
    lane = jnp.arange(M_HEADS * LANE)
    blk = jnp.where((lane[:, None] // LANE == lane[None, :] // LANE) & (lane[:, None] % LANE < HEAD),
                    1.0 / HEAD, 0.0).astype(BF16)
    cos128, sin128 = _rope_tables(n_ctx, T)
    row = jnp.arange(LANE)[:, None]
    cost128 = cos128.T
    sint128 = jnp.where(row < HEAD + A_ROPE // 2, -sin128.T, sin128.T)
    cs = _fft_chan_mats()
    attn_sub = 3 if (S // TOK_TILE) % 3 == 0 else 1
    attn_tq = 512 if T % 512 == 0 else TOK_TILE
    wg_e = e_w_gate.astype(BF16)
    wu_e = e_w_up.astype(BF16)
    wd_e = e_w_down.astype(BF16)

    rows16 = 16
    cvec = jnp.zeros((rows16, D), F32).at[:B].set(c).at[B].set(c_ctx)
    mods = _ada_mods(cvec, ada_w, ada_b).reshape(L, rows16, 1, 6 * D)

    xs = jnp.concatenate([ctx, x], axis=1)
    for l in range(L):
        mods_l = mods[l]
        f, qk, vm, og, gat, gatt, qt_ctx, qt_lat, k, vt4 = _inproj(
            xs, mods_l, norm1_g[l][None], w1[l], wgt[l], a_qnorm_g[l][None], a_kvnorm_g[l][None],
            wqt[l], wk[l], wvt[l], cos128, sin128, cost128, sint128,
            n_ctx_tiles=n_ctx_tiles)
        ya_ctx = _fft_ctx(f, n_ctx, cs)
        ya_lat = _fft_latent(f[:, n_ctx:], cs)
        qm, ktm, colp, rowp = _mlstm_prep(qk, gat, gatt, conv_w[l], conv_b[l], gate_bias[l][None, :],
                                          gate_bias[l][:, None], n_ctx_chunks=n_ctx_chunks)
        hf, hb = _mlstm_scan(qm, ktm, vm, colp, rowp, n_ctx_chunks=n_ctx_chunks)
        yc_ctx = _attention(qt_ctx, k, vt4, n_keys=n_ctx, tq=TOK_TILE, sub=1)
        yc_lat = _attention(qt_lat, k, vt4, n_keys=S, tq=attn_tq, sub=attn_sub)
        xs, hx, aff = _outproj(ya_ctx, ya_lat, hf, hb, og, yc_ctx, yc_lat, xs, mods_l, wa[l], wb[l], wc[l],
                               mg[l], blk, norm2_g[l][None], router_w[l], n_ctx_tiles=n_ctx_tiles)
        posm, bases = _route(jnp.swapaxes(aff, 1, 2), n_ctx=n_ctx, cap_ctx=cap_ctx, cap_lat=cap_lat)
        xe = _gather(hx, posm.reshape(B, N_EXPERTS, S // CHUNK, CHUNK), bases, rows=slot_rows)
        ye = _ffn(xe, wg_e[l], wu_e[l], wd_e[l])
        xs = _combine(xs, ye, jnp.swapaxes(posm, 1, 2), aff, mods_l, bases, n_ctx=n_ctx)
    return _final_norm(xs, final_g[None], n_ctx=n_ctx)
```

```python
import functools
import math

import jax
import jax.numpy as jnp
from jax import lax
from jax.experimental import pallas as pl
from jax.experimental.pallas import tpu as pltpu

F32 = jnp.float32
BF16 = jnp.bfloat16
I32 = jnp.int32

EPS = 1e-6
GRID_W = 64
ROPE_BASE = 10000.0
LANE = 128
HEAD = 64
M_HEADS = 4
A_HEADS = 8
A_ROPE = 32
V_ROWS = 80
N_EXPERTS = 16
EC_FACTOR = 2
TOK_TILE = 256
CHUNK = 128
GATHER_ALIGN_F32 = 8
GATHER_ALIGN_BF16 = 16
VMEM_LIMIT = 56 * 1024 * 1024


def _cparams(n_axes, vmem=None):
    return pltpu.CompilerParams(dimension_semantics=("arbitrary",) * n_axes,
                                vmem_limit_bytes=vmem)


def _dot(a, b):
    return jnp.dot(a, b, preferred_element_type=F32)


def _dot_nt(a, b):
    return lax.dot_general(a, b, (((1,), (1,)), ((), ())), preferred_element_type=F32)


def _split_bf16(a):
    hi = a.astype(BF16)
    lo = (a - hi.astype(F32)).astype(BF16)
    return hi, lo


def _sigmoid(x):
    return 1.0 / (1.0 + jnp.exp(-x))


def _rms(x, g):
    return x * lax.rsqrt(jnp.mean(x * x, axis=-1, keepdims=True) + EPS) * g


def _ada_kernel(c_ref, w_ref, b_ref, o_ref):
    a = c_ref[...]
    a = a * _sigmoid(a)
    a_hi, a_lo = _split_bf16(a)
    w_hi, w_lo = _split_bf16(w_ref[0])
    o_ref[0] = _dot(a_hi, w_hi) + _dot(a_lo, w_hi) + _dot(a_hi, w_lo) + b_ref[0]


def _ada_mods(cvec, ada_w, ada_b):
    L, D, D6 = ada_w.shape
    rows = cvec.shape[0]
    return pl.pallas_call(
        _ada_kernel,
        grid=(L, D6 // D),
        in_specs=[pl.BlockSpec((rows, D), lambda l, j: (0, 0)),
                  pl.BlockSpec((1, D, D), lambda l, j: (l, 0, j)),
                  pl.BlockSpec((1, 1, D), lambda l, j: (l, 0, j))],
        out_specs=pl.BlockSpec((1, rows, D), lambda l, j: (l, 0, j)),
        out_shape=jax.ShapeDtypeStruct((L, rows, D6), F32),
        compiler_params=_cparams(2),
        name="ada_mods",
    )(cvec, ada_w, ada_b.reshape(L, 1, D6))


_C_F = (0, 256)
_C_Q = (256, 768)
_C_K = (768, 1280)
_C_V = (1280, 1792)
_C_O = (1792, 2304)
_C_CQ = (2304, 2688)
_C_CKV = (2688, 2944)
_C_KR = (2944, 3072)
_C_KRS = (3072, 3200)
_C_G = (3200, 3328)
_N_COLS = 3328


def _inproj_kernel(x_ref, mod_ref, n1_ref, w1_ref, wgt_ref, qn_ref, kvn_ref, wqt_ref,
                   wk_ref, wvt_ref, cos_ref, sin_ref, cost_ref, sint_ref,
                   f_ref, qk_ref, vm_ref, o_ref, gat_ref, gatt_ref, qtc_ref, qtl_ref, k_ref, vt_ref, *,
                   d_model, q_scale, n_ctx_tiles):
    D = d_model
    i = pl.program_id(1)
    x = x_ref[0]
    mod = mod_ref[0]
    sh1, sc1 = mod[:, 0:D], mod[:, D:2 * D]
    xm = _rms(x, n1_ref[...]) * (1.0 + sc1) + sh1
    xb = xm.astype(BF16)
    u = _dot(xb, w1_ref[...])
    f_ref[0] = u[:, _C_F[0]:_C_F[1]].astype(BF16)
    qk_ref[0] = u[:, _C_Q[0]:_C_K[1]].astype(BF16)
    lane512 = lax.broadcasted_iota(I32, (1, 4 * LANE), 1)
    ones_m = jnp.where(lane512 % LANE == HEAD, 1.0, 0.0)
    vm_ref[0] = (u[:, _C_V[0]:_C_V[1]] + ones_m).astype(BF16)
    o_ref[0] = u[:, _C_O[0]:_C_O[1]].astype(BF16)
    gat_ref[0] = u[:, _C_G[0]:_C_G[0] + 16]
    gatt_ref[0] = _dot_nt(wgt_ref[...], xb)

    cqn = _rms(u[:, _C_CQ[0]:_C_CQ[1]], qn_ref[...]).astype(BF16)
    cost = jnp.tile(cost_ref[...], (A_HEADS, 1))
    sint = jnp.tile(sint_ref[...], (A_HEADS, 1))
    qa = _dot_nt(wqt_ref[...], cqn)
    half = A_ROPE // 2
    parts = []
    for h in range(A_HEADS):
        r0 = h * LANE
        parts += [qa[r0:r0 + HEAD], qa[r0 + HEAD + half:r0 + HEAD + A_ROPE],
                  qa[r0 + HEAD:r0 + HEAD + half], qa[r0 + HEAD + A_ROPE:r0 + LANE]]
    qt = ((qa * cost + jnp.concatenate(parts, axis=0) * sint) * q_scale).astype(BF16)

    @pl.when(i < n_ctx_tiles)
    def _():
        qtc_ref[0, 0] = qt

    @pl.when(i >= n_ctx_tiles)
    def _():
        qtl_ref[0, 0] = qt

    ckvn = _rms(u[:, _C_CKV[0]:_C_CKV[1]], kvn_ref[...]).astype(BF16)
    cosw = jnp.tile(cos_ref[...], (1, A_HEADS))
    sinw = jnp.tile(sin_ref[...], (1, A_HEADS))
    kw = A_HEADS * LANE
    kr = jnp.tile(u[:, _C_KR[0]:_C_KR[1]], (1, A_HEADS))
    krs = jnp.tile(u[:, _C_KRS[0]:_C_KRS[1]], (1, A_HEADS))
    k = _dot(ckvn, wk_ref[...]) + kr * cosw + krs * sinw
    k_ref[0] = k.astype(BF16)
    vrow = lax.broadcasted_iota(I32, (A_HEADS * V_ROWS, 1), 0)
    ones_a = jnp.where(vrow % V_ROWS == HEAD, 1.0, 0.0)
    vt_ref[0, 0] = (_dot_nt(wvt_ref[...], ckvn) + ones_a).astype(BF16)


def _inproj(xs, mods_l, n1, w1, wgt, qn, kvn, wqt, wk, wvt, cos128, sin128, cost128, sint128,
            *, n_ctx_tiles):
    B, S, D = xs.shape
    nt = S // TOK_TILE
    n_ctx = n_ctx_tiles * TOK_TILE
    n_lat_tiles = nt - n_ctx_tiles
    kw = A_HEADS * LANE
    tok = lambda w: pl.BlockSpec((1, TOK_TILE, w), lambda b, i: (b, i, 0))
    full = lambda a: pl.BlockSpec(a.shape, lambda b, i: (0,) * a.ndim)
    mod_spec = pl.BlockSpec((1, 1, 6 * D), lambda b, i: (jnp.where(i < n_ctx_tiles, B, b), 0, 0))
    tab_spec = pl.BlockSpec((TOK_TILE, LANE), lambda b, i: (i, 0))
    tabt_spec = pl.BlockSpec((LANE, TOK_TILE), lambda b, i: (0, i))
    sd = lambda w, dt: jax.ShapeDtypeStruct((B, S, w), dt)
    kern = functools.partial(_inproj_kernel, d_model=D, n_ctx_tiles=n_ctx_tiles,
                             q_scale=(HEAD + A_ROPE) ** -0.5 * math.log2(math.e))
    qtc_spec = pl.BlockSpec((1, 1, kw, TOK_TILE), lambda b, i: (b, jnp.minimum(i, n_ctx_tiles - 1), 0, 0))
    qtl_spec = pl.BlockSpec((1, 1, kw, TOK_TILE),
                            lambda b, i: (b, jnp.clip(i - n_ctx_tiles, 0, n_lat_tiles - 1), 0, 0))
    return pl.pallas_call(
        kern,
        grid=(B, nt),
        in_specs=[tok(D), mod_spec, full(n1), full(w1), full(wgt), full(qn), full(kvn), full(wqt),
                  full(wk), full(wvt), tab_spec, tab_spec, tabt_spec, tabt_spec],
        out_specs=[tok(256), tok(1024), tok(512), tok(512), tok(16),
                   pl.BlockSpec((1, 16, TOK_TILE), lambda b, i: (b, 0, i)),
                   qtc_spec, qtl_spec, tok(kw),
                   pl.BlockSpec((1, 1, A_HEADS * V_ROWS, TOK_TILE), lambda b, i: (b, i, 0, 0))],
        out_shape=[sd(256, BF16), sd(1024, BF16), sd(512, BF16), sd(512, BF16), sd(16, F32),
                   jax.ShapeDtypeStruct((B, 16, S), F32),
                   jax.ShapeDtypeStruct((B, n_ctx_tiles, kw, TOK_TILE), BF16),
                   jax.ShapeDtypeStruct((B, n_lat_tiles, kw, TOK_TILE), BF16),
                   sd(kw, BF16),
                   jax.ShapeDtypeStruct((B, nt, A_HEADS * V_ROWS, TOK_TILE), BF16)],
        compiler_params=_cparams(2, VMEM_LIMIT),
        name="inproj",
    )(xs, mods_l, n1, w1, wgt, qn, kvn, wqt, wk, wvt, cos128, sin128, cost128, sint128)


def _dft_mats(n):
    idx = jnp.arange(n, dtype=F32)
    ang = 2.0 * math.pi * jnp.mod(idx[:, None] * idx[None, :], n) / n
    return jnp.cos(ang), jnp.sin(ang)


def _fft_chan_mats():
    c, s = _dft_mats(HEAD)
    eye = jnp.eye(4, dtype=F32)
    return jnp.concatenate([jnp.kron(eye, c), jnp.kron(eye, s)], 0).astype(BF16)


def _fft_ctx_kernel(f_ref, ft_ref, cs_ref, o_ref, *, n, scale):
    xst = _dot(ft_ref[...], f_ref[0])
    xr = xst[:n].astype(BF16)
    xi = xst[n:].astype(BF16)
    y = _dot(xr, cs_ref[0:256, :]) + _dot(xi, cs_ref[256:512, :])
    o_ref[0] = (y * scale).astype(BF16)


def _fft_ctx(f, n_ctx, cs):
    B = f.shape[0]
    c, s = _dft_mats(n_ctx)
    ft = jnp.concatenate([c, -s], 0).astype(BF16)
    kern = functools.partial(_fft_ctx_kernel, n=n_ctx, scale=(n_ctx * HEAD) ** -0.5)
    return pl.pallas_call(
        kern,
        grid=(B,),
        in_specs=[pl.BlockSpec((1, n_ctx, 256), lambda b: (b, 0, 0)),
                  pl.BlockSpec(ft.shape, lambda b: (0, 0)),
                  pl.BlockSpec(cs.shape, lambda b: (0, 0))],
        out_specs=pl.BlockSpec((1, n_ctx, 256), lambda b: (b, 0, 0)),
        out_shape=jax.ShapeDtypeStruct((B, n_ctx, 256), BF16),
        compiler_params=_cparams(1),
        name="fft_ctx",
    )(f, ft, cs)


def _fft_stage1_kernel(x_ref, f1_ref, cw_ref, sw_ref, o_ref, *, n1):
    z = _dot(f1_ref[...], x_ref[0])
    zr, zi = z[:n1], z[n1:]
    cw, sw = cw_ref[...], sw_ref[...]
    o_ref[0, 0] = (zr * cw + zi * sw).astype(BF16)
    o_ref[0, 1] = (zi * cw - zr * sw).astype(BF16)


def _fft_stage2_kernel(z_ref, f2_ref, cs_ref, o_ref, *, tb, scale):
    for j in range(tb):
        zcat = jnp.concatenate([z_ref[0, 0, j], z_ref[0, 1, j]], axis=0)
        xst = _dot(f2_ref[...], zcat)
        xr = xst[:HEAD].astype(BF16)
        xi = xst[HEAD:].astype(BF16)
        y = _dot(xr, cs_ref[0:256, :]) + _dot(xi, cs_ref[256:512, :])
        o_ref[0, j] = (y * scale).astype(BF16)


def _fft_latent(f_lat, cs):
    B, T, W = f_lat.shape
    n2 = HEAD
    n1 = T // n2
    cols = n2 * W
    cb = 2048
    c1, s1 = _dft_mats(n1)
    f1 = jnp.concatenate([c1, -s1], 0).astype(BF16)
    t1 = jnp.arange(n1, dtype=F32)[:, None]
    s2 = jnp.arange(n2, dtype=F32)[None, :]
    ang = 2.0 * math.pi * (t1 * s2) / T
    cw = jnp.repeat(jnp.cos(ang), W, axis=1)
    sw = jnp.repeat(jnp.sin(ang), W, axis=1)
    z = pl.pallas_call(
        functools.partial(_fft_stage1_kernel, n1=n1),
        grid=(B, cols // cb),
        in_specs=[pl.BlockSpec((1, n1, cb), lambda b, j: (b, 0, j)),
                  pl.BlockSpec(f1.shape, lambda b, j: (0, 0)),
                  pl.BlockSpec((n1, cb), lambda b, j: (0, j)),
                  pl.BlockSpec((n1, cb), lambda b, j: (0, j))],
        out_specs=pl.BlockSpec((1, 2, n1, cb), lambda b, j: (b, 0, 0, j)),
        out_shape=jax.ShapeDtypeStruct((B, 2, n1, cols), BF16),
        compiler_params=_cparams(2),
        name="fft_stage1",
    )(f_lat.reshape(B, n1, cols), f1, cw, sw)
    z = z.reshape(B, 2, n1, n2, W)
    c2, s2m = _dft_mats(n2)
    f2 = jnp.concatenate([jnp.concatenate([c2, s2m], 1),
                          jnp.concatenate([-s2m, c2], 1)], 0).astype(BF16)
    tb = 8
    y = pl.pallas_call(
        functools.partial(_fft_stage2_kernel, tb=tb, scale=(T * HEAD) ** -0.5),
        grid=(B, n1 // tb),
        in_specs=[pl.BlockSpec((1, 2, tb, n2, W), lambda b, j: (b, 0, j, 0, 0)),
                  pl.BlockSpec(f2.shape, lambda b, j: (0, 0)),
                  pl.BlockSpec(cs.shape, lambda b, j: (0, 0))],
        out_specs=pl.BlockSpec((1, tb, n2, W), lambda b, j: (b, j, 0, 0)),
        out_shape=jax.ShapeDtypeStruct((B, n1, n2, W), BF16),
        compiler_params=_cparams(2),
        name="fft_stage2",
    )(z, f2, cs)
    return jnp.transpose(y, (0, 2, 1, 3)).reshape(B, T, W)


def _mlstm_prep_kernel(cur_ref, prev_ref, next_ref, cw_ref, cb_ref, gat_ref, gatt_ref, br_ref, bc_ref,
                       q_ref, kt_ref, col_ref, row_ref, *, n_chunks, n_ctx_chunks):
    c = pl.program_id(1)
    cur = cur_ref[0].astype(F32)
    first = jnp.logical_or(c == 0, c == n_ctx_chunks)
    last = jnp.logical_or(c == n_ctx_chunks - 1, c == n_chunks - 1)
    prev_row = prev_ref[0].astype(F32)[15:16, :]
    next_row = next_ref[0].astype(F32)[0:1, :]
    prev_row = jnp.where(first, 0.0, prev_row)
    next_row = jnp.where(last, 0.0, next_row)
    rows = lax.broadcasted_iota(I32, (CHUNK, 1), 0)
    up = jnp.where(rows == 0, prev_row, pltpu.roll(cur, 1, axis=0))
    dn = jnp.where(rows == CHUNK - 1, next_row, pltpu.roll(cur, CHUNK - 1, axis=0))
    y = cw_ref[0:1, :] * up + cw_ref[1:2, :] * cur + cw_ref[2:3, :] * dn + cb_ref[...]
    y = y * _sigmoid(y)
    hw = M_HEADS * LANE
    q_ref[0] = (y[:, :hw] * HEAD ** -0.5).astype(BF16)
    kt_ref[0] = y[:, hw:].T.astype(BF16)

    def logsig(v):
        return jnp.minimum(v, 0.0) - jnp.log(1.0 + jnp.exp(-jnp.abs(v)))

    r_i = lax.broadcasted_iota(I32, (CHUNK, CHUNK), 0)
    c_i = lax.broadcasted_iota(I32, (CHUNK, CHUNK), 1)
    lower = jnp.where(c_i <= r_i, 1.0, 0.0).astype(BF16)
    upper = jnp.where(c_i >= r_i, 1.0, 0.0).astype(BF16)

    g = gat_ref[0] + br_ref[...]
    lane = lax.broadcasted_iota(I32, (1, 16), 1)
    lg = jnp.where(lane % 8 >= 4, logsig(g), g)
    hi, lo = _split_bf16(lg)
    cum_f = _dot(lower, hi) + _dot(lower, lo)
    cum_b = _dot(upper, hi) + _dot(upper, lo)
    cum_col = jnp.where(lane < 8, cum_f, cum_b)

    gt = gatt_ref[0] + bc_ref[...]
    row = lax.broadcasted_iota(I32, (16, 1), 0)
    lgt = jnp.where(row % 8 >= 4, logsig(gt), gt)
    hit, lot = _split_bf16(lgt)
    cum_ft = _dot(hit, upper) + _dot(lot, upper)
    cum_bt = _dot(hit, lower) + _dot(lot, lower)
    cumt = jnp.where(row < 8, cum_ft, cum_bt)
    cumt_i = pltpu.roll(cumt, 12, axis=0)
    rterm = lgt - cumt_i
    btot = jnp.broadcast_to(jnp.sum(lgt, axis=1, keepdims=True), (16, CHUNK))
    row_ref[0, 0:16] = jnp.where(row % 8 < 4, rterm, btot)
    dmax = jnp.broadcast_to(jnp.max(rterm, axis=1, keepdims=True), (16, CHUNK))
    row_ref[0, 16:32] = pltpu.roll(btot, 12, axis=0) + dmax
    seen_max = jnp.zeros((CHUNK, 16), F32)
    for d in range(2):
        seen = (c_i <= r_i) if d == 0 else (c_i >= r_i)
        for h in range(M_HEADS):
            idx = 8 * d + h
            cmx = jnp.max(jnp.where(seen, rterm[idx:idx + 1, :], -jnp.inf), axis=1, keepdims=True)
            seen_max = jnp.where(lane == idx, cmx, seen_max)
    col_ref[0] = jnp.where(lane % 8 >= 4, cum_col, seen_max)


def _mlstm_prep(qk, gat, gatt, conv_w, conv_b, bias_row, bias_col, *, n_ctx_chunks):
    B, S, W = qk.shape
    nc = S // CHUNK
    n16 = S // 16
    hw = M_HEADS * LANE
    kern = functools.partial(_mlstm_prep_kernel, n_chunks=nc, n_ctx_chunks=n_ctx_chunks)
    full = lambda a: pl.BlockSpec(a.shape, lambda b, c: (0,) * a.ndim)
    return pl.pallas_call(
        kern,
        grid=(B, nc),
        in_specs=[pl.BlockSpec((1, CHUNK, W), lambda b, c: (b, c, 0)),
                  pl.BlockSpec((1, 16, W), lambda b, c: (b, jnp.maximum(c * 8 - 1, 0), 0)),
                  pl.BlockSpec((1, 16, W), lambda b, c: (b, jnp.minimum((c + 1) * 8, n16 - 1), 0)),
                  full(conv_w), full(conv_b),
                  pl.BlockSpec((1, CHUNK, 16), lambda b, c: (b, c, 0)),
                  pl.BlockSpec((1, 16, CHUNK), lambda b, c: (b, 0, c)),
                  full(bias_row), full(bias_col)],
        out_specs=[pl.BlockSpec((1, CHUNK, hw), lambda b, c: (b, c, 0)),
                   pl.BlockSpec((1, hw, CHUNK), lambda b, c: (b, 0, c)),
                   pl.BlockSpec((1, CHUNK, 16), lambda b, c: (b, c, 0)),
                   pl.BlockSpec((1, 32, CHUNK), lambda b, c: (b, 0, c))],
        out_shape=[jax.ShapeDtypeStruct((B, S, hw), BF16),
                   jax.ShapeDtypeStruct((B, hw, S), BF16),
                   jax.ShapeDtypeStruct((B, S, 16), F32),
                   jax.ShapeDtypeStruct((B, 32, S), F32)],
        compiler_params=_cparams(2),
        name="mlstm_prep",
    )(qk, qk, qk, conv_w, conv_b, gat, gatt, bias_row, bias_col)


def _mlstm_scan_kernel(*refs, n_batch):
    ins, (of_ref, ob_ref, c_scr, m_scr) = refs[:10], refs[10:]
    j = pl.program_id(1)

    @pl.when(j == 0)
    def _():
        c_scr[...] = jnp.zeros_like(c_scr)
        m_scr[...] = jnp.zeros_like(m_scr)

    t_i = lax.broadcasted_iota(I32, (CHUNK, CHUNK), 0)
    s_i = lax.broadcasted_iota(I32, (CHUNK, CHUNK), 1)
    for bb in range(n_batch):
        for d, o_ref in enumerate((of_ref, ob_ref)):
            q_ref, kt_ref, v_ref, col_ref, row_ref = ins[5 * d:5 * d + 5]
            colp = col_ref[bb]
            rowp = row_ref[bb]
            mask = (s_i <= t_i) if d == 0 else (s_i >= t_i)
            outs = []
            for h in range(M_HEADS):
                st = (bb * 2 + d) * M_HEADS + h
                full = (CHUNK, CHUNK)
                bcol = jnp.broadcast_to(colp[:, 8 * d + 4 + h:8 * d + 5 + h], full)
                rmax = bcol + jnp.broadcast_to(colp[:, 8 * d + h:8 * d + h + 1], full)
                rrow = rowp[8 * d + h:8 * d + h + 1, :]
                btot = rowp[8 * d + 4 + h:8 * d + 5 + h, :]
                dmax = rowp[16 + 8 * d + h:17 + 8 * d + h, :]
                m = m_scr[st, 0:1, :]
                inter = bcol + m
                m_t = jnp.maximum(inter, rmax)
                w = jnp.exp(jnp.where(mask, bcol + rrow, -jnp.inf) - m_t)
                a = jnp.exp(inter - m_t)
                qh = q_ref[bb, :, h * LANE:(h + 1) * LANE]
                kth = kt_ref[bb, h * LANE:(h + 1) * LANE, :]
                vh = v_ref[bb, :, h * LANE:(h + 1) * LANE]
                s = _dot(qh, kth) * w
                cst = c_scr[st]
                tot = a * _dot(qh, cst.astype(BF16)) + _dot(s.astype(BF16), vh)
                den = jnp.broadcast_to(tot[:, HEAD:HEAD + 1], full)
                outs.append(tot / jnp.maximum(jnp.abs(den), jnp.exp(-m_t)))
                bm = btot + m
                m_new = jnp.maximum(bm, dmax)
                ws = jnp.exp(btot + rrow - m_new)
                kw = (kth.astype(F32) * ws).astype(BF16)
                c_scr[st] = jnp.exp(bm - m_new) * cst + _dot(kw, vh)
                m_scr[st] = jnp.broadcast_to(m_new, (8, LANE))
            o_ref[bb] = jnp.concatenate(outs, axis=1)


def _mlstm_scan(q, kt, v, colp, rowp, *, n_ctx_chunks):
    B, S, hw = q.shape
    nc = S // CHUNK
    nb = next(n for n in (4, 2, 1) if B % n == 0)

    def rev(j):
        return jnp.where(j < n_ctx_chunks, n_ctx_chunks - 1 - j, nc + n_ctx_chunks - 1 - j)

    def specs(cid):
        return [pl.BlockSpec((nb, CHUNK, hw), lambda b, j: (b, cid(j), 0)),
                pl.BlockSpec((nb, hw, CHUNK), lambda b, j: (b, 0, cid(j))),
                pl.BlockSpec((nb, CHUNK, hw), lambda b, j: (b, cid(j), 0)),
                pl.BlockSpec((nb, CHUNK, 16), lambda b, j: (b, cid(j), 0)),
                pl.BlockSpec((nb, 32, CHUNK), lambda b, j: (b, 0, cid(j)))]

    fwd = lambda j: j
    n_chains = nb * 2 * M_HEADS
    return pl.pallas_call(
        functools.partial(_mlstm_scan_kernel, n_batch=nb),
        grid=(B // nb, nc),
        in_specs=specs(fwd) + specs(rev),
        out_specs=[pl.BlockSpec((nb, CHUNK, hw), lambda b, j: (b, j, 0)),
                   pl.BlockSpec((nb, CHUNK, hw), lambda b, j: (b, rev(j), 0))],
        out_shape=[jax.ShapeDtypeStruct((B, S, hw), F32), jax.ShapeDtypeStruct((B, S, hw), F32)],
        scratch_shapes=[pltpu.VMEM((n_chains, LANE, LANE), F32), pltpu.VMEM((n_chains, 8, LANE), F32)],
        compiler_params=_cparams(2),
        name="mlstm_scan",
    )(q, kt, v, colp, rowp, q, kt, v, colp, rowp)


def _attn_kernel(qt_ref, k_ref, vt_ref, o_ref, s0_scr, s1_scr, p0_scr, p1_scr, al_scr, acc_scr, m_scr, *,
                 n_q, n_kc, sub, qtiles):
    tk = sub * TOK_TILE
    tq = qtiles * TOK_TILE
    n_items = n_q * n_kc
    heads = [slice(hh * LANE, (hh + 1) * LANE) for hh in range(2)]
    vheads = [slice(hh * V_ROWS, (hh + 1) * V_ROWS) for hh in range(2)]
    s_bufs = (s0_scr, s1_scr)
    p_bufs = (p0_scr, p1_scr)

    def split(n):
        if n_q == 1:
            return 0, n
        qi = n // n_kc
        return qi, n - qi * n_kc

    def scores(n, par):
        qi, c = split(n)
        for hh in range(2):
            kc = k_ref[0, pl.ds(pl.multiple_of(c * tk, tk), tk), heads[hh]]
            qt = jnp.concatenate([qt_ref[0, qi * qtiles + j, heads[hh], :] for j in range(qtiles)], axis=1)
            s_bufs[par][hh] = _dot(kc, qt)

    def softmax(n, par):
        _, c = split(n)
        for hh in range(2):
            st = s_bufs[par][hh]
            m = jnp.where(c == 0, -jnp.inf, m_scr[hh])
            m_new = jnp.maximum(m, jnp.max(st, axis=0, keepdims=True))
            p_bufs[par][hh] = jnp.exp2(st - m_new).astype(BF16)
            al_scr[par, hh] = jnp.exp2(m - m_new)
            m_scr[hh] = m_new

    def values(n, par):
        qi, c = split(n)
        for hh in range(2):
            pv = _dot(vt_ref[0, c * sub, vheads[hh], :], p_bufs[par][hh, 0:TOK_TILE])
            for j in range(1, sub):
                pv += _dot(vt_ref[0, c * sub + j, vheads[hh], :],
                           p_bufs[par][hh, j * TOK_TILE:(j + 1) * TOK_TILE])
            acc_scr[hh] = al_scr[par, hh] * acc_scr[hh] + pv

        def finalize():
            o = jnp.concatenate([acc_scr[hh, :HEAD] / acc_scr[hh, HEAD:HEAD + 1] for hh in range(2)], axis=0)
            o_ref[0, pl.ds(pl.multiple_of(qi * tq, tq), tq), :] = o.T.astype(BF16)

        if isinstance(n, int):
            if n % n_kc == n_kc - 1:
                finalize()
        elif n_q > 1:
            pl.when(c == n_kc - 1)(finalize)

    m_scr[...] = jnp.zeros_like(m_scr)
    acc_scr[...] = jnp.zeros_like(acc_scr)
    scores(0, 0)
    if n_items > 1:
        scores(1, 1)
    softmax(0, 0)

    def step(t, par):
        scores(t + 2, par)
        softmax(t + 1, 1 - par)
        values(t, par)

    def pair(i, carry):
        step(2 * i, 0)
        step(2 * i + 1, 1)
        return carry

    n_full = max(n_items - 2, 0)
    lax.fori_loop(0, n_full // 2, pair, 0)
    if n_full % 2:
        step(n_full - 1, (n_full - 1) % 2)
    if n_items > 1:
        softmax(n_items - 1, (n_items - 1) % 2)
        values(n_items - 2, (n_items - 2) % 2)
    values(n_items - 1, (n_items - 1) % 2)


def _attention(qt4, k, vt4, *, n_keys, qtiles, sub, q_per_step=1):
    B, n_qb, kw, _ = qt4.shape
    n_kb = n_keys // TOK_TILE
    tk = sub * TOK_TILE
    tq = qtiles * TOK_TILE
    qb_step = q_per_step * qtiles
    kern = functools.partial(_attn_kernel, n_q=q_per_step, n_kc=n_kb // sub, sub=sub, qtiles=qtiles)
    return pl.pallas_call(
        kern,
        grid=(B, A_HEADS // 2, n_qb // qb_step),
        in_specs=[pl.BlockSpec((1, qb_step, 2 * LANE, TOK_TILE), lambda b, hp, i: (b, i, hp, 0)),
                  pl.BlockSpec((1, n_keys, 2 * LANE), lambda b, hp, i: (b, 0, hp)),
                  pl.BlockSpec((1, n_kb, 2 * V_ROWS, TOK_TILE), lambda b, hp, i: (b, 0, hp, 0))],
        out_specs=pl.BlockSpec((1, qb_step * TOK_TILE, LANE), lambda b, hp, i: (b, i, hp)),
        out_shape=jax.ShapeDtypeStruct((B, n_qb * TOK_TILE, A_HEADS * HEAD), BF16),
        scratch_shapes=[pltpu.VMEM((2, tk, tq), F32), pltpu.VMEM((2, tk, tq), F32),
                        pltpu.VMEM((2, tk, tq), BF16), pltpu.VMEM((2, tk, tq), BF16),
                        pltpu.VMEM((2, 2, 1, tq), F32),
                        pltpu.VMEM((2, V_ROWS, tq), F32), pltpu.VMEM((2, 1, tq), F32)],
        compiler_params=_cparams(3, VMEM_LIMIT),
        name="attention",
    )(qt4, k, vt4)


def _outproj_kernel(yac_ref, yal_ref, hf_ref, hb_ref, o_ref, ycc_ref, ycl_ref, x_ref, mod_ref, wa_ref, wb_ref,
                    wc_ref, mg_ref, blk_ref, n2_ref, rw_ref, xo_ref, hx_ref, aff_ref, *, d_model,
                    n_ctx_tiles):
    D = d_model
    i = pl.program_id(1)
    mod = mod_ref[0]
    g1, sh2, sc2 = mod[:, 2 * D:3 * D], mod[:, 3 * D:4 * D], mod[:, 4 * D:5 * D]
    ya = jnp.where(i < n_ctx_tiles, yac_ref[0], yal_ref[0])
    lane = lax.broadcasted_iota(I32, (1, M_HEADS * LANE), 1)
    h = jnp.where(lane % LANE < HEAD, hf_ref[0] + hb_ref[0], 0.0)
    hi, lo = _split_bf16(h * h)
    ms = _dot(hi, blk_ref[...]) + _dot(lo, blk_ref[...])
    hn = h * lax.rsqrt(ms + EPS) * mg_ref[...]
    yb = (hn * _sigmoid(o_ref[0].astype(F32))).astype(BF16)
    yc = jnp.where(i < n_ctx_tiles, ycc_ref[0], ycl_ref[0])
    mix = _dot(ya, wa_ref[...]) + _dot(yb, wb_ref[...]) + _dot(yc, wc_ref[...])
    x = x_ref[0] + g1 * mix
    xo_ref[0] = x
    hx = _rms(x, n2_ref[...]) * (1.0 + sc2) + sh2
    hx_ref[0] = hx.astype(BF16)
    h_hi, h_lo = _split_bf16(hx)
    r_hi, r_lo = _split_bf16(rw_ref[...])
    logits = _dot(h_hi, r_hi) + _dot(h_lo, r_hi) + _dot(h_hi, r_lo)
    e = jnp.exp(logits - jnp.max(logits, axis=1, keepdims=True))
    aff_ref[0] = e / jnp.sum(e, axis=1, keepdims=True)


def _outproj(ya_ctx, ya_lat, hf, hb, o, yc_ctx, yc_lat, xs, mods_l, wa, wb, wc, mg, blk, n2, rw, *,
             n_ctx_tiles):
    B, S, D = xs.shape
    nt = S // TOK_TILE
    tok = lambda w: pl.BlockSpec((1, TOK_TILE, w), lambda b, i: (b, i, 0))
    full = lambda a: pl.BlockSpec(a.shape, lambda b, i: (0,) * a.ndim)
    mod_spec = pl.BlockSpec((1, 1, 6 * D), lambda b, i: (jnp.where(i < n_ctx_tiles, B, b), 0, 0))
    hw = M_HEADS * LANE
    n_lat_tiles = nt - n_ctx_tiles
    ctx_tok = lambda w: pl.BlockSpec((1, TOK_TILE, w), lambda b, i: (b, jnp.minimum(i, n_ctx_tiles - 1), 0))
    lat_tok = lambda w: pl.BlockSpec(
        (1, TOK_TILE, w), lambda b, i: (b, jnp.clip(i - n_ctx_tiles, 0, n_lat_tiles - 1), 0))
    kern = functools.partial(_outproj_kernel, d_model=D, n_ctx_tiles=n_ctx_tiles)
    return pl.pallas_call(
        kern,
        grid=(B, nt),
        in_specs=[ctx_tok(256), lat_tok(256),
                  tok(hw), tok(hw),
                  tok(hw), ctx_tok(A_HEADS * HEAD), lat_tok(A_HEADS * HEAD), tok(D), mod_spec,
                  full(wa), full(wb), full(wc), full(mg), full(blk), full(n2), full(rw)],
        out_specs=[tok(D), tok(D), tok(N_EXPERTS)],
        out_shape=[jax.ShapeDtypeStruct((B, S, D), F32), jax.ShapeDtypeStruct((B, S, D), BF16),
                   jax.ShapeDtypeStruct((B, S, N_EXPERTS), F32)],
        compiler_params=_cparams(2, VMEM_LIMIT),
        name="outproj",
    )(ya_ctx, ya_lat, hf, hb, o, yc_ctx, yc_lat, xs, mods_l, wa, wb, wc, mg, blk, n2, rw)


def _route_part(a, cap, slot0, tri, pos_ref, base_ref, lane0):
    n = a.shape[1]
    capf = float(cap)
    bits = pltpu.bitcast(a, I32)
    v = jnp.zeros((N_EXPERTS, 1), I32)
    for bit in range(30, -1, -1):
        cand = v | (1 << bit)
        cnt = jnp.sum(jnp.where(bits >= cand, 1.0, 0.0), axis=1, keepdims=True)
        v = jnp.where(cnt >= capf, cand, v)
    gt = bits > v
    eq = bits == v
    need = capf - jnp.sum(jnp.where(gt, 1.0, 0.0), axis=1, keepdims=True)
    idx = lax.broadcasted_iota(I32, (N_EXPERTS, n), 1)
    x = jnp.zeros((N_EXPERTS, 1), I32)
    for bit in range(max(n - 1, 1).bit_length() - 1, -1, -1):
        cand = x | (1 << bit)
        cnt = jnp.sum(jnp.where(eq, jnp.where(idx < cand, 1.0, 0.0), 0.0), axis=1, keepdims=True)
        x = jnp.where(cnt < need, cand, x)
    sel = jnp.where(gt, 1.0, jnp.where(eq, jnp.where(idx <= x, 1.0, 0.0), 0.0))
    running = jnp.zeros((N_EXPERTS, 1), F32)
    for c in range(n // CHUNK):
        blk = sel[:, c * CHUNK:(c + 1) * CHUNK]
        incl = _dot(blk.astype(BF16), tri)
        pos = running + incl - blk + float(slot0)
        cg = lane0 // CHUNK + c
        pos_ref[0, :, lane0 + c * CHUNK:lane0 + (c + 1) * CHUNK] = jnp.where(
            blk > 0.5, pos, -1.0).astype(I32)
        base_ref[0, :, cg:cg + 1] = (running + float(slot0)).astype(I32)
        running = running + incl[:, CHUNK - 1:CHUNK]


def _route_kernel(aff_ref, pos_ref, base_ref, *, n_ctx, cap_ctx, cap_lat):
    r_i = lax.broadcasted_iota(I32, (CHUNK, CHUNK), 0)
    c_i = lax.broadcasted_iota(I32, (CHUNK, CHUNK), 1)
    tri = jnp.where(r_i <= c_i, 1.0, 0.0).astype(BF16)
    base_ref[...] = jnp.zeros_like(base_ref)
    a = aff_ref[0]
    _route_part(a[:, :n_ctx], cap_ctx, 0, tri, pos_ref, base_ref, 0)
    _route_part(a[:, n_ctx:], cap_lat, cap_ctx, tri, pos_ref, base_ref, n_ctx)


def _route(aff_t, *, n_ctx, cap_ctx, cap_lat):
    B, E, S = aff_t.shape
    kern = functools.partial(_route_kernel, n_ctx=n_ctx, cap_ctx=cap_ctx, cap_lat=cap_lat)
    return pl.pallas_call(
        kern,
        grid=(B,),
        in_specs=[pl.BlockSpec((1, E, S), lambda b: (b, 0, 0))],
        out_specs=[pl.BlockSpec((1, E, S), lambda b: (b, 0, 0)),
                   pl.BlockSpec((1, E, LANE), lambda b: (b, 0, 0))],
        out_shape=[jax.ShapeDtypeStruct((B, E, S), I32), jax.ShapeDtypeStruct((B, E, LANE), I32)],
        compiler_params=_cparams(1),
        name="route",
    )(aff_t)


def _window_start(base, align, rows, win):
    w0 = lax.shift_left(lax.shift_right_logical(base, int(math.log2(align))), int(math.log2(align)))
    return pl.multiple_of(jnp.minimum(w0, rows - win), align)


def _gather_kernel(base_ref, h_ref, pos_ref, xe_ref, acc_ref, *, n_chunks, rows, win, unroll):
    b = pl.program_id(0)
    e = pl.program_id(2)
    acc_ref[...] = jnp.zeros_like(acc_ref)

    def group(g, carry):
        picked = []
        for u in range(unroll):
            c = g * unroll + u
            w0 = _window_start(base_ref[b, e, c], GATHER_ALIGN_F32, rows, win)
            posr = pos_ref[0, 0, pl.ds(c, 1), :]
            slot = lax.broadcasted_iota(I32, (win, CHUNK), 0) + w0
            onehot = jnp.where(posr == slot, 1.0, 0.0).astype(BF16)
            hc = h_ref[0, pl.ds(pl.multiple_of(c * CHUNK, CHUNK), CHUNK), :]
            picked.append((w0, _dot(onehot, hc)))
        for w0, rows_c in picked:
            acc_ref[pl.ds(w0, win), :] += rows_c
        return carry

    lax.fori_loop(0, n_chunks // unroll, group, 0)
    xe_ref[0, 0] = acc_ref[...].astype(BF16)


def _gather(hx, pos4, bases, *, rows):
    B, S, D = hx.shape
    nc = S // CHUNK
    dh = D // 2
    win = min(CHUNK + GATHER_ALIGN_F32, rows)
    unroll = next(u for u in (6, 4, 3, 2, 1) if nc % u == 0)
    kern = functools.partial(_gather_kernel, n_chunks=nc, rows=rows, win=win, unroll=unroll)
    return pl.pallas_call(
        kern,
        grid_spec=pltpu.PrefetchScalarGridSpec(
            num_scalar_prefetch=1,
            grid=(B, 2, N_EXPERTS),
            in_specs=[pl.BlockSpec((1, S, dh), lambda b, j, e, bs: (b, 0, j)),
                      pl.BlockSpec((1, 1, nc, CHUNK), lambda b, j, e, bs: (b, e, 0, 0))],
            out_specs=pl.BlockSpec((1, 1, rows, dh), lambda b, j, e, bs: (b, e, 0, j)),
            scratch_shapes=[pltpu.VMEM((rows, dh), F32)]),
        out_shape=jax.ShapeDtypeStruct((B, N_EXPERTS, rows, D), BF16),
        compiler_params=_cparams(3, VMEM_LIMIT),
        name="moe_gather",
    )(bases, hx, pos4)


def _ffn_kernel(x_ref, wg_ref, wu_ref, wd_ref, y_ref, *, row_tile, n_tiles):
    def tile(i, carry):
        r0 = pl.multiple_of(i * row_tile, 16)
        x = x_ref[0, 0, pl.ds(r0, row_tile), :]
        g = _dot(x, wg_ref[0])
        u = _dot(x, wu_ref[0])
        hid = (g * _sigmoid(g) * u).astype(BF16)
        y_ref[0, 0, pl.ds(r0, row_tile), :] = _dot(hid, wd_ref[0]).astype(BF16)
        return carry

    lax.fori_loop(0, n_tiles, tile, 0)


def _ffn_row_tile(rows):
    for t in (352, 256, 176, 128, 96, 64, 32, 16):
        if rows % t == 0:
            return t
    raise ValueError(f"expert slot rows {rows} must be a multiple of 16")


def _ffn(xe, wg, wu, wd):
    B, E, R, D = xe.shape
    F = wg.shape[-1]
    rt = _ffn_row_tile(R)
    kern = functools.partial(_ffn_kernel, row_tile=rt, n_tiles=R // rt)
    return pl.pallas_call(
        kern,
        grid=(E, B),
        in_specs=[pl.BlockSpec((1, 1, R, D), lambda e, b: (b, e, 0, 0)),
                  pl.BlockSpec((1, D, F), lambda e, b: (e, 0, 0)),
                  pl.BlockSpec((1, D, F), lambda e, b: (e, 0, 0)),
                  pl.BlockSpec((1, F, D), lambda e, b: (e, 0, 0))],
        out_specs=pl.BlockSpec((1, 1, R, D), lambda e, b: (b, e, 0, 0)),
        out_shape=jax.ShapeDtypeStruct((B, E, R, D), BF16),
        compiler_params=_cparams(2, VMEM_LIMIT),
        name="moe_ffn",
    )(xe, wg, wu, wd)


def _combine_kernel(base_ref, x_ref, ye_ref, posc_ref, aff_ref, modc_ref, modx_ref, o_ref, *, d_model,
                    blk_chunks, n_ctx_chunks, rows, win):
    D = d_model
    b = pl.program_id(0)
    tb = pl.program_id(1)
    e = pl.program_id(2)

    @pl.when(e == 0)
    def _():
        o_ref[0] = jnp.zeros(o_ref.shape[1:], F32)

    lane = lax.broadcasted_iota(I32, (1, N_EXPERTS), 1)
    for cc in range(blk_chunks):
        rs = slice(cc * CHUNK, (cc + 1) * CHUNK)
        chunk = tb * blk_chunks + cc
        w0 = _window_start(base_ref[b, e, chunk], GATHER_ALIGN_BF16, rows, win)
        pcol = jnp.sum(jnp.where(lane == e, posc_ref[0, rs, :].astype(F32), 0.0), axis=1, keepdims=True)
        gcol = jnp.sum(jnp.where(lane == e, aff_ref[0, rs, :], 0.0), axis=1, keepdims=True)
        slot = (lax.broadcasted_iota(I32, (CHUNK, win), 1) + w0).astype(F32)
        onehot = jnp.where(pcol == slot, 1.0, 0.0).astype(BF16)
        yw = ye_ref[0, 0, pl.ds(w0, win), :]
        o_ref[0, rs, :] += gcol * _dot(onehot, yw)

    @pl.when(e == N_EXPERTS - 1)
    def _():
        g2c = modc_ref[0][:, 5 * D:6 * D]
        g2x = modx_ref[0][:, 5 * D:6 * D]
        for cc in range(blk_chunks):
            rs = slice(cc * CHUNK, (cc + 1) * CHUNK)
            g2 = jnp.where(tb * blk_chunks + cc < n_ctx_chunks, g2c, g2x)
            o_ref[0, rs, :] = x_ref[0, rs, :] + g2 * o_ref[0, rs, :]


def _combine_blk_chunks(nc):
    for k in (11, 8, 6, 4, 3, 2, 1):
        if nc % k == 0:
            return k
    return 1


def _combine(x, ye, posc, aff, mods_l, bases, *, n_ctx):
    B, S, D = x.shape
    R = ye.shape[2]
    nc = S // CHUNK
    k = _combine_blk_chunks(nc)
    tb = k * CHUNK
    win = min(CHUNK + GATHER_ALIGN_BF16, R)
    kern = functools.partial(_combine_kernel, d_model=D, blk_chunks=k, n_ctx_chunks=n_ctx // CHUNK,
                             rows=R, win=win)
    modc_spec = pl.BlockSpec((1, 1, 6 * D), lambda b, t, e, bs: (B, 0, 0))
    modx_spec = pl.BlockSpec((1, 1, 6 * D), lambda b, t, e, bs: (b, 0, 0))
    return pl.pallas_call(
        kern,
        grid_spec=pltpu.PrefetchScalarGridSpec(
            num_scalar_prefetch=1,
            grid=(B, S // tb, N_EXPERTS),
            in_specs=[pl.BlockSpec((1, tb, D), lambda b, t, e, bs: (b, t, 0)),
                      pl.BlockSpec((1, 1, R, D), lambda b, t, e, bs: (b, e, 0, 0)),
                      pl.BlockSpec((1, tb, N_EXPERTS), lambda b, t, e, bs: (b, t, 0)),
                      pl.BlockSpec((1, tb, N_EXPERTS), lambda b, t, e, bs: (b, t, 0)),
                      modc_spec, modx_spec],
            out_specs=pl.BlockSpec((1, tb, D), lambda b, t, e, bs: (b, t, 0))),
        out_shape=jax.ShapeDtypeStruct((B, S, D), F32),
        compiler_params=_cparams(3, VMEM_LIMIT),
        name="moe_combine",
    )(bases, x, ye, posc, aff, mods_l, mods_l)


def _final_kernel(x_ref, g_ref, o_ref):
    o_ref[0] = _rms(x_ref[0], g_ref[...])


def _final_norm(xs, g, *, n_ctx):
    B, S, D = xs.shape
    T = S - n_ctx
    off = n_ctx // TOK_TILE
    return pl.pallas_call(
        _final_kernel,
        grid=(B, T // TOK_TILE),
        in_specs=[pl.BlockSpec((1, TOK_TILE, D), lambda b, i: (b, i + off, 0)),
                  pl.BlockSpec((1, D), lambda b, i: (0, 0))],
        out_specs=pl.BlockSpec((1, TOK_TILE, D), lambda b, i: (b, i, 0)),
        out_shape=jax.ShapeDtypeStruct((B, T, D), F32),
        compiler_params=_cparams(2),
        name="final_norm",
    )(xs, g)


def _pad_heads(w, n_heads):
    lead = w.shape[:-1]
    w = w.reshape(*lead, n_heads, HEAD)
    w = jnp.pad(w, [(0, 0)] * len(lead) + [(0, 0), (0, LANE - HEAD)])
    return w.reshape(*lead, n_heads * LANE)


def _rot_swap(w):
    half = A_ROPE // 2
    return jnp.concatenate([-w[..., half:], w[..., :half]], -1)


def _rope_tables(n_ctx, n_lat):
    rows = n_lat // GRID_W
    row_id = jnp.repeat(jnp.arange(rows, dtype=F32), GRID_W)
    col_id = jnp.tile(jnp.arange(GRID_W, dtype=F32), rows)
    n_freq = A_ROPE // 4
    inv = ROPE_BASE ** (-jnp.arange(n_freq, dtype=F32) / n_freq)
    ang = jnp.concatenate([row_id[:, None] * inv, col_id[:, None] * inv], -1)
    ang = jnp.concatenate([jnp.zeros((n_ctx, A_ROPE // 2), F32), ang], 0)
    S = n_ctx + n_lat
    cos, sin = jnp.cos(ang), jnp.sin(ang)
    pad = jnp.zeros((S, LANE - HEAD - A_ROPE), F32)
    cos128 = jnp.concatenate([jnp.ones((S, HEAD), F32), cos, cos, pad], -1)
    sin128 = jnp.concatenate([jnp.zeros((S, HEAD), F32), sin, sin, pad], -1)
    return cos128, sin128


def kernel(x, c, ctx, c_ctx, ada_w, ada_b, norm1_g, norm2_g, w_in, m_conv_w, m_conv_b, m_ib, m_fb, m_norm_g, a_qnorm_g, a_wq_up, a_kvnorm_g, a_wkv_up, w_out, router_w, e_w_gate, e_w_up, e_w_down, final_g):
    B, T, D = x.shape
    n_ctx = ctx.shape[1]
    L = ada_w.shape[0]
    S = n_ctx + T
    assert D == 16 * HEAD and n_ctx % TOK_TILE == 0 and T % TOK_TILE == 0 and T % (GRID_W * 8) == 0
    n_ctx_tiles = n_ctx // TOK_TILE
    n_ctx_chunks = n_ctx // CHUNK
    cap_ctx = EC_FACTOR * n_ctx // N_EXPERTS
    cap_lat = EC_FACTOR * T // N_EXPERTS
    slot_rows = cap_ctx + cap_lat

    o_f, o_qk, o_v, o_o, o_g, o_cq, o_ckv, o_kr = (0, 256, 768, 1024, 1280, 1296, 1680, 1936)
    w_kr = w_in[:, :, o_kr:o_kr + A_ROPE]
    slot = lambda w: jnp.pad(w, ((0, 0), (0, 0), (HEAD, LANE - HEAD - A_ROPE)))
    w_g = w_in[:, :, o_g:o_g + 16]
    w1 = jnp.concatenate([
        w_in[:, :, o_f:o_f + 256],
        _pad_heads(w_in[:, :, o_qk:o_qk + 256], M_HEADS),
        _pad_heads(w_in[:, :, o_qk + 256:o_qk + 512], M_HEADS),
        _pad_heads(w_in[:, :, o_v:o_v + 256], M_HEADS),
        _pad_heads(w_in[:, :, o_o:o_o + 256], M_HEADS),
        w_in[:, :, o_cq:o_cq + 384],
        w_in[:, :, o_ckv:o_ckv + 256],
        slot(w_kr),
        slot(_rot_swap(w_kr)),
        jnp.pad(w_g, ((0, 0), (0, 0), (0, LANE - 16))),
    ], -1).astype(BF16)
    assert w1.shape[-1] == _N_COLS
    wgt = jnp.swapaxes(w_g, 1, 2).astype(BF16)
    conv_w = jnp.concatenate([_pad_heads(m_conv_w[..., :256], M_HEADS),
                              _pad_heads(m_conv_w[..., 256:], M_HEADS)], -1)
    conv_b = jnp.concatenate([_pad_heads(m_conv_b[..., :256], M_HEADS),
                              _pad_heads(m_conv_b[..., 256:], M_HEADS)], -1)[:, None, :]
    gate_bias = jnp.stack([m_ib, m_fb], 2).reshape(L, 16)
    wq3 = a_wq_up.reshape(L, -1, A_HEADS, HEAD + A_ROPE)
    zq = jnp.zeros(wq3.shape[:-1] + (LANE - HEAD - A_ROPE,), F32)
    wq = jnp.concatenate([wq3, zq], -1).reshape(L, -1, A_HEADS * LANE)
    wqt = jnp.swapaxes(wq, 1, 2).astype(BF16)
    wkv3 = a_wkv_up.reshape(L, -1, A_HEADS, 2 * HEAD)
    zk = jnp.zeros(wkv3.shape[:-1] + (HEAD,), F32)
    wk = jnp.concatenate([wkv3[..., :HEAD], zk], -1).reshape(L, -1, A_HEADS * LANE).astype(BF16)
    zv = jnp.zeros(wkv3.shape[:-1] + (V_ROWS - HEAD,), F32)
    wvt = jnp.swapaxes(jnp.concatenate([wkv3[..., HEAD:], zv], -1).reshape(L, -1, A_HEADS * V_ROWS),
                       1, 2).astype(BF16)
    wa = w_out[:, 0:256].astype(BF16)
    wb = jnp.swapaxes(_pad_heads(jnp.swapaxes(w_out[:, 256:512], 1, 2), M_HEADS), 1, 2).astype(BF16)
    wc = w_out[:, 512:1024].astype(BF16)
    mg = _pad_heads(m_norm_g, M_HEADS)[:, None, :]
    lane = jnp.arange(M_HEADS * LANE)
    blk = jnp.where((lane[:, None] // LANE == lane[None, :] // LANE) & (lane[:, None] % LANE < HEAD),
                    1.0 / HEAD, 0.0).astype(BF16)
    cos128, sin128 = _rope_tables(n_ctx, T)
    row = jnp.arange(LANE)[:, None]
    cost128 = cos128.T
    sint128 = jnp.where(row < HEAD + A_ROPE // 2, -sin128.T, sin128.T)
    cs = _fft_chan_mats()
    attn_sub = 3 if (S // TOK_TILE) % 3 == 0 else 1
    attn_qtiles = 2 if T % (2 * TOK_TILE) == 0 else 1
    wg_e = e_w_gate.astype(BF16)
    wu_e = e_w_up.astype(BF16)
    wd_e = e_w_down.astype(BF16)

    rows16 = 16
    cvec = jnp.zeros((rows16, D), F32).at[:B].set(c).at[B].set(c_ctx)
    mods = _ada_mods(cvec, ada_w, ada_b).reshape(L, rows16, 1, 6 * D)

    xs = jnp.concatenate([ctx, x], axis=1)
    for l in range(L):
        mods_l = mods[l]
        f, qk, vm, og, gat, gatt, qt_ctx, qt_lat, k, vt4 = _inproj(
            xs, mods_l, norm1_g[l][None], w1[l], wgt[l], a_qnorm_g[l][None], a_kvnorm_g[l][None],
            wqt[l], wk[l], wvt[l], cos128, sin128, cost128, sint128,
            n_ctx_tiles=n_ctx_tiles)
        ya_ctx = _fft_ctx(f, n_ctx, cs)
        ya_lat = _fft_latent(f[:, n_ctx:], cs)
        qm, ktm, colp, rowp = _mlstm_prep(qk, gat, gatt, conv_w[l], conv_b[l], gate_bias[l][None, :],
                                          gate_bias[l][:, None], n_ctx_chunks=n_ctx_chunks)
        hf, hb = _mlstm_scan(qm, ktm, vm, colp, rowp, n_ctx_chunks=n_ctx_chunks)
        yc_ctx = _attention(qt_ctx, k, vt4, n_keys=n_ctx, qtiles=1, sub=1)
        yc_lat = _attention(qt_lat, k, vt4, n_keys=S, qtiles=attn_qtiles, sub=attn_sub)
        xs, hx, aff = _outproj(ya_ctx, ya_lat, hf, hb, og, yc_ctx, yc_lat, xs, mods_l, wa[l], wb[l], wc[l],
                               mg[l], blk, norm2_g[l][None], router_w[l], n_ctx_tiles=n_ctx_tiles)
        posm, bases = _route(jnp.swapaxes(aff, 1, 2), n_ctx=n_ctx, cap_ctx=cap_ctx, cap_lat=cap_lat)
        xe = _gather(hx, posm.reshape(B, N_EXPERTS, S // CHUNK, CHUNK), bases, rows=slot_rows)
        ye = _ffn(xe, wg_e[l], wu_e[l], wd_e[l])
        xs = _combine(xs, ye, jnp.swapaxes(posm, 1, 2), aff, mods_l, bases, n_ctx=n_ctx)
    return _final_norm(xs, final_g[None], n_ctx=n_ctx)
```

```python
import functools
import math

import jax
import jax.numpy as jnp
from jax import lax
from jax.experimental import pallas as pl
from jax.experimental.pallas import tpu as pltpu

F32 = jnp.float32
BF16 = jnp.bfloat16
I32 = jnp.int32

EPS = 1e-6
GRID_W = 64
ROPE_BASE = 10000.0
LANE = 128
HEAD = 64
M_HEADS = 4
A_HEADS = 8
A_ROPE = 32
V_ROWS = 80
N_EXPERTS = 16
EC_FACTOR = 2
TOK_TILE = 256
CHUNK = 128
GATHER_ALIGN_BF16 = 16
VMEM_LIMIT = 56 * 1024 * 1024


def _cparams(n_axes, vmem=None):
    return pltpu.CompilerParams(dimension_semantics=("arbitrary",) * n_axes,
                                vmem_limit_bytes=vmem)


def _dot(a, b):
    return jnp.dot(a, b, preferred_element_type=F32)


def _dot_nt(a, b):
    return lax.dot_general(a, b, (((1,), (1,)), ((), ())), preferred_element_type=F32)


def _split_bf16(a):
    hi = a.astype(BF16)
    lo = (a - hi.astype(F32)).astype(BF16)
    return hi, lo


def _sigmoid(x):
    return 1.0 / (1.0 + jnp.exp(-x))


def _rms(x, g):
    return x * lax.rsqrt(jnp.mean(x * x, axis=-1, keepdims=True) + EPS) * g


def _ada_kernel(c_ref, w_ref, b_ref, o_ref):
    a = c_ref[...]
    a = a * _sigmoid(a)
    a_hi, a_lo = _split_bf16(a)
    w_hi, w_lo = _split_bf16(w_ref[0])
    o_ref[0] = _dot(a_hi, w_hi) + _dot(a_lo, w_hi) + _dot(a_hi, w_lo) + b_ref[0]


def _ada_mods(cvec, ada_w, ada_b):
    L, D, D6 = ada_w.shape
    rows = cvec.shape[0]
    return pl.pallas_call(
        _ada_kernel,
        grid=(L, D6 // D),
        in_specs=[pl.BlockSpec((rows, D), lambda l, j: (0, 0)),
                  pl.BlockSpec((1, D, D), lambda l, j: (l, 0, j)),
                  pl.BlockSpec((1, 1, D), lambda l, j: (l, 0, j))],
        out_specs=pl.BlockSpec((1, rows, D), lambda l, j: (l, 0, j)),
        out_shape=jax.ShapeDtypeStruct((L, rows, D6), F32),
        compiler_params=_cparams(2),
        name="ada_mods",
    )(cvec, ada_w, ada_b.reshape(L, 1, D6))


_C_F = (0, 256)
_C_Q = (256, 768)
_C_K = (768, 1280)
_C_V = (1280, 1792)
_C_O = (1792, 2304)
_C_CQ = (2304, 2688)
_C_CKV = (2688, 2944)
_C_KR = (2944, 3072)
_C_KRS = (3072, 3200)
_C_G = (3200, 3328)
_N_COLS = 3328


def _inproj_kernel(x_ref, mod_ref, n1_ref, w1_ref, wgt_ref, qn_ref, kvn_ref, wqt_ref,
                   wk_ref, wvt_ref, cos_ref, sin_ref, cost_ref, sint_ref,
                   f_ref, qk_ref, vm_ref, o_ref, gat_ref, gatt_ref, qtc_ref, qtl_ref, k_ref, vt_ref, *,
                   d_model, q_scale, n_ctx_tiles):
    D = d_model
    i = pl.program_id(1)
    x = x_ref[0]
    mod = mod_ref[0]
    sh1, sc1 = mod[:, 0:D], mod[:, D:2 * D]
    xm = _rms(x, n1_ref[...]) * (1.0 + sc1) + sh1
    xb = xm.astype(BF16)
    u = _dot(xb, w1_ref[...])
    f_ref[0] = u[:, _C_F[0]:_C_F[1]].astype(BF16)
    qk_ref[0] = u[:, _C_Q[0]:_C_K[1]].astype(BF16)
    lane512 = lax.broadcasted_iota(I32, (1, 4 * LANE), 1)
    ones_m = jnp.where(lane512 % LANE == HEAD, 1.0, 0.0)
    vm_ref[0] = (u[:, _C_V[0]:_C_V[1]] + ones_m).astype(BF16)
    o_ref[0] = u[:, _C_O[0]:_C_O[1]].astype(BF16)
    gat_ref[0] = u[:, _C_G[0]:_C_G[0] + 16]
    gatt_ref[0] = _dot_nt(wgt_ref[...], xb)

    cqn = _rms(u[:, _C_CQ[0]:_C_CQ[1]], qn_ref[...]).astype(BF16)
    cost = jnp.tile(cost_ref[...], (A_HEADS, 1))
    sint = jnp.tile(sint_ref[...], (A_HEADS, 1))
    qa = _dot_nt(wqt_ref[...], cqn)
    half = A_ROPE // 2
    parts = []
    for h in range(A_HEADS):
        r0 = h * LANE
        parts += [qa[r0:r0 + HEAD], qa[r0 + HEAD + half:r0 + HEAD + A_ROPE],
                  qa[r0 + HEAD:r0 + HEAD + half], qa[r0 + HEAD + A_ROPE:r0 + LANE]]
    qt = ((qa * cost + jnp.concatenate(parts, axis=0) * sint) * q_scale).astype(BF16)

    @pl.when(i < n_ctx_tiles)
    def _():
        qtc_ref[0, 0] = qt

    @pl.when(i >= n_ctx_tiles)
    def _():
        qtl_ref[0, 0] = qt

    ckvn = _rms(u[:, _C_CKV[0]:_C_CKV[1]], kvn_ref[...]).astype(BF16)
    cosw = jnp.tile(cos_ref[...], (1, A_HEADS))
    sinw = jnp.tile(sin_ref[...], (1, A_HEADS))
    kw = A_HEADS * LANE
    kr = jnp.tile(u[:, _C_KR[0]:_C_KR[1]], (1, A_HEADS))
    krs = jnp.tile(u[:, _C_KRS[0]:_C_KRS[1]], (1, A_HEADS))
    k = _dot(ckvn, wk_ref[...]) + kr * cosw + krs * sinw
    k_ref[0] = k.astype(BF16)
    vrow = lax.broadcasted_iota(I32, (A_HEADS * V_ROWS, 1), 0)
    ones_a = jnp.where(vrow % V_ROWS == HEAD, 1.0, 0.0)
    vt_ref[0, 0] = (_dot_nt(wvt_ref[...], ckvn) + ones_a).astype(BF16)


def _inproj(xs, mods_l, n1, w1, wgt, qn, kvn, wqt, wk, wvt, cos128, sin128, cost128, sint128,
            *, n_ctx_tiles):
    B, S, D = xs.shape
    nt = S // TOK_TILE
    n_ctx = n_ctx_tiles * TOK_TILE
    n_lat_tiles = nt - n_ctx_tiles
    kw = A_HEADS * LANE
    tok = lambda w: pl.BlockSpec((1, TOK_TILE, w), lambda b, i: (b, i, 0))
    full = lambda a: pl.BlockSpec(a.shape, lambda b, i: (0,) * a.ndim)
    mod_spec = pl.BlockSpec((1, 1, 6 * D), lambda b, i: (jnp.where(i < n_ctx_tiles, B, b), 0, 0))
    tab_spec = pl.BlockSpec((TOK_TILE, LANE), lambda b, i: (i, 0))
    tabt_spec = pl.BlockSpec((LANE, TOK_TILE), lambda b, i: (0, i))
    sd = lambda w, dt: jax.ShapeDtypeStruct((B, S, w), dt)
    kern = functools.partial(_inproj_kernel, d_model=D, n_ctx_tiles=n_ctx_tiles,
                             q_scale=(HEAD + A_ROPE) ** -0.5 * math.log2(math.e))
    qtc_spec = pl.BlockSpec((1, 1, kw, TOK_TILE), lambda b, i: (b, jnp.minimum(i, n_ctx_tiles - 1), 0, 0))
    qtl_spec = pl.BlockSpec((1, 1, kw, TOK_TILE),
                            lambda b, i: (b, jnp.clip(i - n_ctx_tiles, 0, n_lat_tiles - 1), 0, 0))
    return pl.pallas_call(
        kern,
        grid=(B, nt),
        in_specs=[tok(D), mod_spec, full(n1), full(w1), full(wgt), full(qn), full(kvn), full(wqt),
                  full(wk), full(wvt), tab_spec, tab_spec, tabt_spec, tabt_spec],
        out_specs=[tok(256), tok(1024), tok(512), tok(512), tok(16),
                   pl.BlockSpec((1, 16, TOK_TILE), lambda b, i: (b, 0, i)),
                   qtc_spec, qtl_spec, tok(kw),
                   pl.BlockSpec((1, 1, A_HEADS * V_ROWS, TOK_TILE), lambda b, i: (b, i, 0, 0))],
        out_shape=[sd(256, BF16), sd(1024, BF16), sd(512, BF16), sd(512, BF16), sd(16, F32),
                   jax.ShapeDtypeStruct((B, 16, S), F32),
                   jax.ShapeDtypeStruct((B, n_ctx_tiles, kw, TOK_TILE), BF16),
                   jax.ShapeDtypeStruct((B, n_lat_tiles, kw, TOK_TILE), BF16),
                   sd(kw, BF16),
                   jax.ShapeDtypeStruct((B, nt, A_HEADS * V_ROWS, TOK_TILE), BF16)],
        compiler_params=_cparams(2, VMEM_LIMIT),
        name="inproj",
    )(xs, mods_l, n1, w1, wgt, qn, kvn, wqt, wk, wvt, cos128, sin128, cost128, sint128)


def _dft_mats(n):
    idx = jnp.arange(n, dtype=F32)
    ang = 2.0 * math.pi * jnp.mod(idx[:, None] * idx[None, :], n) / n
    return jnp.cos(ang), jnp.sin(ang)


def _fft_chan_mats():
    c, s = _dft_mats(HEAD)
    eye = jnp.eye(4, dtype=F32)
    return jnp.concatenate([jnp.kron(eye, c), jnp.kron(eye, s)], 0).astype(BF16)


def _fft_ctx_kernel(f_ref, ft_ref, cs_ref, o_ref, *, n, scale):
    xst = _dot(ft_ref[...], f_ref[0])
    xr = xst[:n].astype(BF16)
    xi = xst[n:].astype(BF16)
    y = _dot(xr, cs_ref[0:256, :]) + _dot(xi, cs_ref[256:512, :])
    o_ref[0] = (y * scale).astype(BF16)


def _fft_ctx(f, n_ctx, cs):
    B = f.shape[0]
    c, s = _dft_mats(n_ctx)
    ft = jnp.concatenate([c, -s], 0).astype(BF16)
    kern = functools.partial(_fft_ctx_kernel, n=n_ctx, scale=(n_ctx * HEAD) ** -0.5)
    return pl.pallas_call(
        kern,
        grid=(B,),
        in_specs=[pl.BlockSpec((1, n_ctx, 256), lambda b: (b, 0, 0)),
                  pl.BlockSpec(ft.shape, lambda b: (0, 0)),
                  pl.BlockSpec(cs.shape, lambda b: (0, 0))],
        out_specs=pl.BlockSpec((1, n_ctx, 256), lambda b: (b, 0, 0)),
        out_shape=jax.ShapeDtypeStruct((B, n_ctx, 256), BF16),
        compiler_params=_cparams(1),
        name="fft_ctx",
    )(f, ft, cs)


def _fft_stage1_kernel(x_ref, f1_ref, cw_ref, sw_ref, o_ref, *, n1):
    z = _dot(f1_ref[...], x_ref[0])
    zr, zi = z[:n1], z[n1:]
    cw, sw = cw_ref[...], sw_ref[...]
    o_ref[0, 0] = (zr * cw + zi * sw).astype(BF16)
    o_ref[0, 1] = (zi * cw - zr * sw).astype(BF16)


def _fft_stage2_kernel(z_ref, f2_ref, cs_ref, o_ref, *, tb, scale):
    for j in range(tb):
        zcat = jnp.concatenate([z_ref[0, 0, j], z_ref[0, 1, j]], axis=0)
        xst = _dot(f2_ref[...], zcat)
        xr = xst[:HEAD].astype(BF16)
        xi = xst[HEAD:].astype(BF16)
        y = _dot(xr, cs_ref[0:256, :]) + _dot(xi, cs_ref[256:512, :])
        o_ref[0, j] = (y * scale).astype(BF16)


def _fft_latent(f_lat, cs):
    B, T, W = f_lat.shape
    n2 = HEAD
    n1 = T // n2
    cols = n2 * W
    cb = 2048
    c1, s1 = _dft_mats(n1)
    f1 = jnp.concatenate([c1, -s1], 0).astype(BF16)
    t1 = jnp.arange(n1, dtype=F32)[:, None]
    s2 = jnp.arange(n2, dtype=F32)[None, :]
    ang = 2.0 * math.pi * (t1 * s2) / T
    cw = jnp.repeat(jnp.cos(ang), W, axis=1)
    sw = jnp.repeat(jnp.sin(ang), W, axis=1)
    z = pl.pallas_call(
        functools.partial(_fft_stage1_kernel, n1=n1),
        grid=(B, cols // cb),
        in_specs=[pl.BlockSpec((1, n1, cb), lambda b, j: (b, 0, j)),
                  pl.BlockSpec(f1.shape, lambda b, j: (0, 0)),
                  pl.BlockSpec((n1, cb), lambda b, j: (0, j)),
                  pl.BlockSpec((n1, cb), lambda b, j: (0, j))],
        out_specs=pl.BlockSpec((1, 2, n1, cb), lambda b, j: (b, 0, 0, j)),
        out_shape=jax.ShapeDtypeStruct((B, 2, n1, cols), BF16),
        compiler_params=_cparams(2),
        name="fft_stage1",
    )(f_lat.reshape(B, n1, cols), f1, cw, sw)
    z = z.reshape(B, 2, n1, n2, W)
    c2, s2m = _dft_mats(n2)
    f2 = jnp.concatenate([jnp.concatenate([c2, s2m], 1),
                          jnp.concatenate([-s2m, c2], 1)], 0).astype(BF16)
    tb = 8
    y = pl.pallas_call(
        functools.partial(_fft_stage2_kernel, tb=tb, scale=(T * HEAD) ** -0.5),
        grid=(B, n1 // tb),
        in_specs=[pl.BlockSpec((1, 2, tb, n2, W), lambda b, j: (b, 0, j, 0, 0)),
                  pl.BlockSpec(f2.shape, lambda b, j: (0, 0)),
                  pl.BlockSpec(cs.shape, lambda b, j: (0, 0))],
        out_specs=pl.BlockSpec((1, tb, n2, W), lambda b, j: (b, j, 0, 0)),
        out_shape=jax.ShapeDtypeStruct((B, n1, n2, W), BF16),
        compiler_params=_cparams(2),
        name="fft_stage2",
    )(z, f2, cs)
    return jnp.transpose(y, (0, 2, 1, 3)).reshape(B, T, W)


def _mlstm_prep_kernel(cur_ref, prev_ref, next_ref, cw_ref, cb_ref, gat_ref, gatt_ref, br_ref, bc_ref,
                       q_ref, kt_ref, col_ref, row_ref, *, n_chunks, n_ctx_chunks):
    c = pl.program_id(1)
    cur = cur_ref[0].astype(F32)
    first = jnp.logical_or(c == 0, c == n_ctx_chunks)
    last = jnp.logical_or(c == n_ctx_chunks - 1, c == n_chunks - 1)
    prev_row = prev_ref[0].astype(F32)[15:16, :]
    next_row = next_ref[0].astype(F32)[0:1, :]
    prev_row = jnp.where(first, 0.0, prev_row)
    next_row = jnp.where(last, 0.0, next_row)
    rows = lax.broadcasted_iota(I32, (CHUNK, 1), 0)
    up = jnp.where(rows == 0, prev_row, pltpu.roll(cur, 1, axis=0))
    dn = jnp.where(rows == CHUNK - 1, next_row, pltpu.roll(cur, CHUNK - 1, axis=0))
    y = cw_ref[0:1, :] * up + cw_ref[1:2, :] * cur + cw_ref[2:3, :] * dn + cb_ref[...]
    y = y * _sigmoid(y)
    hw = M_HEADS * LANE
    q_ref[0] = (y[:, :hw] * HEAD ** -0.5).astype(BF16)
    kt_ref[0] = y[:, hw:].T.astype(BF16)

    def logsig(v):
        return jnp.minimum(v, 0.0) - jnp.log(1.0 + jnp.exp(-jnp.abs(v)))

    r_i = lax.broadcasted_iota(I32, (CHUNK, CHUNK), 0)
    c_i = lax.broadcasted_iota(I32, (CHUNK, CHUNK), 1)
    lower = jnp.where(c_i <= r_i, 1.0, 0.0).astype(BF16)
    upper = jnp.where(c_i >= r_i, 1.0, 0.0).astype(BF16)

    g = gat_ref[0] + br_ref[...]
    lane = lax.broadcasted_iota(I32, (1, 16), 1)
    lg = jnp.where(lane % 8 >= 4, logsig(g), g)
    hi, lo = _split_bf16(lg)
    cum_f = _dot(lower, hi) + _dot(lower, lo)
    cum_b = _dot(upper, hi) + _dot(upper, lo)
    cum_col = jnp.where(lane < 8, cum_f, cum_b)

    gt = gatt_ref[0] + bc_ref[...]
    row = lax.broadcasted_iota(I32, (16, 1), 0)
    lgt = jnp.where(row % 8 >= 4, logsig(gt), gt)
    hit, lot = _split_bf16(lgt)
    cum_ft = _dot(hit, upper) + _dot(lot, upper)
    cum_bt = _dot(hit, lower) + _dot(lot, lower)
    cumt = jnp.where(row < 8, cum_ft, cum_bt)
    cumt_i = pltpu.roll(cumt, 12, axis=0)
    rterm = lgt - cumt_i
    btot = jnp.broadcast_to(jnp.sum(lgt, axis=1, keepdims=True), (16, CHUNK))
    row_ref[0, 0:16] = jnp.where(row % 8 < 4, rterm, btot)
    dmax = jnp.broadcast_to(jnp.max(rterm, axis=1, keepdims=True), (16, CHUNK))
    row_ref[0, 16:32] = pltpu.roll(btot, 12, axis=0) + dmax
    seen_max = jnp.zeros((CHUNK, 16), F32)
    for d in range(2):
        seen = (c_i <= r_i) if d == 0 else (c_i >= r_i)
        for h in range(M_HEADS):
            idx = 8 * d + h
            cmx = jnp.max(jnp.where(seen, rterm[idx:idx + 1, :], -jnp.inf), axis=1, keepdims=True)
            seen_max = jnp.where(lane == idx, cmx, seen_max)
    col_ref[0] = jnp.where(lane % 8 >= 4, cum_col, seen_max)


def _mlstm_prep(qk, gat, gatt, conv_w, conv_b, bias_row, bias_col, *, n_ctx_chunks):
    B, S, W = qk.shape
    nc = S // CHUNK
    n16 = S // 16
    hw = M_HEADS * LANE
    kern = functools.partial(_mlstm_prep_kernel, n_chunks=nc, n_ctx_chunks=n_ctx_chunks)
    full = lambda a: pl.BlockSpec(a.shape, lambda b, c: (0,) * a.ndim)
    return pl.pallas_call(
        kern,
        grid=(B, nc),
        in_specs=[pl.BlockSpec((1, CHUNK, W), lambda b, c: (b, c, 0)),
                  pl.BlockSpec((1, 16, W), lambda b, c: (b, jnp.maximum(c * 8 - 1, 0), 0)),
                  pl.BlockSpec((1, 16, W), lambda b, c: (b, jnp.minimum((c + 1) * 8, n16 - 1), 0)),
                  full(conv_w), full(conv_b),
                  pl.BlockSpec((1, CHUNK, 16), lambda b, c: (b, c, 0)),
                  pl.BlockSpec((1, 16, CHUNK), lambda b, c: (b, 0, c)),
                  full(bias_row), full(bias_col)],
        out_specs=[pl.BlockSpec((1, CHUNK, hw), lambda b, c: (b, c, 0)),
                   pl.BlockSpec((1, hw, CHUNK), lambda b, c: (b, 0, c)),
                   pl.BlockSpec((1, CHUNK, 16), lambda b, c: (b, c, 0)),
                   pl.BlockSpec((1, 32, CHUNK), lambda b, c: (b, 0, c))],
        out_shape=[jax.ShapeDtypeStruct((B, S, hw), BF16),
                   jax.ShapeDtypeStruct((B, hw, S), BF16),
                   jax.ShapeDtypeStruct((B, S, 16), F32),
                   jax.ShapeDtypeStruct((B, 32, S), F32)],
        compiler_params=_cparams(2),
        name="mlstm_prep",
    )(qk, qk, qk, conv_w, conv_b, gat, gatt, bias_row, bias_col)


def _mlstm_scan_kernel(*refs, n_batch):
    ins, (of_ref, ob_ref, c_scr, m_scr) = refs[:10], refs[10:]
    j = pl.program_id(1)

    @pl.when(j == 0)
    def _():
        c_scr[...] = jnp.zeros_like(c_scr)
        m_scr[...] = jnp.zeros_like(m_scr)

    t_i = lax.broadcasted_iota(I32, (CHUNK, CHUNK), 0)
    s_i = lax.broadcasted_iota(I32, (CHUNK, CHUNK), 1)
    for bb in range(n_batch):
        for d, o_ref in enumerate((of_ref, ob_ref)):
            q_ref, kt_ref, v_ref, col_ref, row_ref = ins[5 * d:5 * d + 5]
            colp = col_ref[bb]
            rowp = row_ref[bb]
            mask = (s_i <= t_i) if d == 0 else (s_i >= t_i)
            outs = []
            for h in range(M_HEADS):
                st = (bb * 2 + d) * M_HEADS + h
                full = (CHUNK, CHUNK)
                bcol = jnp.broadcast_to(colp[:, 8 * d + 4 + h:8 * d + 5 + h], full)
                rmax = bcol + jnp.broadcast_to(colp[:, 8 * d + h:8 * d + h + 1], full)
                rrow = rowp[8 * d + h:8 * d + h + 1, :]
                btot = rowp[8 * d + 4 + h:8 * d + 5 + h, :]
                dmax = rowp[16 + 8 * d + h:17 + 8 * d + h, :]
                m = m_scr[st, 0:1, :]
                inter = bcol + m
                m_t = jnp.maximum(inter, rmax)
                w = jnp.exp(jnp.where(mask, bcol + rrow, -jnp.inf) - m_t)
                a = jnp.exp(inter - m_t)
                qh = q_ref[bb, :, h * LANE:(h + 1) * LANE]
                kth = kt_ref[bb, h * LANE:(h + 1) * LANE, :]
                vh = v_ref[bb, :, h * LANE:(h + 1) * LANE]
                s = _dot(qh, kth) * w
                cst = c_scr[st]
                tot = a * _dot(qh, cst.astype(BF16)) + _dot(s.astype(BF16), vh)
                den = jnp.broadcast_to(tot[:, HEAD:HEAD + 1], full)
                outs.append(tot / jnp.maximum(jnp.abs(den), jnp.exp(-m_t)))
                bm = btot + m
                m_new = jnp.maximum(bm, dmax)
                ws = jnp.exp(btot + rrow - m_new)
                kw = (kth.astype(F32) * ws).astype(BF16)
                c_scr[st] = jnp.exp(bm - m_new) * cst + _dot(kw, vh)
                m_scr[st] = jnp.broadcast_to(m_new, (8, LANE))
            o_ref[bb] = jnp.concatenate(outs, axis=1)


def _mlstm_scan(q, kt, v, colp, rowp, *, n_ctx_chunks):
    B, S, hw = q.shape
    nc = S // CHUNK
    nb = next(n for n in (4, 2, 1) if B % n == 0)

    def rev(j):
        return jnp.where(j < n_ctx_chunks, n_ctx_chunks - 1 - j, nc + n_ctx_chunks - 1 - j)

    def specs(cid):
        return [pl.BlockSpec((nb, CHUNK, hw), lambda b, j: (b, cid(j), 0)),
                pl.BlockSpec((nb, hw, CHUNK), lambda b, j: (b, 0, cid(j))),
                pl.BlockSpec((nb, CHUNK, hw), lambda b, j: (b, cid(j), 0)),
                pl.BlockSpec((nb, CHUNK, 16), lambda b, j: (b, cid(j), 0)),
                pl.BlockSpec((nb, 32, CHUNK), lambda b, j: (b, 0, cid(j)))]

    fwd = lambda j: j
    n_chains = nb * 2 * M_HEADS
    return pl.pallas_call(
        functools.partial(_mlstm_scan_kernel, n_batch=nb),
        grid=(B // nb, nc),
        in_specs=specs(fwd) + specs(rev),
        out_specs=[pl.BlockSpec((nb, CHUNK, hw), lambda b, j: (b, j, 0)),
                   pl.BlockSpec((nb, CHUNK, hw), lambda b, j: (b, rev(j), 0))],
        out_shape=[jax.ShapeDtypeStruct((B, S, hw), F32), jax.ShapeDtypeStruct((B, S, hw), F32)],
        scratch_shapes=[pltpu.VMEM((n_chains, LANE, LANE), F32), pltpu.VMEM((n_chains, 8, LANE), F32)],
        compiler_params=_cparams(2),
        name="mlstm_scan",
    )(q, kt, v, colp, rowp, q, kt, v, colp, rowp)


def _attn_kernel(qt_ref, k_ref, vt_ref, o_ref, s0_scr, s1_scr, p0_scr, p1_scr, al_scr, cm_scr, acc_scr,
                 m_scr, *,
                 n_q, n_kc, sub, qtiles):
    tk = sub * TOK_TILE
    tq = qtiles * TOK_TILE
    n_items = n_q * n_kc
    heads = [slice(hh * LANE, (hh + 1) * LANE) for hh in range(2)]
    vheads = [slice(hh * V_ROWS, (hh + 1) * V_ROWS) for hh in range(2)]
    s_bufs = (s0_scr, s1_scr)
    p_bufs = (p0_scr, p1_scr)

    def split(n):
        if n_q == 1:
            return 0, n
        qi = n // n_kc
        return qi, n - qi * n_kc

    def scores(n, par):
        qi, c = split(n)
        for hh in range(2):
            kc = k_ref[0, pl.ds(pl.multiple_of(c * tk, tk), tk), heads[hh]]
            qt = jnp.concatenate([qt_ref[0, qi * qtiles + j, heads[hh], :] for j in range(qtiles)], axis=1)
            st = _dot(kc, qt)
            s_bufs[par][hh] = st
            cm_scr[par, hh] = jnp.max(st, axis=0, keepdims=True)

    def softmax(n, par):
        _, c = split(n)
        for hh in range(2):
            st = s_bufs[par][hh]
            m = jnp.where(c == 0, -jnp.inf, m_scr[hh])
            m_new = jnp.maximum(m, cm_scr[par, hh])
            p_bufs[par][hh] = jnp.exp2(st - m_new).astype(BF16)
            al_scr[par, hh] = jnp.exp2(m - m_new)
            m_scr[hh] = m_new

    def values(n, par):
        qi, c = split(n)
        for hh in range(2):
            pv = _dot(vt_ref[0, c * sub, vheads[hh], :], p_bufs[par][hh, 0:TOK_TILE])
            for j in range(1, sub):
                pv += _dot(vt_ref[0, c * sub + j, vheads[hh], :],
                           p_bufs[par][hh, j * TOK_TILE:(j + 1) * TOK_TILE])
            acc_scr[hh] = al_scr[par, hh] * acc_scr[hh] + pv

        def finalize():
            o = jnp.concatenate([acc_scr[hh, :HEAD] / acc_scr[hh, HEAD:HEAD + 1] for hh in range(2)], axis=0)
            o_ref[0, pl.ds(pl.multiple_of(qi * tq, tq), tq), :] = o.T.astype(BF16)

        if isinstance(n, int):
            if n % n_kc == n_kc - 1:
                finalize()
        elif n_q > 1:
            pl.when(c == n_kc - 1)(finalize)

    m_scr[...] = jnp.zeros_like(m_scr)
    acc_scr[...] = jnp.zeros_like(acc_scr)
    scores(0, 0)
    if n_items > 1:
        scores(1, 1)
    softmax(0, 0)

    def step(t, par):
        scores(t + 2, par)
        softmax(t + 1, 1 - par)
        values(t, par)

    def pair(i, carry):
        step(2 * i, 0)
        step(2 * i + 1, 1)
        return carry

    n_full = max(n_items - 2, 0)
    lax.fori_loop(0, n_full // 2, pair, 0)
    if n_full % 2:
        step(n_full - 1, (n_full - 1) % 2)
    if n_items > 1:
        softmax(n_items - 1, (n_items - 1) % 2)
        values(n_items - 2, (n_items - 2) % 2)
    values(n_items - 1, (n_items - 1) % 2)


def _attention(qt4, k, vt4, *, n_keys, qtiles, sub, q_per_step=1):
    B, n_qb, kw, _ = qt4.shape
    n_kb = n_keys // TOK_TILE
    tk = sub * TOK_TILE
    tq = qtiles * TOK_TILE
    qb_step = q_per_step * qtiles
    kern = functools.partial(_attn_kernel, n_q=q_per_step, n_kc=n_kb // sub, sub=sub, qtiles=qtiles)
    return pl.pallas_call(
        kern,
        grid=(B, A_HEADS // 2, n_qb // qb_step),
        in_specs=[pl.BlockSpec((1, qb_step, 2 * LANE, TOK_TILE), lambda b, hp, i: (b, i, hp, 0)),
                  pl.BlockSpec((1, n_keys, 2 * LANE), lambda b, hp, i: (b, 0, hp)),
                  pl.BlockSpec((1, n_kb, 2 * V_ROWS, TOK_TILE), lambda b, hp, i: (b, 0, hp, 0))],
        out_specs=pl.BlockSpec((1, qb_step * TOK_TILE, LANE), lambda b, hp, i: (b, i, hp)),
        out_shape=jax.ShapeDtypeStruct((B, n_qb * TOK_TILE, A_HEADS * HEAD), BF16),
        scratch_shapes=[pltpu.VMEM((2, tk, tq), F32), pltpu.VMEM((2, tk, tq), F32),
                        pltpu.VMEM((2, tk, tq), BF16), pltpu.VMEM((2, tk, tq), BF16),
                        pltpu.VMEM((2, 2, 1, tq), F32), pltpu.VMEM((2, 2, 1, tq), F32),
                        pltpu.VMEM((2, V_ROWS, tq), F32), pltpu.VMEM((2, 1, tq), F32)],
        compiler_params=_cparams(3, VMEM_LIMIT),
        name="attention",
    )(qt4, k, vt4)


def _outproj_kernel(yac_ref, yal_ref, hf_ref, hb_ref, o_ref, ycc_ref, ycl_ref, x_ref, mod_ref, wa_ref, wb_ref,
                    wc_ref, mg_ref, blk_ref, n2_ref, rw_ref, xo_ref, hx_ref, aff_ref, *, d_model,
                    n_ctx_tiles):
    D = d_model
    i = pl.program_id(1)
    mod = mod_ref[0]
    g1, sh2, sc2 = mod[:, 2 * D:3 * D], mod[:, 3 * D:4 * D], mod[:, 4 * D:5 * D]
    ya = jnp.where(i < n_ctx_tiles, yac_ref[0], yal_ref[0])
    lane = lax.broadcasted_iota(I32, (1, M_HEADS * LANE), 1)
    h = jnp.where(lane % LANE < HEAD, hf_ref[0] + hb_ref[0], 0.0)
    hi, lo = _split_bf16(h * h)
    ms = _dot(hi, blk_ref[...]) + _dot(lo, blk_ref[...])
    hn = h * lax.rsqrt(ms + EPS) * mg_ref[...]
    yb = (hn * _sigmoid(o_ref[0].astype(F32))).astype(BF16)
    yc = jnp.where(i < n_ctx_tiles, ycc_ref[0], ycl_ref[0])
    mix = _dot(ya, wa_ref[...]) + _dot(yb, wb_ref[...]) + _dot(yc, wc_ref[...])
    x = x_ref[0] + g1 * mix
    xo_ref[0] = x
    hx = _rms(x, n2_ref[...]) * (1.0 + sc2) + sh2
    hx_ref[0] = hx.astype(BF16)
    h_hi, h_lo = _split_bf16(hx)
    r_hi, r_lo = _split_bf16(rw_ref[...])
    logits = _dot(h_hi, r_hi) + _dot(h_lo, r_hi) + _dot(h_hi, r_lo)
    e = jnp.exp(logits - jnp.max(logits, axis=1, keepdims=True))
    aff_ref[0] = e / jnp.sum(e, axis=1, keepdims=True)


def _outproj(ya_ctx, ya_lat, hf, hb, o, yc_ctx, yc_lat, xs, mods_l, wa, wb, wc, mg, blk, n2, rw, *,
             n_ctx_tiles):
    B, S, D = xs.shape
    nt = S // TOK_TILE
    tok = lambda w: pl.BlockSpec((1, TOK_TILE, w), lambda b, i: (b, i, 0))
    full = lambda a: pl.BlockSpec(a.shape, lambda b, i: (0,) * a.ndim)
    mod_spec = pl.BlockSpec((1, 1, 6 * D), lambda b, i: (jnp.where(i < n_ctx_tiles, B, b), 0, 0))
    hw = M_HEADS * LANE
    n_lat_tiles = nt - n_ctx_tiles
    ctx_tok = lambda w: pl.BlockSpec((1, TOK_TILE, w), lambda b, i: (b, jnp.minimum(i, n_ctx_tiles - 1), 0))
    lat_tok = lambda w: pl.BlockSpec(
        (1, TOK_TILE, w), lambda b, i: (b, jnp.clip(i - n_ctx_tiles, 0, n_lat_tiles - 1), 0))
    kern = functools.partial(_outproj_kernel, d_model=D, n_ctx_tiles=n_ctx_tiles)
    return pl.pallas_call(
        kern,
        grid=(B, nt),
        in_specs=[ctx_tok(256), lat_tok(256),
                  tok(hw), tok(hw),
                  tok(hw), ctx_tok(A_HEADS * HEAD), lat_tok(A_HEADS * HEAD), tok(D), mod_spec,
                  full(wa), full(wb), full(wc), full(mg), full(blk), full(n2), full(rw)],
        out_specs=[tok(D), tok(D), tok(N_EXPERTS)],
        out_shape=[jax.ShapeDtypeStruct((B, S, D), F32), jax.ShapeDtypeStruct((B, S, D), BF16),
                   jax.ShapeDtypeStruct((B, S, N_EXPERTS), F32)],
        compiler_params=_cparams(2, VMEM_LIMIT),
        name="outproj",
    )(ya_ctx, ya_lat, hf, hb, o, yc_ctx, yc_lat, xs, mods_l, wa, wb, wc, mg, blk, n2, rw)


def _route_part(a, cap, slot0, tri, pos_ref, base_ref, lane0):
    n = a.shape[1]
    capf = float(cap)
    bits = pltpu.bitcast(a, I32)
    v = jnp.zeros((N_EXPERTS, 1), I32)
    for bit in range(30, -1, -1):
        cand = v | (1 << bit)
        cnt = jnp.sum(jnp.where(bits >= cand, 1.0, 0.0), axis=1, keepdims=True)
        v = jnp.where(cnt >= capf, cand, v)
    gt = bits > v
    eq = bits == v
    need = capf - jnp.sum(jnp.where(gt, 1.0, 0.0), axis=1, keepdims=True)
    idx = lax.broadcasted_iota(I32, (N_EXPERTS, n), 1)
    x = jnp.zeros((N_EXPERTS, 1), I32)
    for bit in range(max(n - 1, 1).bit_length() - 1, -1, -1):
        cand = x | (1 << bit)
        cnt = jnp.sum(jnp.where(eq, jnp.where(idx < cand, 1.0, 0.0), 0.0), axis=1, keepdims=True)
        x = jnp.where(cnt < need, cand, x)
    sel = jnp.where(gt, 1.0, jnp.where(eq, jnp.where(idx <= x, 1.0, 0.0), 0.0))
    running = jnp.zeros((N_EXPERTS, 1), F32)
    for c in range(n // CHUNK):
        blk = sel[:, c * CHUNK:(c + 1) * CHUNK]
        incl = _dot(blk.astype(BF16), tri)
        pos = running + incl - blk + float(slot0)
        cg = lane0 // CHUNK + c
        pos_ref[0, :, lane0 + c * CHUNK:lane0 + (c + 1) * CHUNK] = jnp.where(
            blk > 0.5, pos, -1.0).astype(I32)
        base_ref[0, :, cg:cg + 1] = (running + float(slot0)).astype(I32)
        running = running + incl[:, CHUNK - 1:CHUNK]


def _route_kernel(aff_ref, pos_ref, base_ref, *, n_ctx, cap_ctx, cap_lat):
    r_i = lax.broadcasted_iota(I32, (CHUNK, CHUNK), 0)
    c_i = lax.broadcasted_iota(I32, (CHUNK, CHUNK), 1)
    tri = jnp.where(r_i <= c_i, 1.0, 0.0).astype(BF16)
    base_ref[...] = jnp.zeros_like(base_ref)
    a = aff_ref[0]
    _route_part(a[:, :n_ctx], cap_ctx, 0, tri, pos_ref, base_ref, 0)
    _route_part(a[:, n_ctx:], cap_lat, cap_ctx, tri, pos_ref, base_ref, n_ctx)


def _route(aff_t, *, n_ctx, cap_ctx, cap_lat):
    B, E, S = aff_t.shape
    kern = functools.partial(_route_kernel, n_ctx=n_ctx, cap_ctx=cap_ctx, cap_lat=cap_lat)
    return pl.pallas_call(
        kern,
        grid=(B,),
        in_specs=[pl.BlockSpec((1, E, S), lambda b: (b, 0, 0))],
        out_specs=[pl.BlockSpec((1, E, S), lambda b: (b, 0, 0)),
                   pl.BlockSpec((1, E, LANE), lambda b: (b, 0, 0))],
        out_shape=[jax.ShapeDtypeStruct((B, E, S), I32), jax.ShapeDtypeStruct((B, E, LANE), I32)],
        compiler_params=_cparams(1),
        name="route",
    )(aff_t)


def _window_start(base, align, rows, win):
    w0 = lax.shift_left(lax.shift_right_logical(base, int(math.log2(align))), int(math.log2(align)))
    return pl.multiple_of(jnp.minimum(w0, rows - win), align)


def _gather_kernel(base_ref, h_ref, pos_ref, xe_ref, *, n_chunks, rows, win, unroll):
    b = pl.program_id(0)
    e = pl.program_id(2)
    xe_ref[0, 0] = jnp.zeros(xe_ref.shape[2:], BF16)

    def group(g, carry):
        picked = []
        for u in range(unroll):
            c = g * unroll + u
            w0 = _window_start(base_ref[b, e, c], GATHER_ALIGN_BF16, rows, win)
            posr = pos_ref[0, 0, pl.ds(c, 1), :]
            slot = lax.broadcasted_iota(I32, (win, CHUNK), 0) + w0
            onehot = jnp.where(posr == slot, 1.0, 0.0).astype(BF16)
            hc = h_ref[0, pl.ds(pl.multiple_of(c * CHUNK, CHUNK), CHUNK), :]
            picked.append((w0, _dot(onehot, hc).astype(BF16)))
        for w0, rows_c in picked:
            xe_ref[0, 0, pl.ds(w0, win), :] += rows_c
        return carry

    lax.fori_loop(0, n_chunks // unroll, group, 0)


def _gather(hx, pos4, bases, *, rows):
    B, S, D = hx.shape
    nc = S // CHUNK
    dh = D // 2
    win = min(CHUNK + GATHER_ALIGN_BF16, rows)
    unroll = next(u for u in (6, 4, 3, 2, 1) if nc % u == 0)
    kern = functools.partial(_gather_kernel, n_chunks=nc, rows=rows, win=win, unroll=unroll)
    return pl.pallas_call(
        kern,
        grid_spec=pltpu.PrefetchScalarGridSpec(
            num_scalar_prefetch=1,
            grid=(B, 2, N_EXPERTS),
            in_specs=[pl.BlockSpec((1, S, dh), lambda b, j, e, bs: (b, 0, j)),
                      pl.BlockSpec((1, 1, nc, CHUNK), lambda b, j, e, bs: (b, e, 0, 0))],
            out_specs=pl.BlockSpec((1, 1, rows, dh), lambda b, j, e, bs: (b, e, 0, j))),
        out_shape=jax.ShapeDtypeStruct((B, N_EXPERTS, rows, D), BF16),
        compiler_params=_cparams(3, VMEM_LIMIT),
        name="moe_gather",
    )(bases, hx, pos4)


def _ffn_kernel(x_ref, wg_ref, wu_ref, wd_ref, y_ref, *, row_tile, n_tiles):
    def tile(i, carry):
        r0 = pl.multiple_of(i * row_tile, 16)
        x = x_ref[0, 0, pl.ds(r0, row_tile), :]
        g = _dot(x, wg_ref[0])
        u = _dot(x, wu_ref[0])
        hid = (g * _sigmoid(g) * u).astype(BF16)
        y_ref[0, 0, pl.ds(r0, row_tile), :] = _dot(hid, wd_ref[0]).astype(BF16)
        return carry

    lax.fori_loop(0, n_tiles, tile, 0)


def _ffn_row_tile(rows):
    for t in (352, 256, 176, 128, 96, 64, 32, 16):
        if rows % t == 0:
            return t
    raise ValueError(f"expert slot rows {rows} must be a multiple of 16")


def _ffn(xe, wg, wu, wd):
    B, E, R, D = xe.shape
    F = wg.shape[-1]
    rt = _ffn_row_tile(R)
    kern = functools.partial(_ffn_kernel, row_tile=rt, n_tiles=R // rt)
    return pl.pallas_call(
        kern,
        grid=(E, B),
        in_specs=[pl.BlockSpec((1, 1, R, D), lambda e, b: (b, e, 0, 0)),
                  pl.BlockSpec((1, D, F), lambda e, b: (e, 0, 0)),
                  pl.BlockSpec((1, D, F), lambda e, b: (e, 0, 0)),
                  pl.BlockSpec((1, F, D), lambda e, b: (e, 0, 0))],
        out_specs=pl.BlockSpec((1, 1, R, D), lambda e, b: (b, e, 0, 0)),
        out_shape=jax.ShapeDtypeStruct((B, E, R, D), BF16),
        compiler_params=_cparams(2, VMEM_LIMIT),
        name="moe_ffn",
    )(xe, wg, wu, wd)


def _combine_kernel(base_ref, x_ref, ye_ref, posc_ref, aff_ref, modc_ref, modx_ref, o_ref, *, d_model,
                    blk_chunks, n_ctx_chunks, rows, win):
    D = d_model
    b = pl.program_id(0)
    tb = pl.program_id(1)
    e = pl.program_id(2)

    @pl.when(e == 0)
    def _():
        o_ref[0] = jnp.zeros(o_ref.shape[1:], F32)

    lane = lax.broadcasted_iota(I32, (1, N_EXPERTS), 1)
    for cc in range(blk_chunks):
        rs = slice(cc * CHUNK, (cc + 1) * CHUNK)
        chunk = tb * blk_chunks + cc
        w0 = _window_start(base_ref[b, e, chunk], GATHER_ALIGN_BF16, rows, win)
        pcol = jnp.sum(jnp.where(lane == e, posc_ref[0, rs, :].astype(F32), 0.0), axis=1, keepdims=True)
        gcol = jnp.sum(jnp.where(lane == e, aff_ref[0, rs, :], 0.0), axis=1, keepdims=True)
        slot = (lax.broadcasted_iota(I32, (CHUNK, win), 1) + w0).astype(F32)
        onehot = jnp.where(pcol == slot, 1.0, 0.0).astype(BF16)
        yw = ye_ref[0, 0, pl.ds(w0, win), :]
        o_ref[0, rs, :] += gcol * _dot(onehot, yw)

    @pl.when(e == N_EXPERTS - 1)
    def _():
        g2c = modc_ref[0][:, 5 * D:6 * D]
        g2x = modx_ref[0][:, 5 * D:6 * D]
        for cc in range(blk_chunks):
            rs = slice(cc * CHUNK, (cc + 1) * CHUNK)
            g2 = jnp.where(tb * blk_chunks + cc < n_ctx_chunks, g2c, g2x)
            o_ref[0, rs, :] = x_ref[0, rs, :] + g2 * o_ref[0, rs, :]


def _combine_blk_chunks(nc):
    for k in (11, 8, 6, 4, 3, 2, 1):
        if nc % k == 0:
            return k
    return 1


def _combine(x, ye, posc, aff, mods_l, bases, *, n_ctx):
    B, S, D = x.shape
    R = ye.shape[2]
    nc = S // CHUNK
    k = _combine_blk_chunks(nc)
    tb = k * CHUNK
    win = min(CHUNK + GATHER_ALIGN_BF16, R)
    kern = functools.partial(_combine_kernel, d_model=D, blk_chunks=k, n_ctx_chunks=n_ctx // CHUNK,
                             rows=R, win=win)
    modc_spec = pl.BlockSpec((1, 1, 6 * D), lambda b, t, e, bs: (B, 0, 0))
    modx_spec = pl.BlockSpec((1, 1, 6 * D), lambda b, t, e, bs: (b, 0, 0))
    return pl.pallas_call(
        kern,
        grid_spec=pltpu.PrefetchScalarGridSpec(
            num_scalar_prefetch=1,
            grid=(B, S // tb, N_EXPERTS),
            in_specs=[pl.BlockSpec((1, tb, D), lambda b, t, e, bs: (b, t, 0)),
                      pl.BlockSpec((1, 1, R, D), lambda b, t, e, bs: (b, e, 0, 0)),
                      pl.BlockSpec((1, tb, N_EXPERTS), lambda b, t, e, bs: (b, t, 0)),
                      pl.BlockSpec((1, tb, N_EXPERTS), lambda b, t, e, bs: (b, t, 0)),
                      modc_spec, modx_spec],
            out_specs=pl.BlockSpec((1, tb, D), lambda b, t, e, bs: (b, t, 0))),
        out_shape=jax.ShapeDtypeStruct((B, S, D), F32),
        compiler_params=_cparams(3, VMEM_LIMIT),
        name="moe_combine",
    )(bases, x, ye, posc, aff, mods_l, mods_l)


def _final_kernel(x_ref, g_ref, o_ref):
    o_ref[0] = _rms(x_ref[0], g_ref[...])


def _final_norm(xs, g, *, n_ctx):
    B, S, D = xs.shape
    T = S - n_ctx
    off = n_ctx // TOK_TILE
    return pl.pallas_call(
        _final_kernel,
        grid=(B, T // TOK_TILE),
        in_specs=[pl.BlockSpec((1, TOK_TILE, D), lambda b, i: (b, i + off, 0)),
                  pl.BlockSpec((1, D), lambda b, i: (0, 0))],
        out_specs=pl.BlockSpec((1, TOK_TILE, D), lambda b, i: (b, i, 0)),
        out_shape=jax.ShapeDtypeStruct((B, T, D), F32),
        compiler_params=_cparams(2),
        name="final_norm",
    )(xs, g)


def _pad_heads(w, n_heads):
    lead = w.shape[:-1]
    w = w.reshape(*lead, n_heads, HEAD)
    w = jnp.pad(w, [(0, 0)] * len(lead) + [(0, 0), (0, LANE - HEAD)])
    return w.reshape(*lead, n_heads * LANE)


def _rot_swap(w):
    half = A_ROPE // 2
    return jnp.concatenate([-w[..., half:], w[..., :half]], -1)


def _rope_tables(n_ctx, n_lat):
    rows = n_lat // GRID_W
    row_id = jnp.repeat(jnp.arange(rows, dtype=F32), GRID_W)
    col_id = jnp.tile(jnp.arange(GRID_W, dtype=F32), rows)
    n_freq = A_ROPE // 4
    inv = ROPE_BASE ** (-jnp.arange(n_freq, dtype=F32) / n_freq)
    ang = jnp.concatenate([row_id[:, None] * inv, col_id[:, None] * inv], -1)
    ang = jnp.concatenate([jnp.zeros((n_ctx, A_ROPE // 2), F32), ang], 0)
    S = n_ctx + n_lat
    cos, sin = jnp.cos(ang), jnp.sin(ang)
    pad = jnp.zeros((S, LANE - HEAD - A_ROPE), F32)
    cos128 = jnp.concatenate([jnp.ones((S, HEAD), F32), cos, cos, pad], -1)
    sin128 = jnp.concatenate([jnp.zeros((S, HEAD), F32), sin, sin, pad], -1)
    return cos128, sin128


def kernel(x, c, ctx, c_ctx, ada_w, ada_b, norm1_g, norm2_g, w_in, m_conv_w, m_conv_b, m_ib, m_fb, m_norm_g, a_qnorm_g, a_wq_up, a_kvnorm_g, a_wkv_up, w_out, router_w, e_w_gate, e_w_up, e_w_down, final_g):
    B, T, D = x.shape
    n_ctx = ctx.shape[1]
    L = ada_w.shape[0]
    S = n_ctx + T
    assert D == 16 * HEAD and n_ctx % TOK_TILE == 0 and T % TOK_TILE == 0 and T % (GRID_W * 8) == 0
    n_ctx_tiles = n_ctx // TOK_TILE
    n_ctx_chunks = n_ctx // CHUNK
    cap_ctx = EC_FACTOR * n_ctx // N_EXPERTS
    cap_lat = EC_FACTOR * T // N_EXPERTS
    slot_rows = cap_ctx + cap_lat

    o_f, o_qk, o_v, o_o, o_g, o_cq, o_ckv, o_kr = (0, 256, 768, 1024, 1280, 1296, 1680, 1936)
    w_kr = w_in[:, :, o_kr:o_kr + A_ROPE]
    slot = lambda w: jnp.pad(w, ((0, 0), (0, 0), (HEAD, LANE - HEAD - A_ROPE)))
    w_g = w_in[:, :, o_g:o_g + 16]
    w1 = jnp.concatenate([
        w_in[:, :, o_f:o_f + 256],
        _pad_heads(w_in[:, :, o_qk:o_qk + 256], M_HEADS),
        _pad_heads(w_in[:, :, o_qk + 256:o_qk + 512], M_HEADS),
        _pad_heads(w_in[:, :, o_v:o_v + 256], M_HEADS),
        _pad_heads(w_in[:, :, o_o:o_o + 256], M_HEADS),
        w_in[:, :, o_cq:o_cq + 384],
        w_in[:, :, o_ckv:o_ckv + 256],
        slot(w_kr),
        slot(_rot_swap(w_kr)),
        jnp.pad(w_g, ((0, 0), (0, 0), (0, LANE - 16))),
    ], -1).astype(BF16)
    assert w1.shape[-1] == _N_COLS
    wgt = jnp.swapaxes(w_g, 1, 2).astype(BF16)
    conv_w = jnp.concatenate([_pad_heads(m_conv_w[..., :256], M_HEADS),
                              _pad_heads(m_conv_w[..., 256:], M_HEADS)], -1)
    conv_b = jnp.concatenate([_pad_heads(m_conv_b[..., :256], M_HEADS),
                              _pad_heads(m_conv_b[..., 256:], M_HEADS)], -1)[:, None, :]
    gate_bias = jnp.stack([m_ib, m_fb], 2).reshape(L, 16)
    wq3 = a_wq_up.reshape(L, -1, A_HEADS, HEAD + A_ROPE)
    zq = jnp.zeros(wq3.shape[:-1] + (LANE - HEAD - A_ROPE,), F32)
    wq = jnp.concatenate([wq3, zq], -1).reshape(L, -1, A_HEADS * LANE)
    wqt = jnp.swapaxes(wq, 1, 2).astype(BF16)
    wkv3 = a_wkv_up.reshape(L, -1, A_HEADS, 2 * HEAD)
    zk = jnp.zeros(wkv3.shape[:-1] + (HEAD,), F32)
    wk = jnp.concatenate([wkv3[..., :HEAD], zk], -1).reshape(L, -1, A_HEADS * LANE).astype(BF16)
    zv = jnp.zeros(wkv3.shape[:-1] + (V_ROWS - HEAD,), F32)
    wvt = jnp.swapaxes(jnp.concatenate([wkv3[..., HEAD:], zv], -1).reshape(L, -1, A_HEADS * V_ROWS),
                       1, 2).astype(BF16)
    wa = w_out[:, 0:256].astype(BF16)
    wb = jnp.swapaxes(_pad_heads(jnp.swapaxes(w_out[:, 256:512], 1, 2), M_HEADS), 1, 2).astype(BF16)
    wc = w_out[:, 512:1024].astype(BF16)
    mg = _pad_heads(m_norm_g, M_HEADS)[:, None, :]
    lane = jnp.arange(M_HEADS * LANE)
    blk = jnp.where((lane[:, None] // LANE == lane[None, :] // LANE) & (lane[:, None] % LANE < HEAD),
                    1.0 / HEAD, 0.0).astype(BF16)
    cos128, sin128 = _rope_tables(n_ctx, T)
    row = jnp.arange(LANE)[:, None]
    cost128 = cos128.T
    sint128 = jnp.where(row < HEAD + A_ROPE // 2, -sin128.T, sin128.T)
    cs = _fft_chan_mats()
    attn_sub = 3 if (S // TOK_TILE) % 3 == 0 else 1
    attn_qtiles = 4 if T % (4 * TOK_TILE) == 0 else 1
    wg_e = e_w_gate.astype(BF16)
    wu_e = e_w_up.astype(BF16)
    wd_e = e_w_down.astype(BF16)

    rows16 = 16
    cvec = jnp.zeros((rows16, D), F32).at[:B].set(c).at[B].set(c_ctx)
    mods = _ada_mods(cvec, ada_w, ada_b).reshape(L, rows16, 1, 6 * D)

    xs = jnp.concatenate([ctx, x], axis=1)
    for l in range(L):
        mods_l = mods[l]
        f, qk, vm, og, gat, gatt, qt_ctx, qt_lat, k, vt4 = _inproj(
            xs, mods_l, norm1_g[l][None], w1[l], wgt[l], a_qnorm_g[l][None], a_kvnorm_g[l][None],
            wqt[l], wk[l], wvt[l], cos128, sin128, cost128, sint128,
            n_ctx_tiles=n_ctx_tiles)
        ya_ctx = _fft_ctx(f, n_ctx, cs)
        ya_lat = _fft_latent(f[:, n_ctx:], cs)
        qm, ktm, colp, rowp = _mlstm_prep(qk, gat, gatt, conv_w[l], conv_b[l], gate_bias[l][None, :],
                                          gate_bias[l][:, None], n_ctx_chunks=n_ctx_chunks)
        hf, hb = _mlstm_scan(qm, ktm, vm, colp, rowp, n_ctx_chunks=n_ctx_chunks)
        yc_ctx = _attention(qt_ctx, k, vt4, n_keys=n_ctx, qtiles=1, sub=1)
        yc_lat = _attention(qt_lat, k, vt4, n_keys=S, qtiles=attn_qtiles, sub=attn_sub)
        xs, hx, aff = _outproj(ya_ctx, ya_lat, hf, hb, og, yc_ctx, yc_lat, xs, mods_l, wa[l], wb[l], wc[l],
                               mg[l], blk, norm2_g[l][None], router_w[l], n_ctx_tiles=n_ctx_tiles)
        posm, bases = _route(jnp.swapaxes(aff, 1, 2), n_ctx=n_ctx, cap_ctx=cap_ctx, cap_lat=cap_lat)
        xe = _gather(hx, posm.reshape(B, N_EXPERTS, S // CHUNK, CHUNK), bases, rows=slot_rows)
        ye = _ffn(xe, wg_e[l], wu_e[l], wd_e[l])
        xs = _combine(xs, ye, jnp.swapaxes(posm, 1, 2), aff, mods_l, bases, n_ctx=n_ctx)
    return _final_norm(xs, final_g[None], n_ctx=n_ctx)
```

```python
import functools
import math

import jax
import jax.numpy as jnp
from jax import lax
from jax.experimental import pallas as pl
from jax.experimental.pallas import tpu as pltpu

F32 = jnp.float32
BF16 = jnp.bfloat16
I32 = jnp.int32

EPS = 1e-6
GRID_W = 64
ROPE_BASE = 10000.0
LANE = 128
HEAD = 64
M_HEADS = 4
A_HEADS = 8
A_ROPE = 32
V_ROWS = 80
N_EXPERTS = 16
EC_FACTOR = 2
TOK_TILE = 256
CHUNK = 128
GATHER_ALIGN_BF16 = 16
SMALL_FILL = 32
VMEM_LIMIT = 56 * 1024 * 1024


def _cparams(n_axes, vmem=None):
    return pltpu.CompilerParams(dimension_semantics=("arbitrary",) * n_axes,
                                vmem_limit_bytes=vmem)


def _dot(a, b):
    return jnp.dot(a, b, preferred_element_type=F32)


def _dot_nt(a, b):
    return lax.dot_general(a, b, (((1,), (1,)), ((), ())), preferred_element_type=F32)


def _split_bf16(a):
    hi = a.astype(BF16)
    lo = (a - hi.astype(F32)).astype(BF16)
    return hi, lo


def _sigmoid(x):
    return 1.0 / (1.0 + jnp.exp(-x))


def _rms(x, g):
    return x * lax.rsqrt(jnp.mean(x * x, axis=-1, keepdims=True) + EPS) * g


def _ada_kernel(c_ref, w_ref, b_ref, o_ref):
    a = c_ref[...]
    a = a * _sigmoid(a)
    a_hi, a_lo = _split_bf16(a)
    w_hi, w_lo = _split_bf16(w_ref[0])
    o_ref[0] = _dot(a_hi, w_hi) + _dot(a_lo, w_hi) + _dot(a_hi, w_lo) + b_ref[0]


def _ada_mods(cvec, ada_w, ada_b):
    L, D, D6 = ada_w.shape
    rows = cvec.shape[0]
    return pl.pallas_call(
        _ada_kernel,
        grid=(L, D6 // D),
        in_specs=[pl.BlockSpec((rows, D), lambda l, j: (0, 0)),
                  pl.BlockSpec((1, D, D), lambda l, j: (l, 0, j)),
                  pl.BlockSpec((1, 1, D), lambda l, j: (l, 0, j))],
        out_specs=pl.BlockSpec((1, rows, D), lambda l, j: (l, 0, j)),
        out_shape=jax.ShapeDtypeStruct((L, rows, D6), F32),
        compiler_params=_cparams(2),
        name="ada_mods",
    )(cvec, ada_w, ada_b.reshape(L, 1, D6))


_C_F = (0, 256)
_C_Q = (256, 768)
_C_K = (768, 1280)
_C_V = (1280, 1792)
_C_O = (1792, 2304)
_C_CQ = (2304, 2688)
_C_CKV = (2688, 2944)
_C_KR = (2944, 3072)
_C_KRS = (3072, 3200)
_C_G = (3200, 3328)
_N_COLS = 3328


def _inproj_kernel(x_ref, mod_ref, n1_ref, w1_ref, wgt_ref, qn_ref, kvn_ref, wqt_ref,
                   wk_ref, wvt_ref, cos_ref, sin_ref, cost_ref, sint_ref,
                   f_ref, qk_ref, vm_ref, o_ref, gat_ref, gatt_ref, qtc_ref, qtl_ref, k_ref, vt_ref, *,
                   d_model, q_scale, n_ctx_tiles):
    D = d_model
    i = pl.program_id(1)
    x = x_ref[0]
    mod = mod_ref[0]
    sh1, sc1 = mod[:, 0:D], mod[:, D:2 * D]
    xm = _rms(x, n1_ref[...]) * (1.0 + sc1) + sh1
    xb = xm.astype(BF16)
    u = _dot(xb, w1_ref[...])
    f_ref[0] = u[:, _C_F[0]:_C_F[1]].astype(BF16)
    qk_ref[0] = u[:, _C_Q[0]:_C_K[1]].astype(BF16)
    lane512 = lax.broadcasted_iota(I32, (1, 4 * LANE), 1)
    ones_m = jnp.where(lane512 % LANE == HEAD, 1.0, 0.0)
    vm_ref[0] = (u[:, _C_V[0]:_C_V[1]] + ones_m).astype(BF16)
    o_ref[0] = u[:, _C_O[0]:_C_O[1]].astype(BF16)
    gat_ref[0] = u[:, _C_G[0]:_C_G[0] + 16]
    gatt_ref[0] = _dot_nt(wgt_ref[...], xb)

    cqn = _rms(u[:, _C_CQ[0]:_C_CQ[1]], qn_ref[...]).astype(BF16)
    cost = jnp.tile(cost_ref[...], (A_HEADS, 1))
    sint = jnp.tile(sint_ref[...], (A_HEADS, 1))
    qa = _dot_nt(wqt_ref[...], cqn)
    half = A_ROPE // 2
    parts = []
    for h in range(A_HEADS):
        r0 = h * LANE
        parts += [qa[r0:r0 + HEAD], qa[r0 + HEAD + half:r0 + HEAD + A_ROPE],
                  qa[r0 + HEAD:r0 + HEAD + half], qa[r0 + HEAD + A_ROPE:r0 + LANE]]
    qt = ((qa * cost + jnp.concatenate(parts, axis=0) * sint) * q_scale).astype(BF16)

    @pl.when(i < n_ctx_tiles)
    def _():
        qtc_ref[0, 0] = qt

    @pl.when(i >= n_ctx_tiles)
    def _():
        qtl_ref[0, 0] = qt

    ckvn = _rms(u[:, _C_CKV[0]:_C_CKV[1]], kvn_ref[...]).astype(BF16)
    cosw = jnp.tile(cos_ref[...], (1, A_HEADS))
    sinw = jnp.tile(sin_ref[...], (1, A_HEADS))
    kw = A_HEADS * LANE
    kr = jnp.tile(u[:, _C_KR[0]:_C_KR[1]], (1, A_HEADS))
    krs = jnp.tile(u[:, _C_KRS[0]:_C_KRS[1]], (1, A_HEADS))
    k = _dot(ckvn, wk_ref[...]) + kr * cosw + krs * sinw
    k_ref[0] = k.astype(BF16)
    vrow = lax.broadcasted_iota(I32, (A_HEADS * V_ROWS, 1), 0)
    ones_a = jnp.where(vrow % V_ROWS == HEAD, 1.0, 0.0)
    vt_ref[0, 0] = (_dot_nt(wvt_ref[...], ckvn) + ones_a).astype(BF16)


def _inproj(xs, mods_l, n1, w1, wgt, qn, kvn, wqt, wk, wvt, cos128, sin128, cost128, sint128,
            *, n_ctx_tiles):
    B, S, D = xs.shape
    nt = S // TOK_TILE
    n_ctx = n_ctx_tiles * TOK_TILE
    n_lat_tiles = nt - n_ctx_tiles
    kw = A_HEADS * LANE
    tok = lambda w: pl.BlockSpec((1, TOK_TILE, w), lambda b, i: (b, i, 0))
    full = lambda a: pl.BlockSpec(a.shape, lambda b, i: (0,) * a.ndim)
    mod_spec = pl.BlockSpec((1, 1, 6 * D), lambda b, i: (jnp.where(i < n_ctx_tiles, B, b), 0, 0))
    tab_spec = pl.BlockSpec((TOK_TILE, LANE), lambda b, i: (i, 0))
    tabt_spec = pl.BlockSpec((LANE, TOK_TILE), lambda b, i: (0, i))
    sd = lambda w, dt: jax.ShapeDtypeStruct((B, S, w), dt)
    kern = functools.partial(_inproj_kernel, d_model=D, n_ctx_tiles=n_ctx_tiles,
                             q_scale=(HEAD + A_ROPE) ** -0.5 * math.log2(math.e))
    qtc_spec = pl.BlockSpec((1, 1, kw, TOK_TILE), lambda b, i: (b, jnp.minimum(i, n_ctx_tiles - 1), 0, 0))
    qtl_spec = pl.BlockSpec((1, 1, kw, TOK_TILE),
                            lambda b, i: (b, jnp.clip(i - n_ctx_tiles, 0, n_lat_tiles - 1), 0, 0))
    return pl.pallas_call(
        kern,
        grid=(B, nt),
        in_specs=[tok(D), mod_spec, full(n1), full(w1), full(wgt), full(qn), full(kvn), full(wqt),
                  full(wk), full(wvt), tab_spec, tab_spec, tabt_spec, tabt_spec],
        out_specs=[tok(256), tok(1024), tok(512), tok(512), tok(16),
                   pl.BlockSpec((1, 16, TOK_TILE), lambda b, i: (b, 0, i)),
                   qtc_spec, qtl_spec, tok(kw),
                   pl.BlockSpec((1, 1, A_HEADS * V_ROWS, TOK_TILE), lambda b, i: (b, i, 0, 0))],
        out_shape=[sd(256, BF16), sd(1024, BF16), sd(512, BF16), sd(512, BF16), sd(16, F32),
                   jax.ShapeDtypeStruct((B, 16, S), F32),
                   jax.ShapeDtypeStruct((B, n_ctx_tiles, kw, TOK_TILE), BF16),
                   jax.ShapeDtypeStruct((B, n_lat_tiles, kw, TOK_TILE), BF16),
                   sd(kw, BF16),
                   jax.ShapeDtypeStruct((B, nt, A_HEADS * V_ROWS, TOK_TILE), BF16)],
        compiler_params=_cparams(2, VMEM_LIMIT),
        name="inproj",
    )(xs, mods_l, n1, w1, wgt, qn, kvn, wqt, wk, wvt, cos128, sin128, cost128, sint128)


def _dft_mats(n):
    idx = jnp.arange(n, dtype=F32)
    ang = 2.0 * math.pi * jnp.mod(idx[:, None] * idx[None, :], n) / n
    return jnp.cos(ang), jnp.sin(ang)


def _fft_chan_mats():
    c, s = _dft_mats(HEAD)
    eye = jnp.eye(4, dtype=F32)
    return jnp.concatenate([jnp.kron(eye, c), jnp.kron(eye, s)], 0).astype(BF16)


def _fft_ctx_kernel(f_ref, ft_ref, cs_ref, o_ref, *, n, scale):
    xst = _dot(ft_ref[...], f_ref[0])
    xr = xst[:n].astype(BF16)
    xi = xst[n:].astype(BF16)
    y = _dot(xr, cs_ref[0:256, :]) + _dot(xi, cs_ref[256:512, :])
    o_ref[0] = (y * scale).astype(BF16)


def _fft_ctx(f, n_ctx, cs):
    B = f.shape[0]
    c, s = _dft_mats(n_ctx)
    ft = jnp.concatenate([c, -s], 0).astype(BF16)
    kern = functools.partial(_fft_ctx_kernel, n=n_ctx, scale=(n_ctx * HEAD) ** -0.5)
    return pl.pallas_call(
        kern,
        grid=(B,),
        in_specs=[pl.BlockSpec((1, n_ctx, 256), lambda b: (b, 0, 0)),
                  pl.BlockSpec(ft.shape, lambda b: (0, 0)),
                  pl.BlockSpec(cs.shape, lambda b: (0, 0))],
        out_specs=pl.BlockSpec((1, n_ctx, 256), lambda b: (b, 0, 0)),
        out_shape=jax.ShapeDtypeStruct((B, n_ctx, 256), BF16),
        compiler_params=_cparams(1),
        name="fft_ctx",
    )(f, ft, cs)


def _fft_stage1_kernel(x_ref, f1_ref, cw_ref, sw_ref, o_ref, *, n1):
    z = _dot(f1_ref[...], x_ref[0])
    zr, zi = z[:n1], z[n1:]
    cw, sw = cw_ref[...], sw_ref[...]
    o_ref[0, 0] = (zr * cw + zi * sw).astype(BF16)
    o_ref[0, 1] = (zi * cw - zr * sw).astype(BF16)


def _fft_stage2_kernel(z_ref, f2_ref, cs_ref, o_ref, *, tb, scale):
    for j in range(tb):
        zcat = jnp.concatenate([z_ref[0, 0, j], z_ref[0, 1, j]], axis=0)
        xst = _dot(f2_ref[...], zcat)
        xr = xst[:HEAD].astype(BF16)
        xi = xst[HEAD:].astype(BF16)
        y = _dot(xr, cs_ref[0:256, :]) + _dot(xi, cs_ref[256:512, :])
        o_ref[0, j] = (y * scale).astype(BF16)


def _fft_latent(f_lat, cs):
    B, T, W = f_lat.shape
    n2 = HEAD
    n1 = T // n2
    cols = n2 * W
    cb = 2048
    c1, s1 = _dft_mats(n1)
    f1 = jnp.concatenate([c1, -s1], 0).astype(BF16)
    t1 = jnp.arange(n1, dtype=F32)[:, None]
    s2 = jnp.arange(n2, dtype=F32)[None, :]
    ang = 2.0 * math.pi * (t1 * s2) / T
    cw = jnp.repeat(jnp.cos(ang), W, axis=1)
    sw = jnp.repeat(jnp.sin(ang), W, axis=1)
    z = pl.pallas_call(
        functools.partial(_fft_stage1_kernel, n1=n1),
        grid=(B, cols // cb),
        in_specs=[pl.BlockSpec((1, n1, cb), lambda b, j: (b, 0, j)),
                  pl.BlockSpec(f1.shape, lambda b, j: (0, 0)),
                  pl.BlockSpec((n1, cb), lambda b, j: (0, j)),
                  pl.BlockSpec((n1, cb), lambda b, j: (0, j))],
        out_specs=pl.BlockSpec((1, 2, n1, cb), lambda b, j: (b, 0, 0, j)),
        out_shape=jax.ShapeDtypeStruct((B, 2, n1, cols), BF16),
        compiler_params=_cparams(2),
        name="fft_stage1",
    )(f_lat.reshape(B, n1, cols), f1, cw, sw)
    z = z.reshape(B, 2, n1, n2, W)
    c2, s2m = _dft_mats(n2)
    f2 = jnp.concatenate([jnp.concatenate([c2, s2m], 1),
                          jnp.concatenate([-s2m, c2], 1)], 0).astype(BF16)
    tb = 8
    y = pl.pallas_call(
        functools.partial(_fft_stage2_kernel, tb=tb, scale=(T * HEAD) ** -0.5),
        grid=(B, n1 // tb),
        in_specs=[pl.BlockSpec((1, 2, tb, n2, W), lambda b, j: (b, 0, j, 0, 0)),
                  pl.BlockSpec(f2.shape, lambda b, j: (0, 0)),
                  pl.BlockSpec(cs.shape, lambda b, j: (0, 0))],
        out_specs=pl.BlockSpec((1, tb, n2, W), lambda b, j: (b, j, 0, 0)),
        out_shape=jax.ShapeDtypeStruct((B, n1, n2, W), BF16),
        compiler_params=_cparams(2),
        name="fft_stage2",
    )(z, f2, cs)
    return jnp.transpose(y, (0, 2, 1, 3)).reshape(B, T, W)


def _mlstm_prep_kernel(cur_ref, prev_ref, next_ref, cw_ref, cb_ref, gat_ref, gatt_ref, br_ref, bc_ref,
                       q_ref, kt_ref, col_ref, row_ref, *, n_chunks, n_ctx_chunks):
    c = pl.program_id(1)
    cur = cur_ref[0].astype(F32)
    first = jnp.logical_or(c == 0, c == n_ctx_chunks)
    last = jnp.logical_or(c == n_ctx_chunks - 1, c == n_chunks - 1)
    prev_row = prev_ref[0].astype(F32)[15:16, :]
    next_row = next_ref[0].astype(F32)[0:1, :]
    prev_row = jnp.where(first, 0.0, prev_row)
    next_row = jnp.where(last, 0.0, next_row)
    rows = lax.broadcasted_iota(I32, (CHUNK, 1), 0)
    up = jnp.where(rows == 0, prev_row, pltpu.roll(cur, 1, axis=0))
    dn = jnp.where(rows == CHUNK - 1, next_row, pltpu.roll(cur, CHUNK - 1, axis=0))
    y = cw_ref[0:1, :] * up + cw_ref[1:2, :] * cur + cw_ref[2:3, :] * dn + cb_ref[...]
    y = y * _sigmoid(y)
    hw = M_HEADS * LANE
    q_ref[0] = (y[:, :hw] * HEAD ** -0.5).astype(BF16)
    kt_ref[0] = y[:, hw:].T.astype(BF16)

    def logsig(v):
        return jnp.minimum(v, 0.0) - jnp.log(1.0 + jnp.exp(-jnp.abs(v)))

    r_i = lax.broadcasted_iota(I32, (CHUNK, CHUNK), 0)
    c_i = lax.broadcasted_iota(I32, (CHUNK, CHUNK), 1)
    lower = jnp.where(c_i <= r_i, 1.0, 0.0).astype(BF16)
    upper = jnp.where(c_i >= r_i, 1.0, 0.0).astype(BF16)

    g = gat_ref[0] + br_ref[...]
    lane = lax.broadcasted_iota(I32, (1, 16), 1)
    lg = jnp.where(lane % 8 >= 4, logsig(g), g)
    hi, lo = _split_bf16(lg)
    cum_f = _dot(lower, hi) + _dot(lower, lo)
    cum_b = _dot(upper, hi) + _dot(upper, lo)
    cum_col = jnp.where(lane < 8, cum_f, cum_b)

    gt = gatt_ref[0] + bc_ref[...]
    row = lax.broadcasted_iota(I32, (16, 1), 0)
    lgt = jnp.where(row % 8 >= 4, logsig(gt), gt)
    hit, lot = _split_bf16(lgt)
    cum_ft = _dot(hit, upper) + _dot(lot, upper)
    cum_bt = _dot(hit, lower) + _dot(lot, lower)
    cumt = jnp.where(row < 8, cum_ft, cum_bt)
    cumt_i = pltpu.roll(cumt, 12, axis=0)
    rterm = lgt - cumt_i
    btot = jnp.broadcast_to(jnp.sum(lgt, axis=1, keepdims=True), (16, CHUNK))
    row_ref[0, 0:16] = jnp.where(row % 8 < 4, rterm, btot)
    dmax = jnp.broadcast_to(jnp.max(rterm, axis=1, keepdims=True), (16, CHUNK))
    row_ref[0, 16:32] = pltpu.roll(btot, 12, axis=0) + dmax
    seen_max = jnp.zeros((CHUNK, 16), F32)
    for d in range(2):
        seen = (c_i <= r_i) if d == 0 else (c_i >= r_i)
        for h in range(M_HEADS):
            idx = 8 * d + h
            cmx = jnp.max(jnp.where(seen, rterm[idx:idx + 1, :], -jnp.inf), axis=1, keepdims=True)
            seen_max = jnp.where(lane == idx, cmx, seen_max)
    col_ref[0] = jnp.where(lane % 8 >= 4, cum_col, seen_max)


def _mlstm_prep(qk, gat, gatt, conv_w, conv_b, bias_row, bias_col, *, n_ctx_chunks):
    B, S, W = qk.shape
    nc = S // CHUNK
    n16 = S // 16
    hw = M_HEADS * LANE
    kern = functools.partial(_mlstm_prep_kernel, n_chunks=nc, n_ctx_chunks=n_ctx_chunks)
    full = lambda a: pl.BlockSpec(a.shape, lambda b, c: (0,) * a.ndim)
    return pl.pallas_call(
        kern,
        grid=(B, nc),
        in_specs=[pl.BlockSpec((1, CHUNK, W), lambda b, c: (b, c, 0)),
                  pl.BlockSpec((1, 16, W), lambda b, c: (b, jnp.maximum(c * 8 - 1, 0), 0)),
                  pl.BlockSpec((1, 16, W), lambda b, c: (b, jnp.minimum((c + 1) * 8, n16 - 1), 0)),
                  full(conv_w), full(conv_b),
                  pl.BlockSpec((1, CHUNK, 16), lambda b, c: (b, c, 0)),
                  pl.BlockSpec((1, 16, CHUNK), lambda b, c: (b, 0, c)),
                  full(bias_row), full(bias_col)],
        out_specs=[pl.BlockSpec((1, CHUNK, hw), lambda b, c: (b, c, 0)),
                   pl.BlockSpec((1, hw, CHUNK), lambda b, c: (b, 0, c)),
                   pl.BlockSpec((1, CHUNK, 16), lambda b, c: (b, c, 0)),
                   pl.BlockSpec((1, 32, CHUNK), lambda b, c: (b, 0, c))],
        out_shape=[jax.ShapeDtypeStruct((B, S, hw), BF16),
                   jax.ShapeDtypeStruct((B, hw, S), BF16),
                   jax.ShapeDtypeStruct((B, S, 16), F32),
                   jax.ShapeDtypeStruct((B, 32, S), F32)],
        compiler_params=_cparams(2),
        name="mlstm_prep",
    )(qk, qk, qk, conv_w, conv_b, gat, gatt, bias_row, bias_col)


def _mlstm_scan_kernel(*refs, n_batch):
    ins, (of_ref, ob_ref, c_scr, m_scr) = refs[:10], refs[10:]
    j = pl.program_id(1)

    @pl.when(j == 0)
    def _():
        c_scr[...] = jnp.zeros_like(c_scr)
        m_scr[...] = jnp.zeros_like(m_scr)

    t_i = lax.broadcasted_iota(I32, (CHUNK, CHUNK), 0)
    s_i = lax.broadcasted_iota(I32, (CHUNK, CHUNK), 1)
    for bb in range(n_batch):
        for d, o_ref in enumerate((of_ref, ob_ref)):
            q_ref, kt_ref, v_ref, col_ref, row_ref = ins[5 * d:5 * d + 5]
            colp = col_ref[bb]
            rowp = row_ref[bb]
            mask = (s_i <= t_i) if d == 0 else (s_i >= t_i)
            outs = []
            for h in range(M_HEADS):
                st = (bb * 2 + d) * M_HEADS + h
                full = (CHUNK, CHUNK)
                bcol = jnp.broadcast_to(colp[:, 8 * d + 4 + h:8 * d + 5 + h], full)
                rmax = bcol + jnp.broadcast_to(colp[:, 8 * d + h:8 * d + h + 1], full)
                rrow = rowp[8 * d + h:8 * d + h + 1, :]
                btot = rowp[8 * d + 4 + h:8 * d + 5 + h, :]
                dmax = rowp[16 + 8 * d + h:17 + 8 * d + h, :]
                m = m_scr[st, 0:1, :]
                inter = bcol + m
                m_t = jnp.maximum(inter, rmax)
                w = jnp.exp(jnp.where(mask, bcol + rrow, -jnp.inf) - m_t)
                a = jnp.exp(inter - m_t)
                qh = q_ref[bb, :, h * LANE:(h + 1) * LANE]
                kth = kt_ref[bb, h * LANE:(h + 1) * LANE, :]
                vh = v_ref[bb, :, h * LANE:(h + 1) * LANE]
                s = _dot(qh, kth) * w
                cst = c_scr[st]
                tot = a * _dot(qh, cst.astype(BF16)) + _dot(s.astype(BF16), vh)
                den = jnp.broadcast_to(tot[:, HEAD:HEAD + 1], full)
                outs.append(tot / jnp.maximum(jnp.abs(den), jnp.exp(-m_t)))
                bm = btot + m
                m_new = jnp.maximum(bm, dmax)
                ws = jnp.exp(btot + rrow - m_new)
                kw = (kth.astype(F32) * ws).astype(BF16)
                c_scr[st] = jnp.exp(bm - m_new) * cst + _dot(kw, vh)
                m_scr[st] = jnp.broadcast_to(m_new, (8, LANE))
            o_ref[bb] = jnp.concatenate(outs, axis=1)


def _mlstm_scan(q, kt, v, colp, rowp, *, n_ctx_chunks):
    B, S, hw = q.shape
    nc = S // CHUNK
    nb = next(n for n in (4, 2, 1) if B % n == 0)

    def rev(j):
        return jnp.where(j < n_ctx_chunks, n_ctx_chunks - 1 - j, nc + n_ctx_chunks - 1 - j)

    def specs(cid):
        return [pl.BlockSpec((nb, CHUNK, hw), lambda b, j: (b, cid(j), 0)),
                pl.BlockSpec((nb, hw, CHUNK), lambda b, j: (b, 0, cid(j))),
                pl.BlockSpec((nb, CHUNK, hw), lambda b, j: (b, cid(j), 0)),
                pl.BlockSpec((nb, CHUNK, 16), lambda b, j: (b, cid(j), 0)),
                pl.BlockSpec((nb, 32, CHUNK), lambda b, j: (b, 0, cid(j)))]

    fwd = lambda j: j
    n_chains = nb * 2 * M_HEADS
    return pl.pallas_call(
        functools.partial(_mlstm_scan_kernel, n_batch=nb),
        grid=(B // nb, nc),
        in_specs=specs(fwd) + specs(rev),
        out_specs=[pl.BlockSpec((nb, CHUNK, hw), lambda b, j: (b, j, 0)),
                   pl.BlockSpec((nb, CHUNK, hw), lambda b, j: (b, rev(j), 0))],
        out_shape=[jax.ShapeDtypeStruct((B, S, hw), F32), jax.ShapeDtypeStruct((B, S, hw), F32)],
        scratch_shapes=[pltpu.VMEM((n_chains, LANE, LANE), F32), pltpu.VMEM((n_chains, 8, LANE), F32)],
        compiler_params=_cparams(2),
        name="mlstm_scan",
    )(q, kt, v, colp, rowp, q, kt, v, colp, rowp)


def _attn_kernel(qt_ref, k_ref, vt_ref, o_ref, s0_scr, s1_scr, p0_scr, p1_scr, al_scr, cm_scr, acc_scr,
                 m_scr, *,
                 n_q, n_kc, sub, qtiles):
    tk = sub * TOK_TILE
    tq = qtiles * TOK_TILE
    n_items = n_q * n_kc
    heads = [slice(hh * LANE, (hh + 1) * LANE) for hh in range(2)]
    vheads = [slice(hh * V_ROWS, (hh + 1) * V_ROWS) for hh in range(2)]
    s_bufs = (s0_scr, s1_scr)
    p_bufs = (p0_scr, p1_scr)

    def split(n):
        if n_q == 1:
            return 0, n
        qi = n // n_kc
        return qi, n - qi * n_kc

    def scores(n, par):
        qi, c = split(n)
        for hh in range(2):
            kc = k_ref[0, pl.ds(pl.multiple_of(c * tk, tk), tk), heads[hh]]
            qt = jnp.concatenate([qt_ref[0, qi * qtiles + j, heads[hh], :] for j in range(qtiles)], axis=1)
            st = _dot(kc, qt)
            s_bufs[par][hh] = st
            cm_scr[par, hh] = jnp.max(st, axis=0, keepdims=True)

    def softmax(n, par):
        _, c = split(n)
        for hh in range(2):
            st = s_bufs[par][hh]
            m = jnp.where(c == 0, -jnp.inf, m_scr[hh])
            m_new = jnp.maximum(m, cm_scr[par, hh])
            p_bufs[par][hh] = jnp.exp2(st - m_new).astype(BF16)
            al_scr[par, hh] = jnp.exp2(m - m_new)
            m_scr[hh] = m_new

    def values(n, par):
        qi, c = split(n)
        for hh in range(2):
            pv = _dot(vt_ref[0, c * sub, vheads[hh], :], p_bufs[par][hh, 0:TOK_TILE])
            for j in range(1, sub):
                pv += _dot(vt_ref[0, c * sub + j, vheads[hh], :],
                           p_bufs[par][hh, j * TOK_TILE:(j + 1) * TOK_TILE])
            acc_scr[hh] = al_scr[par, hh] * acc_scr[hh] + pv

        def finalize():
            o = jnp.concatenate([acc_scr[hh, :HEAD] / acc_scr[hh, HEAD:HEAD + 1] for hh in range(2)], axis=0)
            o_ref[0, pl.ds(pl.multiple_of(qi * tq, tq), tq), :] = o.T.astype(BF16)

        if isinstance(n, int) and n % n_kc == n_kc - 1:
            finalize()

    m_scr[...] = jnp.zeros_like(m_scr)
    acc_scr[...] = jnp.zeros_like(acc_scr)
    scores(0, 0)
    if n_items > 1:
        scores(1, 1)
    softmax(0, 0)

    def step(t, par):
        scores(t + 2, par)
        softmax(t + 1, 1 - par)
        values(t, par)

    def run(t0, t1):
        if t0 < t1 and t0 % 2:
            step(t0, 1)
            t0 += 1
        if t1 - t0 >= 2:
            def pair(i, carry):
                step(t0 + 2 * i, 0)
                step(t0 + 2 * i + 1, 1)
                return carry
            lax.fori_loop(0, (t1 - t0) // 2, pair, 0)
        if (t1 - t0) % 2:
            step(t1 - 1, (t1 - 1) % 2)

    n_full = max(n_items - 2, 0)
    t = 0
    for last in range(n_kc - 1, n_full, n_kc):
        run(t, last)
        step(last, last % 2)
        t = last + 1
    run(t, n_full)
    if n_items > 1:
        softmax(n_items - 1, (n_items - 1) % 2)
        values(n_items - 2, (n_items - 2) % 2)
    values(n_items - 1, (n_items - 1) % 2)


def _attention(qt4, k, vt4, *, n_keys, qtiles, sub, q_per_step=1):
    B, n_qb, kw, _ = qt4.shape
    n_kb = n_keys // TOK_TILE
    tk = sub * TOK_TILE
    tq = qtiles * TOK_TILE
    qb_step = q_per_step * qtiles
    kern = functools.partial(_attn_kernel, n_q=q_per_step, n_kc=n_kb // sub, sub=sub, qtiles=qtiles)
    return pl.pallas_call(
        kern,
        grid=(B, A_HEADS // 2, n_qb // qb_step),
        in_specs=[pl.BlockSpec((1, qb_step, 2 * LANE, TOK_TILE), lambda b, hp, i: (b, i, hp, 0)),
                  pl.BlockSpec((1, n_keys, 2 * LANE), lambda b, hp, i: (b, 0, hp)),
                  pl.BlockSpec((1, n_kb, 2 * V_ROWS, TOK_TILE), lambda b, hp, i: (b, 0, hp, 0))],
        out_specs=pl.BlockSpec((1, qb_step * TOK_TILE, LANE), lambda b, hp, i: (b, i, hp)),
        out_shape=jax.ShapeDtypeStruct((B, n_qb * TOK_TILE, A_HEADS * HEAD), BF16),
        scratch_shapes=[pltpu.VMEM((2, tk, tq), F32), pltpu.VMEM((2, tk, tq), F32),
                        pltpu.VMEM((2, tk, tq), BF16), pltpu.VMEM((2, tk, tq), BF16),
                        pltpu.VMEM((2, 2, 1, tq), F32), pltpu.VMEM((2, 2, 1, tq), F32),
                        pltpu.VMEM((2, V_ROWS, tq), F32), pltpu.VMEM((2, 1, tq), F32)],
        compiler_params=_cparams(3, VMEM_LIMIT),
        name="attention",
    )(qt4, k, vt4)


def _outproj_kernel(yac_ref, yal_ref, hf_ref, hb_ref, o_ref, ycc_ref, ycl_ref, x_ref, mod_ref, wa_ref, wb_ref,
                    wc_ref, mg_ref, blk_ref, n2_ref, rw_ref, xo_ref, hx_ref, aff_ref, *, d_model,
                    n_ctx_tiles):
    D = d_model
    i = pl.program_id(1)
    mod = mod_ref[0]
    g1, sh2, sc2 = mod[:, 2 * D:3 * D], mod[:, 3 * D:4 * D], mod[:, 4 * D:5 * D]
    ya = jnp.where(i < n_ctx_tiles, yac_ref[0], yal_ref[0])
    lane = lax.broadcasted_iota(I32, (1, M_HEADS * LANE), 1)
    h = jnp.where(lane % LANE < HEAD, hf_ref[0] + hb_ref[0], 0.0)
    hi, lo = _split_bf16(h * h)
    ms = _dot(hi, blk_ref[...]) + _dot(lo, blk_ref[...])
    hn = h * lax.rsqrt(ms + EPS) * mg_ref[...]
    yb = (hn * _sigmoid(o_ref[0].astype(F32))).astype(BF16)
    yc = jnp.where(i < n_ctx_tiles, ycc_ref[0], ycl_ref[0])
    mix = _dot(ya, wa_ref[...]) + _dot(yb, wb_ref[...]) + _dot(yc, wc_ref[...])
    x = x_ref[0] + g1 * mix
    xo_ref[0] = x
    hx = _rms(x, n2_ref[...]) * (1.0 + sc2) + sh2
    hx_ref[0] = hx.astype(BF16)
    h_hi, h_lo = _split_bf16(hx)
    r_hi, r_lo = _split_bf16(rw_ref[...])
    logits = _dot(h_hi, r_hi) + _dot(h_lo, r_hi) + _dot(h_hi, r_lo)
    e = jnp.exp(logits - jnp.max(logits, axis=1, keepdims=True))
    aff_ref[0] = e / jnp.sum(e, axis=1, keepdims=True)


def _outproj(ya_ctx, ya_lat, hf, hb, o, yc_ctx, yc_lat, xs, mods_l, wa, wb, wc, mg, blk, n2, rw, *,
             n_ctx_tiles):
    B, S, D = xs.shape
    nt = S // TOK_TILE
    tok = lambda w: pl.BlockSpec((1, TOK_TILE, w), lambda b, i: (b, i, 0))
    full = lambda a: pl.BlockSpec(a.shape, lambda b, i: (0,) * a.ndim)
    mod_spec = pl.BlockSpec((1, 1, 6 * D), lambda b, i: (jnp.where(i < n_ctx_tiles, B, b), 0, 0))
    hw = M_HEADS * LANE
    n_lat_tiles = nt - n_ctx_tiles
    ctx_tok = lambda w: pl.BlockSpec((1, TOK_TILE, w), lambda b, i: (b, jnp.minimum(i, n_ctx_tiles - 1), 0))
    lat_tok = lambda w: pl.BlockSpec(
        (1, TOK_TILE, w), lambda b, i: (b, jnp.clip(i - n_ctx_tiles, 0, n_lat_tiles - 1), 0))
    kern = functools.partial(_outproj_kernel, d_model=D, n_ctx_tiles=n_ctx_tiles)
    return pl.pallas_call(
        kern,
        grid=(B, nt),
        in_specs=[ctx_tok(256), lat_tok(256),
                  tok(hw), tok(hw),
                  tok(hw), ctx_tok(A_HEADS * HEAD), lat_tok(A_HEADS * HEAD), tok(D), mod_spec,
                  full(wa), full(wb), full(wc), full(mg), full(blk), full(n2), full(rw)],
        out_specs=[tok(D), tok(D), tok(N_EXPERTS)],
        out_shape=[jax.ShapeDtypeStruct((B, S, D), F32), jax.ShapeDtypeStruct((B, S, D), BF16),
                   jax.ShapeDtypeStruct((B, S, N_EXPERTS), F32)],
        compiler_params=_cparams(2, VMEM_LIMIT),
        name="outproj",
    )(ya_ctx, ya_lat, hf, hb, o, yc_ctx, yc_lat, xs, mods_l, wa, wb, wc, mg, blk, n2, rw)


def _route_part(a, cap, slot0, tri, pos_ref, base_ref, lane0):
    n = a.shape[1]
    capf = float(cap)
    bits = pltpu.bitcast(a, I32)
    v = jnp.zeros((N_EXPERTS, 1), I32)
    for bit in range(30, -1, -1):
        cand = v | (1 << bit)
        cnt = jnp.sum(jnp.where(bits >= cand, 1.0, 0.0), axis=1, keepdims=True)
        v = jnp.where(cnt >= capf, cand, v)
    gt = bits > v
    eq = bits == v
    need = capf - jnp.sum(jnp.where(gt, 1.0, 0.0), axis=1, keepdims=True)
    idx = lax.broadcasted_iota(I32, (N_EXPERTS, n), 1)
    x = jnp.zeros((N_EXPERTS, 1), I32)
    for bit in range(max(n - 1, 1).bit_length() - 1, -1, -1):
        cand = x | (1 << bit)
        cnt = jnp.sum(jnp.where(eq, jnp.where(idx < cand, 1.0, 0.0), 0.0), axis=1, keepdims=True)
        x = jnp.where(cnt < need, cand, x)
    sel = jnp.where(gt, 1.0, jnp.where(eq, jnp.where(idx <= x, 1.0, 0.0), 0.0))
    running = jnp.zeros((N_EXPERTS, 1), F32)
    fill = jnp.zeros((N_EXPERTS, 1), F32)
    for c in range(n // CHUNK):
        blk = sel[:, c * CHUNK:(c + 1) * CHUNK]
        incl = _dot(blk.astype(BF16), tri)
        pos = running + incl - blk + float(slot0)
        cg = lane0 // CHUNK + c
        pos_ref[0, :, lane0 + c * CHUNK:lane0 + (c + 1) * CHUNK] = jnp.where(
            blk > 0.5, pos, -1.0).astype(I32)
        base_ref[0, :, cg:cg + 1] = (running + float(slot0)).astype(I32)
        running = running + incl[:, CHUNK - 1:CHUNK]
        fill = jnp.maximum(fill, incl[:, CHUNK - 1:CHUNK])
    return fill


def _route_kernel(aff_ref, pos_ref, base_ref, *, n_ctx, cap_ctx, cap_lat):
    r_i = lax.broadcasted_iota(I32, (CHUNK, CHUNK), 0)
    c_i = lax.broadcasted_iota(I32, (CHUNK, CHUNK), 1)
    tri = jnp.where(r_i <= c_i, 1.0, 0.0).astype(BF16)
    base_ref[...] = jnp.zeros_like(base_ref)
    a = aff_ref[0]
    fill_c = _route_part(a[:, :n_ctx], cap_ctx, 0, tri, pos_ref, base_ref, 0)
    fill_l = _route_part(a[:, n_ctx:], cap_lat, cap_ctx, tri, pos_ref, base_ref, n_ctx)
    base_ref[0, :, LANE - 1:LANE] = jnp.maximum(fill_c, fill_l).astype(I32)


def _route(aff_t, *, n_ctx, cap_ctx, cap_lat):
    B, E, S = aff_t.shape
    assert S // CHUNK < LANE - 1
    kern = functools.partial(_route_kernel, n_ctx=n_ctx, cap_ctx=cap_ctx, cap_lat=cap_lat)
    return pl.pallas_call(
        kern,
        grid=(B,),
        in_specs=[pl.BlockSpec((1, E, S), lambda b: (b, 0, 0))],
        out_specs=[pl.BlockSpec((1, E, S), lambda b: (b, 0, 0)),
                   pl.BlockSpec((1, E, LANE), lambda b: (b, 0, 0))],
        out_shape=[jax.ShapeDtypeStruct((B, E, S), I32), jax.ShapeDtypeStruct((B, E, LANE), I32)],
        compiler_params=_cparams(1),
        name="route",
    )(aff_t)


def _window_start(base, align, rows, win):
    w0 = lax.shift_left(lax.shift_right_logical(base, int(math.log2(align))), int(math.log2(align)))
    return pl.multiple_of(jnp.minimum(w0, rows - win), align)


def _gather_kernel(base_ref, h_ref, pos_ref, xe_ref, *, n_chunks, rows, win, unroll):
    b = pl.program_id(0)
    e = pl.program_id(2)
    xe_ref[0, 0] = jnp.zeros(xe_ref.shape[2:], BF16)

    def gather_all(win_rows):
        def group(g, carry):
            picked = []
            for u in range(unroll):
                c = g * unroll + u
                w0 = _window_start(base_ref[b, e, c], GATHER_ALIGN_BF16, rows, win_rows)
                posr = pos_ref[0, 0, pl.ds(c, 1), :]
                slot = lax.broadcasted_iota(I32, (win_rows, CHUNK), 0) + w0
                onehot = jnp.where(posr == slot, 1.0, 0.0).astype(BF16)
                hc = h_ref[0, pl.ds(pl.multiple_of(c * CHUNK, CHUNK), CHUNK), :]
                picked.append((w0, _dot(onehot, hc).astype(BF16)))
            for w0, rows_c in picked:
                xe_ref[0, 0, pl.ds(w0, win_rows), :] += rows_c
            return carry

        lax.fori_loop(0, n_chunks // unroll, group, 0)

    most = base_ref[b, e, LANE - 1]
    small = min(SMALL_FILL + GATHER_ALIGN_BF16, rows)
    if small < win:
        pl.when(most <= SMALL_FILL)(functools.partial(gather_all, small))
        pl.when(most > SMALL_FILL)(functools.partial(gather_all, win))
    else:
        gather_all(win)


def _gather(hx, pos4, bases, *, rows):
    B, S, D = hx.shape
    nc = S // CHUNK
    dh = D // 2
    win = min(CHUNK + GATHER_ALIGN_BF16, rows)
    unroll = next(u for u in (11, 8, 6, 4, 3, 2, 1) if nc % u == 0)
    kern = functools.partial(_gather_kernel, n_chunks=nc, rows=rows, win=win, unroll=unroll)
    return pl.pallas_call(
        kern,
        grid_spec=pltpu.PrefetchScalarGridSpec(
            num_scalar_prefetch=1,
            grid=(B, 2, N_EXPERTS),
            in_specs=[pl.BlockSpec((1, S, dh), lambda b, j, e, bs: (b, 0, j)),
                      pl.BlockSpec((1, 1, nc, CHUNK), lambda b, j, e, bs: (b, e, 0, 0))],
            out_specs=pl.BlockSpec((1, 1, rows, dh), lambda b, j, e, bs: (b, e, 0, j))),
        out_shape=jax.ShapeDtypeStruct((B, N_EXPERTS, rows, D), BF16),
        compiler_params=_cparams(3, VMEM_LIMIT),
        name="moe_gather",
    )(bases, hx, pos4)


def _ffn_kernel(x_ref, wg_ref, wu_ref, wd_ref, y_ref, *, row_tile, n_tiles):
    def tile(i, carry):
        r0 = pl.multiple_of(i * row_tile, 16)
        x = x_ref[0, 0, pl.ds(r0, row_tile), :]
        g = _dot(x, wg_ref[0])
        u = _dot(x, wu_ref[0])
        hid = (g * _sigmoid(g) * u).astype(BF16)
        y_ref[0, 0, pl.ds(r0, row_tile), :] = _dot(hid, wd_ref[0]).astype(BF16)
        return carry

    lax.fori_loop(0, n_tiles, tile, 0)


def _ffn_row_tile(rows):
    for t in (352, 256, 176, 128, 96, 64, 32, 16):
        if rows % t == 0:
            return t
    raise ValueError(f"expert slot rows {rows} must be a multiple of 16")


def _ffn(xe, wg, wu, wd):
    B, E, R, D = xe.shape
    F = wg.shape[-1]
    rt = _ffn_row_tile(R)
    kern = functools.partial(_ffn_kernel, row_tile=rt, n_tiles=R // rt)
    return pl.pallas_call(
        kern,
        grid=(E, B),
        in_specs=[pl.BlockSpec((1, 1, R, D), lambda e, b: (b, e, 0, 0)),
                  pl.BlockSpec((1, D, F), lambda e, b: (e, 0, 0)),
                  pl.BlockSpec((1, D, F), lambda e, b: (e, 0, 0)),
                  pl.BlockSpec((1, F, D), lambda e, b: (e, 0, 0))],
        out_specs=pl.BlockSpec((1, 1, R, D), lambda e, b: (b, e, 0, 0)),
        out_shape=jax.ShapeDtypeStruct((B, E, R, D), BF16),
        compiler_params=_cparams(2, VMEM_LIMIT),
        name="moe_ffn",
    )(xe, wg, wu, wd)


def _combine_kernel(base_ref, x_ref, ye_ref, posc_ref, aff_ref, modc_ref, modx_ref, o_ref, *, d_model,
                    blk_chunks, n_ctx_chunks, rows, win):
    D = d_model
    b = pl.program_id(0)
    tb = pl.program_id(1)
    e = pl.program_id(2)

    @pl.when(e == 0)
    def _():
        o_ref[0] = jnp.zeros(o_ref.shape[1:], F32)

    lane = lax.broadcasted_iota(I32, (1, N_EXPERTS), 1)
    for cc in range(blk_chunks):
        rs = slice(cc * CHUNK, (cc + 1) * CHUNK)
        chunk = tb * blk_chunks + cc
        w0 = _window_start(base_ref[b, e, chunk], GATHER_ALIGN_BF16, rows, win)
        pcol = jnp.sum(jnp.where(lane == e, posc_ref[0, rs, :].astype(F32), 0.0), axis=1, keepdims=True)
        gcol = jnp.sum(jnp.where(lane == e, aff_ref[0, rs, :], 0.0), axis=1, keepdims=True)
        slot = (lax.broadcasted_iota(I32, (CHUNK, win), 1) + w0).astype(F32)
        onehot = jnp.where(pcol == slot, 1.0, 0.0).astype(BF16)
        yw = ye_ref[0, 0, pl.ds(w0, win), :]
        o_ref[0, rs, :] += gcol * _dot(onehot, yw)

    @pl.when(e == N_EXPERTS - 1)
    def _():
        g2c = modc_ref[0][:, 5 * D:6 * D]
        g2x = modx_ref[0][:, 5 * D:6 * D]
        for cc in range(blk_chunks):
            rs = slice(cc * CHUNK, (cc + 1) * CHUNK)
            g2 = jnp.where(tb * blk_chunks + cc < n_ctx_chunks, g2c, g2x)
            o_ref[0, rs, :] = x_ref[0, rs, :] + g2 * o_ref[0, rs, :]


def _combine_blk_chunks(nc):
    for k in (11, 8, 6, 4, 3, 2, 1):
        if nc % k == 0:
            return k
    return 1


def _combine(x, ye, posc, aff, mods_l, bases, *, n_ctx):
    B, S, D = x.shape
    R = ye.shape[2]
    nc = S // CHUNK
    k = _combine_blk_chunks(nc)
    tb = k * CHUNK
    win = min(CHUNK + GATHER_ALIGN_BF16, R)
    kern = functools.partial(_combine_kernel, d_model=D, blk_chunks=k, n_ctx_chunks=n_ctx // CHUNK,
                             rows=R, win=win)
    modc_spec = pl.BlockSpec((1, 1, 6 * D), lambda b, t, e, bs: (B, 0, 0))
    modx_spec = pl.BlockSpec((1, 1, 6 * D), lambda b, t, e, bs: (b, 0, 0))
    return pl.pallas_call(
        kern,
        grid_spec=pltpu.PrefetchScalarGridSpec(
            num_scalar_prefetch=1,
            grid=(B, S // tb, N_EXPERTS),
            in_specs=[pl.BlockSpec((1, tb, D), lambda b, t, e, bs: (b, t, 0)),
                      pl.BlockSpec((1, 1, R, D), lambda b, t, e, bs: (b, e, 0, 0)),
                      pl.BlockSpec((1, tb, N_EXPERTS), lambda b, t, e, bs: (b, t, 0)),
                      pl.BlockSpec((1, tb, N_EXPERTS), lambda b, t, e, bs: (b, t, 0)),
                      modc_spec, modx_spec],
            out_specs=pl.BlockSpec((1, tb, D), lambda b, t, e, bs: (b, t, 0))),
        out_shape=jax.ShapeDtypeStruct((B, S, D), F32),
        compiler_params=_cparams(3, VMEM_LIMIT),
        name="moe_combine",
    )(bases, x, ye, posc, aff, mods_l, mods_l)


def _final_kernel(x_ref, g_ref, o_ref):
    o_ref[0] = _rms(x_ref[0], g_ref[...])


def _final_norm(xs, g, *, n_ctx):
    B, S, D = xs.shape
    T = S - n_ctx
    off = n_ctx // TOK_TILE
    return pl.pallas_call(
        _final_kernel,
        grid=(B, T // TOK_TILE),
        in_specs=[pl.BlockSpec((1, TOK_TILE, D), lambda b, i: (b, i + off, 0)),
                  pl.BlockSpec((1, D), lambda b, i: (0, 0))],
        out_specs=pl.BlockSpec((1, TOK_TILE, D), lambda b, i: (b, i, 0)),
        out_shape=jax.ShapeDtypeStruct((B, T, D), F32),
        compiler_params=_cparams(2),
        name="final_norm",
    )(xs, g)


def _pad_heads(w, n_heads):
    lead = w.shape[:-1]
    w = w.reshape(*lead, n_heads, HEAD)
    w = jnp.pad(w, [(0, 0)] * len(lead) + [(0, 0), (0, LANE - HEAD)])
    return w.reshape(*lead, n_heads * LANE)


def _rot_swap(w):
    half = A_ROPE // 2
    return jnp.concatenate([-w[..., half:], w[..., :half]], -1)


def _rope_tables(n_ctx, n_lat):
    rows = n_lat // GRID_W
    row_id = jnp.repeat(jnp.arange(rows, dtype=F32), GRID_W)
    col_id = jnp.tile(jnp.arange(GRID_W, dtype=F32), rows)
    n_freq = A_ROPE // 4
    inv = ROPE_BASE ** (-jnp.arange(n_freq, dtype=F32) / n_freq)
    ang = jnp.concatenate([row_id[:, None] * inv, col_id[:, None] * inv], -1)
    ang = jnp.concatenate([jnp.zeros((n_ctx, A_ROPE // 2), F32), ang], 0)
    S = n_ctx + n_lat
    cos, sin = jnp.cos(ang), jnp.sin(ang)
    pad = jnp.zeros((S, LANE - HEAD - A_ROPE), F32)
    cos128 = jnp.concatenate([jnp.ones((S, HEAD), F32), cos, cos, pad], -1)
    sin128 = jnp.concatenate([jnp.zeros((S, HEAD), F32), sin, sin, pad], -1)
    return cos128, sin128


def kernel(x, c, ctx, c_ctx, ada_w, ada_b, norm1_g, norm2_g, w_in, m_conv_w, m_conv_b, m_ib, m_fb, m_norm_g, a_qnorm_g, a_wq_up, a_kvnorm_g, a_wkv_up, w_out, router_w, e_w_gate, e_w_up, e_w_down, final_g):
    B, T, D = x.shape
    n_ctx = ctx.shape[1]
    L = ada_w.shape[0]
    S = n_ctx + T
    assert D == 16 * HEAD and n_ctx % TOK_TILE == 0 and T % TOK_TILE == 0 and T % (GRID_W * 8) == 0
    n_ctx_tiles = n_ctx // TOK_TILE
    n_ctx_chunks = n_ctx // CHUNK
    cap_ctx = EC_FACTOR * n_ctx // N_EXPERTS
    cap_lat = EC_FACTOR * T // N_EXPERTS
    slot_rows = cap_ctx + cap_lat

    o_f, o_qk, o_v, o_o, o_g, o_cq, o_ckv, o_kr = (0, 256, 768, 1024, 1280, 1296, 1680, 1936)
    w_kr = w_in[:, :, o_kr:o_kr + A_ROPE]
    slot = lambda w: jnp.pad(w, ((0, 0), (0, 0), (HEAD, LANE - HEAD - A_ROPE)))
    w_g = w_in[:, :, o_g:o_g + 16]
    w1 = jnp.concatenate([
        w_in[:, :, o_f:o_f + 256],
        _pad_heads(w_in[:, :, o_qk:o_qk + 256], M_HEADS),
        _pad_heads(w_in[:, :, o_qk + 256:o_qk + 512], M_HEADS),
        _pad_heads(w_in[:, :, o_v:o_v + 256], M_HEADS),
        _pad_heads(w_in[:, :, o_o:o_o + 256], M_HEADS),
        w_in[:, :, o_cq:o_cq + 384],
        w_in[:, :, o_ckv:o_ckv + 256],
        slot(w_kr),
        slot(_rot_swap(w_kr)),
        jnp.pad(w_g, ((0, 0), (0, 0), (0, LANE - 16))),
    ], -1).astype(BF16)
    assert w1.shape[-1] == _N_COLS
    wgt = jnp.swapaxes(w_g, 1, 2).astype(BF16)
    conv_w = jnp.concatenate([_pad_heads(m_conv_w[..., :256], M_HEADS),
                              _pad_heads(m_conv_w[..., 256:], M_HEADS)], -1)
    conv_b = jnp.concatenate([_pad_heads(m_conv_b[..., :256], M_HEADS),
                              _pad_heads(m_conv_b[..., 256:], M_HEADS)], -1)[:, None, :]
    gate_bias = jnp.stack([m_ib, m_fb], 2).reshape(L, 16)
    wq3 = a_wq_up.reshape(L, -1, A_HEADS, HEAD + A_ROPE)
    zq = jnp.zeros(wq3.shape[:-1] + (LANE - HEAD - A_ROPE,), F32)
    wq = jnp.concatenate([wq3, zq], -1).reshape(L, -1, A_HEADS * LANE)
    wqt = jnp.swapaxes(wq, 1, 2).astype(BF16)
    wkv3 = a_wkv_up.reshape(L, -1, A_HEADS, 2 * HEAD)
    zk = jnp.zeros(wkv3.shape[:-1] + (HEAD,), F32)
    wk = jnp.concatenate([wkv3[..., :HEAD], zk], -1).reshape(L, -1, A_HEADS * LANE).astype(BF16)
    zv = jnp.zeros(wkv3.shape[:-1] + (V_ROWS - HEAD,), F32)
    wvt = jnp.swapaxes(jnp.concatenate([wkv3[..., HEAD:], zv], -1).reshape(L, -1, A_HEADS * V_ROWS),
                       1, 2).astype(BF16)
    wa = w_out[:, 0:256].astype(BF16)
    wb = jnp.swapaxes(_pad_heads(jnp.swapaxes(w_out[:, 256:512], 1, 2), M_HEADS), 1, 2).astype(BF16)
    wc = w_out[:, 512:1024].astype(BF16)
    mg = _pad_heads(m_norm_g, M_HEADS)[:, None, :]
    lane = jnp.arange(M_HEADS * LANE)
    blk = jnp.where((lane[:, None] // LANE == lane[None, :] // LANE) & (lane[:, None] % LANE < HEAD),
                    1.0 / HEAD, 0.0).astype(BF16)
    cos128, sin128 = _rope_tables(n_ctx, T)
    row = jnp.arange(LANE)[:, None]
    cost128 = cos128.T
    sint128 = jnp.where(row < HEAD + A_ROPE // 2, -sin128.T, sin128.T)
    cs = _fft_chan_mats()
    attn_sub = 3 if (S // TOK_TILE) % 3 == 0 else 1
    attn_qtiles = 4 if T % (4 * TOK_TILE) == 0 else 1
    attn_q_per_step = 2 if T % (2 * attn_qtiles * TOK_TILE) == 0 else 1
    wg_e = e_w_gate.astype(BF16)
    wu_e = e_w_up.astype(BF16)
    wd_e = e_w_down.astype(BF16)

    rows16 = 16
    cvec = jnp.zeros((rows16, D), F32).at[:B].set(c).at[B].set(c_ctx)
    mods = _ada_mods(cvec, ada_w, ada_b).reshape(L, rows16, 1, 6 * D)

    xs = jnp.concatenate([ctx, x], axis=1)
    for l in range(L):
        mods_l = mods[l]
        f, qk, vm, og, gat, gatt, qt_ctx, qt_lat, k, vt4 = _inproj(
            xs, mods_l, norm1_g[l][None], w1[l], wgt[l], a_qnorm_g[l][None], a_kvnorm_g[l][None],
            wqt[l], wk[l], wvt[l], cos128, sin128, cost128, sint128,
            n_ctx_tiles=n_ctx_tiles)
        ya_ctx = _fft_ctx(f, n_ctx, cs)
        ya_lat = _fft_latent(f[:, n_ctx:], cs)
        qm, ktm, colp, rowp = _mlstm_prep(qk, gat, gatt, conv_w[l], conv_b[l], gate_bias[l][None, :],
                                          gate_bias[l][:, None], n_ctx_chunks=n_ctx_chunks)
        hf, hb = _mlstm_scan(qm, ktm, vm, colp, rowp, n_ctx_chunks=n_ctx_chunks)
        yc_ctx = _attention(qt_ctx, k, vt4, n_keys=n_ctx, qtiles=1, sub=1)
        yc_lat = _attention(qt_lat, k, vt4, n_keys=S, qtiles=attn_qtiles, sub=attn_sub,
                            q_per_step=attn_q_per_step)
        xs, hx, aff = _outproj(ya_ctx, ya_lat, hf, hb, og, yc_ctx, yc_lat, xs, mods_l, wa[l], wb[l], wc[l],
                               mg[l], blk, norm2_g[l][None], router_w[l], n_ctx_tiles=n_ctx_tiles)
        posm, bases = _route(jnp.swapaxes(aff, 1, 2), n_ctx=n_ctx, cap_ctx=cap_ctx, cap_lat=cap_lat)
        xe = _gather(hx, posm.reshape(B, N_EXPERTS, S // CHUNK, CHUNK), bases, rows=slot_rows)
        ye = _ffn(xe, wg_e[l], wu_e[l], wd_e[l])
        xs = _combine(xs, ye, jnp.swapaxes(posm, 1, 2), aff, mods_l, bases, n_ctx=n_ctx)
    return _final_norm(xs, final_g[None], n_ctx=n_ctx)
```

```python
import functools
import math

import jax
import jax.numpy as jnp
from jax import lax
from jax.experimental import pallas as pl
from jax.experimental.pallas import tpu as pltpu

F32 = jnp.float32
BF16 = jnp.bfloat16
I32 = jnp.int32

EPS = 1e-6
GRID_W = 64
ROPE_BASE = 10000.0
LANE = 128
HEAD = 64
M_HEADS = 4
M_WIDTH = M_HEADS * HEAD
A_HEADS = 8
A_ROPE = 32
V_ROWS = 80
N_EXPERTS = 16
EC_FACTOR = 2
TOK_TILE = 256
CHUNK = 128
GATHER_ALIGN_BF16 = 16
SMALL_FILL = 32
VMEM_LIMIT = 56 * 1024 * 1024


def _cparams(n_axes, vmem=None):
    return pltpu.CompilerParams(dimension_semantics=("arbitrary",) * n_axes,
                                vmem_limit_bytes=vmem)


def _dot(a, b):
    return jnp.dot(a, b, preferred_element_type=F32)


def _dot_nt(a, b):
    return lax.dot_general(a, b, (((1,), (1,)), ((), ())), preferred_element_type=F32)


def _split_bf16(a):
    hi = a.astype(BF16)
    lo = (a - hi.astype(F32)).astype(BF16)
    return hi, lo


def _sigmoid(x):
    return 1.0 / (1.0 + jnp.exp(-x))


def _rms(x, g):
    return x * lax.rsqrt(jnp.mean(x * x, axis=-1, keepdims=True) + EPS) * g


def _ada_kernel(c_ref, w_ref, b_ref, o_ref):
    a = c_ref[...]
    a = a * _sigmoid(a)
    a_hi, a_lo = _split_bf16(a)
    w_hi, w_lo = _split_bf16(w_ref[0])
    o_ref[0] = _dot(a_hi, w_hi) + _dot(a_lo, w_hi) + _dot(a_hi, w_lo) + b_ref[0]


def _ada_mods(cvec, ada_w, ada_b):
    L, D, D6 = ada_w.shape
    rows = cvec.shape[0]
    return pl.pallas_call(
        _ada_kernel,
        grid=(L, D6 // D),
        in_specs=[pl.BlockSpec((rows, D), lambda l, j: (0, 0)),
                  pl.BlockSpec((1, D, D), lambda l, j: (l, 0, j)),
                  pl.BlockSpec((1, 1, D), lambda l, j: (l, 0, j))],
        out_specs=pl.BlockSpec((1, rows, D), lambda l, j: (l, 0, j)),
        out_shape=jax.ShapeDtypeStruct((L, rows, D6), F32),
        compiler_params=_cparams(2),
        name="ada_mods",
    )(cvec, ada_w, ada_b.reshape(L, 1, D6))


_C_F = (0, 256)
_C_Q = (256, 512)
_C_K = (512, 768)
_C_V = (768, 1024)
_C_O = (1024, 1280)
_C_CQ = (1280, 1664)
_C_CKV = (1664, 1920)
_C_KR = (1920, 2048)
_C_KRS = (2048, 2176)
_C_G = (2176, 2304)
_N_COLS = 2304


def _inproj_kernel(x_ref, mod_ref, n1_ref, w1_ref, wgt_ref, qn_ref, kvn_ref, wqt_ref,
                   wk_ref, wvt_ref, cos_ref, sin_ref, cost_ref, sint_ref,
                   f_ref, qk_ref, vm_ref, o_ref, gat_ref, gatt_ref, qtc_ref, qtl_ref, k_ref, vt_ref, *,
                   d_model, q_scale, n_ctx_tiles):
    D = d_model
    i = pl.program_id(1)
    x = x_ref[0]
    mod = mod_ref[0]
    sh1, sc1 = mod[:, 0:D], mod[:, D:2 * D]
    xm = _rms(x, n1_ref[...]) * (1.0 + sc1) + sh1
    xb = xm.astype(BF16)
    u = _dot(xb, w1_ref[...])
    f_ref[0] = u[:, _C_F[0]:_C_F[1]].astype(BF16)
    qk_ref[0] = u[:, _C_Q[0]:_C_K[1]].astype(BF16)
    vm_ref[0] = u[:, _C_V[0]:_C_V[1]].astype(BF16)
    o_ref[0] = u[:, _C_O[0]:_C_O[1]].astype(BF16)
    gat_ref[0] = u[:, _C_G[0]:_C_G[0] + 16]
    gatt_ref[0] = _dot_nt(wgt_ref[...], xb)

    cqn = _rms(u[:, _C_CQ[0]:_C_CQ[1]], qn_ref[...]).astype(BF16)
    cost = jnp.tile(cost_ref[...], (A_HEADS, 1))
    sint = jnp.tile(sint_ref[...], (A_HEADS, 1))
    qa = _dot_nt(wqt_ref[...], cqn)
    half = A_ROPE // 2
    parts = []
    for h in range(A_HEADS):
        r0 = h * LANE
        parts += [qa[r0:r0 + HEAD], qa[r0 + HEAD + half:r0 + HEAD + A_ROPE],
                  qa[r0 + HEAD:r0 + HEAD + half], qa[r0 + HEAD + A_ROPE:r0 + LANE]]
    qt = ((qa * cost + jnp.concatenate(parts, axis=0) * sint) * q_scale).astype(BF16)

    @pl.when(i < n_ctx_tiles)
    def _():
        qtc_ref[0, 0] = qt

    @pl.when(i >= n_ctx_tiles)
    def _():
        qtl_ref[0, 0] = qt

    ckvn = _rms(u[:, _C_CKV[0]:_C_CKV[1]], kvn_ref[...]).astype(BF16)
    cosw = jnp.tile(cos_ref[...], (1, A_HEADS))
    sinw = jnp.tile(sin_ref[...], (1, A_HEADS))
    kw = A_HEADS * LANE
    kr = jnp.tile(u[:, _C_KR[0]:_C_KR[1]], (1, A_HEADS))
    krs = jnp.tile(u[:, _C_KRS[0]:_C_KRS[1]], (1, A_HEADS))
    k = _dot(ckvn, wk_ref[...]) + kr * cosw + krs * sinw
    k_ref[0] = k.astype(BF16)
    vrow = lax.broadcasted_iota(I32, (A_HEADS * V_ROWS, 1), 0)
    ones_a = jnp.where(vrow % V_ROWS == HEAD, 1.0, 0.0)
    vt_ref[0, 0] = (_dot_nt(wvt_ref[...], ckvn) + ones_a).astype(BF16)


def _inproj(xs, mods_l, n1, w1, wgt, qn, kvn, wqt, wk, wvt, cos128, sin128, cost128, sint128,
            *, n_ctx_tiles):
    B, S, D = xs.shape
    nt = S // TOK_TILE
    n_ctx = n_ctx_tiles * TOK_TILE
    n_lat_tiles = nt - n_ctx_tiles
    kw = A_HEADS * LANE
    tok = lambda w: pl.BlockSpec((1, TOK_TILE, w), lambda b, i: (b, i, 0))
    full = lambda a: pl.BlockSpec(a.shape, lambda b, i: (0,) * a.ndim)
    mod_spec = pl.BlockSpec((1, 1, 6 * D), lambda b, i: (jnp.where(i < n_ctx_tiles, B, b), 0, 0))
    tab_spec = pl.BlockSpec((TOK_TILE, LANE), lambda b, i: (i, 0))
    tabt_spec = pl.BlockSpec((LANE, TOK_TILE), lambda b, i: (0, i))
    sd = lambda w, dt: jax.ShapeDtypeStruct((B, S, w), dt)
    kern = functools.partial(_inproj_kernel, d_model=D, n_ctx_tiles=n_ctx_tiles,
                             q_scale=(HEAD + A_ROPE) ** -0.5 * math.log2(math.e))
    qtc_spec = pl.BlockSpec((1, 1, kw, TOK_TILE), lambda b, i: (b, jnp.minimum(i, n_ctx_tiles - 1), 0, 0))
    qtl_spec = pl.BlockSpec((1, 1, kw, TOK_TILE),
                            lambda b, i: (b, jnp.clip(i - n_ctx_tiles, 0, n_lat_tiles - 1), 0, 0))
    return pl.pallas_call(
        kern,
        grid=(B, nt),
        in_specs=[tok(D), mod_spec, full(n1), full(w1), full(wgt), full(qn), full(kvn), full(wqt),
                  full(wk), full(wvt), tab_spec, tab_spec, tabt_spec, tabt_spec],
        out_specs=[tok(256), tok(2 * M_WIDTH), tok(M_WIDTH), tok(M_WIDTH), tok(16),
                   pl.BlockSpec((1, 16, TOK_TILE), lambda b, i: (b, 0, i)),
                   qtc_spec, qtl_spec, tok(kw),
                   pl.BlockSpec((1, 1, A_HEADS * V_ROWS, TOK_TILE), lambda b, i: (b, i, 0, 0))],
        out_shape=[sd(256, BF16), sd(2 * M_WIDTH, BF16), sd(M_WIDTH, BF16), sd(M_WIDTH, BF16), sd(16, F32),
                   jax.ShapeDtypeStruct((B, 16, S), F32),
                   jax.ShapeDtypeStruct((B, n_ctx_tiles, kw, TOK_TILE), BF16),
                   jax.ShapeDtypeStruct((B, n_lat_tiles, kw, TOK_TILE), BF16),
                   sd(kw, BF16),
                   jax.ShapeDtypeStruct((B, nt, A_HEADS * V_ROWS, TOK_TILE), BF16)],
        compiler_params=_cparams(2, VMEM_LIMIT),
        name="inproj",
    )(xs, mods_l, n1, w1, wgt, qn, kvn, wqt, wk, wvt, cos128, sin128, cost128, sint128)


def _dft_mats(n):
    idx = jnp.arange(n, dtype=F32)
    ang = 2.0 * math.pi * jnp.mod(idx[:, None] * idx[None, :], n) / n
    return jnp.cos(ang), jnp.sin(ang)


def _fft_chan_mats():
    c, s = _dft_mats(HEAD)
    eye = jnp.eye(4, dtype=F32)
    return jnp.concatenate([jnp.kron(eye, c), jnp.kron(eye, s)], 0).astype(BF16)


def _fft_ctx_kernel(f_ref, ft_ref, cs_ref, o_ref, *, n, scale):
    xst = _dot(ft_ref[...], f_ref[0])
    xr = xst[:n].astype(BF16)
    xi = xst[n:].astype(BF16)
    y = _dot(xr, cs_ref[0:256, :]) + _dot(xi, cs_ref[256:512, :])
    o_ref[0] = (y * scale).astype(BF16)


def _fft_ctx(f, n_ctx, cs):
    B = f.shape[0]
    c, s = _dft_mats(n_ctx)
    ft = jnp.concatenate([c, -s], 0).astype(BF16)
    kern = functools.partial(_fft_ctx_kernel, n=n_ctx, scale=(n_ctx * HEAD) ** -0.5)
    return pl.pallas_call(
        kern,
        grid=(B,),
        in_specs=[pl.BlockSpec((1, n_ctx, 256), lambda b: (b, 0, 0)),
                  pl.BlockSpec(ft.shape, lambda b: (0, 0)),
                  pl.BlockSpec(cs.shape, lambda b: (0, 0))],
        out_specs=pl.BlockSpec((1, n_ctx, 256), lambda b: (b, 0, 0)),
        out_shape=jax.ShapeDtypeStruct((B, n_ctx, 256), BF16),
        compiler_params=_cparams(1),
        name="fft_ctx",
    )(f, ft, cs)


def _fft_stage1_kernel(x_ref, f1_ref, cw_ref, sw_ref, o_ref, *, n1):
    z = _dot(f1_ref[...], x_ref[0])
    zr, zi = z[:n1], z[n1:]
    cw, sw = cw_ref[...], sw_ref[...]
    o_ref[0, 0] = (zr * cw + zi * sw).astype(BF16)
    o_ref[0, 1] = (zi * cw - zr * sw).astype(BF16)


def _fft_stage2_kernel(z_ref, f2_ref, cs_ref, o_ref, *, tb, scale):
    for j in range(tb):
        zcat = jnp.concatenate([z_ref[0, 0, j], z_ref[0, 1, j]], axis=0)
        xst = _dot(f2_ref[...], zcat)
        xr = xst[:HEAD].astype(BF16)
        xi = xst[HEAD:].astype(BF16)
        y = _dot(xr, cs_ref[0:256, :]) + _dot(xi, cs_ref[256:512, :])
        o_ref[0, j] = (y * scale).astype(BF16)


def _fft_latent(f_lat, cs):
    B, T, W = f_lat.shape
    n2 = HEAD
    n1 = T // n2
    cols = n2 * W
    cb = 2048
    c1, s1 = _dft_mats(n1)
    f1 = jnp.concatenate([c1, -s1], 0).astype(BF16)
    t1 = jnp.arange(n1, dtype=F32)[:, None]
    s2 = jnp.arange(n2, dtype=F32)[None, :]
    ang = 2.0 * math.pi * (t1 * s2) / T
    cw = jnp.repeat(jnp.cos(ang), W, axis=1)
    sw = jnp.repeat(jnp.sin(ang), W, axis=1)
    z = pl.pallas_call(
        functools.partial(_fft_stage1_kernel, n1=n1),
        grid=(B, cols // cb),
        in_specs=[pl.BlockSpec((1, n1, cb), lambda b, j: (b, 0, j)),
                  pl.BlockSpec(f1.shape, lambda b, j: (0, 0)),
                  pl.BlockSpec((n1, cb), lambda b, j: (0, j)),
                  pl.BlockSpec((n1, cb), lambda b, j: (0, j))],
        out_specs=pl.BlockSpec((1, 2, n1, cb), lambda b, j: (b, 0, 0, j)),
        out_shape=jax.ShapeDtypeStruct((B, 2, n1, cols), BF16),
        compiler_params=_cparams(2),
        name="fft_stage1",
    )(f_lat.reshape(B, n1, cols), f1, cw, sw)
    z = z.reshape(B, 2, n1, n2, W)
    c2, s2m = _dft_mats(n2)
    f2 = jnp.concatenate([jnp.concatenate([c2, s2m], 1),
                          jnp.concatenate([-s2m, c2], 1)], 0).astype(BF16)
    tb = 8
    y = pl.pallas_call(
        functools.partial(_fft_stage2_kernel, tb=tb, scale=(T * HEAD) ** -0.5),
        grid=(B, n1 // tb),
        in_specs=[pl.BlockSpec((1, 2, tb, n2, W), lambda b, j: (b, 0, j, 0, 0)),
                  pl.BlockSpec(f2.shape, lambda b, j: (0, 0)),
                  pl.BlockSpec(cs.shape, lambda b, j: (0, 0))],
        out_specs=pl.BlockSpec((1, tb, n2, W), lambda b, j: (b, j, 0, 0)),
        out_shape=jax.ShapeDtypeStruct((B, n1, n2, W), BF16),
        compiler_params=_cparams(2),
        name="fft_stage2",
    )(z, f2, cs)
    return jnp.transpose(y, (0, 2, 1, 3)).reshape(B, T, W)


def _mlstm_prep_kernel(cur_ref, prev_ref, next_ref, cw_ref, cb_ref, gat_ref, gatt_ref, br_ref, bc_ref,
                       q_ref, kt_ref, col_ref, row_ref, *, n_chunks, n_ctx_chunks):
    c = pl.program_id(1)
    cur = cur_ref[0].astype(F32)
    first = jnp.logical_or(c == 0, c == n_ctx_chunks)
    last = jnp.logical_or(c == n_ctx_chunks - 1, c == n_chunks - 1)
    prev_row = prev_ref[0].astype(F32)[15:16, :]
    next_row = next_ref[0].astype(F32)[0:1, :]
    prev_row = jnp.where(first, 0.0, prev_row)
    next_row = jnp.where(last, 0.0, next_row)
    rows = lax.broadcasted_iota(I32, (CHUNK, 1), 0)
    up = jnp.where(rows == 0, prev_row, pltpu.roll(cur, 1, axis=0))
    dn = jnp.where(rows == CHUNK - 1, next_row, pltpu.roll(cur, CHUNK - 1, axis=0))
    y = cw_ref[0:1, :] * up + cw_ref[1:2, :] * cur + cw_ref[2:3, :] * dn + cb_ref[...]
    y = y * _sigmoid(y)
    q_ref[0] = (y[:, :M_WIDTH] * HEAD ** -0.5).astype(BF16)
    kt_ref[0] = y[:, M_WIDTH:].T.astype(BF16)

    def logsig(v):
        return jnp.minimum(v, 0.0) - jnp.log(1.0 + jnp.exp(-jnp.abs(v)))

    r_i = lax.broadcasted_iota(I32, (CHUNK, CHUNK), 0)
    c_i = lax.broadcasted_iota(I32, (CHUNK, CHUNK), 1)
    lower = jnp.where(c_i <= r_i, 1.0, 0.0).astype(BF16)
    upper = jnp.where(c_i >= r_i, 1.0, 0.0).astype(BF16)

    g = gat_ref[0] + br_ref[...]
    lane = lax.broadcasted_iota(I32, (1, 16), 1)
    lg = jnp.where(lane % 8 >= 4, logsig(g), g)
    hi, lo = _split_bf16(lg)
    cum_f = _dot(lower, hi) + _dot(lower, lo)
    cum_b = _dot(upper, hi) + _dot(upper, lo)
    cum_col = jnp.where(lane < 8, cum_f, cum_b)

    gt = gatt_ref[0] + bc_ref[...]
    row = lax.broadcasted_iota(I32, (16, 1), 0)
    lgt = jnp.where(row % 8 >= 4, logsig(gt), gt)
    hit, lot = _split_bf16(lgt)
    cum_ft = _dot(hit, upper) + _dot(lot, upper)
    cum_bt = _dot(hit, lower) + _dot(lot, lower)
    cumt = jnp.where(row < 8, cum_ft, cum_bt)
    cumt_i = pltpu.roll(cumt, 12, axis=0)
    rterm = lgt - cumt_i
    btot = jnp.broadcast_to(jnp.sum(lgt, axis=1, keepdims=True), (16, CHUNK))
    row_ref[0, 0:16] = jnp.where(row % 8 < 4, rterm, btot)
    dmax = jnp.broadcast_to(jnp.max(rterm, axis=1, keepdims=True), (16, CHUNK))
    row_ref[0, 16:32] = pltpu.roll(btot, 12, axis=0) + dmax
    seen_max = jnp.zeros((CHUNK, 16), F32)
    for d in range(2):
        seen = (c_i <= r_i) if d == 0 else (c_i >= r_i)
        for h in range(M_HEADS):
            idx = 8 * d + h
            cmx = jnp.max(jnp.where(seen, rterm[idx:idx + 1, :], -jnp.inf), axis=1, keepdims=True)
            seen_max = jnp.where(lane == idx, cmx, seen_max)
    col_ref[0] = jnp.where(lane % 8 >= 4, cum_col, seen_max)


def _mlstm_prep(qk, gat, gatt, conv_w, conv_b, bias_row, bias_col, *, n_ctx_chunks):
    B, S, W = qk.shape
    nc = S // CHUNK
    n16 = S // 16
    hw = M_WIDTH
    kern = functools.partial(_mlstm_prep_kernel, n_chunks=nc, n_ctx_chunks=n_ctx_chunks)
    full = lambda a: pl.BlockSpec(a.shape, lambda b, c: (0,) * a.ndim)
    return pl.pallas_call(
        kern,
        grid=(B, nc),
        in_specs=[pl.BlockSpec((1, CHUNK, W), lambda b, c: (b, c, 0)),
                  pl.BlockSpec((1, 16, W), lambda b, c: (b, jnp.maximum(c * 8 - 1, 0), 0)),
                  pl.BlockSpec((1, 16, W), lambda b, c: (b, jnp.minimum((c + 1) * 8, n16 - 1), 0)),
                  full(conv_w), full(conv_b),
                  pl.BlockSpec((1, CHUNK, 16), lambda b, c: (b, c, 0)),
                  pl.BlockSpec((1, 16, CHUNK), lambda b, c: (b, 0, c)),
                  full(bias_row), full(bias_col)],
        out_specs=[pl.BlockSpec((1, CHUNK, hw), lambda b, c: (b, c, 0)),
                   pl.BlockSpec((1, hw, CHUNK), lambda b, c: (b, 0, c)),
                   pl.BlockSpec((1, CHUNK, 16), lambda b, c: (b, c, 0)),
                   pl.BlockSpec((1, 32, CHUNK), lambda b, c: (b, 0, c))],
        out_shape=[jax.ShapeDtypeStruct((B, S, hw), BF16),
                   jax.ShapeDtypeStruct((B, hw, S), BF16),
                   jax.ShapeDtypeStruct((B, S, 16), F32),
                   jax.ShapeDtypeStruct((B, 32, S), F32)],
        compiler_params=_cparams(2),
        name="mlstm_prep",
    )(qk, qk, qk, conv_w, conv_b, gat, gatt, bias_row, bias_col)


def _mlstm_scan_kernel(*refs, n_batch):
    ins, (of_ref, ob_ref, c_scr, m_scr) = refs[:10], refs[10:]
    j = pl.program_id(1)

    @pl.when(j == 0)
    def _():
        c_scr[...] = jnp.zeros_like(c_scr)
        m_scr[...] = jnp.zeros_like(m_scr)

    t_i = lax.broadcasted_iota(I32, (CHUNK, CHUNK), 0)
    s_i = lax.broadcasted_iota(I32, (CHUNK, CHUNK), 1)
    lane_i = lax.broadcasted_iota(I32, (1, LANE), 1)
    row_i = lax.broadcasted_iota(I32, (LANE, 1), 0)
    for bb in range(n_batch):
        for d, o_ref in enumerate((of_ref, ob_ref)):
            q_ref, kt_ref, v_ref, col_ref, row_ref = ins[5 * d:5 * d + 5]
            colp = col_ref[bb]
            rowp = row_ref[bb]
            mask = (s_i <= t_i) if d == 0 else (s_i >= t_i)
            outs = []
            for pair_i in range(M_HEADS // 2):
                lanes = slice(pair_i * LANE, (pair_i + 1) * LANE)
                q_blk = q_ref[bb, :, lanes]
                kt_blk = kt_ref[bb, lanes, :]
                v_blk = v_ref[bb, :, lanes]
                tots = []
                for p in range(2):
                    h = 2 * pair_i + p
                    st = (bb * 2 + d) * M_HEADS + h
                    own = (lane_i < HEAD) if p == 0 else (lane_i >= HEAD)
                    own_r = (row_i < HEAD) if p == 0 else (row_i >= HEAD)
                    one_lane = HEAD if p == 0 else 0
                    full = (CHUNK, CHUNK)
                    bcol = jnp.broadcast_to(colp[:, 8 * d + 4 + h:8 * d + 5 + h], full)
                    rmax = bcol + jnp.broadcast_to(colp[:, 8 * d + h:8 * d + h + 1], full)
                    rrow = rowp[8 * d + h:8 * d + h + 1, :]
                    btot = rowp[8 * d + 4 + h:8 * d + 5 + h, :]
                    dmax = rowp[16 + 8 * d + h:17 + 8 * d + h, :]
                    m = m_scr[st, 0:1, :]
                    inter = bcol + m
                    m_t = jnp.maximum(inter, rmax)
                    w = jnp.exp(jnp.where(mask, bcol + rrow, -jnp.inf) - m_t)
                    a = jnp.exp(inter - m_t)
                    qh = jnp.where(own, q_blk, 0)
                    vh = jnp.where(own, v_blk, jnp.where(lane_i == one_lane, 1.0, 0.0).astype(BF16))
                    s = _dot(qh, kt_blk) * w
                    cst = c_scr[st]
                    tot = a * _dot(qh, cst.astype(BF16)) + _dot(s.astype(BF16), vh)
                    den = jnp.broadcast_to(tot[:, one_lane:one_lane + 1], full)
                    tots.append(tot / jnp.maximum(jnp.abs(den), jnp.exp(-m_t)))
                    bm = btot + m
                    m_new = jnp.maximum(bm, dmax)
                    ws = jnp.exp(btot + rrow - m_new)
                    kw = (jnp.where(own_r, kt_blk, 0).astype(F32) * ws).astype(BF16)
                    c_scr[st] = jnp.exp(bm - m_new) * cst + _dot(kw, vh)
                    m_scr[st] = jnp.broadcast_to(m_new, (8, LANE))
                outs.append(jnp.where(lane_i < HEAD, tots[0], tots[1]))
            o_ref[bb] = jnp.concatenate(outs, axis=1)


def _mlstm_scan(q, kt, v, colp, rowp, *, n_ctx_chunks):
    B, S, hw = q.shape
    nc = S // CHUNK
    nb = next(n for n in (4, 2, 1) if B % n == 0)

    def rev(j):
        return jnp.where(j < n_ctx_chunks, n_ctx_chunks - 1 - j, nc + n_ctx_chunks - 1 - j)

    def specs(cid):
        return [pl.BlockSpec((nb, CHUNK, hw), lambda b, j: (b, cid(j), 0)),
                pl.BlockSpec((nb, hw, CHUNK), lambda b, j: (b, 0, cid(j))),
                pl.BlockSpec((nb, CHUNK, hw), lambda b, j: (b, cid(j), 0)),
                pl.BlockSpec((nb, CHUNK, 16), lambda b, j: (b, cid(j), 0)),
                pl.BlockSpec((nb, 32, CHUNK), lambda b, j: (b, 0, cid(j)))]

    fwd = lambda j: j
    n_chains = nb * 2 * M_HEADS
    return pl.pallas_call(
        functools.partial(_mlstm_scan_kernel, n_batch=nb),
        grid=(B // nb, nc),
        in_specs=specs(fwd) + specs(rev),
        out_specs=[pl.BlockSpec((nb, CHUNK, hw), lambda b, j: (b, j, 0)),
                   pl.BlockSpec((nb, CHUNK, hw), lambda b, j: (b, rev(j), 0))],
        out_shape=[jax.ShapeDtypeStruct((B, S, hw), F32), jax.ShapeDtypeStruct((B, S, hw), F32)],
        scratch_shapes=[pltpu.VMEM((n_chains, LANE, LANE), F32), pltpu.VMEM((n_chains, 8, LANE), F32)],
        compiler_params=_cparams(2),
        name="mlstm_scan",
    )(q, kt, v, colp, rowp, q, kt, v, colp, rowp)


def _attn_kernel(qt_ref, k_ref, vt_ref, o_ref, s0_scr, s1_scr, p0_scr, p1_scr, al_scr, cm_scr, acc_scr,
                 m_scr, *,
                 n_q, n_kc, sub, qtiles):
    tk = sub * TOK_TILE
    tq = qtiles * TOK_TILE
    n_items = n_q * n_kc
    heads = [slice(hh * LANE, (hh + 1) * LANE) for hh in range(2)]
    vheads = [slice(hh * V_ROWS, (hh + 1) * V_ROWS) for hh in range(2)]
    s_bufs = (s0_scr, s1_scr)
    p_bufs = (p0_scr, p1_scr)

    def split(n):
        if n_q == 1:
            return 0, n
        qi = n // n_kc
        return qi, n - qi * n_kc

    def scores(n, par):
        qi, c = split(n)
        for hh in range(2):
            kc = k_ref[0, pl.ds(pl.multiple_of(c * tk, tk), tk), heads[hh]]
            qt = jnp.concatenate([qt_ref[0, qi * qtiles + j, heads[hh], :] for j in range(qtiles)], axis=1)
            st = _dot(kc, qt)
            s_bufs[par][hh] = st
            cm_scr[par, hh] = jnp.max(st, axis=0, keepdims=True)

    def softmax(n, par):
        _, c = split(n)
        for hh in range(2):
            st = s_bufs[par][hh]
            m = jnp.where(c == 0, -jnp.inf, m_scr[hh])
            m_new = jnp.maximum(m, cm_scr[par, hh])
            p_bufs[par][hh] = jnp.exp2(st - m_new).astype(BF16)
            al_scr[par, hh] = jnp.exp2(m - m_new)
            m_scr[hh] = m_new

    def values(n, par):
        qi, c = split(n)
        for hh in range(2):
            pv = _dot(vt_ref[0, c * sub, vheads[hh], :], p_bufs[par][hh, 0:TOK_TILE])
            for j in range(1, sub):
                pv += _dot(vt_ref[0, c * sub + j, vheads[hh], :],
                           p_bufs[par][hh, j * TOK_TILE:(j + 1) * TOK_TILE])
            acc_scr[hh] = al_scr[par, hh] * acc_scr[hh] + pv

        def finalize():
            o = jnp.concatenate([acc_scr[hh, :HEAD] / acc_scr[hh, HEAD:HEAD + 1] for hh in range(2)], axis=0)
            o_ref[0, pl.ds(pl.multiple_of(qi * tq, tq), tq), :] = o.T.astype(BF16)

        if isinstance(n, int) and n % n_kc == n_kc - 1:
            finalize()

    m_scr[...] = jnp.zeros_like(m_scr)
    acc_scr[...] = jnp.zeros_like(acc_scr)
    scores(0, 0)
    if n_items > 1:
        scores(1, 1)
    softmax(0, 0)

    def step(t, par):
        scores(t + 2, par)
        softmax(t + 1, 1 - par)
        values(t, par)

    def run(t0, t1):
        if t0 < t1 and t0 % 2:
            step(t0, 1)
            t0 += 1
        if t1 - t0 >= 2:
            def pair(i, carry):
                step(t0 + 2 * i, 0)
                step(t0 + 2 * i + 1, 1)
                return carry
            lax.fori_loop(0, (t1 - t0) // 2, pair, 0)
        if (t1 - t0) % 2:
            step(t1 - 1, (t1 - 1) % 2)

    n_full = max(n_items - 2, 0)
    t = 0
    for last in range(n_kc - 1, n_full, n_kc):
        run(t, last)
        step(last, last % 2)
        t = last + 1
    run(t, n_full)
    if n_items > 1:
        softmax(n_items - 1, (n_items - 1) % 2)
        values(n_items - 2, (n_items - 2) % 2)
    values(n_items - 1, (n_items - 1) % 2)


def _attention(qt4, k, vt4, *, n_keys, qtiles, sub, q_per_step=1):
    B, n_qb, kw, _ = qt4.shape
    n_kb = n_keys // TOK_TILE
    tk = sub * TOK_TILE
    tq = qtiles * TOK_TILE
    qb_step = q_per_step * qtiles
    kern = functools.partial(_attn_kernel, n_q=q_per_step, n_kc=n_kb // sub, sub=sub, qtiles=qtiles)
    return pl.pallas_call(
        kern,
        grid=(B, A_HEADS // 2, n_qb // qb_step),
        in_specs=[pl.BlockSpec((1, qb_step, 2 * LANE, TOK_TILE), lambda b, hp, i: (b, i, hp, 0)),
                  pl.BlockSpec((1, n_keys, 2 * LANE), lambda b, hp, i: (b, 0, hp)),
                  pl.BlockSpec((1, n_kb, 2 * V_ROWS, TOK_TILE), lambda b, hp, i: (b, 0, hp, 0))],
        out_specs=pl.BlockSpec((1, qb_step * TOK_TILE, LANE), lambda b, hp, i: (b, i, hp)),
        out_shape=jax.ShapeDtypeStruct((B, n_qb * TOK_TILE, A_HEADS * HEAD), BF16),
        scratch_shapes=[pltpu.VMEM((2, tk, tq), F32), pltpu.VMEM((2, tk, tq), F32),
                        pltpu.VMEM((2, tk, tq), BF16), pltpu.VMEM((2, tk, tq), BF16),
                        pltpu.VMEM((2, 2, 1, tq), F32), pltpu.VMEM((2, 2, 1, tq), F32),
                        pltpu.VMEM((2, V_ROWS, tq), F32), pltpu.VMEM((2, 1, tq), F32)],
        compiler_params=_cparams(3, VMEM_LIMIT),
        name="attention",
    )(qt4, k, vt4)


def _outproj_kernel(yac_ref, yal_ref, hf_ref, hb_ref, o_ref, ycc_ref, ycl_ref, x_ref, mod_ref, wa_ref, wb_ref,
                    wc_ref, mg_ref, blk_ref, n2_ref, rw_ref, xo_ref, hx_ref, aff_ref, *, d_model,
                    n_ctx_tiles):
    D = d_model
    i = pl.program_id(1)
    mod = mod_ref[0]
    g1, sh2, sc2 = mod[:, 2 * D:3 * D], mod[:, 3 * D:4 * D], mod[:, 4 * D:5 * D]
    ya = jnp.where(i < n_ctx_tiles, yac_ref[0], yal_ref[0])
    h = hf_ref[0] + hb_ref[0]
    hi, lo = _split_bf16(h * h)
    ms = _dot(hi, blk_ref[...]) + _dot(lo, blk_ref[...])
    hn = h * lax.rsqrt(ms + EPS) * mg_ref[...]
    yb = (hn * _sigmoid(o_ref[0].astype(F32))).astype(BF16)
    yc = jnp.where(i < n_ctx_tiles, ycc_ref[0], ycl_ref[0])
    mix = _dot(ya, wa_ref[...]) + _dot(yb, wb_ref[...]) + _dot(yc, wc_ref[...])
    x = x_ref[0] + g1 * mix
    xo_ref[0] = x
    hx = _rms(x, n2_ref[...]) * (1.0 + sc2) + sh2
    hx_ref[0] = hx.astype(BF16)
    h_hi, h_lo = _split_bf16(hx)
    r_hi, r_lo = _split_bf16(rw_ref[...])
    logits = _dot(h_hi, r_hi) + _dot(h_lo, r_hi) + _dot(h_hi, r_lo)
    e = jnp.exp(logits - jnp.max(logits, axis=1, keepdims=True))
    aff_ref[0] = e / jnp.sum(e, axis=1, keepdims=True)


def _outproj(ya_ctx, ya_lat, hf, hb, o, yc_ctx, yc_lat, xs, mods_l, wa, wb, wc, mg, blk, n2, rw, *,
             n_ctx_tiles):
    B, S, D = xs.shape
    nt = S // TOK_TILE
    tok = lambda w: pl.BlockSpec((1, TOK_TILE, w), lambda b, i: (b, i, 0))
    full = lambda a: pl.BlockSpec(a.shape, lambda b, i: (0,) * a.ndim)
    mod_spec = pl.BlockSpec((1, 1, 6 * D), lambda b, i: (jnp.where(i < n_ctx_tiles, B, b), 0, 0))
    hw = M_WIDTH
    n_lat_tiles = nt - n_ctx_tiles
    ctx_tok = lambda w: pl.BlockSpec((1, TOK_TILE, w), lambda b, i: (b, jnp.minimum(i, n_ctx_tiles - 1), 0))
    lat_tok = lambda w: pl.BlockSpec(
        (1, TOK_TILE, w), lambda b, i: (b, jnp.clip(i - n_ctx_tiles, 0, n_lat_tiles - 1), 0))
    kern = functools.partial(_outproj_kernel, d_model=D, n_ctx_tiles=n_ctx_tiles)
    return pl.pallas_call(
        kern,
        grid=(B, nt),
        in_specs=[ctx_tok(256), lat_tok(256),
                  tok(hw), tok(hw),
                  tok(hw), ctx_tok(A_HEADS * HEAD), lat_tok(A_HEADS * HEAD), tok(D), mod_spec,
                  full(wa), full(wb), full(wc), full(mg), full(blk), full(n2), full(rw)],
        out_specs=[tok(D), tok(D), tok(N_EXPERTS)],
        out_shape=[jax.ShapeDtypeStruct((B, S, D), F32), jax.ShapeDtypeStruct((B, S, D), BF16),
                   jax.ShapeDtypeStruct((B, S, N_EXPERTS), F32)],
        compiler_params=_cparams(2, VMEM_LIMIT),
        name="outproj",
    )(ya_ctx, ya_lat, hf, hb, o, yc_ctx, yc_lat, xs, mods_l, wa, wb, wc, mg, blk, n2, rw)


def _route_part(a, cap, slot0, tri, pos_ref, base_ref, lane0):
    n = a.shape[1]
    capf = float(cap)
    bits = pltpu.bitcast(a, I32)
    v = jnp.zeros((N_EXPERTS, 1), I32)
    for bit in range(30, -1, -1):
        cand = v | (1 << bit)
        cnt = jnp.sum(jnp.where(bits >= cand, 1.0, 0.0), axis=1, keepdims=True)
        v = jnp.where(cnt >= capf, cand, v)
    gt = bits > v
    eq = bits == v
    need = capf - jnp.sum(jnp.where(gt, 1.0, 0.0), axis=1, keepdims=True)
    idx = lax.broadcasted_iota(I32, (N_EXPERTS, n), 1)
    x = jnp.zeros((N_EXPERTS, 1), I32)
    for bit in range(max(n - 1, 1).bit_length() - 1, -1, -1):
        cand = x | (1 << bit)
        cnt = jnp.sum(jnp.where(eq, jnp.where(idx < cand, 1.0, 0.0), 0.0), axis=1, keepdims=True)
        x = jnp.where(cnt < need, cand, x)
    sel = jnp.where(gt, 1.0, jnp.where(eq, jnp.where(idx <= x, 1.0, 0.0), 0.0))
    running = jnp.zeros((N_EXPERTS, 1), F32)
    fill = jnp.zeros((N_EXPERTS, 1), F32)
    for c in range(n // CHUNK):
        blk = sel[:, c * CHUNK:(c + 1) * CHUNK]
        incl = _dot(blk.astype(BF16), tri)
        pos = running + incl - blk + float(slot0)
        cg = lane0 // CHUNK + c
        pos_ref[0, :, lane0 + c * CHUNK:lane0 + (c + 1) * CHUNK] = jnp.where(
            blk > 0.5, pos, -1.0).astype(I32)
        base_ref[0, :, cg:cg + 1] = (running + float(slot0)).astype(I32)
        running = running + incl[:, CHUNK - 1:CHUNK]
        fill = jnp.maximum(fill, incl[:, CHUNK - 1:CHUNK])
    return fill


def _route_kernel(aff_ref, pos_ref, base_ref, *, n_ctx, cap_ctx, cap_lat):
    r_i = lax.broadcasted_iota(I32, (CHUNK, CHUNK), 0)
    c_i = lax.broadcasted_iota(I32, (CHUNK, CHUNK), 1)
    tri = jnp.where(r_i <= c_i, 1.0, 0.0).astype(BF16)
    base_ref[...] = jnp.zeros_like(base_ref)
    a = aff_ref[0]
    fill_c = _route_part(a[:, :n_ctx], cap_ctx, 0, tri, pos_ref, base_ref, 0)
    fill_l = _route_part(a[:, n_ctx:], cap_lat, cap_ctx, tri, pos_ref, base_ref, n_ctx)
    base_ref[0, :, LANE - 1:LANE] = jnp.maximum(fill_c, fill_l).astype(I32)


def _route(aff_t, *, n_ctx, cap_ctx, cap_lat):
    B, E, S = aff_t.shape
    assert S // CHUNK < LANE - 1
    kern = functools.partial(_route_kernel, n_ctx=n_ctx, cap_ctx=cap_ctx, cap_lat=cap_lat)
    return pl.pallas_call(
        kern,
        grid=(B,),
        in_specs=[pl.BlockSpec((1, E, S), lambda b: (b, 0, 0))],
        out_specs=[pl.BlockSpec((1, E, S), lambda b: (b, 0, 0)),
                   pl.BlockSpec((1, E, LANE), lambda b: (b, 0, 0))],
        out_shape=[jax.ShapeDtypeStruct((B, E, S), I32), jax.ShapeDtypeStruct((B, E, LANE), I32)],
        compiler_params=_cparams(1),
        name="route",
    )(aff_t)


def _window_start(base, align, rows, win):
    w0 = lax.shift_left(lax.shift_right_logical(base, int(math.log2(align))), int(math.log2(align)))
    return pl.multiple_of(jnp.minimum(w0, rows - win), align)


def _gather_kernel(base_ref, h_ref, pos_ref, xe_ref, *, n_chunks, rows, win, unroll):
    b = pl.program_id(0)
    e = pl.program_id(2)
    xe_ref[0, 0] = jnp.zeros(xe_ref.shape[2:], BF16)

    def gather_all(win_rows):
        def group(g, carry):
            picked = []
            for u in range(unroll):
                c = g * unroll + u
                w0 = _window_start(base_ref[b, e, c], GATHER_ALIGN_BF16, rows, win_rows)
                posr = pos_ref[0, 0, pl.ds(c, 1), :]
                slot = lax.broadcasted_iota(I32, (win_rows, CHUNK), 0) + w0
                onehot = jnp.where(posr == slot, 1.0, 0.0).astype(BF16)
                hc = h_ref[0, pl.ds(pl.multiple_of(c * CHUNK, CHUNK), CHUNK), :]
                picked.append((w0, _dot(onehot, hc).astype(BF16)))
            for w0, rows_c in picked:
                xe_ref[0, 0, pl.ds(w0, win_rows), :] += rows_c
            return carry

        lax.fori_loop(0, n_chunks // unroll, group, 0)

    most = base_ref[b, e, LANE - 1]
    small = min(SMALL_FILL + GATHER_ALIGN_BF16, rows)
    if small < win:
        pl.when(most <= SMALL_FILL)(functools.partial(gather_all, small))
        pl.when(most > SMALL_FILL)(functools.partial(gather_all, win))
    else:
        gather_all(win)


def _gather(hx, pos4, bases, *, rows):
    B, S, D = hx.shape
    nc = S // CHUNK
    dh = D // 2
    win = min(CHUNK + GATHER_ALIGN_BF16, rows)
    unroll = next(u for u in (11, 8, 6, 4, 3, 2, 1) if nc % u == 0)
    kern = functools.partial(_gather_kernel, n_chunks=nc, rows=rows, win=win, unroll=unroll)
    return pl.pallas_call(
        kern,
        grid_spec=pltpu.PrefetchScalarGridSpec(
            num_scalar_prefetch=1,
            grid=(B, 2, N_EXPERTS),
            in_specs=[pl.BlockSpec((1, S, dh), lambda b, j, e, bs: (b, 0, j)),
                      pl.BlockSpec((1, 1, nc, CHUNK), lambda b, j, e, bs: (b, e, 0, 0))],
            out_specs=pl.BlockSpec((1, 1, rows, dh), lambda b, j, e, bs: (b, e, 0, j))),
        out_shape=jax.ShapeDtypeStruct((B, N_EXPERTS, rows, D), BF16),
        compiler_params=_cparams(3, VMEM_LIMIT),
        name="moe_gather",
    )(bases, hx, pos4)


def _ffn_kernel(x_ref, wg_ref, wu_ref, wd_ref, y_ref, *, row_tile, n_tiles):
    def tile(i, carry):
        r0 = pl.multiple_of(i * row_tile, 16)
        x = x_ref[0, 0, pl.ds(r0, row_tile), :]
        g = _dot(x, wg_ref[0])
        u = _dot(x, wu_ref[0])
        hid = (g * _sigmoid(g) * u).astype(BF16)
        y_ref[0, 0, pl.ds(r0, row_tile), :] = _dot(hid, wd_ref[0]).astype(BF16)
        return carry

    lax.fori_loop(0, n_tiles, tile, 0)


def _ffn_row_tile(rows):
    for t in (352, 256, 176, 128, 96, 64, 32, 16):
        if rows % t == 0:
            return t
    raise ValueError(f"expert slot rows {rows} must be a multiple of 16")


def _ffn(xe, wg, wu, wd):
    B, E, R, D = xe.shape
    F = wg.shape[-1]
    rt = _ffn_row_tile(R)
    kern = functools.partial(_ffn_kernel, row_tile=rt, n_tiles=R // rt)
    return pl.pallas_call(
        kern,
        grid=(E, B),
        in_specs=[pl.BlockSpec((1, 1, R, D), lambda e, b: (b, e, 0, 0)),
                  pl.BlockSpec((1, D, F), lambda e, b: (e, 0, 0)),
                  pl.BlockSpec((1, D, F), lambda e, b: (e, 0, 0)),
                  pl.BlockSpec((1, F, D), lambda e, b: (e, 0, 0))],
        out_specs=pl.BlockSpec((1, 1, R, D), lambda e, b: (b, e, 0, 0)),
        out_shape=jax.ShapeDtypeStruct((B, E, R, D), BF16),
        compiler_params=_cparams(2, VMEM_LIMIT),
        name="moe_ffn",
    )(xe, wg, wu, wd)


def _combine_kernel(base_ref, x_ref, ye_ref, posc_ref, aff_ref, modc_ref, modx_ref, o_ref, *, d_model,
                    blk_chunks, n_ctx_chunks, rows, win):
    D = d_model
    b = pl.program_id(0)
    tb = pl.program_id(1)
    e = pl.program_id(2)

    @pl.when(e == 0)
    def _():
        o_ref[0] = jnp.zeros(o_ref.shape[1:], F32)

    lane = lax.broadcasted_iota(I32, (1, N_EXPERTS), 1)
    for cc in range(blk_chunks):
        rs = slice(cc * CHUNK, (cc + 1) * CHUNK)
        chunk = tb * blk_chunks + cc
        w0 = _window_start(base_ref[b, e, chunk], GATHER_ALIGN_BF16, rows, win)
        pcol = jnp.sum(jnp.where(lane == e, posc_ref[0, rs, :].astype(F32), 0.0), axis=1, keepdims=True)
        gcol = jnp.sum(jnp.where(lane == e, aff_ref[0, rs, :], 0.0), axis=1, keepdims=True)
        slot = (lax.broadcasted_iota(I32, (CHUNK, win), 1) + w0).astype(F32)
        onehot = jnp.where(pcol == slot, 1.0, 0.0).astype(BF16)
        yw = ye_ref[0, 0, pl.ds(w0, win), :]
        o_ref[0, rs, :] += gcol * _dot(onehot, yw)

    @pl.when(e == N_EXPERTS - 1)
    def _():
        g2c = modc_ref[0][:, 5 * D:6 * D]
        g2x = modx_ref[0][:, 5 * D:6 * D]
        for cc in range(blk_chunks):
            rs = slice(cc * CHUNK, (cc + 1) * CHUNK)
            g2 = jnp.where(tb * blk_chunks + cc < n_ctx_chunks, g2c, g2x)
            o_ref[0, rs, :] = x_ref[0, rs, :] + g2 * o_ref[0, rs, :]


def _combine_blk_chunks(nc):
    for k in (11, 8, 6, 4, 3, 2, 1):
        if nc % k == 0:
            return k
    return 1


def _combine(x, ye, posc, aff, mods_l, bases, *, n_ctx):
    B, S, D = x.shape
    R = ye.shape[2]
    nc = S // CHUNK
    k = _combine_blk_chunks(nc)
    tb = k * CHUNK
    win = min(CHUNK + GATHER_ALIGN_BF16, R)
    kern = functools.partial(_combine_kernel, d_model=D, blk_chunks=k, n_ctx_chunks=n_ctx // CHUNK,
                             rows=R, win=win)
    modc_spec = pl.BlockSpec((1, 1, 6 * D), lambda b, t, e, bs: (B, 0, 0))
    modx_spec = pl.BlockSpec((1, 1, 6 * D), lambda b, t, e, bs: (b, 0, 0))
    return pl.pallas_call(
        kern,
        grid_spec=pltpu.PrefetchScalarGridSpec(
            num_scalar_prefetch=1,
            grid=(B, S // tb, N_EXPERTS),
            in_specs=[pl.BlockSpec((1, tb, D), lambda b, t, e, bs: (b, t, 0)),
                      pl.BlockSpec((1, 1, R, D), lambda b, t, e, bs: (b, e, 0, 0)),
                      pl.BlockSpec((1, tb, N_EXPERTS), lambda b, t, e, bs: (b, t, 0)),
                      pl.BlockSpec((1, tb, N_EXPERTS), lambda b, t, e, bs: (b, t, 0)),
                      modc_spec, modx_spec],
            out_specs=pl.BlockSpec((1, tb, D), lambda b, t, e, bs: (b, t, 0))),
        out_shape=jax.ShapeDtypeStruct((B, S, D), F32),
        compiler_params=_cparams(3, VMEM_LIMIT),
        name="moe_combine",
    )(bases, x, ye, posc, aff, mods_l, mods_l)


def _final_kernel(x_ref, g_ref, o_ref):
    o_ref[0] = _rms(x_ref[0], g_ref[...])


def _final_norm(xs, g, *, n_ctx):
    B, S, D = xs.shape
    T = S - n_ctx
    off = n_ctx // TOK_TILE
    return pl.pallas_call(
        _final_kernel,
        grid=(B, T // TOK_TILE),
        in_specs=[pl.BlockSpec((1, TOK_TILE, D), lambda b, i: (b, i + off, 0)),
                  pl.BlockSpec((1, D), lambda b, i: (0, 0))],
        out_specs=pl.BlockSpec((1, TOK_TILE, D), lambda b, i: (b, i, 0)),
        out_shape=jax.ShapeDtypeStruct((B, T, D), F32),
        compiler_params=_cparams(2),
        name="final_norm",
    )(xs, g)


def _rot_swap(w):
    half = A_ROPE // 2
    return jnp.concatenate([-w[..., half:], w[..., :half]], -1)


def _rope_tables(n_ctx, n_lat):
    rows = n_lat // GRID_W
    row_id = jnp.repeat(jnp.arange(rows, dtype=F32), GRID_W)
    col_id = jnp.tile(jnp.arange(GRID_W, dtype=F32), rows)
    n_freq = A_ROPE // 4
    inv = ROPE_BASE ** (-jnp.arange(n_freq, dtype=F32) / n_freq)
    ang = jnp.concatenate([row_id[:, None] * inv, col_id[:, None] * inv], -1)
    ang = jnp.concatenate([jnp.zeros((n_ctx, A_ROPE // 2), F32), ang], 0)
    S = n_ctx + n_lat
    cos, sin = jnp.cos(ang), jnp.sin(ang)
    pad = jnp.zeros((S, LANE - HEAD - A_ROPE), F32)
    cos128 = jnp.concatenate([jnp.ones((S, HEAD), F32), cos, cos, pad], -1)
    sin128 = jnp.concatenate([jnp.zeros((S, HEAD), F32), sin, sin, pad], -1)
    return cos128, sin128


def kernel(x, c, ctx, c_ctx, ada_w, ada_b, norm1_g, norm2_g, w_in, m_conv_w, m_conv_b, m_ib, m_fb, m_norm_g, a_qnorm_g, a_wq_up, a_kvnorm_g, a_wkv_up, w_out, router_w, e_w_gate, e_w_up, e_w_down, final_g):
    B, T, D = x.shape
    n_ctx = ctx.shape[1]
    L = ada_w.shape[0]
    S = n_ctx + T
    assert D == 16 * HEAD and n_ctx % TOK_TILE == 0 and T % TOK_TILE == 0 and T % (GRID_W * 8) == 0
    n_ctx_tiles = n_ctx // TOK_TILE
    n_ctx_chunks = n_ctx // CHUNK
    cap_ctx = EC_FACTOR * n_ctx // N_EXPERTS
    cap_lat = EC_FACTOR * T // N_EXPERTS
    slot_rows = cap_ctx + cap_lat

    o_f, o_qk, o_v, o_o, o_g, o_cq, o_ckv, o_kr = (0, 256, 768, 1024, 1280, 1296, 1680, 1936)
    w_kr = w_in[:, :, o_kr:o_kr + A_ROPE]
    slot = lambda w: jnp.pad(w, ((0, 0), (0, 0), (HEAD, LANE - HEAD - A_ROPE)))
    w_g = w_in[:, :, o_g:o_g + 16]
    w1 = jnp.concatenate([
        w_in[:, :, o_f:o_f + 256],
        w_in[:, :, o_qk:o_qk + 2 * M_WIDTH],
        w_in[:, :, o_v:o_v + M_WIDTH],
        w_in[:, :, o_o:o_o + M_WIDTH],
        w_in[:, :, o_cq:o_cq + 384],
        w_in[:, :, o_ckv:o_ckv + 256],
        slot(w_kr),
        slot(_rot_swap(w_kr)),
        jnp.pad(w_g, ((0, 0), (0, 0), (0, LANE - 16))),
    ], -1).astype(BF16)
    assert w1.shape[-1] == _N_COLS
    wgt = jnp.swapaxes(w_g, 1, 2).astype(BF16)
    conv_w = m_conv_w
    conv_b = m_conv_b[:, None, :]
    gate_bias = jnp.stack([m_ib, m_fb], 2).reshape(L, 16)
    wq3 = a_wq_up.reshape(L, -1, A_HEADS, HEAD + A_ROPE)
    zq = jnp.zeros(wq3.shape[:-1] + (LANE - HEAD - A_ROPE,), F32)
    wq = jnp.concatenate([wq3, zq], -1).reshape(L, -1, A_HEADS * LANE)
    wqt = jnp.swapaxes(wq, 1, 2).astype(BF16)
    wkv3 = a_wkv_up.reshape(L, -1, A_HEADS, 2 * HEAD)
    zk = jnp.zeros(wkv3.shape[:-1] + (HEAD,), F32)
    wk = jnp.concatenate([wkv3[..., :HEAD], zk], -1).reshape(L, -1, A_HEADS * LANE).astype(BF16)
    zv = jnp.zeros(wkv3.shape[:-1] + (V_ROWS - HEAD,), F32)
    wvt = jnp.swapaxes(jnp.concatenate([wkv3[..., HEAD:], zv], -1).reshape(L, -1, A_HEADS * V_ROWS),
                       1, 2).astype(BF16)
    wa = w_out[:, 0:256].astype(BF16)
    wb = w_out[:, 256:512].astype(BF16)
    wc = w_out[:, 512:1024].astype(BF16)
    mg = m_norm_g[:, None, :]
    lane = jnp.arange(M_WIDTH)
    blk = jnp.where(lane[:, None] // HEAD == lane[None, :] // HEAD, 1.0 / HEAD, 0.0).astype(BF16)
    cos128, sin128 = _rope_tables(n_ctx, T)
    row = jnp.arange(LANE)[:, None]
    cost128 = cos128.T
    sint128 = jnp.where(row < HEAD + A_ROPE // 2, -sin128.T, sin128.T)
    cs = _fft_chan_mats()
    attn_sub = 3 if (S // TOK_TILE) % 3 == 0 else 1
    attn_qtiles = 4 if T % (4 * TOK_TILE) == 0 else 1
    attn_q_per_step = 2 if T % (2 * attn_qtiles * TOK_TILE) == 0 else 1
    wg_e = e_w_gate.astype(BF16)
    wu_e = e_w_up.astype(BF16)
    wd_e = e_w_down.astype(BF16)

    rows16 = 16
    cvec = jnp.zeros((rows16, D), F32).at[:B].set(c).at[B].set(c_ctx)
    mods = _ada_mods(cvec, ada_w, ada_b).reshape(L, rows16, 1, 6 * D)

    xs = jnp.concatenate([ctx, x], axis=1)
    for l in range(L):
        mods_l = mods[l]
        f, qk, vm, og, gat, gatt, qt_ctx, qt_lat, k, vt4 = _inproj(
            xs, mods_l, norm1_g[l][None], w1[l], wgt[l], a_qnorm_g[l][None], a_kvnorm_g[l][None],
            wqt[l], wk[l], wvt[l], cos128, sin128, cost128, sint128,
            n_ctx_tiles=n_ctx_tiles)
        ya_ctx = _fft_ctx(f, n_ctx, cs)
        ya_lat = _fft_latent(f[:, n_ctx:], cs)
        qm, ktm, colp, rowp = _mlstm_prep(qk, gat, gatt, conv_w[l], conv_b[l], gate_bias[l][None, :],
                                          gate_bias[l][:, None], n_ctx_chunks=n_ctx_chunks)
        hf, hb = _mlstm_scan(qm, ktm, vm, colp, rowp, n_ctx_chunks=n_ctx_chunks)
        yc_ctx = _attention(qt_ctx, k, vt4, n_keys=n_ctx, qtiles=1, sub=1)
        yc_lat = _attention(qt_lat, k, vt4, n_keys=S, qtiles=attn_qtiles, sub=attn_sub,
                            q_per_step=attn_q_per_step)
        xs, hx, aff = _outproj(ya_ctx, ya_lat, hf, hb, og, yc_ctx, yc_lat, xs, mods_l, wa[l], wb[l], wc[l],
                               mg[l], blk, norm2_g[l][None], router_w[l], n_ctx_tiles=n_ctx_tiles)
        posm, bases = _route(jnp.swapaxes(aff, 1, 2), n_ctx=n_ctx, cap_ctx=cap_ctx, cap_lat=cap_lat)
        xe = _gather(hx, posm.reshape(B, N_EXPERTS, S // CHUNK, CHUNK), bases, rows=slot_rows)
        ye = _ffn(xe, wg_e[l], wu_e[l], wd_e[l])
        xs = _combine(xs, ye, jnp.swapaxes(posm, 1, 2), aff, mods_l, bases, n_ctx=n_ctx)
    return _final_norm(xs, final_g[None], n_ctx=n_ctx)
```

```python
import functools
import math

import jax
import jax.numpy as jnp
from jax import lax
from jax.experimental import pallas as pl
from jax.experimental.pallas import tpu as pltpu

F32 = jnp.float32
BF16 = jnp.bfloat16
I32 = jnp.int32

EPS = 1e-6
GRID_W = 64
ROPE_BASE = 10000.0
LANE = 128
HEAD = 64
M_HEADS = 4
M_WIDTH = M_HEADS * HEAD
A_HEADS = 8
A_ROPE = 32
V_ROWS = 80
N_EXPERTS = 16
EC_FACTOR = 2
TOK_TILE = 256
CHUNK = 128
GATHER_ALIGN_BF16 = 16
SMALL_FILL = 32
VMEM_LIMIT = 56 * 1024 * 1024


def _cparams(n_axes, vmem=None):
    return pltpu.CompilerParams(dimension_semantics=("arbitrary",) * n_axes,
                                vmem_limit_bytes=vmem)


def _dot(a, b):
    return jnp.dot(a, b, preferred_element_type=F32)


def _dot_nt(a, b):
    return lax.dot_general(a, b, (((1,), (1,)), ((), ())), preferred_element_type=F32)


def _split_bf16(a):
    hi = a.astype(BF16)
    lo = (a - hi.astype(F32)).astype(BF16)
    return hi, lo


def _sigmoid(x):
    return 1.0 / (1.0 + jnp.exp(-x))


def _rms(x, g):
    return x * lax.rsqrt(jnp.mean(x * x, axis=-1, keepdims=True) + EPS) * g


def _ada_kernel(c_ref, w_ref, b_ref, o_ref):
    a = c_ref[...]
    a = a * _sigmoid(a)
    a_hi, a_lo = _split_bf16(a)
    w_hi, w_lo = _split_bf16(w_ref[0])
    o_ref[0] = _dot(a_hi, w_hi) + _dot(a_lo, w_hi) + _dot(a_hi, w_lo) + b_ref[0]


def _ada_mods(cvec, ada_w, ada_b):
    L, D, D6 = ada_w.shape
    rows = cvec.shape[0]
    return pl.pallas_call(
        _ada_kernel,
        grid=(L, D6 // D),
        in_specs=[pl.BlockSpec((rows, D), lambda l, j: (0, 0)),
                  pl.BlockSpec((1, D, D), lambda l, j: (l, 0, j)),
                  pl.BlockSpec((1, 1, D), lambda l, j: (l, 0, j))],
        out_specs=pl.BlockSpec((1, rows, D), lambda l, j: (l, 0, j)),
        out_shape=jax.ShapeDtypeStruct((L, rows, D6), F32),
        compiler_params=_cparams(2),
        name="ada_mods",
    )(cvec, ada_w, ada_b.reshape(L, 1, D6))


_C_F = (0, 256)
_C_Q = (256, 512)
_C_K = (512, 768)
_C_V = (768, 1024)
_C_O = (1024, 1280)
_C_CQ = (1280, 1664)
_C_CKV = (1664, 1920)
_C_KR = (1920, 2048)
_C_KRS = (2048, 2176)
_C_G = (2176, 2304)
_N_COLS = 2304


def _inproj_kernel(x_ref, mod_ref, n1_ref, w1_ref, wgt_ref, qn_ref, kvn_ref, wqt_ref,
                   wk_ref, wvt_ref, cos_ref, sin_ref, cost_ref, sint_ref,
                   f_ref, qk_ref, vm_ref, o_ref, gat_ref, gatt_ref, qtc_ref, qtl_ref, k_ref, vt_ref, *,
                   d_model, q_scale, n_ctx_tiles):
    D = d_model
    i = pl.program_id(1)
    x = x_ref[0]
    mod = mod_ref[0]
    sh1, sc1 = mod[:, 0:D], mod[:, D:2 * D]
    xm = _rms(x, n1_ref[...]) * (1.0 + sc1) + sh1
    xb = xm.astype(BF16)
    u = _dot(xb, w1_ref[...])
    f_ref[0] = u[:, _C_F[0]:_C_F[1]].astype(BF16)
    qk_ref[0] = u[:, _C_Q[0]:_C_K[1]].astype(BF16)
    vm_ref[0] = u[:, _C_V[0]:_C_V[1]].astype(BF16)
    o_ref[0] = u[:, _C_O[0]:_C_O[1]].astype(BF16)
    gat_ref[0] = u[:, _C_G[0]:_C_G[0] + 16]
    gatt_ref[0] = _dot_nt(wgt_ref[...], xb)

    cqn = _rms(u[:, _C_CQ[0]:_C_CQ[1]], qn_ref[...]).astype(BF16)
    cost = jnp.tile(cost_ref[...], (A_HEADS, 1))
    sint = jnp.tile(sint_ref[...], (A_HEADS, 1))
    qa = _dot_nt(wqt_ref[...], cqn)
    half = A_ROPE // 2
    parts = []
    for h in range(A_HEADS):
        r0 = h * LANE
        parts += [qa[r0:r0 + HEAD], qa[r0 + HEAD + half:r0 + HEAD + A_ROPE],
                  qa[r0 + HEAD:r0 + HEAD + half], qa[r0 + HEAD + A_ROPE:r0 + LANE]]
    qt = ((qa * cost + jnp.concatenate(parts, axis=0) * sint) * q_scale).astype(BF16)

    @pl.when(i < n_ctx_tiles)
    def _():
        qtc_ref[0, 0] = qt

    @pl.when(i >= n_ctx_tiles)
    def _():
        qtl_ref[0, 0] = qt

    ckvn = _rms(u[:, _C_CKV[0]:_C_CKV[1]], kvn_ref[...]).astype(BF16)
    cosw = jnp.tile(cos_ref[...], (1, A_HEADS))
    sinw = jnp.tile(sin_ref[...], (1, A_HEADS))
    kw = A_HEADS * LANE
    kr = jnp.tile(u[:, _C_KR[0]:_C_KR[1]], (1, A_HEADS))
    krs = jnp.tile(u[:, _C_KRS[0]:_C_KRS[1]], (1, A_HEADS))
    k = _dot(ckvn, wk_ref[...]) + kr * cosw + krs * sinw
    k_ref[0] = k.astype(BF16)
    vrow = lax.broadcasted_iota(I32, (A_HEADS * V_ROWS, 1), 0)
    ones_a = jnp.where(vrow % V_ROWS == HEAD, 1.0, 0.0)
    vt_ref[0, 0] = (_dot_nt(wvt_ref[...], ckvn) + ones_a).astype(BF16)


def _inproj(xs, mods_l, n1, w1, wgt, qn, kvn, wqt, wk, wvt, cos128, sin128, cost128, sint128,
            *, n_ctx_tiles):
    B, S, D = xs.shape
    nt = S // TOK_TILE
    n_ctx = n_ctx_tiles * TOK_TILE
    n_lat_tiles = nt - n_ctx_tiles
    kw = A_HEADS * LANE
    tok = lambda w: pl.BlockSpec((1, TOK_TILE, w), lambda b, i: (b, i, 0))
    full = lambda a: pl.BlockSpec(a.shape, lambda b, i: (0,) * a.ndim)
    mod_spec = pl.BlockSpec((1, 1, 6 * D), lambda b, i: (jnp.where(i < n_ctx_tiles, B, b), 0, 0))
    tab_spec = pl.BlockSpec((TOK_TILE, LANE), lambda b, i: (i, 0))
    tabt_spec = pl.BlockSpec((LANE, TOK_TILE), lambda b, i: (0, i))
    sd = lambda w, dt: jax.ShapeDtypeStruct((B, S, w), dt)
    kern = functools.partial(_inproj_kernel, d_model=D, n_ctx_tiles=n_ctx_tiles,
                             q_scale=(HEAD + A_ROPE) ** -0.5 * math.log2(math.e))
    qtc_spec = pl.BlockSpec((1, 1, kw, TOK_TILE), lambda b, i: (b, jnp.minimum(i, n_ctx_tiles - 1), 0, 0))
    qtl_spec = pl.BlockSpec((1, 1, kw, TOK_TILE),
                            lambda b, i: (b, jnp.clip(i - n_ctx_tiles, 0, n_lat_tiles - 1), 0, 0))
    return pl.pallas_call(
        kern,
        grid=(B, nt),
        in_specs=[tok(D), mod_spec, full(n1), full(w1), full(wgt), full(qn), full(kvn), full(wqt),
                  full(wk), full(wvt), tab_spec, tab_spec, tabt_spec, tabt_spec],
        out_specs=[tok(256), tok(2 * M_WIDTH), tok(M_WIDTH), tok(M_WIDTH), tok(16),
                   pl.BlockSpec((1, 16, TOK_TILE), lambda b, i: (b, 0, i)),
                   qtc_spec, qtl_spec, tok(kw),
                   pl.BlockSpec((1, 1, A_HEADS * V_ROWS, TOK_TILE), lambda b, i: (b, i, 0, 0))],
        out_shape=[sd(256, BF16), sd(2 * M_WIDTH, BF16), sd(M_WIDTH, BF16), sd(M_WIDTH, BF16), sd(16, F32),
                   jax.ShapeDtypeStruct((B, 16, S), F32),
                   jax.ShapeDtypeStruct((B, n_ctx_tiles, kw, TOK_TILE), BF16),
                   jax.ShapeDtypeStruct((B, n_lat_tiles, kw, TOK_TILE), BF16),
                   sd(kw, BF16),
                   jax.ShapeDtypeStruct((B, nt, A_HEADS * V_ROWS, TOK_TILE), BF16)],
        compiler_params=_cparams(2, VMEM_LIMIT),
        name="inproj",
    )(xs, mods_l, n1, w1, wgt, qn, kvn, wqt, wk, wvt, cos128, sin128, cost128, sint128)


def _dft_mats(n):
    idx = jnp.arange(n, dtype=F32)
    ang = 2.0 * math.pi * jnp.mod(idx[:, None] * idx[None, :], n) / n
    return jnp.cos(ang), jnp.sin(ang)


def _fft_chan_mats():
    c, s = _dft_mats(HEAD)
    eye = jnp.eye(4, dtype=F32)
    return jnp.concatenate([jnp.kron(eye, c), jnp.kron(eye, s)], 0).astype(BF16)


def _fft_ctx_kernel(f_ref, ft_ref, cs_ref, o_ref, *, n, scale):
    xst = _dot(ft_ref[...], f_ref[0])
    xr = xst[:n].astype(BF16)
    xi = xst[n:].astype(BF16)
    y = _dot(xr, cs_ref[0:256, :]) + _dot(xi, cs_ref[256:512, :])
    o_ref[0] = (y * scale).astype(BF16)


def _fft_ctx(f, n_ctx, cs):
    B = f.shape[0]
    c, s = _dft_mats(n_ctx)
    ft = jnp.concatenate([c, -s], 0).astype(BF16)
    kern = functools.partial(_fft_ctx_kernel, n=n_ctx, scale=(n_ctx * HEAD) ** -0.5)
    return pl.pallas_call(
        kern,
        grid=(B,),
        in_specs=[pl.BlockSpec((1, n_ctx, 256), lambda b: (b, 0, 0)),
                  pl.BlockSpec(ft.shape, lambda b: (0, 0)),
                  pl.BlockSpec(cs.shape, lambda b: (0, 0))],
        out_specs=pl.BlockSpec((1, n_ctx, 256), lambda b: (b, 0, 0)),
        out_shape=jax.ShapeDtypeStruct((B, n_ctx, 256), BF16),
        compiler_params=_cparams(1),
        name="fft_ctx",
    )(f, ft, cs)


def _fft_stage1_kernel(x_ref, f1_ref, cw_ref, sw_ref, o_ref, *, n1):
    z = _dot(f1_ref[...], x_ref[0])
    zr, zi = z[:n1], z[n1:]
    cw, sw = cw_ref[...], sw_ref[...]
    o_ref[0, 0] = (zr * cw + zi * sw).astype(BF16)
    o_ref[0, 1] = (zi * cw - zr * sw).astype(BF16)


def _fft_stage2_kernel(z_ref, f2_ref, cs_ref, o_ref, *, tb, scale):
    for j in range(tb):
        zcat = jnp.concatenate([z_ref[0, 0, j], z_ref[0, 1, j]], axis=0)
        xst = _dot(f2_ref[...], zcat)
        xr = xst[:HEAD].astype(BF16)
        xi = xst[HEAD:].astype(BF16)
        y = _dot(xr, cs_ref[0:256, :]) + _dot(xi, cs_ref[256:512, :])
        o_ref[0, j] = (y * scale).astype(BF16)


def _fft_latent(f_lat, cs):
    B, T, W = f_lat.shape
    n2 = HEAD
    n1 = T // n2
    cols = n2 * W
    cb = 2048
    c1, s1 = _dft_mats(n1)
    f1 = jnp.concatenate([c1, -s1], 0).astype(BF16)
    t1 = jnp.arange(n1, dtype=F32)[:, None]
    s2 = jnp.arange(n2, dtype=F32)[None, :]
    ang = 2.0 * math.pi * (t1 * s2) / T
    cw = jnp.repeat(jnp.cos(ang), W, axis=1)
    sw = jnp.repeat(jnp.sin(ang), W, axis=1)
    z = pl.pallas_call(
        functools.partial(_fft_stage1_kernel, n1=n1),
        grid=(B, cols // cb),
        in_specs=[pl.BlockSpec((1, n1, cb), lambda b, j: (b, 0, j)),
                  pl.BlockSpec(f1.shape, lambda b, j: (0, 0)),
                  pl.BlockSpec((n1, cb), lambda b, j: (0, j)),
                  pl.BlockSpec((n1, cb), lambda b, j: (0, j))],
        out_specs=pl.BlockSpec((1, 2, n1, cb), lambda b, j: (b, 0, 0, j)),
        out_shape=jax.ShapeDtypeStruct((B, 2, n1, cols), BF16),
        compiler_params=_cparams(2),
        name="fft_stage1",
    )(f_lat.reshape(B, n1, cols), f1, cw, sw)
    z = z.reshape(B, 2, n1, n2, W)
    c2, s2m = _dft_mats(n2)
    f2 = jnp.concatenate([jnp.concatenate([c2, s2m], 1),
                          jnp.concatenate([-s2m, c2], 1)], 0).astype(BF16)
    tb = 8
    y = pl.pallas_call(
        functools.partial(_fft_stage2_kernel, tb=tb, scale=(T * HEAD) ** -0.5),
        grid=(B, n1 // tb),
        in_specs=[pl.BlockSpec((1, 2, tb, n2, W), lambda b, j: (b, 0, j, 0, 0)),
                  pl.BlockSpec(f2.shape, lambda b, j: (0, 0)),
                  pl.BlockSpec(cs.shape, lambda b, j: (0, 0))],
        out_specs=pl.BlockSpec((1, tb, n2, W), lambda b, j: (b, j, 0, 0)),
        out_shape=jax.ShapeDtypeStruct((B, n1, n2, W), BF16),
        compiler_params=_cparams(2),
        name="fft_stage2",
    )(z, f2, cs)
    return jnp.transpose(y, (0, 2, 1, 3)).reshape(B, T, W)


def _mlstm_prep_kernel(cur_ref, prev_ref, next_ref, cw_ref, cb_ref, gat_ref, gatt_ref, br_ref, bc_ref,
                       q_ref, kt_ref, col_ref, row_ref, *, n_chunks, n_ctx_chunks):
    c = pl.program_id(1)
    cur = cur_ref[0].astype(F32)
    first = jnp.logical_or(c == 0, c == n_ctx_chunks)
    last = jnp.logical_or(c == n_ctx_chunks - 1, c == n_chunks - 1)
    prev_row = prev_ref[0].astype(F32)[15:16, :]
    next_row = next_ref[0].astype(F32)[0:1, :]
    prev_row = jnp.where(first, 0.0, prev_row)
    next_row = jnp.where(last, 0.0, next_row)
    rows = lax.broadcasted_iota(I32, (CHUNK, 1), 0)
    up = jnp.where(rows == 0, prev_row, pltpu.roll(cur, 1, axis=0))
    dn = jnp.where(rows == CHUNK - 1, next_row, pltpu.roll(cur, CHUNK - 1, axis=0))
    y = cw_ref[0:1, :] * up + cw_ref[1:2, :] * cur + cw_ref[2:3, :] * dn + cb_ref[...]
    y = y * _sigmoid(y)
    q_ref[0] = (y[:, :M_WIDTH] * HEAD ** -0.5).astype(BF16)
    kt_ref[0] = y[:, M_WIDTH:].T.astype(BF16)

    def logsig(v):
        return jnp.minimum(v, 0.0) - jnp.log(1.0 + jnp.exp(-jnp.abs(v)))

    r_i = lax.broadcasted_iota(I32, (CHUNK, CHUNK), 0)
    c_i = lax.broadcasted_iota(I32, (CHUNK, CHUNK), 1)
    lower = jnp.where(c_i <= r_i, 1.0, 0.0).astype(BF16)
    upper = jnp.where(c_i >= r_i, 1.0, 0.0).astype(BF16)

    g = gat_ref[0] + br_ref[...]
    lane = lax.broadcasted_iota(I32, (1, 16), 1)
    lg = jnp.where(lane % 8 >= 4, logsig(g), g)
    hi, lo = _split_bf16(lg)
    cum_f = _dot(lower, hi) + _dot(lower, lo)
    cum_b = _dot(upper, hi) + _dot(upper, lo)
    cum_col = jnp.where(lane < 8, cum_f, cum_b)

    gt = gatt_ref[0] + bc_ref[...]
    row = lax.broadcasted_iota(I32, (16, 1), 0)
    lgt = jnp.where(row % 8 >= 4, logsig(gt), gt)
    hit, lot = _split_bf16(lgt)
    cum_ft = _dot(hit, upper) + _dot(lot, upper)
    cum_bt = _dot(hit, lower) + _dot(lot, lower)
    cumt = jnp.where(row < 8, cum_ft, cum_bt)
    cumt_i = pltpu.roll(cumt, 12, axis=0)
    rterm = lgt - cumt_i
    btot = jnp.broadcast_to(jnp.sum(lgt, axis=1, keepdims=True), (16, CHUNK))
    row_ref[0, 0:16] = jnp.where(row % 8 < 4, rterm, btot)
    dmax = jnp.broadcast_to(jnp.max(rterm, axis=1, keepdims=True), (16, CHUNK))
    row_ref[0, 16:32] = pltpu.roll(btot, 12, axis=0) + dmax
    seen_max = jnp.zeros((CHUNK, 16), F32)
    for d in range(2):
        seen = (c_i <= r_i) if d == 0 else (c_i >= r_i)
        for h in range(M_HEADS):
            idx = 8 * d + h
            cmx = jnp.max(jnp.where(seen, rterm[idx:idx + 1, :], -jnp.inf), axis=1, keepdims=True)
            seen_max = jnp.where(lane == idx, cmx, seen_max)
    col_ref[0] = jnp.where(lane % 8 >= 4, cum_col, seen_max)


def _mlstm_prep(qk, gat, gatt, conv_w, conv_b, bias_row, bias_col, *, n_ctx_chunks):
    B, S, W = qk.shape
    nc = S // CHUNK
    n16 = S // 16
    hw = M_WIDTH
    kern = functools.partial(_mlstm_prep_kernel, n_chunks=nc, n_ctx_chunks=n_ctx_chunks)
    full = lambda a: pl.BlockSpec(a.shape, lambda b, c: (0,) * a.ndim)
    return pl.pallas_call(
        kern,
        grid=(B, nc),
        in_specs=[pl.BlockSpec((1, CHUNK, W), lambda b, c: (b, c, 0)),
                  pl.BlockSpec((1, 16, W), lambda b, c: (b, jnp.maximum(c * 8 - 1, 0), 0)),
                  pl.BlockSpec((1, 16, W), lambda b, c: (b, jnp.minimum((c + 1) * 8, n16 - 1), 0)),
                  full(conv_w), full(conv_b),
                  pl.BlockSpec((1, CHUNK, 16), lambda b, c: (b, c, 0)),
                  pl.BlockSpec((1, 16, CHUNK), lambda b, c: (b, 0, c)),
                  full(bias_row), full(bias_col)],
        out_specs=[pl.BlockSpec((1, CHUNK, hw), lambda b, c: (b, c, 0)),
                   pl.BlockSpec((1, hw, CHUNK), lambda b, c: (b, 0, c)),
                   pl.BlockSpec((1, CHUNK, 16), lambda b, c: (b, c, 0)),
                   pl.BlockSpec((1, 32, CHUNK), lambda b, c: (b, 0, c))],
        out_shape=[jax.ShapeDtypeStruct((B, S, hw), BF16),
                   jax.ShapeDtypeStruct((B, hw, S), BF16),
                   jax.ShapeDtypeStruct((B, S, 16), F32),
                   jax.ShapeDtypeStruct((B, 32, S), F32)],
        compiler_params=_cparams(2),
        name="mlstm_prep",
    )(qk, qk, qk, conv_w, conv_b, gat, gatt, bias_row, bias_col)


def _mlstm_scan_kernel(*refs, n_batch):
    ins, (of_ref, ob_ref, c_scr, m_scr) = refs[:10], refs[10:]
    j = pl.program_id(1)

    @pl.when(j == 0)
    def _():
        c_scr[...] = jnp.zeros_like(c_scr)
        m_scr[...] = jnp.zeros_like(m_scr)

    t_i = lax.broadcasted_iota(I32, (CHUNK, CHUNK), 0)
    s_i = lax.broadcasted_iota(I32, (CHUNK, CHUNK), 1)
    lane_i = lax.broadcasted_iota(I32, (1, LANE), 1)
    row_i = lax.broadcasted_iota(I32, (LANE, 1), 0)
    for bb in range(n_batch):
        for d, o_ref in enumerate((of_ref, ob_ref)):
            q_ref, kt_ref, v_ref, col_ref, row_ref = ins[5 * d:5 * d + 5]
            colp = col_ref[bb]
            rowp = row_ref[bb]
            mask = (s_i <= t_i) if d == 0 else (s_i >= t_i)
            outs = []
            for pair_i in range(M_HEADS // 2):
                lanes = slice(pair_i * LANE, (pair_i + 1) * LANE)
                q_blk = q_ref[bb, :, lanes]
                kt_blk = kt_ref[bb, lanes, :]
                v_blk = v_ref[bb, :, lanes]
                tots = []
                for p in range(2):
                    h = 2 * pair_i + p
                    st = (bb * 2 + d) * M_HEADS + h
                    own = (lane_i < HEAD) if p == 0 else (lane_i >= HEAD)
                    own_r = (row_i < HEAD) if p == 0 else (row_i >= HEAD)
                    one_lane = HEAD if p == 0 else 0
                    full = (CHUNK, CHUNK)
                    bcol = jnp.broadcast_to(colp[:, 8 * d + 4 + h:8 * d + 5 + h], full)
                    rmax = bcol + jnp.broadcast_to(colp[:, 8 * d + h:8 * d + h + 1], full)
                    rrow = rowp[8 * d + h:8 * d + h + 1, :]
                    btot = rowp[8 * d + 4 + h:8 * d + 5 + h, :]
                    dmax = rowp[16 + 8 * d + h:17 + 8 * d + h, :]
                    m = m_scr[st, 0:1, :]
                    inter = bcol + m
                    m_t = jnp.maximum(inter, rmax)
                    w = jnp.exp(jnp.where(mask, bcol + rrow, -jnp.inf) - m_t)
                    a = jnp.exp(inter - m_t)
                    qh = jnp.where(own, q_blk, 0)
                    vh = jnp.where(own, v_blk, jnp.where(lane_i == one_lane, 1.0, 0.0).astype(BF16))
                    s = _dot(qh, kt_blk) * w
                    cst = c_scr[st]
                    tot = a * _dot(qh, cst.astype(BF16)) + _dot(s.astype(BF16), vh)
                    den = jnp.broadcast_to(tot[:, one_lane:one_lane + 1], full)
                    tots.append(tot / jnp.maximum(jnp.abs(den), jnp.exp(-m_t)))
                    bm = btot + m
                    m_new = jnp.maximum(bm, dmax)
                    ws = jnp.exp(btot + rrow - m_new)
                    kw = (jnp.where(own_r, kt_blk, 0).astype(F32) * ws).astype(BF16)
                    c_scr[st] = jnp.exp(bm - m_new) * cst + _dot(kw, vh)
                    m_scr[st] = jnp.broadcast_to(m_new, (8, LANE))
                outs.append(jnp.where(lane_i < HEAD, tots[0], tots[1]))
            o_ref[bb] = jnp.concatenate(outs, axis=1)


def _mlstm_scan(q, kt, v, colp, rowp, *, n_ctx_chunks):
    B, S, hw = q.shape
    nc = S // CHUNK
    nb = next(n for n in (4, 2, 1) if B % n == 0)

    def rev(j):
        return jnp.where(j < n_ctx_chunks, n_ctx_chunks - 1 - j, nc + n_ctx_chunks - 1 - j)

    def specs(cid):
        return [pl.BlockSpec((nb, CHUNK, hw), lambda b, j: (b, cid(j), 0)),
                pl.BlockSpec((nb, hw, CHUNK), lambda b, j: (b, 0, cid(j))),
                pl.BlockSpec((nb, CHUNK, hw), lambda b, j: (b, cid(j), 0)),
                pl.BlockSpec((nb, CHUNK, 16), lambda b, j: (b, cid(j), 0)),
                pl.BlockSpec((nb, 32, CHUNK), lambda b, j: (b, 0, cid(j)))]

    fwd = lambda j: j
    n_chains = nb * 2 * M_HEADS
    return pl.pallas_call(
        functools.partial(_mlstm_scan_kernel, n_batch=nb),
        grid=(B // nb, nc),
        in_specs=specs(fwd) + specs(rev),
        out_specs=[pl.BlockSpec((nb, CHUNK, hw), lambda b, j: (b, j, 0)),
                   pl.BlockSpec((nb, CHUNK, hw), lambda b, j: (b, rev(j), 0))],
        out_shape=[jax.ShapeDtypeStruct((B, S, hw), F32), jax.ShapeDtypeStruct((B, S, hw), F32)],
        scratch_shapes=[pltpu.VMEM((n_chains, LANE, LANE), F32), pltpu.VMEM((n_chains, 8, LANE), F32)],
        compiler_params=_cparams(2),
        name="mlstm_scan",
    )(q, kt, v, colp, rowp, q, kt, v, colp, rowp)


def _attn_kernel(qt_ref, k_ref, vt_ref, o_ref, s0_scr, s1_scr, p0_scr, p1_scr, al_scr, cm_scr, acc_scr,
                 m_scr, *,
                 n_q, n_kc, sub, qtiles):
    tk = sub * TOK_TILE
    tq = qtiles * TOK_TILE
    n_items = n_q * n_kc
    heads = [slice(hh * LANE, (hh + 1) * LANE) for hh in range(2)]
    vheads = [slice(hh * V_ROWS, (hh + 1) * V_ROWS) for hh in range(2)]
    s_bufs = (s0_scr, s1_scr)
    p_bufs = (p0_scr, p1_scr)

    def split(n):
        if n_q == 1:
            return 0, n
        qi = n // n_kc
        return qi, n - qi * n_kc

    def scores(n, par):
        qi, c = split(n)
        for hh in range(2):
            kc = k_ref[0, pl.ds(pl.multiple_of(c * tk, tk), tk), heads[hh]]
            qt = jnp.concatenate([qt_ref[0, qi * qtiles + j, heads[hh], :] for j in range(qtiles)], axis=1)
            st = _dot(kc, qt)
            s_bufs[par][hh] = st
            cm_scr[par, hh] = jnp.max(st, axis=0, keepdims=True)

    def softmax(n, par):
        _, c = split(n)
        for hh in range(2):
            st = s_bufs[par][hh]
            m = jnp.where(c == 0, -jnp.inf, m_scr[hh])
            m_new = jnp.maximum(m, cm_scr[par, hh])
            p_bufs[par][hh] = jnp.exp2(st - m_new).astype(BF16)
            al_scr[par, hh] = jnp.exp2(m - m_new)
            m_scr[hh] = m_new

    def values(n, par):
        qi, c = split(n)
        for hh in range(2):
            pv = _dot(vt_ref[0, c * sub, vheads[hh], :], p_bufs[par][hh, 0:TOK_TILE])
            for j in range(1, sub):
                pv += _dot(vt_ref[0, c * sub + j, vheads[hh], :],
                           p_bufs[par][hh, j * TOK_TILE:(j + 1) * TOK_TILE])
            acc_scr[hh] = al_scr[par, hh] * acc_scr[hh] + pv

        def finalize():
            o = jnp.concatenate([acc_scr[hh, :HEAD] / acc_scr[hh, HEAD:HEAD + 1] for hh in range(2)], axis=0)
            o_ref[0, pl.ds(pl.multiple_of(qi * tq, tq), tq), :] = o.T.astype(BF16)

        if isinstance(n, int) and n % n_kc == n_kc - 1:
            finalize()

    m_scr[...] = jnp.zeros_like(m_scr)
    acc_scr[...] = jnp.zeros_like(acc_scr)
    scores(0, 0)
    if n_items > 1:
        scores(1, 1)
    softmax(0, 0)

    def step(t, par):
        scores(t + 2, par)
        softmax(t + 1, 1 - par)
        values(t, par)

    def run(t0, t1):
        if t0 < t1 and t0 % 2:
            step(t0, 1)
            t0 += 1
        if t1 - t0 >= 2:
            def pair(i, carry):
                step(t0 + 2 * i, 0)
                step(t0 + 2 * i + 1, 1)
                return carry
            lax.fori_loop(0, (t1 - t0) // 2, pair, 0)
        if (t1 - t0) % 2:
            step(t1 - 1, (t1 - 1) % 2)

    n_full = max(n_items - 2, 0)
    t = 0
    for last in range(n_kc - 1, n_full, n_kc):
        run(t, last)
        step(last, last % 2)
        t = last + 1
    run(t, n_full)
    if n_items > 1:
        softmax(n_items - 1, (n_items - 1) % 2)
        values(n_items - 2, (n_items - 2) % 2)
    values(n_items - 1, (n_items - 1) % 2)


def _attention(qt4, k, vt4, *, n_keys, qtiles, sub, q_per_step=1):
    B, n_qb, kw, _ = qt4.shape
    n_kb = n_keys // TOK_TILE
    tk = sub * TOK_TILE
    tq = qtiles * TOK_TILE
    qb_step = q_per_step * qtiles
    kern = functools.partial(_attn_kernel, n_q=q_per_step, n_kc=n_kb // sub, sub=sub, qtiles=qtiles)
    return pl.pallas_call(
        kern,
        grid=(B, A_HEADS // 2, n_qb // qb_step),
        in_specs=[pl.BlockSpec((1, qb_step, 2 * LANE, TOK_TILE), lambda b, hp, i: (b, i, hp, 0)),
                  pl.BlockSpec((1, n_keys, 2 * LANE), lambda b, hp, i: (b, 0, hp)),
                  pl.BlockSpec((1, n_kb, 2 * V_ROWS, TOK_TILE), lambda b, hp, i: (b, 0, hp, 0))],
        out_specs=pl.BlockSpec((1, qb_step * TOK_TILE, LANE), lambda b, hp, i: (b, i, hp)),
        out_shape=jax.ShapeDtypeStruct((B, n_qb * TOK_TILE, A_HEADS * HEAD), BF16),
        scratch_shapes=[pltpu.VMEM((2, tk, tq), F32), pltpu.VMEM((2, tk, tq), F32),
                        pltpu.VMEM((2, tk, tq), BF16), pltpu.VMEM((2, tk, tq), BF16),
                        pltpu.VMEM((2, 2, 1, tq), F32), pltpu.VMEM((2, 2, 1, tq), F32),
                        pltpu.VMEM((2, V_ROWS, tq), F32), pltpu.VMEM((2, 1, tq), F32)],
        compiler_params=_cparams(3, VMEM_LIMIT),
        name="attention",
    )(qt4, k, vt4)


def _outproj_kernel(yac_ref, yal_ref, hf_ref, hb_ref, o_ref, ycc_ref, ycl_ref, x_ref, mod_ref, wa_ref, wb_ref,
                    wc_ref, mg_ref, blk_ref, n2_ref, rw_ref, xo_ref, hx_ref, aff_ref, *, d_model,
                    n_ctx_tiles):
    D = d_model
    i = pl.program_id(1)
    mod = mod_ref[0]
    g1, sh2, sc2 = mod[:, 2 * D:3 * D], mod[:, 3 * D:4 * D], mod[:, 4 * D:5 * D]
    ya = jnp.where(i < n_ctx_tiles, yac_ref[0], yal_ref[0])
    h = hf_ref[0] + hb_ref[0]
    hi, lo = _split_bf16(h * h)
    ms = _dot(hi, blk_ref[...]) + _dot(lo, blk_ref[...])
    hn = h * lax.rsqrt(ms + EPS) * mg_ref[...]
    yb = (hn * _sigmoid(o_ref[0].astype(F32))).astype(BF16)
    yc = jnp.where(i < n_ctx_tiles, ycc_ref[0], ycl_ref[0])
    mix = _dot(ya, wa_ref[...]) + _dot(yb, wb_ref[...]) + _dot(yc, wc_ref[...])
    x = x_ref[0] + g1 * mix
    xo_ref[0] = x
    hx = _rms(x, n2_ref[...]) * (1.0 + sc2) + sh2
    hx_ref[0] = hx.astype(BF16)
    h_hi, h_lo = _split_bf16(hx)
    r_hi, r_lo = _split_bf16(rw_ref[...])
    logits = _dot(h_hi, r_hi) + _dot(h_lo, r_hi) + _dot(h_hi, r_lo)
    e = jnp.exp(logits - jnp.max(logits, axis=1, keepdims=True))
    aff_ref[0] = e / jnp.sum(e, axis=1, keepdims=True)


def _outproj(ya_ctx, ya_lat, hf, hb, o, yc_ctx, yc_lat, xs, mods_l, wa, wb, wc, mg, blk, n2, rw, *,
             n_ctx_tiles):
    B, S, D = xs.shape
    nt = S // TOK_TILE
    tok = lambda w: pl.BlockSpec((1, TOK_TILE, w), lambda b, i: (b, i, 0))
    full = lambda a: pl.BlockSpec(a.shape, lambda b, i: (0,) * a.ndim)
    mod_spec = pl.BlockSpec((1, 1, 6 * D), lambda b, i: (jnp.where(i < n_ctx_tiles, B, b), 0, 0))
    hw = M_WIDTH
    n_lat_tiles = nt - n_ctx_tiles
    ctx_tok = lambda w: pl.BlockSpec((1, TOK_TILE, w), lambda b, i: (b, jnp.minimum(i, n_ctx_tiles - 1), 0))
    lat_tok = lambda w: pl.BlockSpec(
        (1, TOK_TILE, w), lambda b, i: (b, jnp.clip(i - n_ctx_tiles, 0, n_lat_tiles - 1), 0))
    kern = functools.partial(_outproj_kernel, d_model=D, n_ctx_tiles=n_ctx_tiles)
    return pl.pallas_call(
        kern,
        grid=(B, nt),
        in_specs=[ctx_tok(256), lat_tok(256),
                  tok(hw), tok(hw),
                  tok(hw), ctx_tok(A_HEADS * HEAD), lat_tok(A_HEADS * HEAD), tok(D), mod_spec,
                  full(wa), full(wb), full(wc), full(mg), full(blk), full(n2), full(rw)],
        out_specs=[tok(D), tok(D), tok(N_EXPERTS)],
        out_shape=[jax.ShapeDtypeStruct((B, S, D), F32), jax.ShapeDtypeStruct((B, S, D), BF16),
                   jax.ShapeDtypeStruct((B, S, N_EXPERTS), F32)],
        compiler_params=_cparams(2, VMEM_LIMIT),
        name="outproj",
    )(ya_ctx, ya_lat, hf, hb, o, yc_ctx, yc_lat, xs, mods_l, wa, wb, wc, mg, blk, n2, rw)


def _route_part(a, cap, slot0, tri, pos_ref, base_ref, lane0):
    n = a.shape[1]
    capf = float(cap)
    bits = pltpu.bitcast(a, I32)
    v = jnp.zeros((N_EXPERTS, 1), I32)
    for bit in range(30, -1, -1):
        cand = v | (1 << bit)
        cnt = jnp.sum(jnp.where(bits >= cand, 1.0, 0.0), axis=1, keepdims=True)
        v = jnp.where(cnt >= capf, cand, v)
    gt = bits > v
    eq = bits == v
    need = capf - jnp.sum(jnp.where(gt, 1.0, 0.0), axis=1, keepdims=True)
    idx = lax.broadcasted_iota(I32, (N_EXPERTS, n), 1)
    x = jnp.zeros((N_EXPERTS, 1), I32)
    for bit in range(max(n - 1, 1).bit_length() - 1, -1, -1):
        cand = x | (1 << bit)
        cnt = jnp.sum(jnp.where(eq, jnp.where(idx < cand, 1.0, 0.0), 0.0), axis=1, keepdims=True)
        x = jnp.where(cnt < need, cand, x)
    sel = jnp.where(gt, 1.0, jnp.where(eq, jnp.where(idx <= x, 1.0, 0.0), 0.0))
    running = jnp.zeros((N_EXPERTS, 1), F32)
    fill = jnp.zeros((N_EXPERTS, 1), F32)
    for c in range(n // CHUNK):
        blk = sel[:, c * CHUNK:(c + 1) * CHUNK]
        incl = _dot(blk.astype(BF16), tri)
        pos = running + incl - blk + float(slot0)
        cg = lane0 // CHUNK + c
        pos_ref[0, :, lane0 + c * CHUNK:lane0 + (c + 1) * CHUNK] = jnp.where(
            blk > 0.5, pos, -1.0).astype(I32)
        base_ref[0, :, cg:cg + 1] = (running + float(slot0)).astype(I32)
        running = running + incl[:, CHUNK - 1:CHUNK]
        fill = jnp.maximum(fill, incl[:, CHUNK - 1:CHUNK])
    return fill


def _route_kernel(aff_ref, pos_ref, base_ref, *, n_ctx, cap_ctx, cap_lat):
    r_i = lax.broadcasted_iota(I32, (CHUNK, CHUNK), 0)
    c_i = lax.broadcasted_iota(I32, (CHUNK, CHUNK), 1)
    tri = jnp.where(r_i <= c_i, 1.0, 0.0).astype(BF16)
    base_ref[...] = jnp.zeros_like(base_ref)
    a = aff_ref[0]
    fill_c = _route_part(a[:, :n_ctx], cap_ctx, 0, tri, pos_ref, base_ref, 0)
    fill_l = _route_part(a[:, n_ctx:], cap_lat, cap_ctx, tri, pos_ref, base_ref, n_ctx)
    base_ref[0, :, LANE - 1:LANE] = jnp.maximum(fill_c, fill_l).astype(I32)


def _route(aff_t, *, n_ctx, cap_ctx, cap_lat):
    B, E, S = aff_t.shape
    assert S // CHUNK < LANE - 1
    kern = functools.partial(_route_kernel, n_ctx=n_ctx, cap_ctx=cap_ctx, cap_lat=cap_lat)
    return pl.pallas_call(
        kern,
        grid=(B,),
        in_specs=[pl.BlockSpec((1, E, S), lambda b: (b, 0, 0))],
        out_specs=[pl.BlockSpec((1, E, S), lambda b: (b, 0, 0)),
                   pl.BlockSpec((1, E, LANE), lambda b: (b, 0, 0))],
        out_shape=[jax.ShapeDtypeStruct((B, E, S), I32), jax.ShapeDtypeStruct((B, E, LANE), I32)],
        compiler_params=_cparams(1),
        name="route",
    )(aff_t)


def _window_start(base, align, rows, win):
    w0 = lax.shift_left(lax.shift_right_logical(base, int(math.log2(align))), int(math.log2(align)))
    return pl.multiple_of(jnp.minimum(w0, rows - win), align)


def _gather_kernel(base_ref, h_ref, pos_ref, xe_ref, *, n_chunks, rows, win, unroll):
    b = pl.program_id(0)
    e = pl.program_id(2)
    xe_ref[0, 0] = jnp.zeros(xe_ref.shape[2:], BF16)

    def gather_all(win_rows):
        def group(g, carry):
            picked = []
            for u in range(unroll):
                c = g * unroll + u
                w0 = _window_start(base_ref[b, e, c], GATHER_ALIGN_BF16, rows, win_rows)
                posr = pos_ref[0, 0, pl.ds(c, 1), :]
                slot = lax.broadcasted_iota(I32, (win_rows, CHUNK), 0) + w0
                onehot = jnp.where(posr == slot, 1.0, 0.0).astype(BF16)
                hc = h_ref[0, pl.ds(pl.multiple_of(c * CHUNK, CHUNK), CHUNK), :]
                picked.append((w0, _dot(onehot, hc).astype(BF16)))
            for w0, rows_c in picked:
                xe_ref[0, 0, pl.ds(w0, win_rows), :] += rows_c
            return carry

        lax.fori_loop(0, n_chunks // unroll, group, 0)

    most = base_ref[b, e, LANE - 1]
    small = min(SMALL_FILL + GATHER_ALIGN_BF16, rows)
    if small < win:
        pl.when(most <= SMALL_FILL)(functools.partial(gather_all, small))
        pl.when(most > SMALL_FILL)(functools.partial(gather_all, win))
    else:
        gather_all(win)


def _gather(hx, pos4, bases, *, rows):
    B, S, D = hx.shape
    nc = S // CHUNK
    dh = D // 2
    win = min(CHUNK + GATHER_ALIGN_BF16, rows)
    unroll = next(u for u in (11, 8, 6, 4, 3, 2, 1) if nc % u == 0)
    kern = functools.partial(_gather_kernel, n_chunks=nc, rows=rows, win=win, unroll=unroll)
    return pl.pallas_call(
        kern,
        grid_spec=pltpu.PrefetchScalarGridSpec(
            num_scalar_prefetch=1,
            grid=(B, 2, N_EXPERTS),
            in_specs=[pl.BlockSpec((1, S, dh), lambda b, j, e, bs: (b, 0, j)),
                      pl.BlockSpec((1, 1, nc, CHUNK), lambda b, j, e, bs: (b, e, 0, 0))],
            out_specs=pl.BlockSpec((1, 1, rows, dh), lambda b, j, e, bs: (b, e, 0, j))),
        out_shape=jax.ShapeDtypeStruct((B, N_EXPERTS, rows, D), BF16),
        compiler_params=_cparams(3, VMEM_LIMIT),
        name="moe_gather",
    )(bases, hx, pos4)


def _ffn_kernel(x_ref, wg_ref, wu_ref, wd_ref, y_ref, *, row_tile, n_tiles):
    def tile(i, carry):
        r0 = pl.multiple_of(i * row_tile, 16)
        x = x_ref[0, 0, pl.ds(r0, row_tile), :]
        g = _dot(x, wg_ref[0])
        u = _dot(x, wu_ref[0])
        hid = (g * _sigmoid(g) * u).astype(BF16)
        y_ref[0, 0, pl.ds(r0, row_tile), :] = _dot(hid, wd_ref[0]).astype(BF16)
        return carry

    lax.fori_loop(0, n_tiles, tile, 0)


def _ffn_row_tile(rows):
    for t in (352, 256, 176, 128, 96, 64, 32, 16):
        if rows % t == 0:
            return t
    raise ValueError(f"expert slot rows {rows} must be a multiple of 16")


def _ffn(xe, wg, wu, wd):
    B, E, R, D = xe.shape
    F = wg.shape[-1]
    rt = _ffn_row_tile(R)
    kern = functools.partial(_ffn_kernel, row_tile=rt, n_tiles=R // rt)
    return pl.pallas_call(
        kern,
        grid=(E, B),
        in_specs=[pl.BlockSpec((1, 1, R, D), lambda e, b: (b, e, 0, 0)),
                  pl.BlockSpec((1, D, F), lambda e, b: (e, 0, 0)),
                  pl.BlockSpec((1, D, F), lambda e, b: (e, 0, 0)),
                  pl.BlockSpec((1, F, D), lambda e, b: (e, 0, 0))],
        out_specs=pl.BlockSpec((1, 1, R, D), lambda e, b: (b, e, 0, 0)),
        out_shape=jax.ShapeDtypeStruct((B, E, R, D), BF16),
        compiler_params=_cparams(2, VMEM_LIMIT),
        name="moe_ffn",
    )(xe, wg, wu, wd)


def _combine_kernel(base_ref, x_ref, ye_ref, posc_ref, aff_ref, g2c_ref, g2x_ref, o_ref, *, blk_chunks,
                    n_ctx_chunks, rows, win):
    b = pl.program_id(0)
    tb = pl.program_id(2)
    e = pl.program_id(3)

    @pl.when(e == 0)
    def _():
        o_ref[0] = jnp.zeros(o_ref.shape[1:], F32)

    lane = lax.broadcasted_iota(I32, (1, N_EXPERTS), 1)
    for cc in range(blk_chunks):
        rs = slice(cc * CHUNK, (cc + 1) * CHUNK)
        chunk = tb * blk_chunks + cc
        w0 = _window_start(base_ref[b, e, chunk], GATHER_ALIGN_BF16, rows, win)
        pcol = jnp.sum(jnp.where(lane == e, posc_ref[0, rs, :].astype(F32), 0.0), axis=1, keepdims=True)
        gcol = jnp.sum(jnp.where(lane == e, aff_ref[0, rs, :], 0.0), axis=1, keepdims=True)
        slot = (lax.broadcasted_iota(I32, (CHUNK, win), 1) + w0).astype(F32)
        onehot = jnp.where(pcol == slot, 1.0, 0.0).astype(BF16)
        yw = ye_ref[0, 0, pl.ds(w0, win), :]
        o_ref[0, rs, :] += gcol * _dot(onehot, yw)

    @pl.when(e == N_EXPERTS - 1)
    def _():
        g2c = g2c_ref[0, 0]
        g2x = g2x_ref[0, 0]
        for cc in range(blk_chunks):
            rs = slice(cc * CHUNK, (cc + 1) * CHUNK)
            g2 = jnp.where(tb * blk_chunks + cc < n_ctx_chunks, g2c, g2x)
            o_ref[0, rs, :] = x_ref[0, rs, :] + g2 * o_ref[0, rs, :]


def _combine_blk_chunks(nc):
    for k in (22, 11, 8, 6, 4, 3, 2, 1):
        if nc % k == 0:
            return k
    return 1


def _combine(x, ye, posc, aff, mods_l, bases, *, n_ctx):
    B, S, D = x.shape
    R = ye.shape[2]
    nc = S // CHUNK
    k = _combine_blk_chunks(nc)
    tb = k * CHUNK
    win = min(CHUNK + GATHER_ALIGN_BF16, R)
    kern = functools.partial(_combine_kernel, blk_chunks=k, n_ctx_chunks=n_ctx // CHUNK, rows=R, win=win)
    dh = D // 2
    mods_h = mods_l.reshape(mods_l.shape[0], 12, 1, dh)
    g2c_spec = pl.BlockSpec((1, 1, 1, dh), lambda b, j, t, e, bs: (B, 10 + j, 0, 0))
    g2x_spec = pl.BlockSpec((1, 1, 1, dh), lambda b, j, t, e, bs: (b, 10 + j, 0, 0))
    return pl.pallas_call(
        kern,
        grid_spec=pltpu.PrefetchScalarGridSpec(
            num_scalar_prefetch=1,
            grid=(B, 2, S // tb, N_EXPERTS),
            in_specs=[pl.BlockSpec((1, tb, dh), lambda b, j, t, e, bs: (b, t, j)),
                      pl.BlockSpec((1, 1, R, dh), lambda b, j, t, e, bs: (b, e, 0, j)),
                      pl.BlockSpec((1, tb, N_EXPERTS), lambda b, j, t, e, bs: (b, t, 0)),
                      pl.BlockSpec((1, tb, N_EXPERTS), lambda b, j, t, e, bs: (b, t, 0)),
                      g2c_spec, g2x_spec],
            out_specs=pl.BlockSpec((1, tb, dh), lambda b, j, t, e, bs: (b, t, j))),
        out_shape=jax.ShapeDtypeStruct((B, S, D), F32),
        compiler_params=_cparams(4, VMEM_LIMIT),
        name="moe_combine",
    )(bases, x, ye, posc, aff, mods_h, mods_h)


def _final_kernel(x_ref, g_ref, o_ref):
    o_ref[0] = _rms(x_ref[0], g_ref[...])


def _final_norm(xs, g, *, n_ctx):
    B, S, D = xs.shape
    T = S - n_ctx
    off = n_ctx // TOK_TILE
    return pl.pallas_call(
        _final_kernel,
        grid=(B, T // TOK_TILE),
        in_specs=[pl.BlockSpec((1, TOK_TILE, D), lambda b, i: (b, i + off, 0)),
                  pl.BlockSpec((1, D), lambda b, i: (0, 0))],
        out_specs=pl.BlockSpec((1, TOK_TILE, D), lambda b, i: (b, i, 0)),
        out_shape=jax.ShapeDtypeStruct((B, T, D), F32),
        compiler_params=_cparams(2),
        name="final_norm",
    )(xs, g)


def _rot_swap(w):
    half = A_ROPE // 2
    return jnp.concatenate([-w[..., half:], w[..., :half]], -1)


def _rope_tables(n_ctx, n_lat):
    rows = n_lat // GRID_W
    row_id = jnp.repeat(jnp.arange(rows, dtype=F32), GRID_W)
    col_id = jnp.tile(jnp.arange(GRID_W, dtype=F32), rows)
    n_freq = A_ROPE // 4
    inv = ROPE_BASE ** (-jnp.arange(n_freq, dtype=F32) / n_freq)
    ang = jnp.concatenate([row_id[:, None] * inv, col_id[:, None] * inv], -1)
    ang = jnp.concatenate([jnp.zeros((n_ctx, A_ROPE // 2), F32), ang], 0)
    S = n_ctx + n_lat
    cos, sin = jnp.cos(ang), jnp.sin(ang)
    pad = jnp.zeros((S, LANE - HEAD - A_ROPE), F32)
    cos128 = jnp.concatenate([jnp.ones((S, HEAD), F32), cos, cos, pad], -1)
    sin128 = jnp.concatenate([jnp.zeros((S, HEAD), F32), sin, sin, pad], -1)
    return cos128, sin128


def kernel(x, c, ctx, c_ctx, ada_w, ada_b, norm1_g, norm2_g, w_in, m_conv_w, m_conv_b, m_ib, m_fb, m_norm_g, a_qnorm_g, a_wq_up, a_kvnorm_g, a_wkv_up, w_out, router_w, e_w_gate, e_w_up, e_w_down, final_g):
    B, T, D = x.shape
    n_ctx = ctx.shape[1]
    L = ada_w.shape[0]
    S = n_ctx + T
    assert D == 16 * HEAD and n_ctx % TOK_TILE == 0 and T % TOK_TILE == 0 and T % (GRID_W * 8) == 0
    n_ctx_tiles = n_ctx // TOK_TILE
    n_ctx_chunks = n_ctx // CHUNK
    cap_ctx = EC_FACTOR * n_ctx // N_EXPERTS
    cap_lat = EC_FACTOR * T // N_EXPERTS
    slot_rows = cap_ctx + cap_lat

    o_f, o_qk, o_v, o_o, o_g, o_cq, o_ckv, o_kr = (0, 256, 768, 1024, 1280, 1296, 1680, 1936)
    w_kr = w_in[:, :, o_kr:o_kr + A_ROPE]
    slot = lambda w: jnp.pad(w, ((0, 0), (0, 0), (HEAD, LANE - HEAD - A_ROPE)))
    w_g = w_in[:, :, o_g:o_g + 16]
    w1 = jnp.concatenate([
        w_in[:, :, o_f:o_f + 256],
        w_in[:, :, o_qk:o_qk + 2 * M_WIDTH],
        w_in[:, :, o_v:o_v + M_WIDTH],
        w_in[:, :, o_o:o_o + M_WIDTH],
        w_in[:, :, o_cq:o_cq + 384],
        w_in[:, :, o_ckv:o_ckv + 256],
        slot(w_kr),
        slot(_rot_swap(w_kr)),
        jnp.pad(w_g, ((0, 0), (0, 0), (0, LANE - 16))),
    ], -1).astype(BF16)
    assert w1.shape[-1] == _N_COLS
    wgt = jnp.swapaxes(w_g, 1, 2).astype(BF16)
    conv_w = m_conv_w
    conv_b = m_conv_b[:, None, :]
    gate_bias = jnp.stack([m_ib, m_fb], 2).reshape(L, 16)
    wq3 = a_wq_up.reshape(L, -1, A_HEADS, HEAD + A_ROPE)
    zq = jnp.zeros(wq3.shape[:-1] + (LANE - HEAD - A_ROPE,), F32)
    wq = jnp.concatenate([wq3, zq], -1).reshape(L, -1, A_HEADS * LANE)
    wqt = jnp.swapaxes(wq, 1, 2).astype(BF16)
    wkv3 = a_wkv_up.reshape(L, -1, A_HEADS, 2 * HEAD)
    zk = jnp.zeros(wkv3.shape[:-1] + (HEAD,), F32)
    wk = jnp.concatenate([wkv3[..., :HEAD], zk], -1).reshape(L, -1, A_HEADS * LANE).astype(BF16)
    zv = jnp.zeros(wkv3.shape[:-1] + (V_ROWS - HEAD,), F32)
    wvt = jnp.swapaxes(jnp.concatenate([wkv3[..., HEAD:], zv], -1).reshape(L, -1, A_HEADS * V_ROWS),
                       1, 2).astype(BF16)
    wa = w_out[:, 0:256].astype(BF16)
    wb = w_out[:, 256:512].astype(BF16)
    wc = w_out[:, 512:1024].astype(BF16)
    mg = m_norm_g[:, None, :]
    lane = jnp.arange(M_WIDTH)
    blk = jnp.where(lane[:, None] // HEAD == lane[None, :] // HEAD, 1.0 / HEAD, 0.0).astype(BF16)
    cos128, sin128 = _rope_tables(n_ctx, T)
    row = jnp.arange(LANE)[:, None]
    cost128 = cos128.T
    sint128 = jnp.where(row < HEAD + A_ROPE // 2, -sin128.T, sin128.T)
    cs = _fft_chan_mats()
    attn_sub = 3 if (S // TOK_TILE) % 3 == 0 else 1
    attn_qtiles = 4 if T % (4 * TOK_TILE) == 0 else 1
    attn_q_per_step = next(n for n in (4, 2, 1) if T % (n * attn_qtiles * TOK_TILE) == 0)
    wg_e = e_w_gate.astype(BF16)
    wu_e = e_w_up.astype(BF16)
    wd_e = e_w_down.astype(BF16)

    rows16 = 16
    cvec = jnp.zeros((rows16, D), F32).at[:B].set(c).at[B].set(c_ctx)
    mods = _ada_mods(cvec, ada_w, ada_b).reshape(L, rows16, 1, 6 * D)

    xs = jnp.concatenate([ctx, x], axis=1)
    for l in range(L):
        mods_l = mods[l]
        f, qk, vm, og, gat, gatt, qt_ctx, qt_lat, k, vt4 = _inproj(
            xs, mods_l, norm1_g[l][None], w1[l], wgt[l], a_qnorm_g[l][None], a_kvnorm_g[l][None],
            wqt[l], wk[l], wvt[l], cos128, sin128, cost128, sint128,
            n_ctx_tiles=n_ctx_tiles)
        ya_ctx = _fft_ctx(f, n_ctx, cs)
        ya_lat = _fft_latent(f[:, n_ctx:], cs)
        qm, ktm, colp, rowp = _mlstm_prep(qk, gat, gatt, conv_w[l], conv_b[l], gate_bias[l][None, :],
                                          gate_bias[l][:, None], n_ctx_chunks=n_ctx_chunks)
        hf, hb = _mlstm_scan(qm, ktm, vm, colp, rowp, n_ctx_chunks=n_ctx_chunks)
        yc_ctx = _attention(qt_ctx, k, vt4, n_keys=n_ctx, qtiles=1, sub=1)
        yc_lat = _attention(qt_lat, k, vt4, n_keys=S, qtiles=attn_qtiles, sub=attn_sub,
                            q_per_step=attn_q_per_step)
        xs, hx, aff = _outproj(ya_ctx, ya_lat, hf, hb, og, yc_ctx, yc_lat, xs, mods_l, wa[l], wb[l], wc[l],
                               mg[l], blk, norm2_g[l][None], router_w[l], n_ctx_tiles=n_ctx_tiles)
        posm, bases = _route(jnp.swapaxes(aff, 1, 2), n_ctx=n_ctx, cap_ctx=cap_ctx, cap_lat=cap_lat)
        xe = _gather(hx, posm.reshape(B, N_EXPERTS, S // CHUNK, CHUNK), bases, rows=slot_rows)
        ye = _ffn(xe, wg_e[l], wu_e[l], wd_e[l])
        xs = _combine(xs, ye, jnp.swapaxes(posm, 1, 2), aff, mods_l, bases, n_ctx=n_ctx)
    return _final_norm(xs, final_g[None], n_ctx=n_ctx)
```

```python
import functools
import math

import jax
import jax.numpy as jnp
from jax import lax
from jax.experimental import pallas as pl
from jax.experimental.pallas import tpu as pltpu

F32 = jnp.float32
BF16 = jnp.bfloat16
I32 = jnp.int32

EPS = 1e-6
GRID_W = 64
ROPE_BASE = 10000.0
LANE = 128
HEAD = 64
M_HEADS = 4
M_WIDTH = M_HEADS * HEAD
A_HEADS = 8
A_ROPE = 32
V_ROWS = 80
N_EXPERTS = 16
EC_FACTOR = 2
TOK_TILE = 256
CHUNK = 128
GATHER_ALIGN_BF16 = 16
SMALL_FILL = 32
VMEM_LIMIT = 56 * 1024 * 1024


def _cparams(n_axes, vmem=None):
    return pltpu.CompilerParams(dimension_semantics=("arbitrary",) * n_axes,
                                vmem_limit_bytes=vmem)


def _dot(a, b):
    return jnp.dot(a, b, preferred_element_type=F32)


def _dot_nt(a, b):
    return lax.dot_general(a, b, (((1,), (1,)), ((), ())), preferred_element_type=F32)


def _split_bf16(a):
    hi = a.astype(BF16)
    lo = (a - hi.astype(F32)).astype(BF16)
    return hi, lo


def _sigmoid(x):
    return 1.0 / (1.0 + jnp.exp(-x))


def _rms(x, g):
    return x * lax.rsqrt(jnp.mean(x * x, axis=-1, keepdims=True) + EPS) * g


def _ada_kernel(c_ref, w_ref, b_ref, o_ref):
    a = c_ref[...]
    a = a * _sigmoid(a)
    a_hi, a_lo = _split_bf16(a)
    w_hi, w_lo = _split_bf16(w_ref[0])
    o_ref[0] = _dot(a_hi, w_hi) + _dot(a_lo, w_hi) + _dot(a_hi, w_lo) + b_ref[0]


def _ada_mods(cvec, ada_w, ada_b):
    L, D, D6 = ada_w.shape
    rows = cvec.shape[0]
    return pl.pallas_call(
        _ada_kernel,
        grid=(L, D6 // D),
        in_specs=[pl.BlockSpec((rows, D), lambda l, j: (0, 0)),
                  pl.BlockSpec((1, D, D), lambda l, j: (l, 0, j)),
                  pl.BlockSpec((1, 1, D), lambda l, j: (l, 0, j))],
        out_specs=pl.BlockSpec((1, rows, D), lambda l, j: (l, 0, j)),
        out_shape=jax.ShapeDtypeStruct((L, rows, D6), F32),
        compiler_params=_cparams(2),
        name="ada_mods",
    )(cvec, ada_w, ada_b.reshape(L, 1, D6))


_C_F = (0, 256)
_C_Q = (256, 512)
_C_K = (512, 768)
_C_V = (768, 1024)
_C_O = (1024, 1280)
_C_CQ = (1280, 1664)
_C_CKV = (1664, 1920)
_C_KR = (1920, 2048)
_C_KRS = (2048, 2176)
_C_G = (2176, 2304)
_N_COLS = 2304


def _inproj_kernel(x_ref, mod_ref, n1_ref, w1_ref, wgt_ref, qn_ref, kvn_ref, wqt_ref,
                   wk_ref, wvt_ref, cos_ref, sin_ref, cost_ref, sint_ref,
                   f_ref, qk_ref, vm_ref, o_ref, gat_ref, gatt_ref, qtc_ref, qtl_ref, k_ref, vt_ref, *,
                   d_model, q_scale, n_ctx_tiles):
    D = d_model
    i = pl.program_id(1)
    x = x_ref[0]
    mod = mod_ref[0]
    sh1, sc1 = mod[:, 0:D], mod[:, D:2 * D]
    xm = _rms(x, n1_ref[...]) * (1.0 + sc1) + sh1
    xb = xm.astype(BF16)
    u = _dot(xb, w1_ref[...])
    f_ref[0] = u[:, _C_F[0]:_C_F[1]].astype(BF16)
    qk_ref[0] = u[:, _C_Q[0]:_C_K[1]].astype(BF16)
    vm_ref[0] = u[:, _C_V[0]:_C_V[1]].astype(BF16)
    o_ref[0] = u[:, _C_O[0]:_C_O[1]].astype(BF16)
    gat_ref[0] = u[:, _C_G[0]:_C_G[0] + 16]
    gatt_ref[0] = _dot_nt(wgt_ref[...], xb)

    cqn = _rms(u[:, _C_CQ[0]:_C_CQ[1]], qn_ref[...]).astype(BF16)
    cost = jnp.tile(cost_ref[...], (A_HEADS, 1))
    sint = jnp.tile(sint_ref[...], (A_HEADS, 1))
    qa = _dot_nt(wqt_ref[...], cqn)
    half = A_ROPE // 2
    parts = []
    for h in range(A_HEADS):
        r0 = h * LANE
        parts += [qa[r0:r0 + HEAD], qa[r0 + HEAD + half:r0 + HEAD + A_ROPE],
                  qa[r0 + HEAD:r0 + HEAD + half], qa[r0 + HEAD + A_ROPE:r0 + LANE]]
    qt = ((qa * cost + jnp.concatenate(parts, axis=0) * sint) * q_scale).astype(BF16)

    @pl.when(i < n_ctx_tiles)
    def _():
        qtc_ref[0, 0] = qt

    @pl.when(i >= n_ctx_tiles)
    def _():
        qtl_ref[0, 0] = qt

    ckvn = _rms(u[:, _C_CKV[0]:_C_CKV[1]], kvn_ref[...]).astype(BF16)
    cosw = jnp.tile(cos_ref[...], (1, A_HEADS))
    sinw = jnp.tile(sin_ref[...], (1, A_HEADS))
    kw = A_HEADS * LANE
    kr = jnp.tile(u[:, _C_KR[0]:_C_KR[1]], (1, A_HEADS))
    krs = jnp.tile(u[:, _C_KRS[0]:_C_KRS[1]], (1, A_HEADS))
    k = _dot(ckvn, wk_ref[...]) + kr * cosw + krs * sinw
    k_ref[0] = k.astype(BF16)
    vrow = lax.broadcasted_iota(I32, (A_HEADS * V_ROWS, 1), 0)
    ones_a = jnp.where(vrow % V_ROWS == HEAD, 1.0, 0.0)
    vt_ref[0, 0] = (_dot_nt(wvt_ref[...], ckvn) + ones_a).astype(BF16)


def _inproj(xs, mods_l, n1, w1, wgt, qn, kvn, wqt, wk, wvt, cos128, sin128, cost128, sint128,
            *, n_ctx_tiles):
    B, S, D = xs.shape
    nt = S // TOK_TILE
    n_ctx = n_ctx_tiles * TOK_TILE
    n_lat_tiles = nt - n_ctx_tiles
    kw = A_HEADS * LANE
    tok = lambda w: pl.BlockSpec((1, TOK_TILE, w), lambda b, i: (b, i, 0))
    full = lambda a: pl.BlockSpec(a.shape, lambda b, i: (0,) * a.ndim)
    mod_spec = pl.BlockSpec((1, 1, 6 * D), lambda b, i: (jnp.where(i < n_ctx_tiles, B, b), 0, 0))
    tab_spec = pl.BlockSpec((TOK_TILE, LANE), lambda b, i: (i, 0))
    tabt_spec = pl.BlockSpec((LANE, TOK_TILE), lambda b, i: (0, i))
    sd = lambda w, dt: jax.ShapeDtypeStruct((B, S, w), dt)
    kern = functools.partial(_inproj_kernel, d_model=D, n_ctx_tiles=n_ctx_tiles,
                             q_scale=(HEAD + A_ROPE) ** -0.5 * math.log2(math.e))
    qtc_spec = pl.BlockSpec((1, 1, kw, TOK_TILE), lambda b, i: (b, jnp.minimum(i, n_ctx_tiles - 1), 0, 0))
    qtl_spec = pl.BlockSpec((1, 1, kw, TOK_TILE),
                            lambda b, i: (b, jnp.clip(i - n_ctx_tiles, 0, n_lat_tiles - 1), 0, 0))
    return pl.pallas_call(
        kern,
        grid=(B, nt),
        in_specs=[tok(D), mod_spec, full(n1), full(w1), full(wgt), full(qn), full(kvn), full(wqt),
                  full(wk), full(wvt), tab_spec, tab_spec, tabt_spec, tabt_spec],
        out_specs=[tok(256), tok(2 * M_WIDTH), tok(M_WIDTH), tok(M_WIDTH), tok(16),
                   pl.BlockSpec((1, 16, TOK_TILE), lambda b, i: (b, 0, i)),
                   qtc_spec, qtl_spec, tok(kw),
                   pl.BlockSpec((1, 1, A_HEADS * V_ROWS, TOK_TILE), lambda b, i: (b, i, 0, 0))],
        out_shape=[sd(256, BF16), sd(2 * M_WIDTH, BF16), sd(M_WIDTH, BF16), sd(M_WIDTH, BF16), sd(16, F32),
                   jax.ShapeDtypeStruct((B, 16, S), F32),
                   jax.ShapeDtypeStruct((B, n_ctx_tiles, kw, TOK_TILE), BF16),
                   jax.ShapeDtypeStruct((B, n_lat_tiles, kw, TOK_TILE), BF16),
                   sd(kw, BF16),
                   jax.ShapeDtypeStruct((B, nt, A_HEADS * V_ROWS, TOK_TILE), BF16)],
        compiler_params=_cparams(2, VMEM_LIMIT),
        name="inproj",
    )(xs, mods_l, n1, w1, wgt, qn, kvn, wqt, wk, wvt, cos128, sin128, cost128, sint128)


def _dft_mats(n):
    idx = jnp.arange(n, dtype=F32)
    ang = 2.0 * math.pi * jnp.mod(idx[:, None] * idx[None, :], n) / n
    return jnp.cos(ang), jnp.sin(ang)


def _fft_chan_mats():
    c, s = _dft_mats(HEAD)
    eye = jnp.eye(4, dtype=F32)
    return jnp.concatenate([jnp.kron(eye, c), jnp.kron(eye, s)], 0).astype(BF16)


def _fft_ctx_kernel(f_ref, ft_ref, cs_ref, o_ref, *, n, scale):
    xst = _dot(ft_ref[...], f_ref[0])
    xr = xst[:n].astype(BF16)
    xi = xst[n:].astype(BF16)
    y = _dot(xr, cs_ref[0:256, :]) + _dot(xi, cs_ref[256:512, :])
    o_ref[0] = (y * scale).astype(BF16)


def _fft_ctx(f, n_ctx, cs):
    B = f.shape[0]
    c, s = _dft_mats(n_ctx)
    ft = jnp.concatenate([c, -s], 0).astype(BF16)
    kern = functools.partial(_fft_ctx_kernel, n=n_ctx, scale=(n_ctx * HEAD) ** -0.5)
    return pl.pallas_call(
        kern,
        grid=(B,),
        in_specs=[pl.BlockSpec((1, n_ctx, 256), lambda b: (b, 0, 0)),
                  pl.BlockSpec(ft.shape, lambda b: (0, 0)),
                  pl.BlockSpec(cs.shape, lambda b: (0, 0))],
        out_specs=pl.BlockSpec((1, n_ctx, 256), lambda b: (b, 0, 0)),
        out_shape=jax.ShapeDtypeStruct((B, n_ctx, 256), BF16),
        compiler_params=_cparams(1),
        name="fft_ctx",
    )(f, ft, cs)


def _fft_stage1_kernel(x_ref, f1_ref, cw_ref, sw_ref, o_ref, *, n1):
    z = _dot(f1_ref[...], x_ref[0])
    zr, zi = z[:n1], z[n1:]
    cw, sw = cw_ref[...], sw_ref[...]
    o_ref[0, 0] = (zr * cw + zi * sw).astype(BF16)
    o_ref[0, 1] = (zi * cw - zr * sw).astype(BF16)


def _fft_stage2_kernel(z_ref, f2_ref, cs_ref, o_ref, *, tb, scale):
    for j in range(tb):
        zcat = jnp.concatenate([z_ref[0, 0, j], z_ref[0, 1, j]], axis=0)
        xst = _dot(f2_ref[...], zcat)
        xr = xst[:HEAD].astype(BF16)
        xi = xst[HEAD:].astype(BF16)
        y = _dot(xr, cs_ref[0:256, :]) + _dot(xi, cs_ref[256:512, :])
        o_ref[0, j] = (y * scale).astype(BF16)


def _fft_latent(f_lat, cs):
    B, T, W = f_lat.shape
    n2 = HEAD
    n1 = T // n2
    cols = n2 * W
    cb = 2048
    c1, s1 = _dft_mats(n1)
    f1 = jnp.concatenate([c1, -s1], 0).astype(BF16)
    t1 = jnp.arange(n1, dtype=F32)[:, None]
    s2 = jnp.arange(n2, dtype=F32)[None, :]
    ang = 2.0 * math.pi * (t1 * s2) / T
    cw = jnp.repeat(jnp.cos(ang), W, axis=1)
    sw = jnp.repeat(jnp.sin(ang), W, axis=1)
    z = pl.pallas_call(
        functools.partial(_fft_stage1_kernel, n1=n1),
        grid=(B, cols // cb),
        in_specs=[pl.BlockSpec((1, n1, cb), lambda b, j: (b, 0, j)),
                  pl.BlockSpec(f1.shape, lambda b, j: (0, 0)),
                  pl.BlockSpec((n1, cb), lambda b, j: (0, j)),
                  pl.BlockSpec((n1, cb), lambda b, j: (0, j))],
        out_specs=pl.BlockSpec((1, 2, n1, cb), lambda b, j: (b, 0, 0, j)),
        out_shape=jax.ShapeDtypeStruct((B, 2, n1, cols), BF16),
        compiler_params=_cparams(2),
        name="fft_stage1",
    )(f_lat.reshape(B, n1, cols), f1, cw, sw)
    z = z.reshape(B, 2, n1, n2, W)
    c2, s2m = _dft_mats(n2)
    f2 = jnp.concatenate([jnp.concatenate([c2, s2m], 1),
                          jnp.concatenate([-s2m, c2], 1)], 0).astype(BF16)
    tb = 8
    y = pl.pallas_call(
        functools.partial(_fft_stage2_kernel, tb=tb, scale=(T * HEAD) ** -0.5),
        grid=(B, n1 // tb),
        in_specs=[pl.BlockSpec((1, 2, tb, n2, W), lambda b, j: (b, 0, j, 0, 0)),
                  pl.BlockSpec(f2.shape, lambda b, j: (0, 0)),
                  pl.BlockSpec(cs.shape, lambda b, j: (0, 0))],
        out_specs=pl.BlockSpec((1, tb, n2, W), lambda b, j: (b, j, 0, 0)),
        out_shape=jax.ShapeDtypeStruct((B, n1, n2, W), BF16),
        compiler_params=_cparams(2),
        name="fft_stage2",
    )(z, f2, cs)
    return jnp.transpose(y, (0, 2, 1, 3)).reshape(B, T, W)


def _mlstm_prep_kernel(cur_ref, prev_ref, next_ref, cw_ref, cb_ref, gat_ref, gatt_ref, br_ref, bc_ref,
                       q_ref, kt_ref, col_ref, row_ref, *, n_chunks, n_ctx_chunks):
    c = pl.program_id(1)
    cur = cur_ref[0].astype(F32)
    first = jnp.logical_or(c == 0, c == n_ctx_chunks)
    last = jnp.logical_or(c == n_ctx_chunks - 1, c == n_chunks - 1)
    prev_row = prev_ref[0].astype(F32)[15:16, :]
    next_row = next_ref[0].astype(F32)[0:1, :]
    prev_row = jnp.where(first, 0.0, prev_row)
    next_row = jnp.where(last, 0.0, next_row)
    rows = lax.broadcasted_iota(I32, (CHUNK, 1), 0)
    up = jnp.where(rows == 0, prev_row, pltpu.roll(cur, 1, axis=0))
    dn = jnp.where(rows == CHUNK - 1, next_row, pltpu.roll(cur, CHUNK - 1, axis=0))
    y = cw_ref[0:1, :] * up + cw_ref[1:2, :] * cur + cw_ref[2:3, :] * dn + cb_ref[...]
    y = y * _sigmoid(y)
    q_ref[0] = (y[:, :M_WIDTH] * HEAD ** -0.5).astype(BF16)
    kt_ref[0] = y[:, M_WIDTH:].T.astype(BF16)

    def logsig(v):
        return jnp.minimum(v, 0.0) - jnp.log(1.0 + jnp.exp(-jnp.abs(v)))

    r_i = lax.broadcasted_iota(I32, (CHUNK, CHUNK), 0)
    c_i = lax.broadcasted_iota(I32, (CHUNK, CHUNK), 1)
    lower = jnp.where(c_i <= r_i, 1.0, 0.0).astype(BF16)
    upper = jnp.where(c_i >= r_i, 1.0, 0.0).astype(BF16)

    g = gat_ref[0] + br_ref[...]
    lane = lax.broadcasted_iota(I32, (1, 16), 1)
    lg = jnp.where(lane % 8 >= 4, logsig(g), g)
    hi, lo = _split_bf16(lg)
    cum_f = _dot(lower, hi) + _dot(lower, lo)
    cum_b = _dot(upper, hi) + _dot(upper, lo)
    cum_col = jnp.where(lane < 8, cum_f, cum_b)

    gt = gatt_ref[0] + bc_ref[...]
    row = lax.broadcasted_iota(I32, (16, 1), 0)
    lgt = jnp.where(row % 8 >= 4, logsig(gt), gt)
    hit, lot = _split_bf16(lgt)
    cum_ft = _dot(hit, upper) + _dot(lot, upper)
    cum_bt = _dot(hit, lower) + _dot(lot, lower)
    cumt = jnp.where(row < 8, cum_ft, cum_bt)
    cumt_i = pltpu.roll(cumt, 12, axis=0)
    rterm = lgt - cumt_i
    btot = jnp.broadcast_to(jnp.sum(lgt, axis=1, keepdims=True), (16, CHUNK))
    row_ref[0, 0:16] = jnp.where(row % 8 < 4, rterm, btot)
    dmax = jnp.broadcast_to(jnp.max(rterm, axis=1, keepdims=True), (16, CHUNK))
    row_ref[0, 16:32] = pltpu.roll(btot, 12, axis=0) + dmax
    seen_max = jnp.zeros((CHUNK, 16), F32)
    for d in range(2):
        seen = (c_i <= r_i) if d == 0 else (c_i >= r_i)
        for h in range(M_HEADS):
            idx = 8 * d + h
            cmx = jnp.max(jnp.where(seen, rterm[idx:idx + 1, :], -jnp.inf), axis=1, keepdims=True)
            seen_max = jnp.where(lane == idx, cmx, seen_max)
    col_ref[0] = jnp.where(lane % 8 >= 4, cum_col, seen_max)


def _mlstm_prep(qk, gat, gatt, conv_w, conv_b, bias_row, bias_col, *, n_ctx_chunks):
    B, S, W = qk.shape
    nc = S // CHUNK
    n16 = S // 16
    hw = M_WIDTH
    kern = functools.partial(_mlstm_prep_kernel, n_chunks=nc, n_ctx_chunks=n_ctx_chunks)
    full = lambda a: pl.BlockSpec(a.shape, lambda b, c: (0,) * a.ndim)
    return pl.pallas_call(
        kern,
        grid=(B, nc),
        in_specs=[pl.BlockSpec((1, CHUNK, W), lambda b, c: (b, c, 0)),
                  pl.BlockSpec((1, 16, W), lambda b, c: (b, jnp.maximum(c * 8 - 1, 0), 0)),
                  pl.BlockSpec((1, 16, W), lambda b, c: (b, jnp.minimum((c + 1) * 8, n16 - 1), 0)),
                  full(conv_w), full(conv_b),
                  pl.BlockSpec((1, CHUNK, 16), lambda b, c: (b, c, 0)),
                  pl.BlockSpec((1, 16, CHUNK), lambda b, c: (b, 0, c)),
                  full(bias_row), full(bias_col)],
        out_specs=[pl.BlockSpec((1, CHUNK, hw), lambda b, c: (b, c, 0)),
                   pl.BlockSpec((1, hw, CHUNK), lambda b, c: (b, 0, c)),
                   pl.BlockSpec((1, CHUNK, 16), lambda b, c: (b, c, 0)),
                   pl.BlockSpec((1, 32, CHUNK), lambda b, c: (b, 0, c))],
        out_shape=[jax.ShapeDtypeStruct((B, S, hw), BF16),
                   jax.ShapeDtypeStruct((B, hw, S), BF16),
                   jax.ShapeDtypeStruct((B, S, 16), F32),
                   jax.ShapeDtypeStruct((B, 32, S), F32)],
        compiler_params=_cparams(2),
        name="mlstm_prep",
    )(qk, qk, qk, conv_w, conv_b, gat, gatt, bias_row, bias_col)


def _mlstm_scan_kernel(*refs, n_batch):
    ins, (of_ref, ob_ref, c_scr, m_scr) = refs[:10], refs[10:]
    j = pl.program_id(1)

    @pl.when(j == 0)
    def _():
        c_scr[...] = jnp.zeros_like(c_scr)
        m_scr[...] = jnp.zeros_like(m_scr)

    t_i = lax.broadcasted_iota(I32, (CHUNK, CHUNK), 0)
    s_i = lax.broadcasted_iota(I32, (CHUNK, CHUNK), 1)
    lane_i = lax.broadcasted_iota(I32, (1, LANE), 1)
    row_i = lax.broadcasted_iota(I32, (LANE, 1), 0)
    for bb in range(n_batch):
        for d, o_ref in enumerate((of_ref, ob_ref)):
            q_ref, kt_ref, v_ref, col_ref, row_ref = ins[5 * d:5 * d + 5]
            colp = col_ref[bb]
            rowp = row_ref[bb]
            mask = (s_i <= t_i) if d == 0 else (s_i >= t_i)
            outs = []
            for pair_i in range(M_HEADS // 2):
                lanes = slice(pair_i * LANE, (pair_i + 1) * LANE)
                q_blk = q_ref[bb, :, lanes]
                kt_blk = kt_ref[bb, lanes, :]
                v_blk = v_ref[bb, :, lanes]
                tots = []
                for p in range(2):
                    h = 2 * pair_i + p
                    st = (bb * 2 + d) * M_HEADS + h
                    own = (lane_i < HEAD) if p == 0 else (lane_i >= HEAD)
                    own_r = (row_i < HEAD) if p == 0 else (row_i >= HEAD)
                    one_lane = HEAD if p == 0 else 0
                    full = (CHUNK, CHUNK)
                    bcol = jnp.broadcast_to(colp[:, 8 * d + 4 + h:8 * d + 5 + h], full)
                    rmax = bcol + jnp.broadcast_to(colp[:, 8 * d + h:8 * d + h + 1], full)
                    rrow = rowp[8 * d + h:8 * d + h + 1, :]
                    btot = rowp[8 * d + 4 + h:8 * d + 5 + h, :]
                    dmax = rowp[16 + 8 * d + h:17 + 8 * d + h, :]
                    m = m_scr[st, 0:1, :]
                    inter = bcol + m
                    m_t = jnp.maximum(inter, rmax)
                    w = jnp.exp(jnp.where(mask, bcol + rrow, -jnp.inf) - m_t)
                    a = jnp.exp(inter - m_t)
                    qh = jnp.where(own, q_blk, 0)
                    vh = jnp.where(own, v_blk, jnp.where(lane_i == one_lane, 1.0, 0.0).astype(BF16))
                    s = _dot(qh, kt_blk) * w
                    cst = c_scr[st]
                    tot = a * _dot(qh, cst.astype(BF16)) + _dot(s.astype(BF16), vh)
                    den = jnp.broadcast_to(tot[:, one_lane:one_lane + 1], full)
                    tots.append(tot / jnp.maximum(jnp.abs(den), jnp.exp(-m_t)))
                    bm = btot + m
                    m_new = jnp.maximum(bm, dmax)
                    ws = jnp.exp(btot + rrow - m_new)
                    kw = (jnp.where(own_r, kt_blk, 0).astype(F32) * ws).astype(BF16)
                    c_scr[st] = jnp.exp(bm - m_new) * cst + _dot(kw, vh)
                    m_scr[st] = jnp.broadcast_to(m_new, (8, LANE))
                outs.append(jnp.where(lane_i < HEAD, tots[0], tots[1]))
            o_ref[bb] = jnp.concatenate(outs, axis=1)


def _mlstm_scan(q, kt, v, colp, rowp, *, n_ctx_chunks):
    B, S, hw = q.shape
    nc = S // CHUNK
    nb = next(n for n in (4, 2, 1) if B % n == 0)

    def rev(j):
        return jnp.where(j < n_ctx_chunks, n_ctx_chunks - 1 - j, nc + n_ctx_chunks - 1 - j)

    def specs(cid):
        return [pl.BlockSpec((nb, CHUNK, hw), lambda b, j: (b, cid(j), 0)),
                pl.BlockSpec((nb, hw, CHUNK), lambda b, j: (b, 0, cid(j))),
                pl.BlockSpec((nb, CHUNK, hw), lambda b, j: (b, cid(j), 0)),
                pl.BlockSpec((nb, CHUNK, 16), lambda b, j: (b, cid(j), 0)),
                pl.BlockSpec((nb, 32, CHUNK), lambda b, j: (b, 0, cid(j)))]

    fwd = lambda j: j
    n_chains = nb * 2 * M_HEADS
    return pl.pallas_call(
        functools.partial(_mlstm_scan_kernel, n_batch=nb),
        grid=(B // nb, nc),
        in_specs=specs(fwd) + specs(rev),
        out_specs=[pl.BlockSpec((nb, CHUNK, hw), lambda b, j: (b, j, 0)),
                   pl.BlockSpec((nb, CHUNK, hw), lambda b, j: (b, rev(j), 0))],
        out_shape=[jax.ShapeDtypeStruct((B, S, hw), F32), jax.ShapeDtypeStruct((B, S, hw), F32)],
        scratch_shapes=[pltpu.VMEM((n_chains, LANE, LANE), F32), pltpu.VMEM((n_chains, 8, LANE), F32)],
        compiler_params=_cparams(2),
        name="mlstm_scan",
    )(q, kt, v, colp, rowp, q, kt, v, colp, rowp)


def _attn_kernel(qt_ref, k_ref, vt_ref, o_ref, s0_scr, s1_scr, p0_scr, p1_scr, al_scr, cm_scr, acc_scr,
                 m_scr, *,
                 n_q, n_kc, sub, qtiles):
    tk = sub * TOK_TILE
    tq = qtiles * TOK_TILE
    n_items = n_q * n_kc
    heads = [slice(hh * LANE, (hh + 1) * LANE) for hh in range(2)]
    vheads = [slice(hh * V_ROWS, (hh + 1) * V_ROWS) for hh in range(2)]
    s_bufs = (s0_scr, s1_scr)
    p_bufs = (p0_scr, p1_scr)

    def split(n):
        if n_q == 1:
            return 0, n
        qi = n // n_kc
        return qi, n - qi * n_kc

    def scores(n, par):
        qi, c = split(n)
        for hh in range(2):
            kc = k_ref[0, pl.ds(pl.multiple_of(c * tk, tk), tk), heads[hh]]
            qt = jnp.concatenate([qt_ref[0, qi * qtiles + j, heads[hh], :] for j in range(qtiles)], axis=1)
            st = _dot(kc, qt)
            s_bufs[par][hh] = st
            cm_scr[par, hh] = jnp.max(st, axis=0, keepdims=True)

    def softmax(n, par):
        _, c = split(n)
        for hh in range(2):
            st = s_bufs[par][hh]
            m = jnp.where(c == 0, -jnp.inf, m_scr[hh])
            m_new = jnp.maximum(m, cm_scr[par, hh])
            p_bufs[par][hh] = jnp.exp2(st - m_new).astype(BF16)
            al_scr[par, hh] = jnp.exp2(m - m_new)
            m_scr[hh] = m_new

    def values(n, par):
        qi, c = split(n)
        for hh in range(2):
            pv = _dot(vt_ref[0, c * sub, vheads[hh], :], p_bufs[par][hh, 0:TOK_TILE])
            for j in range(1, sub):
                pv += _dot(vt_ref[0, c * sub + j, vheads[hh], :],
                           p_bufs[par][hh, j * TOK_TILE:(j + 1) * TOK_TILE])
            acc_scr[hh] = al_scr[par, hh] * acc_scr[hh] + pv

        def finalize():
            o = jnp.concatenate([acc_scr[hh, :HEAD] / acc_scr[hh, HEAD:HEAD + 1] for hh in range(2)], axis=0)
            o_ref[0, pl.ds(pl.multiple_of(qi * tq, tq), tq), :] = o.T.astype(BF16)

        if isinstance(n, int) and n % n_kc == n_kc - 1:
            finalize()

    m_scr[...] = jnp.zeros_like(m_scr)
    acc_scr[...] = jnp.zeros_like(acc_scr)
    scores(0, 0)
    if n_items > 1:
        scores(1, 1)
    softmax(0, 0)

    def step(t, par):
        scores(t + 2, par)
        softmax(t + 1, 1 - par)
        values(t, par)

    def run(t0, t1):
        if t0 < t1 and t0 % 2:
            step(t0, 1)
            t0 += 1
        if t1 - t0 >= 2:
            def pair(i, carry):
                step(t0 + 2 * i, 0)
                step(t0 + 2 * i + 1, 1)
                return carry
            lax.fori_loop(0, (t1 - t0) // 2, pair, 0)
        if (t1 - t0) % 2:
            step(t1 - 1, (t1 - 1) % 2)

    n_full = max(n_items - 2, 0)
    t = 0
    for last in range(n_kc - 1, n_full, n_kc):
        run(t, last)
        step(last, last % 2)
        t = last + 1
    run(t, n_full)
    if n_items > 1:
        softmax(n_items - 1, (n_items - 1) % 2)
        values(n_items - 2, (n_items - 2) % 2)
    values(n_items - 1, (n_items - 1) % 2)


def _attention(qt4, k, vt4, *, n_keys, qtiles, sub, q_per_step=1):
    B, n_qb, kw, _ = qt4.shape
    n_kb = n_keys // TOK_TILE
    tk = sub * TOK_TILE
    tq = qtiles * TOK_TILE
    qb_step = q_per_step * qtiles
    kern = functools.partial(_attn_kernel, n_q=q_per_step, n_kc=n_kb // sub, sub=sub, qtiles=qtiles)
    return pl.pallas_call(
        kern,
        grid=(B, A_HEADS // 2, n_qb // qb_step),
        in_specs=[pl.BlockSpec((1, qb_step, 2 * LANE, TOK_TILE), lambda b, hp, i: (b, i, hp, 0)),
                  pl.BlockSpec((1, n_keys, 2 * LANE), lambda b, hp, i: (b, 0, hp)),
                  pl.BlockSpec((1, n_kb, 2 * V_ROWS, TOK_TILE), lambda b, hp, i: (b, 0, hp, 0))],
        out_specs=pl.BlockSpec((1, qb_step * TOK_TILE, LANE), lambda b, hp, i: (b, i, hp)),
        out_shape=jax.ShapeDtypeStruct((B, n_qb * TOK_TILE, A_HEADS * HEAD), BF16),
        scratch_shapes=[pltpu.VMEM((2, tk, tq), F32), pltpu.VMEM((2, tk, tq), F32),
                        pltpu.VMEM((2, tk, tq), BF16), pltpu.VMEM((2, tk, tq), BF16),
                        pltpu.VMEM((2, 2, 1, tq), F32), pltpu.VMEM((2, 2, 1, tq), F32),
                        pltpu.VMEM((2, V_ROWS, tq), F32), pltpu.VMEM((2, 1, tq), F32)],
        compiler_params=_cparams(3, VMEM_LIMIT),
        name="attention",
    )(qt4, k, vt4)


def _outproj_kernel(yac_ref, yal_ref, hf_ref, hb_ref, o_ref, ycc_ref, ycl_ref, x_ref, mod_ref, wa_ref, wb_ref,
                    wc_ref, mg_ref, blk_ref, n2_ref, rw_ref, xo_ref, hx_ref, aff_ref, *, d_model,
                    n_ctx_tiles):
    D = d_model
    i = pl.program_id(1)
    mod = mod_ref[0]
    g1, sh2, sc2 = mod[:, 2 * D:3 * D], mod[:, 3 * D:4 * D], mod[:, 4 * D:5 * D]
    ya = jnp.where(i < n_ctx_tiles, yac_ref[0], yal_ref[0])
    h = hf_ref[0] + hb_ref[0]
    hi, lo = _split_bf16(h * h)
    ms = _dot(hi, blk_ref[...]) + _dot(lo, blk_ref[...])
    hn = h * lax.rsqrt(ms + EPS) * mg_ref[...]
    yb = (hn * _sigmoid(o_ref[0].astype(F32))).astype(BF16)
    yc = jnp.where(i < n_ctx_tiles, ycc_ref[0], ycl_ref[0])
    mix = _dot(ya, wa_ref[...]) + _dot(yb, wb_ref[...]) + _dot(yc, wc_ref[...])
    x = x_ref[0] + g1 * mix
    xo_ref[0] = x
    hx = _rms(x, n2_ref[...]) * (1.0 + sc2) + sh2
    hx_ref[0] = hx.astype(BF16)
    h_hi, h_lo = _split_bf16(hx)
    r_hi, r_lo = _split_bf16(rw_ref[...])
    logits = _dot(h_hi, r_hi) + _dot(h_lo, r_hi) + _dot(h_hi, r_lo)
    e = jnp.exp(logits - jnp.max(logits, axis=1, keepdims=True))
    aff_ref[0] = e / jnp.sum(e, axis=1, keepdims=True)


def _outproj(ya_ctx, ya_lat, hf, hb, o, yc_ctx, yc_lat, xs, mods_l, wa, wb, wc, mg, blk, n2, rw, *,
             n_ctx_tiles):
    B, S, D = xs.shape
    nt = S // TOK_TILE
    tok = lambda w: pl.BlockSpec((1, TOK_TILE, w), lambda b, i: (b, i, 0))
    full = lambda a: pl.BlockSpec(a.shape, lambda b, i: (0,) * a.ndim)
    mod_spec = pl.BlockSpec((1, 1, 6 * D), lambda b, i: (jnp.where(i < n_ctx_tiles, B, b), 0, 0))
    hw = M_WIDTH
    n_lat_tiles = nt - n_ctx_tiles
    ctx_tok = lambda w: pl.BlockSpec((1, TOK_TILE, w), lambda b, i: (b, jnp.minimum(i, n_ctx_tiles - 1), 0))
    lat_tok = lambda w: pl.BlockSpec(
        (1, TOK_TILE, w), lambda b, i: (b, jnp.clip(i - n_ctx_tiles, 0, n_lat_tiles - 1), 0))
    kern = functools.partial(_outproj_kernel, d_model=D, n_ctx_tiles=n_ctx_tiles)
    return pl.pallas_call(
        kern,
        grid=(B, nt),
        in_specs=[ctx_tok(256), lat_tok(256),
                  tok(hw), tok(hw),
                  tok(hw), ctx_tok(A_HEADS * HEAD), lat_tok(A_HEADS * HEAD), tok(D), mod_spec,
                  full(wa), full(wb), full(wc), full(mg), full(blk), full(n2), full(rw)],
        out_specs=[tok(D), tok(D), tok(N_EXPERTS)],
        out_shape=[jax.ShapeDtypeStruct((B, S, D), F32), jax.ShapeDtypeStruct((B, S, D), BF16),
                   jax.ShapeDtypeStruct((B, S, N_EXPERTS), F32)],
        compiler_params=_cparams(2, VMEM_LIMIT),
        name="outproj",
    )(ya_ctx, ya_lat, hf, hb, o, yc_ctx, yc_lat, xs, mods_l, wa, wb, wc, mg, blk, n2, rw)


def _route_part(a, cap, slot0, tri, pos_ref, base_ref, lane0):
    n = a.shape[1]
    capf = float(cap)
    bits = pltpu.bitcast(a, I32)
    v = jnp.zeros((N_EXPERTS, 1), I32)
    for bit in range(30, -1, -1):
        cand = v | (1 << bit)
        cnt = jnp.sum(jnp.where(bits >= cand, 1.0, 0.0), axis=1, keepdims=True)
        v = jnp.where(cnt >= capf, cand, v)
    gt = bits > v
    eq = bits == v
    need = capf - jnp.sum(jnp.where(gt, 1.0, 0.0), axis=1, keepdims=True)
    idx = lax.broadcasted_iota(I32, (N_EXPERTS, n), 1)
    x = jnp.zeros((N_EXPERTS, 1), I32)
    for bit in range(max(n - 1, 1).bit_length() - 1, -1, -1):
        cand = x | (1 << bit)
        cnt = jnp.sum(jnp.where(eq, jnp.where(idx < cand, 1.0, 0.0), 0.0), axis=1, keepdims=True)
        x = jnp.where(cnt < need, cand, x)
    sel = jnp.where(gt, 1.0, jnp.where(eq, jnp.where(idx <= x, 1.0, 0.0), 0.0))
    running = jnp.zeros((N_EXPERTS, 1), F32)
    fill = jnp.zeros((N_EXPERTS, 1), F32)
    for c in range(n // CHUNK):
        blk = sel[:, c * CHUNK:(c + 1) * CHUNK]
        incl = _dot(blk.astype(BF16), tri)
        pos = running + incl - blk + float(slot0)
        cg = lane0 // CHUNK + c
        pos_ref[0, :, lane0 + c * CHUNK:lane0 + (c + 1) * CHUNK] = jnp.where(
            blk > 0.5, pos, -1.0).astype(I32)
        base_ref[0, :, cg:cg + 1] = (running + float(slot0)).astype(I32)
        running = running + incl[:, CHUNK - 1:CHUNK]
        fill = jnp.maximum(fill, incl[:, CHUNK - 1:CHUNK])
    return fill


def _route_kernel(aff_ref, pos_ref, base_ref, *, n_ctx, cap_ctx, cap_lat):
    r_i = lax.broadcasted_iota(I32, (CHUNK, CHUNK), 0)
    c_i = lax.broadcasted_iota(I32, (CHUNK, CHUNK), 1)
    tri = jnp.where(r_i <= c_i, 1.0, 0.0).astype(BF16)
    base_ref[...] = jnp.zeros_like(base_ref)
    a = aff_ref[0]
    fill_c = _route_part(a[:, :n_ctx], cap_ctx, 0, tri, pos_ref, base_ref, 0)
    fill_l = _route_part(a[:, n_ctx:], cap_lat, cap_ctx, tri, pos_ref, base_ref, n_ctx)
    base_ref[0, :, LANE - 1:LANE] = jnp.maximum(fill_c, fill_l).astype(I32)


def _route(aff_t, *, n_ctx, cap_ctx, cap_lat):
    B, E, S = aff_t.shape
    assert S // CHUNK < LANE - 1
    kern = functools.partial(_route_kernel, n_ctx=n_ctx, cap_ctx=cap_ctx, cap_lat=cap_lat)
    return pl.pallas_call(
        kern,
        grid=(B,),
        in_specs=[pl.BlockSpec((1, E, S), lambda b: (b, 0, 0))],
        out_specs=[pl.BlockSpec((1, E, S), lambda b: (b, 0, 0)),
                   pl.BlockSpec((1, E, LANE), lambda b: (b, 0, 0))],
        out_shape=[jax.ShapeDtypeStruct((B, E, S), I32), jax.ShapeDtypeStruct((B, E, LANE), I32)],
        compiler_params=_cparams(1),
        name="route",
    )(aff_t)


def _window_start(base, align, rows, win):
    w0 = lax.shift_left(lax.shift_right_logical(base, int(math.log2(align))), int(math.log2(align)))
    return pl.multiple_of(jnp.minimum(w0, rows - win), align)


def _gather_kernel(base_ref, h_ref, pos_ref, xe_ref, *, n_chunks, rows, win, unroll):
    b = pl.program_id(0)
    e = pl.program_id(2)
    xe_ref[0, 0] = jnp.zeros(xe_ref.shape[2:], BF16)

    def gather_all(win_rows):
        def group(g, carry):
            picked = []
            for u in range(unroll):
                c = g * unroll + u
                w0 = _window_start(base_ref[b, e, c], GATHER_ALIGN_BF16, rows, win_rows)
                posr = pos_ref[0, 0, pl.ds(c, 1), :]
                slot = lax.broadcasted_iota(I32, (win_rows, CHUNK), 0) + w0
                onehot = jnp.where(posr == slot, 1.0, 0.0).astype(BF16)
                hc = h_ref[0, pl.ds(pl.multiple_of(c * CHUNK, CHUNK), CHUNK), :]
                picked.append((w0, _dot(onehot, hc).astype(BF16)))
            for w0, rows_c in picked:
                xe_ref[0, 0, pl.ds(w0, win_rows), :] += rows_c
            return carry

        lax.fori_loop(0, n_chunks // unroll, group, 0)

    most = base_ref[b, e, LANE - 1]
    small = min(SMALL_FILL + GATHER_ALIGN_BF16, rows)
    if small < win:
        pl.when(most <= SMALL_FILL)(functools.partial(gather_all, small))
        pl.when(most > SMALL_FILL)(functools.partial(gather_all, win))
    else:
        gather_all(win)


def _gather(hx, pos4, bases, *, rows):
    B, S, D = hx.shape
    nc = S // CHUNK
    dh = D // 2
    win = min(CHUNK + GATHER_ALIGN_BF16, rows)
    unroll = next(u for u in (11, 8, 6, 4, 3, 2, 1) if nc % u == 0)
    kern = functools.partial(_gather_kernel, n_chunks=nc, rows=rows, win=win, unroll=unroll)
    return pl.pallas_call(
        kern,
        grid_spec=pltpu.PrefetchScalarGridSpec(
            num_scalar_prefetch=1,
            grid=(B, 2, N_EXPERTS),
            in_specs=[pl.BlockSpec((1, S, dh), lambda b, j, e, bs: (b, 0, j)),
                      pl.BlockSpec((1, 1, nc, CHUNK), lambda b, j, e, bs: (b, e, 0, 0))],
            out_specs=pl.BlockSpec((1, 1, rows, dh), lambda b, j, e, bs: (b, e, 0, j))),
        out_shape=jax.ShapeDtypeStruct((B, N_EXPERTS, rows, D), BF16),
        compiler_params=_cparams(3, VMEM_LIMIT),
        name="moe_gather",
    )(bases, hx, pos4)


def _ffn_kernel(x_ref, wg_ref, wu_ref, wd_ref, y_ref, *, row_tile, n_tiles):
    def tile(i, carry):
        r0 = pl.multiple_of(i * row_tile, 16)
        x = x_ref[0, 0, pl.ds(r0, row_tile), :]
        g = _dot(x, wg_ref[0])
        u = _dot(x, wu_ref[0])
        hid = (g * _sigmoid(g) * u).astype(BF16)
        y_ref[0, 0, pl.ds(r0, row_tile), :] = _dot(hid, wd_ref[0]).astype(BF16)
        return carry

    lax.fori_loop(0, n_tiles, tile, 0)


def _ffn_row_tile(rows):
    for t in (352, 256, 176, 128, 96, 64, 32, 16):
        if rows % t == 0:
            return t
    raise ValueError(f"expert slot rows {rows} must be a multiple of 16")


def _ffn(xe, wg, wu, wd):
    B, E, R, D = xe.shape
    F = wg.shape[-1]
    rt = _ffn_row_tile(R)
    kern = functools.partial(_ffn_kernel, row_tile=rt, n_tiles=R // rt)
    return pl.pallas_call(
        kern,
        grid=(E, B),
        in_specs=[pl.BlockSpec((1, 1, R, D), lambda e, b: (b, e, 0, 0)),
                  pl.BlockSpec((1, D, F), lambda e, b: (e, 0, 0)),
                  pl.BlockSpec((1, D, F), lambda e, b: (e, 0, 0)),
                  pl.BlockSpec((1, F, D), lambda e, b: (e, 0, 0))],
        out_specs=pl.BlockSpec((1, 1, R, D), lambda e, b: (b, e, 0, 0)),
        out_shape=jax.ShapeDtypeStruct((B, E, R, D), BF16),
        compiler_params=_cparams(2, VMEM_LIMIT),
        name="moe_ffn",
    )(xe, wg, wu, wd)


def _combine_kernel(base_ref, x_ref, ye_ref, posc_ref, aff_ref, modc_ref, modx_ref, o_ref, *, d_model,
                    blk_chunks, n_ctx_chunks, rows, win):
    D = d_model
    b = pl.program_id(0)
    tb = pl.program_id(1)
    ep = pl.program_id(2)
    experts = (2 * ep, 2 * ep + 1)

    @pl.when(ep == 0)
    def _():
        o_ref[0] = jnp.zeros(o_ref.shape[1:], F32)

    lane = lax.broadcasted_iota(I32, (1, N_EXPERTS), 1)
    small = min(SMALL_FILL + GATHER_ALIGN_BF16, rows)

    def columns(rs, e):
        pcol = jnp.sum(jnp.where(lane == e, posc_ref[0, rs, :].astype(F32), 0.0), axis=1, keepdims=True)
        gcol = jnp.sum(jnp.where(lane == e, aff_ref[0, rs, :], 0.0), axis=1, keepdims=True)
        return pcol, gcol

    def scatter_narrow():
        lane4 = lax.broadcasted_iota(I32, (CHUNK, 4 * small), 1)
        second = lane4 >= 2 * small
        lane2 = jnp.where(second, lane4 - 2 * small, lane4)
        low = lane2 >= small
        lane1 = jnp.where(low, lane2 - small, lane2)
        for cc in range(blk_chunks):
            rs = slice(cc * CHUNK, (cc + 1) * CHUNK)
            chunk = tb * blk_chunks + cc
            (p0, g0), (p1, g1) = columns(rs, experts[0]), columns(rs, experts[1])
            w0 = _window_start(base_ref[b, experts[0], chunk], GATHER_ALIGN_BF16, rows, small)
            w1 = _window_start(base_ref[b, experts[1], chunk], GATHER_ALIGN_BF16, rows, small)
            slot = (lane1 + jnp.where(second, w1, w0)).astype(F32)
            g = jnp.where(second, g1, g0)
            g_hi = g.astype(BF16).astype(F32)
            gate = jnp.where(low, g - g_hi, g_hi)
            sel = jnp.where(jnp.where(second, p1, p0) == slot, gate, 0.0).astype(BF16)
            y0 = ye_ref[0, 0, pl.ds(w0, small), :]
            y1 = ye_ref[0, 1, pl.ds(w1, small), :]
            o_ref[0, rs, :] += _dot(sel, jnp.concatenate([y0, y0, y1, y1], axis=0))

    def scatter_wide():
        for cc in range(blk_chunks):
            rs = slice(cc * CHUNK, (cc + 1) * CHUNK)
            chunk = tb * blk_chunks + cc
            for k, e in enumerate(experts):
                pcol, gcol = columns(rs, e)
                w0 = _window_start(base_ref[b, e, chunk], GATHER_ALIGN_BF16, rows, win)
                slot = (lax.broadcasted_iota(I32, (CHUNK, win), 1) + w0).astype(F32)
                onehot = jnp.where(pcol == slot, 1.0, 0.0).astype(BF16)
                o_ref[0, rs, :] += gcol * _dot(onehot, ye_ref[0, k, pl.ds(w0, win), :])

    most = jnp.maximum(base_ref[b, experts[0], LANE - 1], base_ref[b, experts[1], LANE - 1])
    if small < win:
        pl.when(most <= SMALL_FILL)(scatter_narrow)
        pl.when(most > SMALL_FILL)(scatter_wide)
    else:
        scatter_wide()

    @pl.when(ep == N_EXPERTS // 2 - 1)
    def _():
        g2c = modc_ref[0][:, 5 * D:6 * D]
        g2x = modx_ref[0][:, 5 * D:6 * D]
        for cc in range(blk_chunks):
            rs = slice(cc * CHUNK, (cc + 1) * CHUNK)
            g2 = jnp.where(tb * blk_chunks + cc < n_ctx_chunks, g2c, g2x)
            o_ref[0, rs, :] = x_ref[0, rs, :] + g2 * o_ref[0, rs, :]


def _combine_blk_chunks(nc):
    for k in (11, 8, 6, 4, 3, 2, 1):
        if nc % k == 0:
            return k
    return 1


def _combine(x, ye, posc, aff, mods_l, bases, *, n_ctx):
    B, S, D = x.shape
    R = ye.shape[2]
    nc = S // CHUNK
    k = _combine_blk_chunks(nc)
    tb = k * CHUNK
    win = min(CHUNK + GATHER_ALIGN_BF16, R)
    kern = functools.partial(_combine_kernel, d_model=D, blk_chunks=k, n_ctx_chunks=n_ctx // CHUNK,
                             rows=R, win=win)
    modc_spec = pl.BlockSpec((1, 1, 6 * D), lambda b, t, e, bs: (B, 0, 0))
    modx_spec = pl.BlockSpec((1, 1, 6 * D), lambda b, t, e, bs: (b, 0, 0))
    return pl.pallas_call(
        kern,
        grid_spec=pltpu.PrefetchScalarGridSpec(
            num_scalar_prefetch=1,
            grid=(B, S // tb, N_EXPERTS // 2),
            in_specs=[pl.BlockSpec((1, tb, D), lambda b, t, e, bs: (b, t, 0)),
                      pl.BlockSpec((1, 2, R, D), lambda b, t, e, bs: (b, e, 0, 0)),
                      pl.BlockSpec((1, tb, N_EXPERTS), lambda b, t, e, bs: (b, t, 0)),
                      pl.BlockSpec((1, tb, N_EXPERTS), lambda b, t, e, bs: (b, t, 0)),
                      modc_spec, modx_spec],
            out_specs=pl.BlockSpec((1, tb, D), lambda b, t, e, bs: (b, t, 0))),
        out_shape=jax.ShapeDtypeStruct((B, S, D), F32),
        compiler_params=_cparams(3, VMEM_LIMIT),
        name="moe_combine",
    )(bases, x, ye, posc, aff, mods_l, mods_l)


def _final_kernel(x_ref, g_ref, o_ref):
    o_ref[0] = _rms(x_ref[0], g_ref[...])


def _final_norm(xs, g, *, n_ctx):
    B, S, D = xs.shape
    T = S - n_ctx
    off = n_ctx // TOK_TILE
    return pl.pallas_call(
        _final_kernel,
        grid=(B, T // TOK_TILE),
        in_specs=[pl.BlockSpec((1, TOK_TILE, D), lambda b, i: (b, i + off, 0)),
                  pl.BlockSpec((1, D), lambda b, i: (0, 0))],
        out_specs=pl.BlockSpec((1, TOK_TILE, D), lambda b, i: (b, i, 0)),
        out_shape=jax.ShapeDtypeStruct((B, T, D), F32),
        compiler_params=_cparams(2),
        name="final_norm",
    )(xs, g)


def _rot_swap(w):
    half = A_ROPE // 2
    return jnp.concatenate([-w[..., half:], w[..., :half]], -1)


def _rope_tables(n_ctx, n_lat):
    rows = n_lat // GRID_W
    row_id = jnp.repeat(jnp.arange(rows, dtype=F32), GRID_W)
    col_id = jnp.tile(jnp.arange(GRID_W, dtype=F32), rows)
    n_freq = A_ROPE // 4
    inv = ROPE_BASE ** (-jnp.arange(n_freq, dtype=F32) / n_freq)
    ang = jnp.concatenate([row_id[:, None] * inv, col_id[:, None] * inv], -1)
    ang = jnp.concatenate([jnp.zeros((n_ctx, A_ROPE // 2), F32), ang], 0)
    S = n_ctx + n_lat
    cos, sin = jnp.cos(ang), jnp.sin(ang)
    pad = jnp.zeros((S, LANE - HEAD - A_ROPE), F32)
    cos128 = jnp.concatenate([jnp.ones((S, HEAD), F32), cos, cos, pad], -1)
    sin128 = jnp.concatenate([jnp.zeros((S, HEAD), F32), sin, sin, pad], -1)
    return cos128, sin128


def kernel(x, c, ctx, c_ctx, ada_w, ada_b, norm1_g, norm2_g, w_in, m_conv_w, m_conv_b, m_ib, m_fb, m_norm_g, a_qnorm_g, a_wq_up, a_kvnorm_g, a_wkv_up, w_out, router_w, e_w_gate, e_w_up, e_w_down, final_g):
    B, T, D = x.shape
    n_ctx = ctx.shape[1]
    L = ada_w.shape[0]
    S = n_ctx + T
    assert D == 16 * HEAD and n_ctx % TOK_TILE == 0 and T % TOK_TILE == 0 and T % (GRID_W * 8) == 0
    n_ctx_tiles = n_ctx // TOK_TILE
    n_ctx_chunks = n_ctx // CHUNK
    cap_ctx = EC_FACTOR * n_ctx // N_EXPERTS
    cap_lat = EC_FACTOR * T // N_EXPERTS
    slot_rows = cap_ctx + cap_lat

    o_f, o_qk, o_v, o_o, o_g, o_cq, o_ckv, o_kr = (0, 256, 768, 1024, 1280, 1296, 1680, 1936)
    w_kr = w_in[:, :, o_kr:o_kr + A_ROPE]
    slot = lambda w: jnp.pad(w, ((0, 0), (0, 0), (HEAD, LANE - HEAD - A_ROPE)))
    w_g = w_in[:, :, o_g:o_g + 16]
    w1 = jnp.concatenate([
        w_in[:, :, o_f:o_f + 256],
        w_in[:, :, o_qk:o_qk + 2 * M_WIDTH],
        w_in[:, :, o_v:o_v + M_WIDTH],
        w_in[:, :, o_o:o_o + M_WIDTH],
        w_in[:, :, o_cq:o_cq + 384],
        w_in[:, :, o_ckv:o_ckv + 256],
        slot(w_kr),
        slot(_rot_swap(w_kr)),
        jnp.pad(w_g, ((0, 0), (0, 0), (0, LANE - 16))),
    ], -1).astype(BF16)
    assert w1.shape[-1] == _N_COLS
    wgt = jnp.swapaxes(w_g, 1, 2).astype(BF16)
    conv_w = m_conv_w
    conv_b = m_conv_b[:, None, :]
    gate_bias = jnp.stack([m_ib, m_fb], 2).reshape(L, 16)
    wq3 = a_wq_up.reshape(L, -1, A_HEADS, HEAD + A_ROPE)
    zq = jnp.zeros(wq3.shape[:-1] + (LANE - HEAD - A_ROPE,), F32)
    wq = jnp.concatenate([wq3, zq], -1).reshape(L, -1, A_HEADS * LANE)
    wqt = jnp.swapaxes(wq, 1, 2).astype(BF16)
    wkv3 = a_wkv_up.reshape(L, -1, A_HEADS, 2 * HEAD)
    zk = jnp.zeros(wkv3.shape[:-1] + (HEAD,), F32)
    wk = jnp.concatenate([wkv3[..., :HEAD], zk], -1).reshape(L, -1, A_HEADS * LANE).astype(BF16)
    zv = jnp.zeros(wkv3.shape[:-1] + (V_ROWS - HEAD,), F32)
    wvt = jnp.swapaxes(jnp.concatenate([wkv3[..., HEAD:], zv], -1).reshape(L, -1, A_HEADS * V_ROWS),
                       1, 2).astype(BF16)
    wa = w_out[:, 0:256].astype(BF16)
    wb = w_out[:, 256:512].astype(BF16)
    wc = w_out[:, 512:1024].astype(BF16)
    mg = m_norm_g[:, None, :]
    lane = jnp.arange(M_WIDTH)
    blk = jnp.where(lane[:, None] // HEAD == lane[None, :] // HEAD, 1.0 / HEAD, 0.0).astype(BF16)
    cos128, sin128 = _rope_tables(n_ctx, T)
    row = jnp.arange(LANE)[:, None]
    cost128 = cos128.T
    sint128 = jnp.where(row < HEAD + A_ROPE // 2, -sin128.T, sin128.T)
    cs = _fft_chan_mats()
    attn_sub = 3 if (S // TOK_TILE) % 3 == 0 else 1
    attn_qtiles = 4 if T % (4 * TOK_TILE) == 0 else 1
    attn_q_per_step = next(n for n in (4, 2, 1) if T % (n * attn_qtiles * TOK_TILE) == 0)
    wg_e = e_w_gate.astype(BF16)
    wu_e = e_w_up.astype(BF16)
    wd_e = e_w_down.astype(BF16)

    rows16 = 16
    cvec = jnp.zeros((rows16, D), F32).at[:B].set(c).at[B].set(c_ctx)
    mods = _ada_mods(cvec, ada_w, ada_b).reshape(L, rows16, 1, 6 * D)

    xs = jnp.concatenate([ctx, x], axis=1)
    for l in range(L):
        mods_l = mods[l]
        f, qk, vm, og, gat, gatt, qt_ctx, qt_lat, k, vt4 = _inproj(
            xs, mods_l, norm1_g[l][None], w1[l], wgt[l], a_qnorm_g[l][None], a_kvnorm_g[l][None],
            wqt[l], wk[l], wvt[l], cos128, sin128, cost128, sint128,
            n_ctx_tiles=n_ctx_tiles)
        ya_ctx = _fft_ctx(f, n_ctx, cs)
        ya_lat = _fft_latent(f[:, n_ctx:], cs)
        qm, ktm, colp, rowp = _mlstm_prep(qk, gat, gatt, conv_w[l], conv_b[l], gate_bias[l][None, :],
                                          gate_bias[l][:, None], n_ctx_chunks=n_ctx_chunks)
        hf, hb = _mlstm_scan(qm, ktm, vm, colp, rowp, n_ctx_chunks=n_ctx_chunks)
        yc_ctx = _attention(qt_ctx, k, vt4, n_keys=n_ctx, qtiles=1, sub=1)
        yc_lat = _attention(qt_lat, k, vt4, n_keys=S, qtiles=attn_qtiles, sub=attn_sub,
                            q_per_step=attn_q_per_step)
        xs, hx, aff = _outproj(ya_ctx, ya_lat, hf, hb, og, yc_ctx, yc_lat, xs, mods_l, wa[l], wb[l], wc[l],
                               mg[l], blk, norm2_g[l][None], router_w[l], n_ctx_tiles=n_ctx_tiles)
        posm, bases = _route(jnp.swapaxes(aff, 1, 2), n_ctx=n_ctx, cap_ctx=cap_ctx, cap_lat=cap_lat)
        xe = _gather(hx, posm.reshape(B, N_EXPERTS, S // CHUNK, CHUNK), bases, rows=slot_rows)
        ye = _ffn(xe, wg_e[l], wu_e[l], wd_e[l])
        xs = _combine(xs, ye, jnp.swapaxes(posm, 1, 2), aff, mods_l, bases, n_ctx=n_ctx)
    return _final_norm(xs, final_g[None], n_ctx=n_ctx)
```

```python
import functools
import math

import jax
import jax.numpy as jnp
from jax import lax
from jax.experimental import pallas as pl
from jax.experimental.pallas import tpu as pltpu

F32 = jnp.float32
BF16 = jnp.bfloat16
I32 = jnp.int32

EPS = 1e-6
GRID_W = 64
ROPE_BASE = 10000.0
LANE = 128
HEAD = 64
M_HEADS = 4
M_WIDTH = M_HEADS * HEAD
A_HEADS = 8
A_ROPE = 32
V_ROWS = 80
N_EXPERTS = 16
EC_FACTOR = 2
TOK_TILE = 256
CHUNK = 128
GATHER_ALIGN_BF16 = 16
SMALL_FILL = 32
COMBINE_GROUP = 4
VMEM_LIMIT = 56 * 1024 * 1024


def _cparams(n_axes, vmem=None):
    return pltpu.CompilerParams(dimension_semantics=("arbitrary",) * n_axes,
                                vmem_limit_bytes=vmem)


def _dot(a, b):
    return jnp.dot(a, b, preferred_element_type=F32)


def _dot_nt(a, b):
    return lax.dot_general(a, b, (((1,), (1,)), ((), ())), preferred_element_type=F32)


def _split_bf16(a):
    hi = a.astype(BF16)
    lo = (a - hi.astype(F32)).astype(BF16)
    return hi, lo


def _sigmoid(x):
    return 1.0 / (1.0 + jnp.exp(-x))


def _rms(x, g):
    return x * lax.rsqrt(jnp.mean(x * x, axis=-1, keepdims=True) + EPS) * g


def _ada_kernel(c_ref, w_ref, b_ref, o_ref):
    a = c_ref[...]
    a = a * _sigmoid(a)
    a_hi, a_lo = _split_bf16(a)
    w_hi, w_lo = _split_bf16(w_ref[0])
    o_ref[0] = _dot(a_hi, w_hi) + _dot(a_lo, w_hi) + _dot(a_hi, w_lo) + b_ref[0]


def _ada_mods(cvec, ada_w, ada_b):
    L, D, D6 = ada_w.shape
    rows = cvec.shape[0]
    return pl.pallas_call(
        _ada_kernel,
        grid=(L, D6 // D),
        in_specs=[pl.BlockSpec((rows, D), lambda l, j: (0, 0)),
                  pl.BlockSpec((1, D, D), lambda l, j: (l, 0, j)),
                  pl.BlockSpec((1, 1, D), lambda l, j: (l, 0, j))],
        out_specs=pl.BlockSpec((1, rows, D), lambda l, j: (l, 0, j)),
        out_shape=jax.ShapeDtypeStruct((L, rows, D6), F32),
        compiler_params=_cparams(2),
        name="ada_mods",
    )(cvec, ada_w, ada_b.reshape(L, 1, D6))


_C_F = (0, 256)
_C_Q = (256, 512)
_C_K = (512, 768)
_C_V = (768, 1024)
_C_O = (1024, 1280)
_C_CQ = (1280, 1664)
_C_CKV = (1664, 1920)
_C_KR = (1920, 2048)
_C_KRS = (2048, 2176)
_C_G = (2176, 2304)
_N_COLS = 2304


def _inproj_kernel(x_ref, mod_ref, n1_ref, w1_ref, wgt_ref, qn_ref, kvn_ref, wqt_ref,
                   wk_ref, wvt_ref, cos_ref, sin_ref, cost_ref, sint_ref,
                   f_ref, qk_ref, vm_ref, o_ref, gat_ref, gatt_ref, qtc_ref, qtl_ref, k_ref, vt_ref, *,
                   d_model, q_scale, n_ctx_tiles):
    D = d_model
    i = pl.program_id(1)
    x = x_ref[0]
    mod = mod_ref[0]
    sh1, sc1 = mod[:, 0:D], mod[:, D:2 * D]
    xm = _rms(x, n1_ref[...]) * (1.0 + sc1) + sh1
    xb = xm.astype(BF16)
    u = _dot(xb, w1_ref[...])
    f_ref[0] = u[:, _C_F[0]:_C_F[1]].astype(BF16)
    qk_ref[0] = u[:, _C_Q[0]:_C_K[1]].astype(BF16)
    vm_ref[0] = u[:, _C_V[0]:_C_V[1]].astype(BF16)
    o_ref[0] = u[:, _C_O[0]:_C_O[1]].astype(BF16)
    gat_ref[0] = u[:, _C_G[0]:_C_G[0] + 16]
    gatt_ref[0] = _dot_nt(wgt_ref[...], xb)

    cqn = _rms(u[:, _C_CQ[0]:_C_CQ[1]], qn_ref[...]).astype(BF16)
    cost = jnp.tile(cost_ref[...], (A_HEADS, 1))
    sint = jnp.tile(sint_ref[...], (A_HEADS, 1))
    qa = _dot_nt(wqt_ref[...], cqn)
    half = A_ROPE // 2
    parts = []
    for h in range(A_HEADS):
        r0 = h * LANE
        parts += [qa[r0:r0 + HEAD], qa[r0 + HEAD + half:r0 + HEAD + A_ROPE],
                  qa[r0 + HEAD:r0 + HEAD + half], qa[r0 + HEAD + A_ROPE:r0 + LANE]]
    qt = ((qa * cost + jnp.concatenate(parts, axis=0) * sint) * q_scale).astype(BF16)

    @pl.when(i < n_ctx_tiles)
    def _():
        qtc_ref[0, 0] = qt

    @pl.when(i >= n_ctx_tiles)
    def _():
        qtl_ref[0, 0] = qt

    ckvn = _rms(u[:, _C_CKV[0]:_C_CKV[1]], kvn_ref[...]).astype(BF16)
    cosw = jnp.tile(cos_ref[...], (1, A_HEADS))
    sinw = jnp.tile(sin_ref[...], (1, A_HEADS))
    kw = A_HEADS * LANE
    kr = jnp.tile(u[:, _C_KR[0]:_C_KR[1]], (1, A_HEADS))
    krs = jnp.tile(u[:, _C_KRS[0]:_C_KRS[1]], (1, A_HEADS))
    k = _dot(ckvn, wk_ref[...]) + kr * cosw + krs * sinw
    k_ref[0] = k.astype(BF16)
    vrow = lax.broadcasted_iota(I32, (A_HEADS * V_ROWS, 1), 0)
    ones_a = jnp.where(vrow % V_ROWS == HEAD, 1.0, 0.0)
    vt_ref[0, 0] = (_dot_nt(wvt_ref[...], ckvn) + ones_a).astype(BF16)


def _inproj(xs, mods_l, n1, w1, wgt, qn, kvn, wqt, wk, wvt, cos128, sin128, cost128, sint128,
            *, n_ctx_tiles):
    B, S, D = xs.shape
    nt = S // TOK_TILE
    n_ctx = n_ctx_tiles * TOK_TILE
    n_lat_tiles = nt - n_ctx_tiles
    kw = A_HEADS * LANE
    tok = lambda w: pl.BlockSpec((1, TOK_TILE, w), lambda b, i: (b, i, 0))
    full = lambda a: pl.BlockSpec(a.shape, lambda b, i: (0,) * a.ndim)
    mod_spec = pl.BlockSpec((1, 1, 6 * D), lambda b, i: (jnp.where(i < n_ctx_tiles, B, b), 0, 0))
    tab_spec = pl.BlockSpec((TOK_TILE, LANE), lambda b, i: (i, 0))
    tabt_spec = pl.BlockSpec((LANE, TOK_TILE), lambda b, i: (0, i))
    sd = lambda w, dt: jax.ShapeDtypeStruct((B, S, w), dt)
    kern = functools.partial(_inproj_kernel, d_model=D, n_ctx_tiles=n_ctx_tiles,
                             q_scale=(HEAD + A_ROPE) ** -0.5 * math.log2(math.e))
    qtc_spec = pl.BlockSpec((1, 1, kw, TOK_TILE), lambda b, i: (b, jnp.minimum(i, n_ctx_tiles - 1), 0, 0))
    qtl_spec = pl.BlockSpec((1, 1, kw, TOK_TILE),
                            lambda b, i: (b, jnp.clip(i - n_ctx_tiles, 0, n_lat_tiles - 1), 0, 0))
    return pl.pallas_call(
        kern,
        grid=(B, nt),
        in_specs=[tok(D), mod_spec, full(n1), full(w1), full(wgt), full(qn), full(kvn), full(wqt),
                  full(wk), full(wvt), tab_spec, tab_spec, tabt_spec, tabt_spec],
        out_specs=[tok(256), tok(2 * M_WIDTH), tok(M_WIDTH), tok(M_WIDTH), tok(16),
                   pl.BlockSpec((1, 16, TOK_TILE), lambda b, i: (b, 0, i)),
                   qtc_spec, qtl_spec, tok(kw),
                   pl.BlockSpec((1, 1, A_HEADS * V_ROWS, TOK_TILE), lambda b, i: (b, i, 0, 0))],
        out_shape=[sd(256, BF16), sd(2 * M_WIDTH, BF16), sd(M_WIDTH, BF16), sd(M_WIDTH, BF16), sd(16, F32),
                   jax.ShapeDtypeStruct((B, 16, S), F32),
                   jax.ShapeDtypeStruct((B, n_ctx_tiles, kw, TOK_TILE), BF16),
                   jax.ShapeDtypeStruct((B, n_lat_tiles, kw, TOK_TILE), BF16),
                   sd(kw, BF16),
                   jax.ShapeDtypeStruct((B, nt, A_HEADS * V_ROWS, TOK_TILE), BF16)],
        compiler_params=_cparams(2, VMEM_LIMIT),
        name="inproj",
    )(xs, mods_l, n1, w1, wgt, qn, kvn, wqt, wk, wvt, cos128, sin128, cost128, sint128)


def _dft_mats(n):
    idx = jnp.arange(n, dtype=F32)
    ang = 2.0 * math.pi * jnp.mod(idx[:, None] * idx[None, :], n) / n
    return jnp.cos(ang), jnp.sin(ang)


def _fft_chan_mats():
    c, s = _dft_mats(HEAD)
    eye = jnp.eye(4, dtype=F32)
    return jnp.concatenate([jnp.kron(eye, c), jnp.kron(eye, s)], 0).astype(BF16)


def _fft_ctx_kernel(f_ref, ft_ref, cs_ref, o_ref, *, n, scale):
    xst = _dot(ft_ref[...], f_ref[0])
    xr = xst[:n].astype(BF16)
    xi = xst[n:].astype(BF16)
    y = _dot(xr, cs_ref[0:256, :]) + _dot(xi, cs_ref[256:512, :])
    o_ref[0] = (y * scale).astype(BF16)


def _fft_ctx(f, n_ctx, cs):
    B = f.shape[0]
    c, s = _dft_mats(n_ctx)
    ft = jnp.concatenate([c, -s], 0).astype(BF16)
    kern = functools.partial(_fft_ctx_kernel, n=n_ctx, scale=(n_ctx * HEAD) ** -0.5)
    return pl.pallas_call(
        kern,
        grid=(B,),
        in_specs=[pl.BlockSpec((1, n_ctx, 256), lambda b: (b, 0, 0)),
                  pl.BlockSpec(ft.shape, lambda b: (0, 0)),
                  pl.BlockSpec(cs.shape, lambda b: (0, 0))],
        out_specs=pl.BlockSpec((1, n_ctx, 256), lambda b: (b, 0, 0)),
        out_shape=jax.ShapeDtypeStruct((B, n_ctx, 256), BF16),
        compiler_params=_cparams(1),
        name="fft_ctx",
    )(f, ft, cs)


def _fft_stage1_kernel(x_ref, f1_ref, cw_ref, sw_ref, o_ref, *, n1):
    z = _dot(f1_ref[...], x_ref[0])
    zr, zi = z[:n1], z[n1:]
    cw, sw = cw_ref[...], sw_ref[...]
    o_ref[0, 0] = (zr * cw + zi * sw).astype(BF16)
    o_ref[0, 1] = (zi * cw - zr * sw).astype(BF16)


def _fft_stage2_kernel(z_ref, f2_ref, cs_ref, o_ref, *, tb, scale):
    for j in range(tb):
        zcat = jnp.concatenate([z_ref[0, 0, j], z_ref[0, 1, j]], axis=0)
        xst = _dot(f2_ref[...], zcat)
        xr = xst[:HEAD].astype(BF16)
        xi = xst[HEAD:].astype(BF16)
        y = _dot(xr, cs_ref[0:256, :]) + _dot(xi, cs_ref[256:512, :])
        o_ref[0, j] = (y * scale).astype(BF16)


def _fft_latent(f_lat, cs):
    B, T, W = f_lat.shape
    n2 = HEAD
    n1 = T // n2
    cols = n2 * W
    cb = 2048
    c1, s1 = _dft_mats(n1)
    f1 = jnp.concatenate([c1, -s1], 0).astype(BF16)
    t1 = jnp.arange(n1, dtype=F32)[:, None]
    s2 = jnp.arange(n2, dtype=F32)[None, :]
    ang = 2.0 * math.pi * (t1 * s2) / T
    cw = jnp.repeat(jnp.cos(ang), W, axis=1)
    sw = jnp.repeat(jnp.sin(ang), W, axis=1)
    z = pl.pallas_call(
        functools.partial(_fft_stage1_kernel, n1=n1),
        grid=(B, cols // cb),
        in_specs=[pl.BlockSpec((1, n1, cb), lambda b, j: (b, 0, j)),
                  pl.BlockSpec(f1.shape, lambda b, j: (0, 0)),
                  pl.BlockSpec((n1, cb), lambda b, j: (0, j)),
                  pl.BlockSpec((n1, cb), lambda b, j: (0, j))],
        out_specs=pl.BlockSpec((1, 2, n1, cb), lambda b, j: (b, 0, 0, j)),
        out_shape=jax.ShapeDtypeStruct((B, 2, n1, cols), BF16),
        compiler_params=_cparams(2),
        name="fft_stage1",
    )(f_lat.reshape(B, n1, cols), f1, cw, sw)
    z = z.reshape(B, 2, n1, n2, W)
    c2, s2m = _dft_mats(n2)
    f2 = jnp.concatenate([jnp.concatenate([c2, s2m], 1),
                          jnp.concatenate([-s2m, c2], 1)], 0).astype(BF16)
    tb = 8
    y = pl.pallas_call(
        functools.partial(_fft_stage2_kernel, tb=tb, scale=(T * HEAD) ** -0.5),
        grid=(B, n1 // tb),
        in_specs=[pl.BlockSpec((1, 2, tb, n2, W), lambda b, j: (b, 0, j, 0, 0)),
                  pl.BlockSpec(f2.shape, lambda b, j: (0, 0)),
                  pl.BlockSpec(cs.shape, lambda b, j: (0, 0))],
        out_specs=pl.BlockSpec((1, tb, n2, W), lambda b, j: (b, j, 0, 0)),
        out_shape=jax.ShapeDtypeStruct((B, n1, n2, W), BF16),
        compiler_params=_cparams(2),
        name="fft_stage2",
    )(z, f2, cs)
    return jnp.transpose(y, (0, 2, 1, 3)).reshape(B, T, W)


def _mlstm_prep_kernel(cur_ref, prev_ref, next_ref, cw_ref, cb_ref, gat_ref, gatt_ref, br_ref, bc_ref,
                       q_ref, kt_ref, col_ref, row_ref, *, n_chunks, n_ctx_chunks):
    c = pl.program_id(1)
    cur = cur_ref[0].astype(F32)
    first = jnp.logical_or(c == 0, c == n_ctx_chunks)
    last = jnp.logical_or(c == n_ctx_chunks - 1, c == n_chunks - 1)
    prev_row = prev_ref[0].astype(F32)[15:16, :]
    next_row = next_ref[0].astype(F32)[0:1, :]
    prev_row = jnp.where(first, 0.0, prev_row)
    next_row = jnp.where(last, 0.0, next_row)
    rows = lax.broadcasted_iota(I32, (CHUNK, 1), 0)
    up = jnp.where(rows == 0, prev_row, pltpu.roll(cur, 1, axis=0))
    dn = jnp.where(rows == CHUNK - 1, next_row, pltpu.roll(cur, CHUNK - 1, axis=0))
    y = cw_ref[0:1, :] * up + cw_ref[1:2, :] * cur + cw_ref[2:3, :] * dn + cb_ref[...]
    y = y * _sigmoid(y)
    q_ref[0] = (y[:, :M_WIDTH] * HEAD ** -0.5).astype(BF16)
    kt_ref[0] = y[:, M_WIDTH:].T.astype(BF16)

    def logsig(v):
        return jnp.minimum(v, 0.0) - jnp.log(1.0 + jnp.exp(-jnp.abs(v)))

    r_i = lax.broadcasted_iota(I32, (CHUNK, CHUNK), 0)
    c_i = lax.broadcasted_iota(I32, (CHUNK, CHUNK), 1)
    lower = jnp.where(c_i <= r_i, 1.0, 0.0).astype(BF16)
    upper = jnp.where(c_i >= r_i, 1.0, 0.0).astype(BF16)

    g = gat_ref[0] + br_ref[...]
    lane = lax.broadcasted_iota(I32, (1, 16), 1)
    lg = jnp.where(lane % 8 >= 4, logsig(g), g)
    hi, lo = _split_bf16(lg)
    cum_f = _dot(lower, hi) + _dot(lower, lo)
    cum_b = _dot(upper, hi) + _dot(upper, lo)
    cum_col = jnp.where(lane < 8, cum_f, cum_b)

    gt = gatt_ref[0] + bc_ref[...]
    row = lax.broadcasted_iota(I32, (16, 1), 0)
    lgt = jnp.where(row % 8 >= 4, logsig(gt), gt)
    hit, lot = _split_bf16(lgt)
    cum_ft = _dot(hit, upper) + _dot(lot, upper)
    cum_bt = _dot(hit, lower) + _dot(lot, lower)
    cumt = jnp.where(row < 8, cum_ft, cum_bt)
    cumt_i = pltpu.roll(cumt, 12, axis=0)
    rterm = lgt - cumt_i
    btot = jnp.broadcast_to(jnp.sum(lgt, axis=1, keepdims=True), (16, CHUNK))
    row_ref[0, 0:16] = jnp.where(row % 8 < 4, rterm, btot)
    dmax = jnp.broadcast_to(jnp.max(rterm, axis=1, keepdims=True), (16, CHUNK))
    row_ref[0, 16:32] = pltpu.roll(btot, 12, axis=0) + dmax
    seen_max = jnp.zeros((CHUNK, 16), F32)
    for d in range(2):
        seen = (c_i <= r_i) if d == 0 else (c_i >= r_i)
        for h in range(M_HEADS):
            idx = 8 * d + h
            cmx = jnp.max(jnp.where(seen, rterm[idx:idx + 1, :], -jnp.inf), axis=1, keepdims=True)
            seen_max = jnp.where(lane == idx, cmx, seen_max)
    col_ref[0] = jnp.where(lane % 8 >= 4, cum_col, seen_max)


def _mlstm_prep(qk, gat, gatt, conv_w, conv_b, bias_row, bias_col, *, n_ctx_chunks):
    B, S, W = qk.shape
    nc = S // CHUNK
    n16 = S // 16
    hw = M_WIDTH
    kern = functools.partial(_mlstm_prep_kernel, n_chunks=nc, n_ctx_chunks=n_ctx_chunks)
    full = lambda a: pl.BlockSpec(a.shape, lambda b, c: (0,) * a.ndim)
    return pl.pallas_call(
        kern,
        grid=(B, nc),
        in_specs=[pl.BlockSpec((1, CHUNK, W), lambda b, c: (b, c, 0)),
                  pl.BlockSpec((1, 16, W), lambda b, c: (b, jnp.maximum(c * 8 - 1, 0), 0)),
                  pl.BlockSpec((1, 16, W), lambda b, c: (b, jnp.minimum((c + 1) * 8, n16 - 1), 0)),
                  full(conv_w), full(conv_b),
                  pl.BlockSpec((1, CHUNK, 16), lambda b, c: (b, c, 0)),
                  pl.BlockSpec((1, 16, CHUNK), lambda b, c: (b, 0, c)),
                  full(bias_row), full(bias_col)],
        out_specs=[pl.BlockSpec((1, CHUNK, hw), lambda b, c: (b, c, 0)),
                   pl.BlockSpec((1, hw, CHUNK), lambda b, c: (b, 0, c)),
                   pl.BlockSpec((1, CHUNK, 16), lambda b, c: (b, c, 0)),
                   pl.BlockSpec((1, 32, CHUNK), lambda b, c: (b, 0, c))],
        out_shape=[jax.ShapeDtypeStruct((B, S, hw), BF16),
                   jax.ShapeDtypeStruct((B, hw, S), BF16),
                   jax.ShapeDtypeStruct((B, S, 16), F32),
                   jax.ShapeDtypeStruct((B, 32, S), F32)],
        compiler_params=_cparams(2),
        name="mlstm_prep",
    )(qk, qk, qk, conv_w, conv_b, gat, gatt, bias_row, bias_col)


def _mlstm_scan_kernel(*refs, n_batch):
    ins, (of_ref, ob_ref, c_scr, m_scr) = refs[:10], refs[10:]
    j = pl.program_id(1)

    @pl.when(j == 0)
    def _():
        c_scr[...] = jnp.zeros_like(c_scr)
        m_scr[...] = jnp.zeros_like(m_scr)

    t_i = lax.broadcasted_iota(I32, (CHUNK, CHUNK), 0)
    s_i = lax.broadcasted_iota(I32, (CHUNK, CHUNK), 1)
    lane_i = lax.broadcasted_iota(I32, (1, LANE), 1)
    row_i = lax.broadcasted_iota(I32, (LANE, 1), 0)
    for bb in range(n_batch):
        for d, o_ref in enumerate((of_ref, ob_ref)):
            q_ref, kt_ref, v_ref, col_ref, row_ref = ins[5 * d:5 * d + 5]
            colp = col_ref[bb]
            rowp = row_ref[bb]
            mask = (s_i <= t_i) if d == 0 else (s_i >= t_i)
            outs = []
            for pair_i in range(M_HEADS // 2):
                lanes = slice(pair_i * LANE, (pair_i + 1) * LANE)
                q_blk = q_ref[bb, :, lanes]
                kt_blk = kt_ref[bb, lanes, :]
                v_blk = v_ref[bb, :, lanes]
                tots = []
                for p in range(2):
                    h = 2 * pair_i + p
                    st = (bb * 2 + d) * M_HEADS + h
                    own = (lane_i < HEAD) if p == 0 else (lane_i >= HEAD)
                    own_r = (row_i < HEAD) if p == 0 else (row_i >= HEAD)
                    one_lane = HEAD if p == 0 else 0
                    full = (CHUNK, CHUNK)
                    bcol = jnp.broadcast_to(colp[:, 8 * d + 4 + h:8 * d + 5 + h], full)
                    rmax = bcol + jnp.broadcast_to(colp[:, 8 * d + h:8 * d + h + 1], full)
                    rrow = rowp[8 * d + h:8 * d + h + 1, :]
                    btot = rowp[8 * d + 4 + h:8 * d + 5 + h, :]
                    dmax = rowp[16 + 8 * d + h:17 + 8 * d + h, :]
                    m = m_scr[st, 0:1, :]
                    inter = bcol + m
                    m_t = jnp.maximum(inter, rmax)
                    w = jnp.exp(jnp.where(mask, bcol + rrow, -jnp.inf) - m_t)
                    a = jnp.exp(inter - m_t)
                    qh = jnp.where(own, q_blk, 0)
                    vh = jnp.where(own, v_blk, jnp.where(lane_i == one_lane, 1.0, 0.0).astype(BF16))
                    s = _dot(qh, kt_blk) * w
                    cst = c_scr[st]
                    tot = a * _dot(qh, cst.astype(BF16)) + _dot(s.astype(BF16), vh)
                    den = jnp.broadcast_to(tot[:, one_lane:one_lane + 1], full)
                    tots.append(tot / jnp.maximum(jnp.abs(den), jnp.exp(-m_t)))
                    bm = btot + m
                    m_new = jnp.maximum(bm, dmax)
                    ws = jnp.exp(btot + rrow - m_new)
                    kw = (jnp.where(own_r, kt_blk, 0).astype(F32) * ws).astype(BF16)
                    c_scr[st] = jnp.exp(bm - m_new) * cst + _dot(kw, vh)
                    m_scr[st] = jnp.broadcast_to(m_new, (8, LANE))
                outs.append(jnp.where(lane_i < HEAD, tots[0], tots[1]))
            o_ref[bb] = jnp.concatenate(outs, axis=1)


def _mlstm_scan(q, kt, v, colp, rowp, *, n_ctx_chunks):
    B, S, hw = q.shape
    nc = S // CHUNK
    nb = next(n for n in (4, 2, 1) if B % n == 0)

    def rev(j):
        return jnp.where(j < n_ctx_chunks, n_ctx_chunks - 1 - j, nc + n_ctx_chunks - 1 - j)

    def specs(cid):
        return [pl.BlockSpec((nb, CHUNK, hw), lambda b, j: (b, cid(j), 0)),
                pl.BlockSpec((nb, hw, CHUNK), lambda b, j: (b, 0, cid(j))),
                pl.BlockSpec((nb, CHUNK, hw), lambda b, j: (b, cid(j), 0)),
                pl.BlockSpec((nb, CHUNK, 16), lambda b, j: (b, cid(j), 0)),
                pl.BlockSpec((nb, 32, CHUNK), lambda b, j: (b, 0, cid(j)))]

    fwd = lambda j: j
    n_chains = nb * 2 * M_HEADS
    return pl.pallas_call(
        functools.partial(_mlstm_scan_kernel, n_batch=nb),
        grid=(B // nb, nc),
        in_specs=specs(fwd) + specs(rev),
        out_specs=[pl.BlockSpec((nb, CHUNK, hw), lambda b, j: (b, j, 0)),
                   pl.BlockSpec((nb, CHUNK, hw), lambda b, j: (b, rev(j), 0))],
        out_shape=[jax.ShapeDtypeStruct((B, S, hw), F32), jax.ShapeDtypeStruct((B, S, hw), F32)],
        scratch_shapes=[pltpu.VMEM((n_chains, LANE, LANE), F32), pltpu.VMEM((n_chains, 8, LANE), F32)],
        compiler_params=_cparams(2),
        name="mlstm_scan",
    )(q, kt, v, colp, rowp, q, kt, v, colp, rowp)


def _attn_kernel(qt_ref, k_ref, vt_ref, o_ref, s0_scr, s1_scr, p0_scr, p1_scr, al_scr, cm_scr, acc_scr,
                 m_scr, *,
                 n_q, n_kc, sub, qtiles):
    tk = sub * TOK_TILE
    tq = qtiles * TOK_TILE
    n_items = n_q * n_kc
    heads = [slice(hh * LANE, (hh + 1) * LANE) for hh in range(2)]
    vheads = [slice(hh * V_ROWS, (hh + 1) * V_ROWS) for hh in range(2)]
    s_bufs = (s0_scr, s1_scr)
    p_bufs = (p0_scr, p1_scr)

    def split(n):
        if n_q == 1:
            return 0, n
        qi = n // n_kc
        return qi, n - qi * n_kc

    def scores(n, par):
        qi, c = split(n)
        for hh in range(2):
            kc = k_ref[0, pl.ds(pl.multiple_of(c * tk, tk), tk), heads[hh]]
            qt = jnp.concatenate([qt_ref[0, qi * qtiles + j, heads[hh], :] for j in range(qtiles)], axis=1)
            st = _dot(kc, qt)
            s_bufs[par][hh] = st
            cm_scr[par, hh] = jnp.max(st, axis=0, keepdims=True)

    def softmax(n, par):
        _, c = split(n)
        for hh in range(2):
            st = s_bufs[par][hh]
            m = jnp.where(c == 0, -jnp.inf, m_scr[hh])
            m_new = jnp.maximum(m, cm_scr[par, hh])
            p_bufs[par][hh] = jnp.exp2(st - m_new).astype(BF16)
            al_scr[par, hh] = jnp.exp2(m - m_new)
            m_scr[hh] = m_new

    def values(n, par):
        qi, c = split(n)
        for hh in range(2):
            pv = _dot(vt_ref[0, c * sub, vheads[hh], :], p_bufs[par][hh, 0:TOK_TILE])
            for j in range(1, sub):
                pv += _dot(vt_ref[0, c * sub + j, vheads[hh], :],
                           p_bufs[par][hh, j * TOK_TILE:(j + 1) * TOK_TILE])
            acc_scr[hh] = al_scr[par, hh] * acc_scr[hh] + pv

        def finalize():
            o = jnp.concatenate([acc_scr[hh, :HEAD] / acc_scr[hh, HEAD:HEAD + 1] for hh in range(2)], axis=0)
            o_ref[0, pl.ds(pl.multiple_of(qi * tq, tq), tq), :] = o.T.astype(BF16)

        if isinstance(n, int) and n % n_kc == n_kc - 1:
            finalize()

    m_scr[...] = jnp.zeros_like(m_scr)
    acc_scr[...] = jnp.zeros_like(acc_scr)
    scores(0, 0)
    if n_items > 1:
        scores(1, 1)
    softmax(0, 0)

    def step(t, par):
        scores(t + 2, par)
        softmax(t + 1, 1 - par)
        values(t, par)

    def run(t0, t1):
        if t0 < t1 and t0 % 2:
            step(t0, 1)
            t0 += 1
        if t1 - t0 >= 2:
            def pair(i, carry):
                step(t0 + 2 * i, 0)
                step(t0 + 2 * i + 1, 1)
                return carry
            lax.fori_loop(0, (t1 - t0) // 2, pair, 0)
        if (t1 - t0) % 2:
            step(t1 - 1, (t1 - 1) % 2)

    n_full = max(n_items - 2, 0)
    t = 0
    for last in range(n_kc - 1, n_full, n_kc):
        run(t, last)
        step(last, last % 2)
        t = last + 1
    run(t, n_full)
    if n_items > 1:
        softmax(n_items - 1, (n_items - 1) % 2)
        values(n_items - 2, (n_items - 2) % 2)
    values(n_items - 1, (n_items - 1) % 2)


def _attention(qt4, k, vt4, *, n_keys, qtiles, sub, q_per_step=1):
    B, n_qb, kw, _ = qt4.shape
    n_kb = n_keys // TOK_TILE
    tk = sub * TOK_TILE
    tq = qtiles * TOK_TILE
    qb_step = q_per_step * qtiles
    kern = functools.partial(_attn_kernel, n_q=q_per_step, n_kc=n_kb // sub, sub=sub, qtiles=qtiles)
    return pl.pallas_call(
        kern,
        grid=(B, A_HEADS // 2, n_qb // qb_step),
        in_specs=[pl.BlockSpec((1, qb_step, 2 * LANE, TOK_TILE), lambda b, hp, i: (b, i, hp, 0)),
                  pl.BlockSpec((1, n_keys, 2 * LANE), lambda b, hp, i: (b, 0, hp)),
                  pl.BlockSpec((1, n_kb, 2 * V_ROWS, TOK_TILE), lambda b, hp, i: (b, 0, hp, 0))],
        out_specs=pl.BlockSpec((1, qb_step * TOK_TILE, LANE), lambda b, hp, i: (b, i, hp)),
        out_shape=jax.ShapeDtypeStruct((B, n_qb * TOK_TILE, A_HEADS * HEAD), BF16),
        scratch_shapes=[pltpu.VMEM((2, tk, tq), F32), pltpu.VMEM((2, tk, tq), F32),
                        pltpu.VMEM((2, tk, tq), BF16), pltpu.VMEM((2, tk, tq), BF16),
                        pltpu.VMEM((2, 2, 1, tq), F32), pltpu.VMEM((2, 2, 1, tq), F32),
                        pltpu.VMEM((2, V_ROWS, tq), F32), pltpu.VMEM((2, 1, tq), F32)],
        compiler_params=_cparams(3, VMEM_LIMIT),
        name="attention",
    )(qt4, k, vt4)


def _outproj_kernel(yac_ref, yal_ref, hf_ref, hb_ref, o_ref, ycc_ref, ycl_ref, x_ref, mod_ref, wa_ref, wb_ref,
                    wc_ref, mg_ref, blk_ref, n2_ref, rw_ref, xo_ref, hx_ref, aff_ref, *, d_model,
                    n_ctx_tiles):
    D = d_model
    i = pl.program_id(1)
    mod = mod_ref[0]
    g1, sh2, sc2 = mod[:, 2 * D:3 * D], mod[:, 3 * D:4 * D], mod[:, 4 * D:5 * D]
    ya = jnp.where(i < n_ctx_tiles, yac_ref[0], yal_ref[0])
    h = hf_ref[0] + hb_ref[0]
    hi, lo = _split_bf16(h * h)
    ms = _dot(hi, blk_ref[...]) + _dot(lo, blk_ref[...])
    hn = h * lax.rsqrt(ms + EPS) * mg_ref[...]
    yb = (hn * _sigmoid(o_ref[0].astype(F32))).astype(BF16)
    yc = jnp.where(i < n_ctx_tiles, ycc_ref[0], ycl_ref[0])
    mix = _dot(ya, wa_ref[...]) + _dot(yb, wb_ref[...]) + _dot(yc, wc_ref[...])
    x = x_ref[0] + g1 * mix
    xo_ref[0] = x
    hx = _rms(x, n2_ref[...]) * (1.0 + sc2) + sh2
    hx_ref[0] = hx.astype(BF16)
    h_hi, h_lo = _split_bf16(hx)
    r_hi, r_lo = _split_bf16(rw_ref[...])
    logits = _dot(h_hi, r_hi) + _dot(h_lo, r_hi) + _dot(h_hi, r_lo)
    e = jnp.exp(logits - jnp.max(logits, axis=1, keepdims=True))
    aff_ref[0] = e / jnp.sum(e, axis=1, keepdims=True)


def _outproj(ya_ctx, ya_lat, hf, hb, o, yc_ctx, yc_lat, xs, mods_l, wa, wb, wc, mg, blk, n2, rw, *,
             n_ctx_tiles):
    B, S, D = xs.shape
    nt = S // TOK_TILE
    tok = lambda w: pl.BlockSpec((1, TOK_TILE, w), lambda b, i: (b, i, 0))
    full = lambda a: pl.BlockSpec(a.shape, lambda b, i: (0,) * a.ndim)
    mod_spec = pl.BlockSpec((1, 1, 6 * D), lambda b, i: (jnp.where(i < n_ctx_tiles, B, b), 0, 0))
    hw = M_WIDTH
    n_lat_tiles = nt - n_ctx_tiles
    ctx_tok = lambda w: pl.BlockSpec((1, TOK_TILE, w), lambda b, i: (b, jnp.minimum(i, n_ctx_tiles - 1), 0))
    lat_tok = lambda w: pl.BlockSpec(
        (1, TOK_TILE, w), lambda b, i: (b, jnp.clip(i - n_ctx_tiles, 0, n_lat_tiles - 1), 0))
    kern = functools.partial(_outproj_kernel, d_model=D, n_ctx_tiles=n_ctx_tiles)
    return pl.pallas_call(
        kern,
        grid=(B, nt),
        in_specs=[ctx_tok(256), lat_tok(256),
                  tok(hw), tok(hw),
                  tok(hw), ctx_tok(A_HEADS * HEAD), lat_tok(A_HEADS * HEAD), tok(D), mod_spec,
                  full(wa), full(wb), full(wc), full(mg), full(blk), full(n2), full(rw)],
        out_specs=[tok(D), tok(D), tok(N_EXPERTS)],
        out_shape=[jax.ShapeDtypeStruct((B, S, D), F32), jax.ShapeDtypeStruct((B, S, D), BF16),
                   jax.ShapeDtypeStruct((B, S, N_EXPERTS), F32)],
        compiler_params=_cparams(2, VMEM_LIMIT),
        name="outproj",
    )(ya_ctx, ya_lat, hf, hb, o, yc_ctx, yc_lat, xs, mods_l, wa, wb, wc, mg, blk, n2, rw)


def _route_part(a, cap, slot0, tri, pos_ref, base_ref, lane0):
    n = a.shape[1]
    capf = float(cap)
    bits = pltpu.bitcast(a, I32)
    v = jnp.zeros((N_EXPERTS, 1), I32)
    for bit in range(30, -1, -1):
        cand = v | (1 << bit)
        cnt = jnp.sum(jnp.where(bits >= cand, 1.0, 0.0), axis=1, keepdims=True)
        v = jnp.where(cnt >= capf, cand, v)
    gt = bits > v
    eq = bits == v
    need = capf - jnp.sum(jnp.where(gt, 1.0, 0.0), axis=1, keepdims=True)
    idx = lax.broadcasted_iota(I32, (N_EXPERTS, n), 1)
    x = jnp.zeros((N_EXPERTS, 1), I32)
    for bit in range(max(n - 1, 1).bit_length() - 1, -1, -1):
        cand = x | (1 << bit)
        cnt = jnp.sum(jnp.where(eq, jnp.where(idx < cand, 1.0, 0.0), 0.0), axis=1, keepdims=True)
        x = jnp.where(cnt < need, cand, x)
    sel = jnp.where(gt, 1.0, jnp.where(eq, jnp.where(idx <= x, 1.0, 0.0), 0.0))
    running = jnp.zeros((N_EXPERTS, 1), F32)
    fill = jnp.zeros((N_EXPERTS, 1), F32)
    for c in range(n // CHUNK):
        blk = sel[:, c * CHUNK:(c + 1) * CHUNK]
        incl = _dot(blk.astype(BF16), tri)
        pos = running + incl - blk + float(slot0)
        cg = lane0 // CHUNK + c
        pos_ref[0, :, lane0 + c * CHUNK:lane0 + (c + 1) * CHUNK] = jnp.where(
            blk > 0.5, pos, -1.0).astype(I32)
        base_ref[0, :, cg:cg + 1] = (running + float(slot0)).astype(I32)
        running = running + incl[:, CHUNK - 1:CHUNK]
        fill = jnp.maximum(fill, incl[:, CHUNK - 1:CHUNK])
    return fill


def _route_kernel(aff_ref, pos_ref, base_ref, *, n_ctx, cap_ctx, cap_lat):
    r_i = lax.broadcasted_iota(I32, (CHUNK, CHUNK), 0)
    c_i = lax.broadcasted_iota(I32, (CHUNK, CHUNK), 1)
    tri = jnp.where(r_i <= c_i, 1.0, 0.0).astype(BF16)
    base_ref[...] = jnp.zeros_like(base_ref)
    a = aff_ref[0]
    fill_c = _route_part(a[:, :n_ctx], cap_ctx, 0, tri, pos_ref, base_ref, 0)
    fill_l = _route_part(a[:, n_ctx:], cap_lat, cap_ctx, tri, pos_ref, base_ref, n_ctx)
    base_ref[0, :, LANE - 1:LANE] = jnp.maximum(fill_c, fill_l).astype(I32)


def _route(aff_t, *, n_ctx, cap_ctx, cap_lat):
    B, E, S = aff_t.shape
    assert S // CHUNK < LANE - 1
    kern = functools.partial(_route_kernel, n_ctx=n_ctx, cap_ctx=cap_ctx, cap_lat=cap_lat)
    return pl.pallas_call(
        kern,
        grid=(B,),
        in_specs=[pl.BlockSpec((1, E, S), lambda b: (b, 0, 0))],
        out_specs=[pl.BlockSpec((1, E, S), lambda b: (b, 0, 0)),
                   pl.BlockSpec((1, E, LANE), lambda b: (b, 0, 0))],
        out_shape=[jax.ShapeDtypeStruct((B, E, S), I32), jax.ShapeDtypeStruct((B, E, LANE), I32)],
        compiler_params=_cparams(1),
        name="route",
    )(aff_t)


def _window_start(base, align, rows, win):
    w0 = lax.shift_left(lax.shift_right_logical(base, int(math.log2(align))), int(math.log2(align)))
    return pl.multiple_of(jnp.minimum(w0, rows - win), align)


def _gather_kernel(base_ref, h_ref, pos_ref, xe_ref, *, n_chunks, rows, win, unroll):
    b = pl.program_id(0)
    e = pl.program_id(2)
    xe_ref[0, 0] = jnp.zeros(xe_ref.shape[2:], BF16)

    def gather_all(win_rows):
        def group(g, carry):
            picked = []
            for u in range(unroll):
                c = g * unroll + u
                w0 = _window_start(base_ref[b, e, c], GATHER_ALIGN_BF16, rows, win_rows)
                posr = pos_ref[0, 0, pl.ds(c, 1), :]
                slot = lax.broadcasted_iota(I32, (win_rows, CHUNK), 0) + w0
                onehot = jnp.where(posr == slot, 1.0, 0.0).astype(BF16)
                hc = h_ref[0, pl.ds(pl.multiple_of(c * CHUNK, CHUNK), CHUNK), :]
                picked.append((w0, _dot(onehot, hc).astype(BF16)))
            for w0, rows_c in picked:
                xe_ref[0, 0, pl.ds(w0, win_rows), :] += rows_c
            return carry

        lax.fori_loop(0, n_chunks // unroll, group, 0)

    most = base_ref[b, e, LANE - 1]
    small = min(SMALL_FILL + GATHER_ALIGN_BF16, rows)
    if small < win:
        pl.when(most <= SMALL_FILL)(functools.partial(gather_all, small))
        pl.when(most > SMALL_FILL)(functools.partial(gather_all, win))
    else:
        gather_all(win)


def _gather(hx, pos4, bases, *, rows):
    B, S, D = hx.shape
    nc = S // CHUNK
    dh = D // 2
    win = min(CHUNK + GATHER_ALIGN_BF16, rows)
    unroll = next(u for u in (11, 8, 6, 4, 3, 2, 1) if nc % u == 0)
    kern = functools.partial(_gather_kernel, n_chunks=nc, rows=rows, win=win, unroll=unroll)
    return pl.pallas_call(
        kern,
        grid_spec=pltpu.PrefetchScalarGridSpec(
            num_scalar_prefetch=1,
            grid=(B, 2, N_EXPERTS),
            in_specs=[pl.BlockSpec((1, S, dh), lambda b, j, e, bs: (b, 0, j)),
                      pl.BlockSpec((1, 1, nc, CHUNK), lambda b, j, e, bs: (b, e, 0, 0))],
            out_specs=pl.BlockSpec((1, 1, rows, dh), lambda b, j, e, bs: (b, e, 0, j))),
        out_shape=jax.ShapeDtypeStruct((B, N_EXPERTS, rows, D), BF16),
        compiler_params=_cparams(3, VMEM_LIMIT),
        name="moe_gather",
    )(bases, hx, pos4)


def _ffn_kernel(x_ref, wg_ref, wu_ref, wd_ref, y_ref, wg_s, wu_s, wd_s, *, row_tile, n_tiles):
    @pl.when(pl.program_id(1) == 0)
    def _():
        wg_s[...] = wg_ref[0].astype(BF16)
        wu_s[...] = wu_ref[0].astype(BF16)
        wd_s[...] = wd_ref[0].astype(BF16)

    def tile(i, carry):
        r0 = pl.multiple_of(i * row_tile, 16)
        x = x_ref[0, 0, pl.ds(r0, row_tile), :]
        g = _dot(x, wg_s[...])
        u = _dot(x, wu_s[...])
        hid = (g * _sigmoid(g) * u).astype(BF16)
        y_ref[0, 0, pl.ds(r0, row_tile), :] = _dot(hid, wd_s[...]).astype(BF16)
        return carry

    lax.fori_loop(0, n_tiles, tile, 0)


def _ffn_row_tile(rows):
    for t in (352, 256, 176, 128, 96, 64, 32, 16):
        if rows % t == 0:
            return t
    raise ValueError(f"expert slot rows {rows} must be a multiple of 16")


def _ffn(xe, wg, wu, wd):
    B, E, R, D = xe.shape
    F = wg.shape[-1]
    rt = _ffn_row_tile(R)
    kern = functools.partial(_ffn_kernel, row_tile=rt, n_tiles=R // rt)
    return pl.pallas_call(
        kern,
        grid=(E, B),
        in_specs=[pl.BlockSpec((1, 1, R, D), lambda e, b: (b, e, 0, 0)),
                  pl.BlockSpec((1, D, F), lambda e, b: (e, 0, 0)),
                  pl.BlockSpec((1, D, F), lambda e, b: (e, 0, 0)),
                  pl.BlockSpec((1, F, D), lambda e, b: (e, 0, 0))],
        out_specs=pl.BlockSpec((1, 1, R, D), lambda e, b: (b, e, 0, 0)),
        out_shape=jax.ShapeDtypeStruct((B, E, R, D), BF16),
        scratch_shapes=[pltpu.VMEM((D, F), BF16), pltpu.VMEM((D, F), BF16), pltpu.VMEM((F, D), BF16)],
        compiler_params=_cparams(2, VMEM_LIMIT),
        name="moe_ffn",
    )(xe, wg, wu, wd)


def _combine_kernel(base_ref, x_ref, ye_ref, posc_ref, aff_ref, modc_ref, modx_ref, o_ref, *, d_model,
                    blk_chunks, n_ctx_chunks, rows, win):
    D = d_model
    b = pl.program_id(0)
    tb = pl.program_id(1)
    ep = pl.program_id(2)
    group = ye_ref.shape[1]
    experts = [group * ep + k for k in range(group)]

    @pl.when(ep == 0)
    def _():
        o_ref[0] = jnp.zeros(o_ref.shape[1:], F32)

    lane = lax.broadcasted_iota(I32, (1, N_EXPERTS), 1)
    small = min(SMALL_FILL + GATHER_ALIGN_BF16, rows)

    def columns(rs, e):
        pcol = jnp.sum(jnp.where(lane == e, posc_ref[0, rs, :].astype(F32), 0.0), axis=1, keepdims=True)
        gcol = jnp.sum(jnp.where(lane == e, aff_ref[0, rs, :], 0.0), axis=1, keepdims=True)
        return pcol, gcol

    def scatter_narrow():
        lanes = lax.broadcasted_iota(I32, (CHUNK, group * 2 * small), 1)
        from_k = [lanes >= 2 * small * k for k in range(1, group)]
        within = lanes
        for m in from_k:
            within = jnp.where(m, within - 2 * small, within)
        low = within >= small
        lane1 = jnp.where(low, within - small, within)
        for cc in range(blk_chunks):
            rs = slice(cc * CHUNK, (cc + 1) * CHUNK)
            chunk = tb * blk_chunks + cc
            cols = [columns(rs, e) for e in experts]
            starts = [_window_start(base_ref[b, e, chunk], GATHER_ALIGN_BF16, rows, small) for e in experts]
            w, p, g = starts[0], cols[0][0], cols[0][1]
            for k, m in enumerate(from_k, start=1):
                w, p, g = jnp.where(m, starts[k], w), jnp.where(m, cols[k][0], p), jnp.where(m, cols[k][1], g)
            g_hi = g.astype(BF16).astype(F32)
            gate = jnp.where(low, g - g_hi, g_hi)
            sel = jnp.where(p == (lane1 + w).astype(F32), gate, 0.0).astype(BF16)
            ys = [ye_ref[0, k, pl.ds(starts[k], small), :] for k in range(group)]
            o_ref[0, rs, :] += _dot(sel, jnp.concatenate([y for y in ys for _ in range(2)], axis=0))

    def scatter_wide():
        for cc in range(blk_chunks):
            rs = slice(cc * CHUNK, (cc + 1) * CHUNK)
            chunk = tb * blk_chunks + cc
            for k, e in enumerate(experts):
                pcol, gcol = columns(rs, e)
                w0 = _window_start(base_ref[b, e, chunk], GATHER_ALIGN_BF16, rows, win)
                slot = (lax.broadcasted_iota(I32, (CHUNK, win), 1) + w0).astype(F32)
                onehot = jnp.where(pcol == slot, 1.0, 0.0).astype(BF16)
                o_ref[0, rs, :] += gcol * _dot(onehot, ye_ref[0, k, pl.ds(w0, win), :])

    most = functools.reduce(jnp.maximum, [base_ref[b, e, LANE - 1] for e in experts])
    if small < win:
        pl.when(most <= SMALL_FILL)(scatter_narrow)
        pl.when(most > SMALL_FILL)(scatter_wide)
    else:
        scatter_wide()

    @pl.when(ep == N_EXPERTS // group - 1)
    def _():
        g2c = modc_ref[0][:, 5 * D:6 * D]
        g2x = modx_ref[0][:, 5 * D:6 * D]
        for cc in range(blk_chunks):
            rs = slice(cc * CHUNK, (cc + 1) * CHUNK)
            g2 = jnp.where(tb * blk_chunks + cc < n_ctx_chunks, g2c, g2x)
            o_ref[0, rs, :] = x_ref[0, rs, :] + g2 * o_ref[0, rs, :]


def _combine_blk_chunks(nc):
    for k in (11, 8, 6, 4, 3, 2, 1):
        if nc % k == 0:
            return k
    return 1


def _combine(x, ye, posc, aff, mods_l, bases, *, n_ctx):
    B, S, D = x.shape
    R = ye.shape[2]
    nc = S // CHUNK
    k = _combine_blk_chunks(nc)
    tb = k * CHUNK
    win = min(CHUNK + GATHER_ALIGN_BF16, R)
    kern = functools.partial(_combine_kernel, d_model=D, blk_chunks=k, n_ctx_chunks=n_ctx // CHUNK,
                             rows=R, win=win)
    modc_spec = pl.BlockSpec((1, 1, 6 * D), lambda b, t, e, bs: (B, 0, 0))
    modx_spec = pl.BlockSpec((1, 1, 6 * D), lambda b, t, e, bs: (b, 0, 0))
    return pl.pallas_call(
        kern,
        grid_spec=pltpu.PrefetchScalarGridSpec(
            num_scalar_prefetch=1,
            grid=(B, S // tb, N_EXPERTS // COMBINE_GROUP),
            in_specs=[pl.BlockSpec((1, tb, D), lambda b, t, e, bs: (b, t, 0)),
                      pl.BlockSpec((1, COMBINE_GROUP, R, D), lambda b, t, e, bs: (b, e, 0, 0)),
                      pl.BlockSpec((1, tb, N_EXPERTS), lambda b, t, e, bs: (b, t, 0)),
                      pl.BlockSpec((1, tb, N_EXPERTS), lambda b, t, e, bs: (b, t, 0)),
                      modc_spec, modx_spec],
            out_specs=pl.BlockSpec((1, tb, D), lambda b, t, e, bs: (b, t, 0))),
        out_shape=jax.ShapeDtypeStruct((B, S, D), F32),
        compiler_params=_cparams(3, VMEM_LIMIT),
        name="moe_combine",
    )(bases, x, ye, posc, aff, mods_l, mods_l)


def _final_kernel(x_ref, g_ref, o_ref):
    o_ref[0] = _rms(x_ref[0], g_ref[...])


def _final_norm(xs, g, *, n_ctx):
    B, S, D = xs.shape
    T = S - n_ctx
    off = n_ctx // TOK_TILE
    return pl.pallas_call(
        _final_kernel,
        grid=(B, T // TOK_TILE),
        in_specs=[pl.BlockSpec((1, TOK_TILE, D), lambda b, i: (b, i + off, 0)),
                  pl.BlockSpec((1, D), lambda b, i: (0, 0))],
        out_specs=pl.BlockSpec((1, TOK_TILE, D), lambda b, i: (b, i, 0)),
        out_shape=jax.ShapeDtypeStruct((B, T, D), F32),
        compiler_params=_cparams(2),
        name="final_norm",
    )(xs, g)


def _rot_swap(w):
    half = A_ROPE // 2
    return jnp.concatenate([-w[..., half:], w[..., :half]], -1)


def _rope_tables(n_ctx, n_lat):
    rows = n_lat // GRID_W
    row_id = jnp.repeat(jnp.arange(rows, dtype=F32), GRID_W)
    col_id = jnp.tile(jnp.arange(GRID_W, dtype=F32), rows)
    n_freq = A_ROPE // 4
    inv = ROPE_BASE ** (-jnp.arange(n_freq, dtype=F32) / n_freq)
    ang = jnp.concatenate([row_id[:, None] * inv, col_id[:, None] * inv], -1)
    ang = jnp.concatenate([jnp.zeros((n_ctx, A_ROPE // 2), F32), ang], 0)
    S = n_ctx + n_lat
    cos, sin = jnp.cos(ang), jnp.sin(ang)
    pad = jnp.zeros((S, LANE - HEAD - A_ROPE), F32)
    cos128 = jnp.concatenate([jnp.ones((S, HEAD), F32), cos, cos, pad], -1)
    sin128 = jnp.concatenate([jnp.zeros((S, HEAD), F32), sin, sin, pad], -1)
    return cos128, sin128


def kernel(x, c, ctx, c_ctx, ada_w, ada_b, norm1_g, norm2_g, w_in, m_conv_w, m_conv_b, m_ib, m_fb, m_norm_g, a_qnorm_g, a_wq_up, a_kvnorm_g, a_wkv_up, w_out, router_w, e_w_gate, e_w_up, e_w_down, final_g):
    B, T, D = x.shape
    n_ctx = ctx.shape[1]
    L = ada_w.shape[0]
    S = n_ctx + T
    assert D == 16 * HEAD and n_ctx % TOK_TILE == 0 and T % TOK_TILE == 0 and T % (GRID_W * 8) == 0
    n_ctx_tiles = n_ctx // TOK_TILE
    n_ctx_chunks = n_ctx // CHUNK
    cap_ctx = EC_FACTOR * n_ctx // N_EXPERTS
    cap_lat = EC_FACTOR * T // N_EXPERTS
    slot_rows = cap_ctx + cap_lat

    o_f, o_qk, o_v, o_o, o_g, o_cq, o_ckv, o_kr = (0, 256, 768, 1024, 1280, 1296, 1680, 1936)
    w_kr = w_in[:, :, o_kr:o_kr + A_ROPE]
    slot = lambda w: jnp.pad(w, ((0, 0), (0, 0), (HEAD, LANE - HEAD - A_ROPE)))
    w_g = w_in[:, :, o_g:o_g + 16]
    w1 = jnp.concatenate([
        w_in[:, :, o_f:o_f + 256],
        w_in[:, :, o_qk:o_qk + 2 * M_WIDTH],
        w_in[:, :, o_v:o_v + M_WIDTH],
        w_in[:, :, o_o:o_o + M_WIDTH],
        w_in[:, :, o_cq:o_cq + 384],
        w_in[:, :, o_ckv:o_ckv + 256],
        slot(w_kr),
        slot(_rot_swap(w_kr)),
        jnp.pad(w_g, ((0, 0), (0, 0), (0, LANE - 16))),
    ], -1).astype(BF16)
    assert w1.shape[-1] == _N_COLS
    wgt = jnp.swapaxes(w_g, 1, 2).astype(BF16)
    conv_w = m_conv_w
    conv_b = m_conv_b[:, None, :]
    gate_bias = jnp.stack([m_ib, m_fb], 2).reshape(L, 16)
    wq3 = a_wq_up.reshape(L, -1, A_HEADS, HEAD + A_ROPE)
    zq = jnp.zeros(wq3.shape[:-1] + (LANE - HEAD - A_ROPE,), F32)
    wq = jnp.concatenate([wq3, zq], -1).reshape(L, -1, A_HEADS * LANE)
    wqt = jnp.swapaxes(wq, 1, 2).astype(BF16)
    wkv3 = a_wkv_up.reshape(L, -1, A_HEADS, 2 * HEAD)
    zk = jnp.zeros(wkv3.shape[:-1] + (HEAD,), F32)
    wk = jnp.concatenate([wkv3[..., :HEAD], zk], -1).reshape(L, -1, A_HEADS * LANE).astype(BF16)
    zv = jnp.zeros(wkv3.shape[:-1] + (V_ROWS - HEAD,), F32)
    wvt = jnp.swapaxes(jnp.concatenate([wkv3[..., HEAD:], zv], -1).reshape(L, -1, A_HEADS * V_ROWS),
                       1, 2).astype(BF16)
    wa = w_out[:, 0:256].astype(BF16)
    wb = w_out[:, 256:512].astype(BF16)
    wc = w_out[:, 512:1024].astype(BF16)
    mg = m_norm_g[:, None, :]
    lane = jnp.arange(M_WIDTH)
    blk = jnp.where(lane[:, None] // HEAD == lane[None, :] // HEAD, 1.0 / HEAD, 0.0).astype(BF16)
    cos128, sin128 = _rope_tables(n_ctx, T)
    row = jnp.arange(LANE)[:, None]
    cost128 = cos128.T
    sint128 = jnp.where(row < HEAD + A_ROPE // 2, -sin128.T, sin128.T)
    cs = _fft_chan_mats()
    attn_sub = 3 if (S // TOK_TILE) % 3 == 0 else 1
    attn_qtiles = 4 if T % (4 * TOK_TILE) == 0 else 1
    attn_q_per_step = next(n for n in (4, 2, 1) if T % (n * attn_qtiles * TOK_TILE) == 0)
    wg_e, wu_e, wd_e = e_w_gate, e_w_up, e_w_down

    rows16 = 16
    cvec = jnp.zeros((rows16, D), F32).at[:B].set(c).at[B].set(c_ctx)
    mods = _ada_mods(cvec, ada_w, ada_b).reshape(L, rows16, 1, 6 * D)

    xs = jnp.concatenate([ctx, x], axis=1)
    for l in range(L):
        mods_l = mods[l]
        f, qk, vm, og, gat, gatt, qt_ctx, qt_lat, k, vt4 = _inproj(
            xs, mods_l, norm1_g[l][None], w1[l], wgt[l], a_qnorm_g[l][None], a_kvnorm_g[l][None],
            wqt[l], wk[l], wvt[l], cos128, sin128, cost128, sint128,
            n_ctx_tiles=n_ctx_tiles)
        ya_ctx = _fft_ctx(f, n_ctx, cs)
        ya_lat = _fft_latent(f[:, n_ctx:], cs)
        qm, ktm, colp, rowp = _mlstm_prep(qk, gat, gatt, conv_w[l], conv_b[l], gate_bias[l][None, :],
                                          gate_bias[l][:, None], n_ctx_chunks=n_ctx_chunks)
        hf, hb = _mlstm_scan(qm, ktm, vm, colp, rowp, n_ctx_chunks=n_ctx_chunks)
        yc_ctx = _attention(qt_ctx, k, vt4, n_keys=n_ctx, qtiles=1, sub=1)
        yc_lat = _attention(qt_lat, k, vt4, n_keys=S, qtiles=attn_qtiles, sub=attn_sub,
                            q_per_step=attn_q_per_step)
        xs, hx, aff = _outproj(ya_ctx, ya_lat, hf, hb, og, yc_ctx, yc_lat, xs, mods_l, wa[l], wb[l], wc[l],
                               mg[l], blk, norm2_g[l][None], router_w[l], n_ctx_tiles=n_ctx_tiles)
        posm, bases = _route(jnp.swapaxes(aff, 1, 2), n_ctx=n_ctx, cap_ctx=cap_ctx, cap_lat=cap_lat)
        xe = _gather(hx, posm.reshape(B, N_EXPERTS, S // CHUNK, CHUNK), bases, rows=slot_rows)
        ye = _ffn(xe, wg_e[l], wu_e[l], wd_e[l])
        xs = _combine(xs, ye, jnp.swapaxes(posm, 1, 2), aff, mods_l, bases, n_ctx=n_ctx)
    return _final_norm(xs, final_g[None], n_ctx=n_ctx)
```

```python
import functools
import math

import jax
import jax.numpy as jnp
from jax import lax
from jax.experimental import pallas as pl
from jax.experimental.pallas import tpu as pltpu

F32 = jnp.float32
BF16 = jnp.bfloat16
I32 = jnp.int32

EPS = 1e-6
GRID_W = 64
ROPE_BASE = 10000.0
LANE = 128
HEAD = 64
M_HEADS = 4
M_WIDTH = M_HEADS * HEAD
A_HEADS = 8
A_ROPE = 32
V_ROWS = 80
N_EXPERTS = 16
EC_FACTOR = 2
TOK_TILE = 256
CHUNK = 128
GATHER_ALIGN_BF16 = 16
SMALL_FILL = 32
COMBINE_GROUP = 4
VMEM_LIMIT = 56 * 1024 * 1024


def _cparams(n_axes, vmem=None):
    return pltpu.CompilerParams(dimension_semantics=("arbitrary",) * n_axes,
                                vmem_limit_bytes=vmem)


def _dot(a, b):
    return jnp.dot(a, b, preferred_element_type=F32)


def _dot_nt(a, b):
    return lax.dot_general(a, b, (((1,), (1,)), ((), ())), preferred_element_type=F32)


def _split_bf16(a):
    hi = a.astype(BF16)
    lo = (a - hi.astype(F32)).astype(BF16)
    return hi, lo


def _sigmoid(x):
    return 1.0 / (1.0 + jnp.exp(-x))


def _rms(x, g):
    return x * lax.rsqrt(jnp.mean(x * x, axis=-1, keepdims=True) + EPS) * g


def _ada_kernel(c_ref, w_ref, b_ref, o_ref):
    a = c_ref[...]
    a = a * _sigmoid(a)
    a_hi, a_lo = _split_bf16(a)
    w_hi, w_lo = _split_bf16(w_ref[0])
    o_ref[0] = _dot(a_hi, w_hi) + _dot(a_lo, w_hi) + _dot(a_hi, w_lo) + b_ref[0]


def _ada_mods(cvec, ada_w, ada_b):
    L, D, D6 = ada_w.shape
    rows = cvec.shape[0]
    return pl.pallas_call(
        _ada_kernel,
        grid=(L, D6 // D),
        in_specs=[pl.BlockSpec((rows, D), lambda l, j: (0, 0)),
                  pl.BlockSpec((1, D, D), lambda l, j: (l, 0, j)),
                  pl.BlockSpec((1, 1, D), lambda l, j: (l, 0, j))],
        out_specs=pl.BlockSpec((1, rows, D), lambda l, j: (l, 0, j)),
        out_shape=jax.ShapeDtypeStruct((L, rows, D6), F32),
        compiler_params=_cparams(2),
        name="ada_mods",
    )(cvec, ada_w, ada_b.reshape(L, 1, D6))


_C_F = (0, 256)
_C_Q = (256, 512)
_C_K = (512, 768)
_C_V = (768, 1024)
_C_O = (1024, 1280)
_C_CQ = (1280, 1664)
_C_CKV = (1664, 1920)
_C_KR = (1920, 2048)
_C_KRS = (2048, 2176)
_C_G = (2176, 2304)
_N_COLS = 2304


def _inproj_kernel(x_ref, mod_ref, n1_ref, w1_ref, wgt_ref, qn_ref, kvn_ref, wqt_ref,
                   wk_ref, wvt_ref, cos_ref, sin_ref, cost_ref, sint_ref,
                   f_ref, qk_ref, vm_ref, o_ref, gat_ref, gatt_ref, qtc_ref, qtl_ref, k_ref, vt_ref, *,
                   d_model, q_scale, n_ctx_tiles):
    D = d_model
    i = pl.program_id(1)
    x = x_ref[0]
    mod = mod_ref[0]
    sh1, sc1 = mod[:, 0:D], mod[:, D:2 * D]
    xm = _rms(x, n1_ref[...]) * (1.0 + sc1) + sh1
    xb = xm.astype(BF16)
    u = _dot(xb, w1_ref[...])
    f_ref[0] = u[:, _C_F[0]:_C_F[1]].astype(BF16)
    qk_ref[0] = u[:, _C_Q[0]:_C_K[1]].astype(BF16)
    vm_ref[0] = u[:, _C_V[0]:_C_V[1]].astype(BF16)
    o_ref[0] = u[:, _C_O[0]:_C_O[1]].astype(BF16)
    gat_ref[0] = u[:, _C_G[0]:_C_G[0] + 16]
    gatt_ref[0] = _dot_nt(wgt_ref[...], xb)

    cqn = _rms(u[:, _C_CQ[0]:_C_CQ[1]], qn_ref[...]).astype(BF16)
    cost = jnp.tile(cost_ref[...], (A_HEADS, 1))
    sint = jnp.tile(sint_ref[...], (A_HEADS, 1))
    qa = _dot_nt(wqt_ref[...], cqn)
    half = A_ROPE // 2
    parts = []
    for h in range(A_HEADS):
        r0 = h * LANE
        parts += [qa[r0:r0 + HEAD], qa[r0 + HEAD + half:r0 + HEAD + A_ROPE],
                  qa[r0 + HEAD:r0 + HEAD + half], qa[r0 + HEAD + A_ROPE:r0 + LANE]]
    qt = ((qa * cost + jnp.concatenate(parts, axis=0) * sint) * q_scale).astype(BF16)

    @pl.when(i < n_ctx_tiles)
    def _():
        qtc_ref[0, 0] = qt

    @pl.when(i >= n_ctx_tiles)
    def _():
        qtl_ref[0, 0] = qt

    ckvn = _rms(u[:, _C_CKV[0]:_C_CKV[1]], kvn_ref[...]).astype(BF16)
    cosw = jnp.tile(cos_ref[...], (1, A_HEADS))
    sinw = jnp.tile(sin_ref[...], (1, A_HEADS))
    kw = A_HEADS * LANE
    kr = jnp.tile(u[:, _C_KR[0]:_C_KR[1]], (1, A_HEADS))
    krs = jnp.tile(u[:, _C_KRS[0]:_C_KRS[1]], (1, A_HEADS))
    k = _dot(ckvn, wk_ref[...]) + kr * cosw + krs * sinw
    k_ref[0] = k.astype(BF16)
    vrow = lax.broadcasted_iota(I32, (A_HEADS * V_ROWS, 1), 0)
    ones_a = jnp.where(vrow % V_ROWS == HEAD, 1.0, 0.0)
    vt_ref[0, 0] = (_dot_nt(wvt_ref[...], ckvn) + ones_a).astype(BF16)


def _inproj(xs, mods_l, n1, w1, wgt, qn, kvn, wqt, wk, wvt, cos128, sin128, cost128, sint128,
            *, n_ctx_tiles):
    B, S, D = xs.shape
    nt = S // TOK_TILE
    n_ctx = n_ctx_tiles * TOK_TILE
    n_lat_tiles = nt - n_ctx_tiles
    kw = A_HEADS * LANE
    tok = lambda w: pl.BlockSpec((1, TOK_TILE, w), lambda b, i: (b, i, 0))
    full = lambda a: pl.BlockSpec(a.shape, lambda b, i: (0,) * a.ndim)
    mod_spec = pl.BlockSpec((1, 1, 6 * D), lambda b, i: (jnp.where(i < n_ctx_tiles, B, b), 0, 0))
    tab_spec = pl.BlockSpec((TOK_TILE, LANE), lambda b, i: (i, 0))
    tabt_spec = pl.BlockSpec((LANE, TOK_TILE), lambda b, i: (0, i))
    sd = lambda w, dt: jax.ShapeDtypeStruct((B, S, w), dt)
    kern = functools.partial(_inproj_kernel, d_model=D, n_ctx_tiles=n_ctx_tiles,
                             q_scale=(HEAD + A_ROPE) ** -0.5 * math.log2(math.e))
    qtc_spec = pl.BlockSpec((1, 1, kw, TOK_TILE), lambda b, i: (b, jnp.minimum(i, n_ctx_tiles - 1), 0, 0))
    qtl_spec = pl.BlockSpec((1, 1, kw, TOK_TILE),
                            lambda b, i: (b, jnp.clip(i - n_ctx_tiles, 0, n_lat_tiles - 1), 0, 0))
    return pl.pallas_call(
        kern,
        grid=(B, nt),
        in_specs=[tok(D), mod_spec, full(n1), full(w1), full(wgt), full(qn), full(kvn), full(wqt),
                  full(wk), full(wvt), tab_spec, tab_spec, tabt_spec, tabt_spec],
        out_specs=[tok(256), tok(2 * M_WIDTH), tok(M_WIDTH), tok(M_WIDTH), tok(16),
                   pl.BlockSpec((1, 16, TOK_TILE), lambda b, i: (b, 0, i)),
                   qtc_spec, qtl_spec, tok(kw),
                   pl.BlockSpec((1, 1, A_HEADS * V_ROWS, TOK_TILE), lambda b, i: (b, i, 0, 0))],
        out_shape=[sd(256, BF16), sd(2 * M_WIDTH, BF16), sd(M_WIDTH, BF16), sd(M_WIDTH, BF16), sd(16, F32),
                   jax.ShapeDtypeStruct((B, 16, S), F32),
                   jax.ShapeDtypeStruct((B, n_ctx_tiles, kw, TOK_TILE), BF16),
                   jax.ShapeDtypeStruct((B, n_lat_tiles, kw, TOK_TILE), BF16),
                   sd(kw, BF16),
                   jax.ShapeDtypeStruct((B, nt, A_HEADS * V_ROWS, TOK_TILE), BF16)],
        compiler_params=_cparams(2, VMEM_LIMIT),
        name="inproj",
    )(xs, mods_l, n1, w1, wgt, qn, kvn, wqt, wk, wvt, cos128, sin128, cost128, sint128)


def _dft_mats(n):
    idx = jnp.arange(n, dtype=F32)
    ang = 2.0 * math.pi * jnp.mod(idx[:, None] * idx[None, :], n) / n
    return jnp.cos(ang), jnp.sin(ang)


def _fft_chan_mats():
    c, s = _dft_mats(HEAD)
    eye = jnp.eye(4, dtype=F32)
    return jnp.concatenate([jnp.kron(eye, c), jnp.kron(eye, s)], 0).astype(BF16)


def _fft_ctx_kernel(f_ref, ft_ref, cs_ref, o_ref, *, n, scale):
    xst = _dot(ft_ref[...], f_ref[0])
    xr = xst[:n].astype(BF16)
    xi = xst[n:].astype(BF16)
    y = _dot(xr, cs_ref[0:256, :]) + _dot(xi, cs_ref[256:512, :])
    o_ref[0] = (y * scale).astype(BF16)


def _fft_ctx(f, n_ctx, cs):
    B = f.shape[0]
    c, s = _dft_mats(n_ctx)
    ft = jnp.concatenate([c, -s], 0).astype(BF16)
    kern = functools.partial(_fft_ctx_kernel, n=n_ctx, scale=(n_ctx * HEAD) ** -0.5)
    return pl.pallas_call(
        kern,
        grid=(B,),
        in_specs=[pl.BlockSpec((1, n_ctx, 256), lambda b: (b, 0, 0)),
                  pl.BlockSpec(ft.shape, lambda b: (0, 0)),
                  pl.BlockSpec(cs.shape, lambda b: (0, 0))],
        out_specs=pl.BlockSpec((1, n_ctx, 256), lambda b: (b, 0, 0)),
        out_shape=jax.ShapeDtypeStruct((B, n_ctx, 256), BF16),
        compiler_params=_cparams(1),
        name="fft_ctx",
    )(f, ft, cs)


def _fft_stage1_kernel(x_ref, f1_ref, cw_ref, sw_ref, o_ref, *, n1):
    z = _dot(f1_ref[...], x_ref[0])
    zr, zi = z[:n1], z[n1:]
    cw, sw = cw_ref[...], sw_ref[...]
    o_ref[0, 0] = (zr * cw + zi * sw).astype(BF16)
    o_ref[0, 1] = (zi * cw - zr * sw).astype(BF16)


def _fft_stage2_kernel(z_ref, f2_ref, cs_ref, o_ref, *, tb, scale):
    for j in range(tb):
        zcat = jnp.concatenate([z_ref[0, 0, j], z_ref[0, 1, j]], axis=0)
        xst = _dot(f2_ref[...], zcat)
        xr = xst[:HEAD].astype(BF16)
        xi = xst[HEAD:].astype(BF16)
        y = _dot(xr, cs_ref[0:256, :]) + _dot(xi, cs_ref[256:512, :])
        o_ref[0, j] = (y * scale).astype(BF16)


def _fft_latent(f_lat, cs):
    B, T, W = f_lat.shape
    n2 = HEAD
    n1 = T // n2
    cols = n2 * W
    cb = 2048
    c1, s1 = _dft_mats(n1)
    f1 = jnp.concatenate([c1, -s1], 0).astype(BF16)
    t1 = jnp.arange(n1, dtype=F32)[:, None]
    s2 = jnp.arange(n2, dtype=F32)[None, :]
    ang = 2.0 * math.pi * (t1 * s2) / T
    cw = jnp.repeat(jnp.cos(ang), W, axis=1)
    sw = jnp.repeat(jnp.sin(ang), W, axis=1)
    z = pl.pallas_call(
        functools.partial(_fft_stage1_kernel, n1=n1),
        grid=(B, cols // cb),
        in_specs=[pl.BlockSpec((1, n1, cb), lambda b, j: (b, 0, j)),
                  pl.BlockSpec(f1.shape, lambda b, j: (0, 0)),
                  pl.BlockSpec((n1, cb), lambda b, j: (0, j)),
                  pl.BlockSpec((n1, cb), lambda b, j: (0, j))],
        out_specs=pl.BlockSpec((1, 2, n1, cb), lambda b, j: (b, 0, 0, j)),
        out_shape=jax.ShapeDtypeStruct((B, 2, n1, cols), BF16),
        compiler_params=_cparams(2),
        name="fft_stage1",
    )(f_lat.reshape(B, n1, cols), f1, cw, sw)
    z = z.reshape(B, 2, n1, n2, W)
    c2, s2m = _dft_mats(n2)
    f2 = jnp.concatenate([jnp.concatenate([c2, s2m], 1),
                          jnp.concatenate([-s2m, c2], 1)], 0).astype(BF16)
    tb = 8
    y = pl.pallas_call(
        functools.partial(_fft_stage2_kernel, tb=tb, scale=(T * HEAD) ** -0.5),
        grid=(B, n1 // tb),
        in_specs=[pl.BlockSpec((1, 2, tb, n2, W), lambda b, j: (b, 0, j, 0, 0)),
                  pl.BlockSpec(f2.shape, lambda b, j: (0, 0)),
                  pl.BlockSpec(cs.shape, lambda b, j: (0, 0))],
        out_specs=pl.BlockSpec((1, tb, n2, W), lambda b, j: (b, j, 0, 0)),
        out_shape=jax.ShapeDtypeStruct((B, n1, n2, W), BF16),
        compiler_params=_cparams(2),
        name="fft_stage2",
    )(z, f2, cs)
    return jnp.transpose(y, (0, 2, 1, 3)).reshape(B, T, W)


def _mlstm_prep_kernel(cur_ref, prev_ref, next_ref, cw_ref, cb_ref, gat_ref, gatt_ref, br_ref, bc_ref,
                       q_ref, kt_ref, col_ref, row_ref, *, n_chunks, n_ctx_chunks):
    c = pl.program_id(1)
    cur = cur_ref[0].astype(F32)
    first = jnp.logical_or(c == 0, c == n_ctx_chunks)
    last = jnp.logical_or(c == n_ctx_chunks - 1, c == n_chunks - 1)
    prev_row = prev_ref[0].astype(F32)[15:16, :]
    next_row = next_ref[0].astype(F32)[0:1, :]
    prev_row = jnp.where(first, 0.0, prev_row)
    next_row = jnp.where(last, 0.0, next_row)
    rows = lax.broadcasted_iota(I32, (CHUNK, 1), 0)
    up = jnp.where(rows == 0, prev_row, pltpu.roll(cur, 1, axis=0))
    dn = jnp.where(rows == CHUNK - 1, next_row, pltpu.roll(cur, CHUNK - 1, axis=0))
    y = cw_ref[0:1, :] * up + cw_ref[1:2, :] * cur + cw_ref[2:3, :] * dn + cb_ref[...]
    y = y * _sigmoid(y)
    q_ref[0] = (y[:, :M_WIDTH] * HEAD ** -0.5).astype(BF16)
    kt_ref[0] = y[:, M_WIDTH:].T.astype(BF16)

    def logsig(v):
        return jnp.minimum(v, 0.0) - jnp.log(1.0 + jnp.exp(-jnp.abs(v)))

    r_i = lax.broadcasted_iota(I32, (CHUNK, CHUNK), 0)
    c_i = lax.broadcasted_iota(I32, (CHUNK, CHUNK), 1)
    lower = jnp.where(c_i <= r_i, 1.0, 0.0).astype(BF16)
    upper = jnp.where(c_i >= r_i, 1.0, 0.0).astype(BF16)

    g = gat_ref[0] + br_ref[...]
    lane = lax.broadcasted_iota(I32, (1, 16), 1)
    lg = jnp.where(lane % 8 >= 4, logsig(g), g)
    hi, lo = _split_bf16(lg)
    cum_f = _dot(lower, hi) + _dot(lower, lo)
    cum_b = _dot(upper, hi) + _dot(upper, lo)
    cum_col = jnp.where(lane < 8, cum_f, cum_b)

    gt = gatt_ref[0] + bc_ref[...]
    row = lax.broadcasted_iota(I32, (16, 1), 0)
    lgt = jnp.where(row % 8 >= 4, logsig(gt), gt)
    hit, lot = _split_bf16(lgt)
    cum_ft = _dot(hit, upper) + _dot(lot, upper)
    cum_bt = _dot(hit, lower) + _dot(lot, lower)
    cumt = jnp.where(row < 8, cum_ft, cum_bt)
    cumt_i = pltpu.roll(cumt, 12, axis=0)
    rterm = lgt - cumt_i
    btot = jnp.broadcast_to(jnp.sum(lgt, axis=1, keepdims=True), (16, CHUNK))
    row_ref[0, 0:16] = jnp.where(row % 8 < 4, rterm, btot)
    dmax = jnp.broadcast_to(jnp.max(rterm, axis=1, keepdims=True), (16, CHUNK))
    row_ref[0, 16:32] = pltpu.roll(btot, 12, axis=0) + dmax
    seen_max = jnp.zeros((CHUNK, 16), F32)
    for d in range(2):
        seen = (c_i <= r_i) if d == 0 else (c_i >= r_i)
        for h in range(M_HEADS):
            idx = 8 * d + h
            cmx = jnp.max(jnp.where(seen, rterm[idx:idx + 1, :], -jnp.inf), axis=1, keepdims=True)
            seen_max = jnp.where(lane == idx, cmx, seen_max)
    col_ref[0] = jnp.where(lane % 8 >= 4, cum_col, seen_max)


def _mlstm_prep(qk, gat, gatt, conv_w, conv_b, bias_row, bias_col, *, n_ctx_chunks):
    B, S, W = qk.shape
    nc = S // CHUNK
    n16 = S // 16
    hw = M_WIDTH
    kern = functools.partial(_mlstm_prep_kernel, n_chunks=nc, n_ctx_chunks=n_ctx_chunks)
    full = lambda a: pl.BlockSpec(a.shape, lambda b, c: (0,) * a.ndim)
    return pl.pallas_call(
        kern,
        grid=(B, nc),
        in_specs=[pl.BlockSpec((1, CHUNK, W), lambda b, c: (b, c, 0)),
                  pl.BlockSpec((1, 16, W), lambda b, c: (b, jnp.maximum(c * 8 - 1, 0), 0)),
                  pl.BlockSpec((1, 16, W), lambda b, c: (b, jnp.minimum((c + 1) * 8, n16 - 1), 0)),
                  full(conv_w), full(conv_b),
                  pl.BlockSpec((1, CHUNK, 16), lambda b, c: (b, c, 0)),
                  pl.BlockSpec((1, 16, CHUNK), lambda b, c: (b, 0, c)),
                  full(bias_row), full(bias_col)],
        out_specs=[pl.BlockSpec((1, CHUNK, hw), lambda b, c: (b, c, 0)),
                   pl.BlockSpec((1, hw, CHUNK), lambda b, c: (b, 0, c)),
                   pl.BlockSpec((1, CHUNK, 16), lambda b, c: (b, c, 0)),
                   pl.BlockSpec((1, 32, CHUNK), lambda b, c: (b, 0, c))],
        out_shape=[jax.ShapeDtypeStruct((B, S, hw), BF16),
                   jax.ShapeDtypeStruct((B, hw, S), BF16),
                   jax.ShapeDtypeStruct((B, S, 16), F32),
                   jax.ShapeDtypeStruct((B, 32, S), F32)],
        compiler_params=_cparams(2),
        name="mlstm_prep",
    )(qk, qk, qk, conv_w, conv_b, gat, gatt, bias_row, bias_col)


def _mlstm_scan_kernel(*refs, n_batch):
    ins, (of_ref, ob_ref, c_scr, m_scr) = refs[:10], refs[10:]
    j = pl.program_id(1)

    @pl.when(j == 0)
    def _():
        c_scr[...] = jnp.zeros_like(c_scr)
        m_scr[...] = jnp.zeros_like(m_scr)

    t_i = lax.broadcasted_iota(I32, (CHUNK, CHUNK), 0)
    s_i = lax.broadcasted_iota(I32, (CHUNK, CHUNK), 1)
    lane_i = lax.broadcasted_iota(I32, (1, LANE), 1)
    row_i = lax.broadcasted_iota(I32, (LANE, 1), 0)
    for bb in range(n_batch):
        for d, o_ref in enumerate((of_ref, ob_ref)):
            q_ref, kt_ref, v_ref, col_ref, row_ref = ins[5 * d:5 * d + 5]
            colp = col_ref[bb]
            rowp = row_ref[bb]
            mask = (s_i <= t_i) if d == 0 else (s_i >= t_i)
            outs = []
            for pair_i in range(M_HEADS // 2):
                lanes = slice(pair_i * LANE, (pair_i + 1) * LANE)
                q_blk = q_ref[bb, :, lanes]
                kt_blk = kt_ref[bb, lanes, :]
                v_blk = v_ref[bb, :, lanes]
                tots = []
                for p in range(2):
                    h = 2 * pair_i + p
                    st = (bb * 2 + d) * M_HEADS + h
                    own = (lane_i < HEAD) if p == 0 else (lane_i >= HEAD)
                    own_r = (row_i < HEAD) if p == 0 else (row_i >= HEAD)
                    one_lane = HEAD if p == 0 else 0
                    full = (CHUNK, CHUNK)
                    bcol = jnp.broadcast_to(colp[:, 8 * d + 4 + h:8 * d + 5 + h], full)
                    rmax = bcol + jnp.broadcast_to(colp[:, 8 * d + h:8 * d + h + 1], full)
                    rrow = rowp[8 * d + h:8 * d + h + 1, :]
                    btot = rowp[8 * d + 4 + h:8 * d + 5 + h, :]
                    dmax = rowp[16 + 8 * d + h:17 + 8 * d + h, :]
                    m = m_scr[st, 0:1, :]
                    inter = bcol + m
                    m_t = jnp.maximum(inter, rmax)
                    w = jnp.exp(jnp.where(mask, bcol + rrow, -jnp.inf) - m_t)
                    a = jnp.exp(inter - m_t)
                    qh = jnp.where(own, q_blk, 0)
                    vh = jnp.where(own, v_blk, jnp.where(lane_i == one_lane, 1.0, 0.0).astype(BF16))
                    s = _dot(qh, kt_blk) * w
                    cst = c_scr[st]
                    tot = a * _dot(qh, cst.astype(BF16)) + _dot(s.astype(BF16), vh)
                    den = jnp.broadcast_to(tot[:, one_lane:one_lane + 1], full)
                    tots.append(tot / jnp.maximum(jnp.abs(den), jnp.exp(-m_t)))
                    bm = btot + m
                    m_new = jnp.maximum(bm, dmax)
                    ws = jnp.exp(btot + rrow - m_new)
                    kw = (jnp.where(own_r, kt_blk, 0).astype(F32) * ws).astype(BF16)
                    c_scr[st] = jnp.exp(bm - m_new) * cst + _dot(kw, vh)
                    m_scr[st] = jnp.broadcast_to(m_new, (8, LANE))
                outs.append(jnp.where(lane_i < HEAD, tots[0], tots[1]))
            o_ref[bb] = jnp.concatenate(outs, axis=1)


def _mlstm_scan(q, kt, v, colp, rowp, *, n_ctx_chunks):
    B, S, hw = q.shape
    nc = S // CHUNK
    nb = next(n for n in (4, 2, 1) if B % n == 0)

    def rev(j):
        return jnp.where(j < n_ctx_chunks, n_ctx_chunks - 1 - j, nc + n_ctx_chunks - 1 - j)

    def specs(cid):
        return [pl.BlockSpec((nb, CHUNK, hw), lambda b, j: (b, cid(j), 0)),
                pl.BlockSpec((nb, hw, CHUNK), lambda b, j: (b, 0, cid(j))),
                pl.BlockSpec((nb, CHUNK, hw), lambda b, j: (b, cid(j), 0)),
                pl.BlockSpec((nb, CHUNK, 16), lambda b, j: (b, cid(j), 0)),
                pl.BlockSpec((nb, 32, CHUNK), lambda b, j: (b, 0, cid(j)))]

    fwd = lambda j: j
    n_chains = nb * 2 * M_HEADS
    return pl.pallas_call(
        functools.partial(_mlstm_scan_kernel, n_batch=nb),
        grid=(B // nb, nc),
        in_specs=specs(fwd) + specs(rev),
        out_specs=[pl.BlockSpec((nb, CHUNK, hw), lambda b, j: (b, j, 0)),
                   pl.BlockSpec((nb, CHUNK, hw), lambda b, j: (b, rev(j), 0))],
        out_shape=[jax.ShapeDtypeStruct((B, S, hw), F32), jax.ShapeDtypeStruct((B, S, hw), F32)],
        scratch_shapes=[pltpu.VMEM((n_chains, LANE, LANE), F32), pltpu.VMEM((n_chains, 8, LANE), F32)],
        compiler_params=_cparams(2),
        name="mlstm_scan",
    )(q, kt, v, colp, rowp, q, kt, v, colp, rowp)


def _attn_kernel(qt_ref, k_ref, vt_ref, o_ref, s0_scr, s1_scr, p0_scr, p1_scr, al_scr, cm_scr, acc_scr,
                 m_scr, *,
                 n_q, n_kc, sub, qtiles):
    tk = sub * TOK_TILE
    tq = qtiles * TOK_TILE
    n_items = n_q * n_kc
    heads = [slice(hh * LANE, (hh + 1) * LANE) for hh in range(2)]
    vheads = [slice(hh * V_ROWS, (hh + 1) * V_ROWS) for hh in range(2)]
    s_bufs = (s0_scr, s1_scr)
    p_bufs = (p0_scr, p1_scr)

    def split(n):
        if n_q == 1:
            return 0, n
        qi = n // n_kc
        return qi, n - qi * n_kc

    def scores(n, par):
        qi, c = split(n)
        for hh in range(2):
            kc = k_ref[0, pl.ds(pl.multiple_of(c * tk, tk), tk), heads[hh]]
            qt = jnp.concatenate([qt_ref[0, qi * qtiles + j, heads[hh], :] for j in range(qtiles)], axis=1)
            st = _dot(kc, qt)
            s_bufs[par][hh] = st
            cm_scr[par, hh] = jnp.max(st, axis=0, keepdims=True)

    def softmax(n, par):
        _, c = split(n)
        for hh in range(2):
            st = s_bufs[par][hh]
            m = jnp.where(c == 0, -jnp.inf, m_scr[hh])
            m_new = jnp.maximum(m, cm_scr[par, hh])
            p_bufs[par][hh] = jnp.exp2(st - m_new).astype(BF16)
            al_scr[par, hh] = jnp.exp2(m - m_new)
            m_scr[hh] = m_new

    def values(n, par):
        qi, c = split(n)
        for hh in range(2):
            pv = _dot(vt_ref[0, c * sub, vheads[hh], :], p_bufs[par][hh, 0:TOK_TILE])
            for j in range(1, sub):
                pv += _dot(vt_ref[0, c * sub + j, vheads[hh], :],
                           p_bufs[par][hh, j * TOK_TILE:(j + 1) * TOK_TILE])
            acc_scr[hh] = al_scr[par, hh] * acc_scr[hh] + pv

        def finalize():
            o = jnp.concatenate([acc_scr[hh, :HEAD] / acc_scr[hh, HEAD:HEAD + 1] for hh in range(2)], axis=0)
            o_ref[0, pl.ds(pl.multiple_of(qi * tq, tq), tq), :] = o.T.astype(BF16)

        if isinstance(n, int) and n % n_kc == n_kc - 1:
            finalize()

    m_scr[...] = jnp.zeros_like(m_scr)
    acc_scr[...] = jnp.zeros_like(acc_scr)
    scores(0, 0)
    if n_items > 1:
        scores(1, 1)
    softmax(0, 0)

    def step(t, par):
        scores(t + 2, par)
        softmax(t + 1, 1 - par)
        values(t, par)

    def run(t0, t1):
        if t0 < t1 and t0 % 2:
            step(t0, 1)
            t0 += 1
        if t1 - t0 >= 2:
            def pair(i, carry):
                step(t0 + 2 * i, 0)
                step(t0 + 2 * i + 1, 1)
                return carry
            lax.fori_loop(0, (t1 - t0) // 2, pair, 0)
        if (t1 - t0) % 2:
            step(t1 - 1, (t1 - 1) % 2)

    n_full = max(n_items - 2, 0)
    t = 0
    for last in range(n_kc - 1, n_full, n_kc):
        run(t, last)
        step(last, last % 2)
        t = last + 1
    run(t, n_full)
    if n_items > 1:
        softmax(n_items - 1, (n_items - 1) % 2)
        values(n_items - 2, (n_items - 2) % 2)
    values(n_items - 1, (n_items - 1) % 2)


def _attention(qt4, k, vt4, *, n_keys, qtiles, sub, q_per_step=1):
    B, n_qb, kw, _ = qt4.shape
    n_kb = n_keys // TOK_TILE
    tk = sub * TOK_TILE
    tq = qtiles * TOK_TILE
    qb_step = q_per_step * qtiles
    kern = functools.partial(_attn_kernel, n_q=q_per_step, n_kc=n_kb // sub, sub=sub, qtiles=qtiles)
    return pl.pallas_call(
        kern,
        grid=(B, A_HEADS // 2, n_qb // qb_step),
        in_specs=[pl.BlockSpec((1, qb_step, 2 * LANE, TOK_TILE), lambda b, hp, i: (b, i, hp, 0)),
                  pl.BlockSpec((1, n_keys, 2 * LANE), lambda b, hp, i: (b, 0, hp)),
                  pl.BlockSpec((1, n_kb, 2 * V_ROWS, TOK_TILE), lambda b, hp, i: (b, 0, hp, 0))],
        out_specs=pl.BlockSpec((1, qb_step * TOK_TILE, LANE), lambda b, hp, i: (b, i, hp)),
        out_shape=jax.ShapeDtypeStruct((B, n_qb * TOK_TILE, A_HEADS * HEAD), BF16),
        scratch_shapes=[pltpu.VMEM((2, tk, tq), F32), pltpu.VMEM((2, tk, tq), F32),
                        pltpu.VMEM((2, tk, tq), BF16), pltpu.VMEM((2, tk, tq), BF16),
                        pltpu.VMEM((2, 2, 1, tq), F32), pltpu.VMEM((2, 2, 1, tq), F32),
                        pltpu.VMEM((2, V_ROWS, tq), F32), pltpu.VMEM((2, 1, tq), F32)],
        compiler_params=_cparams(3, VMEM_LIMIT),
        name="attention",
    )(qt4, k, vt4)


def _outproj_kernel(yac_ref, yal_ref, hf_ref, hb_ref, o_ref, ycc_ref, ycl_ref, x_ref, mod_ref, wa_ref, wb_ref,
                    wc_ref, mg_ref, blk_ref, n2_ref, rw_ref, xo_ref, hx_ref, aff_ref, afft_ref, *, d_model,
                    n_ctx_tiles):
    D = d_model
    i = pl.program_id(1)
    mod = mod_ref[0]
    g1, sh2, sc2 = mod[:, 2 * D:3 * D], mod[:, 3 * D:4 * D], mod[:, 4 * D:5 * D]
    ya = jnp.where(i < n_ctx_tiles, yac_ref[0], yal_ref[0])
    h = hf_ref[0] + hb_ref[0]
    hi, lo = _split_bf16(h * h)
    ms = _dot(hi, blk_ref[...]) + _dot(lo, blk_ref[...])
    hn = h * lax.rsqrt(ms + EPS) * mg_ref[...]
    yb = (hn * _sigmoid(o_ref[0].astype(F32))).astype(BF16)
    yc = jnp.where(i < n_ctx_tiles, ycc_ref[0], ycl_ref[0])
    mix = _dot(ya, wa_ref[...]) + _dot(yb, wb_ref[...]) + _dot(yc, wc_ref[...])
    x = x_ref[0] + g1 * mix
    xo_ref[0] = x
    hx = _rms(x, n2_ref[...]) * (1.0 + sc2) + sh2
    hx_ref[0] = hx.astype(BF16)
    h_hi, h_lo = _split_bf16(hx)
    r_hi, r_lo = _split_bf16(rw_ref[...])
    lane = lax.broadcasted_iota(I32, (1, LANE), 1)
    logits = jnp.where(lane < N_EXPERTS, _dot(h_hi, r_hi) + _dot(h_lo, r_hi) + _dot(h_hi, r_lo), -jnp.inf)
    e = jnp.exp(logits - jnp.max(logits, axis=1, keepdims=True))
    aff = e / jnp.sum(e, axis=1, keepdims=True)
    aff_ref[0] = aff[:, :N_EXPERTS]
    afft_ref[0] = aff.T[:N_EXPERTS, :]


def _outproj(ya_ctx, ya_lat, hf, hb, o, yc_ctx, yc_lat, xs, mods_l, wa, wb, wc, mg, blk, n2, rw, *,
             n_ctx_tiles):
    B, S, D = xs.shape
    nt = S // TOK_TILE
    tok = lambda w: pl.BlockSpec((1, TOK_TILE, w), lambda b, i: (b, i, 0))
    full = lambda a: pl.BlockSpec(a.shape, lambda b, i: (0,) * a.ndim)
    mod_spec = pl.BlockSpec((1, 1, 6 * D), lambda b, i: (jnp.where(i < n_ctx_tiles, B, b), 0, 0))
    hw = M_WIDTH
    n_lat_tiles = nt - n_ctx_tiles
    ctx_tok = lambda w: pl.BlockSpec((1, TOK_TILE, w), lambda b, i: (b, jnp.minimum(i, n_ctx_tiles - 1), 0))
    lat_tok = lambda w: pl.BlockSpec(
        (1, TOK_TILE, w), lambda b, i: (b, jnp.clip(i - n_ctx_tiles, 0, n_lat_tiles - 1), 0))
    kern = functools.partial(_outproj_kernel, d_model=D, n_ctx_tiles=n_ctx_tiles)
    return pl.pallas_call(
        kern,
        grid=(B, nt),
        in_specs=[ctx_tok(256), lat_tok(256),
                  tok(hw), tok(hw),
                  tok(hw), ctx_tok(A_HEADS * HEAD), lat_tok(A_HEADS * HEAD), tok(D), mod_spec,
                  full(wa), full(wb), full(wc), full(mg), full(blk), full(n2), full(rw)],
        out_specs=[tok(D), tok(D), tok(N_EXPERTS),
                   pl.BlockSpec((1, N_EXPERTS, TOK_TILE), lambda b, i: (b, 0, i))],
        out_shape=[jax.ShapeDtypeStruct((B, S, D), F32), jax.ShapeDtypeStruct((B, S, D), BF16),
                   jax.ShapeDtypeStruct((B, S, N_EXPERTS), F32), jax.ShapeDtypeStruct((B, N_EXPERTS, S), F32)],
        compiler_params=_cparams(2, VMEM_LIMIT),
        name="outproj",
    )(ya_ctx, ya_lat, hf, hb, o, yc_ctx, yc_lat, xs, mods_l, wa, wb, wc, mg, blk, n2, rw)


def _route_part(a, cap, slot0, tri, pos_ref, posc_ref, base_ref, lane0):
    n = a.shape[1]
    capf = float(cap)
    bits = pltpu.bitcast(a, I32)
    v = jnp.zeros((N_EXPERTS, 1), I32)
    for bit in range(30, -1, -1):
        cand = v | (1 << bit)
        cnt = jnp.sum(jnp.where(bits >= cand, 1.0, 0.0), axis=1, keepdims=True)
        v = jnp.where(cnt >= capf, cand, v)
    gt = bits > v
    eq = bits == v
    need = capf - jnp.sum(jnp.where(gt, 1.0, 0.0), axis=1, keepdims=True)
    idx = lax.broadcasted_iota(I32, (N_EXPERTS, n), 1)
    x = jnp.zeros((N_EXPERTS, 1), I32)
    for bit in range(max(n - 1, 1).bit_length() - 1, -1, -1):
        cand = x | (1 << bit)
        cnt = jnp.sum(jnp.where(eq, jnp.where(idx < cand, 1.0, 0.0), 0.0), axis=1, keepdims=True)
        x = jnp.where(cnt < need, cand, x)
    sel = jnp.where(gt, 1.0, jnp.where(eq, jnp.where(idx <= x, 1.0, 0.0), 0.0))
    running = jnp.zeros((N_EXPERTS, 1), F32)
    fill = jnp.zeros((N_EXPERTS, 1), F32)
    for c in range(n // CHUNK):
        blk = sel[:, c * CHUNK:(c + 1) * CHUNK]
        incl = _dot(blk.astype(BF16), tri)
        pos = running + incl - blk + float(slot0)
        cg = lane0 // CHUNK + c
        slot_id = jnp.where(blk > 0.5, pos, -1.0)
        pos_ref[0, :, lane0 + c * CHUNK:lane0 + (c + 1) * CHUNK] = slot_id.astype(I32)
        padded = jnp.concatenate([slot_id, jnp.zeros((CHUNK - N_EXPERTS, CHUNK), F32)], axis=0)
        posc_ref[0, lane0 + c * CHUNK:lane0 + (c + 1) * CHUNK, :] = padded.T[:, :N_EXPERTS].astype(I32)
        base_ref[0, :, cg:cg + 1] = (running + float(slot0)).astype(I32)
        running = running + incl[:, CHUNK - 1:CHUNK]
        fill = jnp.maximum(fill, incl[:, CHUNK - 1:CHUNK])
    return fill


def _route_kernel(aff_ref, pos_ref, posc_ref, base_ref, *, n_ctx, cap_ctx, cap_lat):
    r_i = lax.broadcasted_iota(I32, (CHUNK, CHUNK), 0)
    c_i = lax.broadcasted_iota(I32, (CHUNK, CHUNK), 1)
    tri = jnp.where(r_i <= c_i, 1.0, 0.0).astype(BF16)
    base_ref[...] = jnp.zeros_like(base_ref)
    a = aff_ref[0]
    fill_c = _route_part(a[:, :n_ctx], cap_ctx, 0, tri, pos_ref, posc_ref, base_ref, 0)
    fill_l = _route_part(a[:, n_ctx:], cap_lat, cap_ctx, tri, pos_ref, posc_ref, base_ref, n_ctx)
    base_ref[0, :, LANE - 1:LANE] = jnp.maximum(fill_c, fill_l).astype(I32)


def _route(aff_t, *, n_ctx, cap_ctx, cap_lat):
    B, E, S = aff_t.shape
    assert S // CHUNK < LANE - 1
    kern = functools.partial(_route_kernel, n_ctx=n_ctx, cap_ctx=cap_ctx, cap_lat=cap_lat)
    return pl.pallas_call(
        kern,
        grid=(B,),
        in_specs=[pl.BlockSpec((1, E, S), lambda b: (b, 0, 0))],
        out_specs=[pl.BlockSpec((1, E, S), lambda b: (b, 0, 0)),
                   pl.BlockSpec((1, S, E), lambda b: (b, 0, 0)),
                   pl.BlockSpec((1, E, LANE), lambda b: (b, 0, 0))],
        out_shape=[jax.ShapeDtypeStruct((B, E, S), I32), jax.ShapeDtypeStruct((B, S, E), I32),
                   jax.ShapeDtypeStruct((B, E, LANE), I32)],
        compiler_params=_cparams(1),
        name="route",
    )(aff_t)


def _window_start(base, align, rows, win):
    w0 = lax.shift_left(lax.shift_right_logical(base, int(math.log2(align))), int(math.log2(align)))
    return pl.multiple_of(jnp.minimum(w0, rows - win), align)


def _gather_kernel(base_ref, h_ref, pos_ref, xe_ref, *, n_chunks, rows, win, unroll):
    b = pl.program_id(0)
    e = pl.program_id(2)
    xe_ref[0, 0] = jnp.zeros(xe_ref.shape[2:], BF16)

    def gather_all(win_rows):
        def group(g, carry):
            picked = []
            for u in range(unroll):
                c = g * unroll + u
                w0 = _window_start(base_ref[b, e, c], GATHER_ALIGN_BF16, rows, win_rows)
                posr = pos_ref[0, 0, pl.ds(c, 1), :]
                slot = lax.broadcasted_iota(I32, (win_rows, CHUNK), 0) + w0
                onehot = jnp.where(posr == slot, 1.0, 0.0).astype(BF16)
                hc = h_ref[0, pl.ds(pl.multiple_of(c * CHUNK, CHUNK), CHUNK), :]
                picked.append((w0, _dot(onehot, hc).astype(BF16)))
            for w0, rows_c in picked:
                xe_ref[0, 0, pl.ds(w0, win_rows), :] += rows_c
            return carry

        lax.fori_loop(0, n_chunks // unroll, group, 0)

    most = base_ref[b, e, LANE - 1]
    small = min(SMALL_FILL + GATHER_ALIGN_BF16, rows)
    if small < win:
        pl.when(most <= SMALL_FILL)(functools.partial(gather_all, small))
        pl.when(most > SMALL_FILL)(functools.partial(gather_all, win))
    else:
        gather_all(win)


def _gather(hx, pos4, bases, *, rows):
    B, S, D = hx.shape
    nc = S // CHUNK
    dh = D // 2
    win = min(CHUNK + GATHER_ALIGN_BF16, rows)
    unroll = next(u for u in (11, 8, 6, 4, 3, 2, 1) if nc % u == 0)
    kern = functools.partial(_gather_kernel, n_chunks=nc, rows=rows, win=win, unroll=unroll)
    return pl.pallas_call(
        kern,
        grid_spec=pltpu.PrefetchScalarGridSpec(
            num_scalar_prefetch=1,
            grid=(B, 2, N_EXPERTS),
            in_specs=[pl.BlockSpec((1, S, dh), lambda b, j, e, bs: (b, 0, j)),
                      pl.BlockSpec((1, 1, nc, CHUNK), lambda b, j, e, bs: (b, e, 0, 0))],
            out_specs=pl.BlockSpec((1, 1, rows, dh), lambda b, j, e, bs: (b, e, 0, j))),
        out_shape=jax.ShapeDtypeStruct((B, N_EXPERTS, rows, D), BF16),
        compiler_params=_cparams(3, VMEM_LIMIT),
        name="moe_gather",
    )(bases, hx, pos4)


def _ffn_kernel(x_ref, wg_ref, wu_ref, wd_ref, y_ref, wg_s, wu_s, wd_s, *, row_tile, n_tiles):
    @pl.when(pl.program_id(1) == 0)
    def _():
        wg_s[...] = wg_ref[0, 0].astype(BF16)
        wu_s[...] = wu_ref[0, 0].astype(BF16)
        wd_s[...] = wd_ref[0, 0].astype(BF16)

    def tile(i, carry):
        r0 = pl.multiple_of(i * row_tile, 16)
        x = x_ref[0, 0, pl.ds(r0, row_tile), :]
        g = _dot(x, wg_s[...])
        u = _dot(x, wu_s[...])
        hid = (g * _sigmoid(g) * u).astype(BF16)
        y_ref[0, 0, pl.ds(r0, row_tile), :] = _dot(hid, wd_s[...]).astype(BF16)
        return carry

    lax.fori_loop(0, n_tiles, tile, 0)


def _ffn_row_tile(rows):
    for t in (352, 256, 176, 128, 96, 64, 32, 16):
        if rows % t == 0:
            return t
    raise ValueError(f"expert slot rows {rows} must be a multiple of 16")


def _ffn(xe, wg, wu, wd, *, layer):
    B, E, R, D = xe.shape
    F = wg.shape[-1]
    rt = _ffn_row_tile(R)
    kern = functools.partial(_ffn_kernel, row_tile=rt, n_tiles=R // rt)
    return pl.pallas_call(
        kern,
        grid=(E, B),
        in_specs=[pl.BlockSpec((1, 1, R, D), lambda e, b: (b, e, 0, 0)),
                  pl.BlockSpec((1, 1, D, F), lambda e, b: (layer, e, 0, 0)),
                  pl.BlockSpec((1, 1, D, F), lambda e, b: (layer, e, 0, 0)),
                  pl.BlockSpec((1, 1, F, D), lambda e, b: (layer, e, 0, 0))],
        out_specs=pl.BlockSpec((1, 1, R, D), lambda e, b: (b, e, 0, 0)),
        out_shape=jax.ShapeDtypeStruct((B, E, R, D), BF16),
        scratch_shapes=[pltpu.VMEM((D, F), BF16), pltpu.VMEM((D, F), BF16), pltpu.VMEM((F, D), BF16)],
        compiler_params=_cparams(2, VMEM_LIMIT),
        name="moe_ffn",
    )(xe, wg, wu, wd)


def _combine_kernel(base_ref, x_ref, ye_ref, posc_ref, aff_ref, modc_ref, modx_ref, o_ref, *, d_model,
                    blk_chunks, n_ctx_chunks, rows, win):
    D = d_model
    b = pl.program_id(0)
    tb = pl.program_id(1)
    ep = pl.program_id(2)
    group = ye_ref.shape[1]
    experts = [group * ep + k for k in range(group)]

    @pl.when(ep == 0)
    def _():
        o_ref[0] = jnp.zeros(o_ref.shape[1:], F32)

    lane = lax.broadcasted_iota(I32, (1, N_EXPERTS), 1)
    small = min(SMALL_FILL + GATHER_ALIGN_BF16, rows)

    def columns(rs, e):
        pcol = jnp.sum(jnp.where(lane == e, posc_ref[0, rs, :].astype(F32), 0.0), axis=1, keepdims=True)
        gcol = jnp.sum(jnp.where(lane == e, aff_ref[0, rs, :], 0.0), axis=1, keepdims=True)
        return pcol, gcol

    def scatter_narrow():
        lanes = lax.broadcasted_iota(I32, (CHUNK, group * 2 * small), 1)
        from_k = [lanes >= 2 * small * k for k in range(1, group)]
        within = lanes
        for m in from_k:
            within = jnp.where(m, within - 2 * small, within)
        low = within >= small
        lane1 = jnp.where(low, within - small, within)
        for cc in range(blk_chunks):
            rs = slice(cc * CHUNK, (cc + 1) * CHUNK)
            chunk = tb * blk_chunks + cc
            cols = [columns(rs, e) for e in experts]
            starts = [_window_start(base_ref[b, e, chunk], GATHER_ALIGN_BF16, rows, small) for e in experts]
            w, p, g = starts[0], cols[0][0], cols[0][1]
            for k, m in enumerate(from_k, start=1):
                w, p, g = jnp.where(m, starts[k], w), jnp.where(m, cols[k][0], p), jnp.where(m, cols[k][1], g)
            g_hi = g.astype(BF16).astype(F32)
            gate = jnp.where(low, g - g_hi, g_hi)
            sel = jnp.where(p == (lane1 + w).astype(F32), gate, 0.0).astype(BF16)
            ys = [ye_ref[0, k, pl.ds(starts[k], small), :] for k in range(group)]
            o_ref[0, rs, :] += _dot(sel, jnp.concatenate([y for y in ys for _ in range(2)], axis=0))

    def scatter_wide():
        for cc in range(blk_chunks):
            rs = slice(cc * CHUNK, (cc + 1) * CHUNK)
            chunk = tb * blk_chunks + cc
            for k, e in enumerate(experts):
                pcol, gcol = columns(rs, e)
                w0 = _window_start(base_ref[b, e, chunk], GATHER_ALIGN_BF16, rows, win)
                slot = (lax.broadcasted_iota(I32, (CHUNK, win), 1) + w0).astype(F32)
                onehot = jnp.where(pcol == slot, 1.0, 0.0).astype(BF16)
                o_ref[0, rs, :] += gcol * _dot(onehot, ye_ref[0, k, pl.ds(w0, win), :])

    most = functools.reduce(jnp.maximum, [base_ref[b, e, LANE - 1] for e in experts])
    if small < win:
        pl.when(most <= SMALL_FILL)(scatter_narrow)
        pl.when(most > SMALL_FILL)(scatter_wide)
    else:
        scatter_wide()

    @pl.when(ep == N_EXPERTS // group - 1)
    def _():
        g2c = modc_ref[0][:, 5 * D:6 * D]
        g2x = modx_ref[0][:, 5 * D:6 * D]
        for cc in range(blk_chunks):
            rs = slice(cc * CHUNK, (cc + 1) * CHUNK)
            g2 = jnp.where(tb * blk_chunks + cc < n_ctx_chunks, g2c, g2x)
            o_ref[0, rs, :] = x_ref[0, rs, :] + g2 * o_ref[0, rs, :]


def _combine_blk_chunks(nc):
    for k in (11, 8, 6, 4, 3, 2, 1):
        if nc % k == 0:
            return k
    return 1


def _combine(x, ye, posc, aff, mods_l, bases, *, n_ctx):
    B, S, D = x.shape
    R = ye.shape[2]
    nc = S // CHUNK
    k = _combine_blk_chunks(nc)
    tb = k * CHUNK
    win = min(CHUNK + GATHER_ALIGN_BF16, R)
    kern = functools.partial(_combine_kernel, d_model=D, blk_chunks=k, n_ctx_chunks=n_ctx // CHUNK,
                             rows=R, win=win)
    modc_spec = pl.BlockSpec((1, 1, 6 * D), lambda b, t, e, bs: (B, 0, 0))
    modx_spec = pl.BlockSpec((1, 1, 6 * D), lambda b, t, e, bs: (b, 0, 0))
    return pl.pallas_call(
        kern,
        grid_spec=pltpu.PrefetchScalarGridSpec(
            num_scalar_prefetch=1,
            grid=(B, S // tb, N_EXPERTS // COMBINE_GROUP),
            in_specs=[pl.BlockSpec((1, tb, D), lambda b, t, e, bs: (b, t, 0)),
                      pl.BlockSpec((1, COMBINE_GROUP, R, D), lambda b, t, e, bs: (b, e, 0, 0)),
                      pl.BlockSpec((1, tb, N_EXPERTS), lambda b, t, e, bs: (b, t, 0)),
                      pl.BlockSpec((1, tb, N_EXPERTS), lambda b, t, e, bs: (b, t, 0)),
                      modc_spec, modx_spec],
            out_specs=pl.BlockSpec((1, tb, D), lambda b, t, e, bs: (b, t, 0))),
        out_shape=jax.ShapeDtypeStruct((B, S, D), F32),
        compiler_params=_cparams(3, VMEM_LIMIT),
        name="moe_combine",
    )(bases, x, ye, posc, aff, mods_l, mods_l)


def _final_kernel(x_ref, g_ref, o_ref):
    o_ref[0] = _rms(x_ref[0], g_ref[...])


def _final_norm(xs, g, *, n_ctx):
    B, S, D = xs.shape
    T = S - n_ctx
    off = n_ctx // TOK_TILE
    return pl.pallas_call(
        _final_kernel,
        grid=(B, T // TOK_TILE),
        in_specs=[pl.BlockSpec((1, TOK_TILE, D), lambda b, i: (b, i + off, 0)),
                  pl.BlockSpec((1, D), lambda b, i: (0, 0))],
        out_specs=pl.BlockSpec((1, TOK_TILE, D), lambda b, i: (b, i, 0)),
        out_shape=jax.ShapeDtypeStruct((B, T, D), F32),
        compiler_params=_cparams(2),
        name="final_norm",
    )(xs, g)


def _rot_swap(w):
    half = A_ROPE // 2
    return jnp.concatenate([-w[..., half:], w[..., :half]], -1)


def _rope_tables(n_ctx, n_lat):
    rows = n_lat // GRID_W
    row_id = jnp.repeat(jnp.arange(rows, dtype=F32), GRID_W)
    col_id = jnp.tile(jnp.arange(GRID_W, dtype=F32), rows)
    n_freq = A_ROPE // 4
    inv = ROPE_BASE ** (-jnp.arange(n_freq, dtype=F32) / n_freq)
    ang = jnp.concatenate([row_id[:, None] * inv, col_id[:, None] * inv], -1)
    ang = jnp.concatenate([jnp.zeros((n_ctx, A_ROPE // 2), F32), ang], 0)
    S = n_ctx + n_lat
    cos, sin = jnp.cos(ang), jnp.sin(ang)
    pad = jnp.zeros((S, LANE - HEAD - A_ROPE), F32)
    cos128 = jnp.concatenate([jnp.ones((S, HEAD), F32), cos, cos, pad], -1)
    sin128 = jnp.concatenate([jnp.zeros((S, HEAD), F32), sin, sin, pad], -1)
    return cos128, sin128


def kernel(x, c, ctx, c_ctx, ada_w, ada_b, norm1_g, norm2_g, w_in, m_conv_w, m_conv_b, m_ib, m_fb, m_norm_g, a_qnorm_g, a_wq_up, a_kvnorm_g, a_wkv_up, w_out, router_w, e_w_gate, e_w_up, e_w_down, final_g):
    B, T, D = x.shape
    n_ctx = ctx.shape[1]
    L = ada_w.shape[0]
    S = n_ctx + T
    assert D == 16 * HEAD and n_ctx % TOK_TILE == 0 and T % TOK_TILE == 0 and T % (GRID_W * 8) == 0
    n_ctx_tiles = n_ctx // TOK_TILE
    n_ctx_chunks = n_ctx // CHUNK
    cap_ctx = EC_FACTOR * n_ctx // N_EXPERTS
    cap_lat = EC_FACTOR * T // N_EXPERTS
    slot_rows = cap_ctx + cap_lat

    o_f, o_qk, o_v, o_o, o_g, o_cq, o_ckv, o_kr = (0, 256, 768, 1024, 1280, 1296, 1680, 1936)
    w_kr = w_in[:, :, o_kr:o_kr + A_ROPE]
    slot = lambda w: jnp.pad(w, ((0, 0), (0, 0), (HEAD, LANE - HEAD - A_ROPE)))
    w_g = w_in[:, :, o_g:o_g + 16]
    w1 = jnp.concatenate([
        w_in[:, :, o_f:o_f + 256],
        w_in[:, :, o_qk:o_qk + 2 * M_WIDTH],
        w_in[:, :, o_v:o_v + M_WIDTH],
        w_in[:, :, o_o:o_o + M_WIDTH],
        w_in[:, :, o_cq:o_cq + 384],
        w_in[:, :, o_ckv:o_ckv + 256],
        slot(w_kr),
        slot(_rot_swap(w_kr)),
        jnp.pad(w_g, ((0, 0), (0, 0), (0, LANE - 16))),
    ], -1).astype(BF16)
    assert w1.shape[-1] == _N_COLS
    wgt = jnp.swapaxes(w_g, 1, 2).astype(BF16)
    conv_w = m_conv_w
    conv_b = m_conv_b[:, None, :]
    gate_bias = jnp.stack([m_ib, m_fb], 2).reshape(L, 16)
    wq3 = a_wq_up.reshape(L, -1, A_HEADS, HEAD + A_ROPE)
    zq = jnp.zeros(wq3.shape[:-1] + (LANE - HEAD - A_ROPE,), F32)
    wq = jnp.concatenate([wq3, zq], -1).reshape(L, -1, A_HEADS * LANE)
    wqt = jnp.swapaxes(wq, 1, 2).astype(BF16)
    wkv3 = a_wkv_up.reshape(L, -1, A_HEADS, 2 * HEAD)
    zk = jnp.zeros(wkv3.shape[:-1] + (HEAD,), F32)
    wk = jnp.concatenate([wkv3[..., :HEAD], zk], -1).reshape(L, -1, A_HEADS * LANE).astype(BF16)
    zv = jnp.zeros(wkv3.shape[:-1] + (V_ROWS - HEAD,), F32)
    wvt = jnp.swapaxes(jnp.concatenate([wkv3[..., HEAD:], zv], -1).reshape(L, -1, A_HEADS * V_ROWS),
                       1, 2).astype(BF16)
    wa = w_out[:, 0:256].astype(BF16)
    wb = w_out[:, 256:512].astype(BF16)
    wc = w_out[:, 512:1024].astype(BF16)
    mg = m_norm_g[:, None, :]
    lane = jnp.arange(M_WIDTH)
    blk = jnp.where(lane[:, None] // HEAD == lane[None, :] // HEAD, 1.0 / HEAD, 0.0).astype(BF16)
    rw_pad = jnp.pad(router_w, ((0, 0), (0, 0), (0, LANE - N_EXPERTS)))
    cos128, sin128 = _rope_tables(n_ctx, T)
    row = jnp.arange(LANE)[:, None]
    cost128 = cos128.T
    sint128 = jnp.where(row < HEAD + A_ROPE // 2, -sin128.T, sin128.T)
    cs = _fft_chan_mats()
    attn_sub = 3 if (S // TOK_TILE) % 3 == 0 else 1
    attn_qtiles = 4 if T % (4 * TOK_TILE) == 0 else 1
    attn_q_per_step = next(n for n in (4, 2, 1) if T % (n * attn_qtiles * TOK_TILE) == 0)

    rows16 = 16
    cvec = jnp.zeros((rows16, D), F32).at[:B].set(c).at[B].set(c_ctx)
    mods = _ada_mods(cvec, ada_w, ada_b).reshape(L, rows16, 1, 6 * D)

    xs = jnp.concatenate([ctx, x], axis=1)
    for l in range(L):
        mods_l = mods[l]
        f, qk, vm, og, gat, gatt, qt_ctx, qt_lat, k, vt4 = _inproj(
            xs, mods_l, norm1_g[l][None], w1[l], wgt[l], a_qnorm_g[l][None], a_kvnorm_g[l][None],
            wqt[l], wk[l], wvt[l], cos128, sin128, cost128, sint128,
            n_ctx_tiles=n_ctx_tiles)
        ya_ctx = _fft_ctx(f, n_ctx, cs)
        ya_lat = _fft_latent(f[:, n_ctx:], cs)
        qm, ktm, colp, rowp = _mlstm_prep(qk, gat, gatt, conv_w[l], conv_b[l], gate_bias[l][None, :],
                                          gate_bias[l][:, None], n_ctx_chunks=n_ctx_chunks)
        hf, hb = _mlstm_scan(qm, ktm, vm, colp, rowp, n_ctx_chunks=n_ctx_chunks)
        yc_ctx = _attention(qt_ctx, k, vt4, n_keys=n_ctx, qtiles=1, sub=1)
        yc_lat = _attention(qt_lat, k, vt4, n_keys=S, qtiles=attn_qtiles, sub=attn_sub,
                            q_per_step=attn_q_per_step)
        xs, hx, aff, aff_t = _outproj(ya_ctx, ya_lat, hf, hb, og, yc_ctx, yc_lat, xs, mods_l, wa[l], wb[l], wc[l],
                               mg[l], blk, norm2_g[l][None], rw_pad[l], n_ctx_tiles=n_ctx_tiles)
        posm, posc, bases = _route(aff_t, n_ctx=n_ctx, cap_ctx=cap_ctx, cap_lat=cap_lat)
        xe = _gather(hx, posm.reshape(B, N_EXPERTS, S // CHUNK, CHUNK), bases, rows=slot_rows)
        ye = _ffn(xe, e_w_gate, e_w_up, e_w_down, layer=l)
        xs = _combine(xs, ye, posc, aff, mods_l, bases, n_ctx=n_ctx)
    return _final_norm(xs, final_g[None], n_ctx=n_ctx)
```

```python
import functools
import math

import jax
import jax.numpy as jnp
from jax import lax
from jax.experimental import pallas as pl
from jax.experimental.pallas import tpu as pltpu

F32 = jnp.float32
BF16 = jnp.bfloat16
I32 = jnp.int32

EPS = 1e-6
GRID_W = 64
ROPE_BASE = 10000.0
LANE = 128
HEAD = 64
M_HEADS = 4
M_WIDTH = M_HEADS * HEAD
A_HEADS = 8
A_ROPE = 32
V_ROWS = 80
N_EXPERTS = 16
EC_FACTOR = 2
TOK_TILE = 256
CHUNK = 128
GATHER_ALIGN_BF16 = 16
SMALL_FILL = 32
COMBINE_GROUP = 4
VMEM_LIMIT = 56 * 1024 * 1024


def _cparams(n_axes, vmem=None):
    return pltpu.CompilerParams(dimension_semantics=("arbitrary",) * n_axes,
                                vmem_limit_bytes=vmem)


def _dot(a, b):
    return jnp.dot(a, b, preferred_element_type=F32)


def _dot_nt(a, b):
    return lax.dot_general(a, b, (((1,), (1,)), ((), ())), preferred_element_type=F32)


def _split_bf16(a):
    hi = a.astype(BF16)
    lo = (a - hi.astype(F32)).astype(BF16)
    return hi, lo


def _sigmoid(x):
    return 1.0 / (1.0 + jnp.exp(-x))


def _rms(x, g):
    return x * lax.rsqrt(jnp.mean(x * x, axis=-1, keepdims=True) + EPS) * g


def _ada_kernel(c_ref, w_ref, b_ref, o_ref):
    a = c_ref[...]
    a = a * _sigmoid(a)
    a_hi, a_lo = _split_bf16(a)
    w_hi, w_lo = _split_bf16(w_ref[0])
    o_ref[0] = _dot(a_hi, w_hi) + _dot(a_lo, w_hi) + _dot(a_hi, w_lo) + b_ref[0]


def _ada_mods(cvec, ada_w, ada_b):
    L, D, D6 = ada_w.shape
    rows = cvec.shape[0]
    return pl.pallas_call(
        _ada_kernel,
        grid=(L, D6 // D),
        in_specs=[pl.BlockSpec((rows, D), lambda l, j: (0, 0)),
                  pl.BlockSpec((1, D, D), lambda l, j: (l, 0, j)),
                  pl.BlockSpec((1, 1, D), lambda l, j: (l, 0, j))],
        out_specs=pl.BlockSpec((1, rows, D), lambda l, j: (l, 0, j)),
        out_shape=jax.ShapeDtypeStruct((L, rows, D6), F32),
        compiler_params=_cparams(2),
        name="ada_mods",
    )(cvec, ada_w, ada_b.reshape(L, 1, D6))


_C_F = (0, 256)
_C_Q = (256, 512)
_C_K = (512, 768)
_C_V = (768, 1024)
_C_O = (1024, 1280)
_C_CQ = (1280, 1664)
_C_CKV = (1664, 1920)
_C_KR = (1920, 2048)
_C_KRS = (2048, 2176)
_C_G = (2176, 2304)
_N_COLS = 2304


def _inproj_kernel(x_ref, mod_ref, n1_ref, w1_ref, wgt_ref, qn_ref, kvn_ref, wqt_ref,
                   wk_ref, wvt_ref, cos_ref, sin_ref, cost_ref, sint_ref,
                   f_ref, qk_ref, vm_ref, o_ref, gat_ref, gatt_ref, qtc_ref, qtl_ref, k_ref, vt_ref, *,
                   d_model, q_scale, n_ctx_tiles):
    D = d_model
    i = pl.program_id(1)
    x = x_ref[0]
    mod = mod_ref[0]
    sh1, sc1 = mod[:, 0:D], mod[:, D:2 * D]
    xm = _rms(x, n1_ref[...]) * (1.0 + sc1) + sh1
    xb = xm.astype(BF16)
    u = _dot(xb, w1_ref[...])
    f_ref[0] = u[:, _C_F[0]:_C_F[1]].astype(BF16)
    qk_ref[0] = u[:, _C_Q[0]:_C_K[1]].astype(BF16)
    vm_ref[0] = u[:, _C_V[0]:_C_V[1]].astype(BF16)
    o_ref[0] = u[:, _C_O[0]:_C_O[1]].astype(BF16)
    gat_ref[0] = u[:, _C_G[0]:_C_G[0] + 16]
    gatt_ref[0] = _dot_nt(wgt_ref[...], xb)

    cqn = _rms(u[:, _C_CQ[0]:_C_CQ[1]], qn_ref[...]).astype(BF16)
    cost = jnp.tile(cost_ref[...], (A_HEADS, 1))
    sint = jnp.tile(sint_ref[...], (A_HEADS, 1))
    qa = _dot_nt(wqt_ref[...], cqn)
    half = A_ROPE // 2
    parts = []
    for h in range(A_HEADS):
        r0 = h * LANE
        parts += [qa[r0:r0 + HEAD], qa[r0 + HEAD + half:r0 + HEAD + A_ROPE],
                  qa[r0 + HEAD:r0 + HEAD + half], qa[r0 + HEAD + A_ROPE:r0 + LANE]]
    qt = ((qa * cost + jnp.concatenate(parts, axis=0) * sint) * q_scale).astype(BF16)

    @pl.when(i < n_ctx_tiles)
    def _():
        qtc_ref[0, 0] = qt

    @pl.when(i >= n_ctx_tiles)
    def _():
        qtl_ref[0, 0] = qt

    ckvn = _rms(u[:, _C_CKV[0]:_C_CKV[1]], kvn_ref[...]).astype(BF16)
    cosw = jnp.tile(cos_ref[...], (1, A_HEADS))
    sinw = jnp.tile(sin_ref[...], (1, A_HEADS))
    kw = A_HEADS * LANE
    kr = jnp.tile(u[:, _C_KR[0]:_C_KR[1]], (1, A_HEADS))
    krs = jnp.tile(u[:, _C_KRS[0]:_C_KRS[1]], (1, A_HEADS))
    k = _dot(ckvn, wk_ref[...]) + kr * cosw + krs * sinw
    k_ref[0] = k.astype(BF16)
    vrow = lax.broadcasted_iota(I32, (A_HEADS * V_ROWS, 1), 0)
    ones_a = jnp.where(vrow % V_ROWS == HEAD, 1.0, 0.0)
    vt_ref[0, 0] = (_dot_nt(wvt_ref[...], ckvn) + ones_a).astype(BF16)


def _inproj(xs, mods_l, n1, w1, wgt, qn, kvn, wqt, wk, wvt, cos128, sin128, cost128, sint128,
            *, n_ctx_tiles):
    B, S, D = xs.shape
    nt = S // TOK_TILE
    n_ctx = n_ctx_tiles * TOK_TILE
    n_lat_tiles = nt - n_ctx_tiles
    kw = A_HEADS * LANE
    tok = lambda w: pl.BlockSpec((1, TOK_TILE, w), lambda b, i: (b, i, 0))
    full = lambda a: pl.BlockSpec(a.shape, lambda b, i: (0,) * a.ndim)
    mod_spec = pl.BlockSpec((1, 1, 6 * D), lambda b, i: (jnp.where(i < n_ctx_tiles, B, b), 0, 0))
    tab_spec = pl.BlockSpec((TOK_TILE, LANE), lambda b, i: (i, 0))
    tabt_spec = pl.BlockSpec((LANE, TOK_TILE), lambda b, i: (0, i))
    sd = lambda w, dt: jax.ShapeDtypeStruct((B, S, w), dt)
    kern = functools.partial(_inproj_kernel, d_model=D, n_ctx_tiles=n_ctx_tiles,
                             q_scale=(HEAD + A_ROPE) ** -0.5 * math.log2(math.e))
    qtc_spec = pl.BlockSpec((1, 1, kw, TOK_TILE), lambda b, i: (b, jnp.minimum(i, n_ctx_tiles - 1), 0, 0))
    qtl_spec = pl.BlockSpec((1, 1, kw, TOK_TILE),
                            lambda b, i: (b, jnp.clip(i - n_ctx_tiles, 0, n_lat_tiles - 1), 0, 0))
    return pl.pallas_call(
        kern,
        grid=(B, nt),
        in_specs=[tok(D), mod_spec, full(n1), full(w1), full(wgt), full(qn), full(kvn), full(wqt),
                  full(wk), full(wvt), tab_spec, tab_spec, tabt_spec, tabt_spec],
        out_specs=[tok(256), tok(2 * M_WIDTH), tok(M_WIDTH), tok(M_WIDTH), tok(16),
                   pl.BlockSpec((1, 16, TOK_TILE), lambda b, i: (b, 0, i)),
                   qtc_spec, qtl_spec, tok(kw),
                   pl.BlockSpec((1, 1, A_HEADS * V_ROWS, TOK_TILE), lambda b, i: (b, i, 0, 0))],
        out_shape=[sd(256, BF16), sd(2 * M_WIDTH, BF16), sd(M_WIDTH, BF16), sd(M_WIDTH, BF16), sd(16, F32),
                   jax.ShapeDtypeStruct((B, 16, S), F32),
                   jax.ShapeDtypeStruct((B, n_ctx_tiles, kw, TOK_TILE), BF16),
                   jax.ShapeDtypeStruct((B, n_lat_tiles, kw, TOK_TILE), BF16),
                   sd(kw, BF16),
                   jax.ShapeDtypeStruct((B, nt, A_HEADS * V_ROWS, TOK_TILE), BF16)],
        compiler_params=_cparams(2, VMEM_LIMIT),
        name="inproj",
    )(xs, mods_l, n1, w1, wgt, qn, kvn, wqt, wk, wvt, cos128, sin128, cost128, sint128)


def _dft_mats(n):
    idx = jnp.arange(n, dtype=F32)
    ang = 2.0 * math.pi * jnp.mod(idx[:, None] * idx[None, :], n) / n
    return jnp.cos(ang), jnp.sin(ang)


def _fft_chan_mats():
    c, s = _dft_mats(HEAD)
    eye = jnp.eye(4, dtype=F32)
    return jnp.concatenate([jnp.kron(eye, c), jnp.kron(eye, s)], 0).astype(BF16)


def _fft_ctx_kernel(f_ref, ft_ref, cs_ref, o_ref, *, n, scale):
    xst = _dot(ft_ref[...], f_ref[0])
    xr = xst[:n].astype(BF16)
    xi = xst[n:].astype(BF16)
    y = _dot(xr, cs_ref[0:256, :]) + _dot(xi, cs_ref[256:512, :])
    o_ref[0] = (y * scale).astype(BF16)


def _fft_ctx(f, n_ctx, cs):
    B = f.shape[0]
    c, s = _dft_mats(n_ctx)
    ft = jnp.concatenate([c, -s], 0).astype(BF16)
    kern = functools.partial(_fft_ctx_kernel, n=n_ctx, scale=(n_ctx * HEAD) ** -0.5)
    return pl.pallas_call(
        kern,
        grid=(B,),
        in_specs=[pl.BlockSpec((1, n_ctx, 256), lambda b: (b, 0, 0)),
                  pl.BlockSpec(ft.shape, lambda b: (0, 0)),
                  pl.BlockSpec(cs.shape, lambda b: (0, 0))],
        out_specs=pl.BlockSpec((1, n_ctx, 256), lambda b: (b, 0, 0)),
        out_shape=jax.ShapeDtypeStruct((B, n_ctx, 256), BF16),
        compiler_params=_cparams(1),
        name="fft_ctx",
    )(f, ft, cs)


def _fft_stage1_kernel(x_ref, f1_ref, cw_ref, sw_ref, o_ref, *, n1):
    z = _dot(f1_ref[...], x_ref[0])
    zr, zi = z[:n1], z[n1:]
    cw, sw = cw_ref[...], sw_ref[...]
    o_ref[0, 0] = (zr * cw + zi * sw).astype(BF16)
    o_ref[0, 1] = (zi * cw - zr * sw).astype(BF16)


def _fft_stage2_kernel(z_ref, f2_ref, cs_ref, o_ref, *, tb, scale):
    for j in range(tb):
        zcat = jnp.concatenate([z_ref[0, 0, j], z_ref[0, 1, j]], axis=0)
        xst = _dot(f2_ref[...], zcat)
        xr = xst[:HEAD].astype(BF16)
        xi = xst[HEAD:].astype(BF16)
        y = _dot(xr, cs_ref[0:256, :]) + _dot(xi, cs_ref[256:512, :])
        o_ref[0, j] = (y * scale).astype(BF16)


def _fft_latent(f_lat, cs):
    B, T, W = f_lat.shape
    n2 = HEAD
    n1 = T // n2
    cols = n2 * W
    cb = 2048
    c1, s1 = _dft_mats(n1)
    f1 = jnp.concatenate([c1, -s1], 0).astype(BF16)
    t1 = jnp.arange(n1, dtype=F32)[:, None]
    s2 = jnp.arange(n2, dtype=F32)[None, :]
    ang = 2.0 * math.pi * (t1 * s2) / T
    cw = jnp.repeat(jnp.cos(ang), W, axis=1)
    sw = jnp.repeat(jnp.sin(ang), W, axis=1)
    z = pl.pallas_call(
        functools.partial(_fft_stage1_kernel, n1=n1),
        grid=(B, cols // cb),
        in_specs=[pl.BlockSpec((1, n1, cb), lambda b, j: (b, 0, j)),
                  pl.BlockSpec(f1.shape, lambda b, j: (0, 0)),
                  pl.BlockSpec((n1, cb), lambda b, j: (0, j)),
                  pl.BlockSpec((n1, cb), lambda b, j: (0, j))],
        out_specs=pl.BlockSpec((1, 2, n1, cb), lambda b, j: (b, 0, 0, j)),
        out_shape=jax.ShapeDtypeStruct((B, 2, n1, cols), BF16),
        compiler_params=_cparams(2),
        name="fft_stage1",
    )(f_lat.reshape(B, n1, cols), f1, cw, sw)
    z = z.reshape(B, 2, n1, n2, W)
    c2, s2m = _dft_mats(n2)
    f2 = jnp.concatenate([jnp.concatenate([c2, s2m], 1),
                          jnp.concatenate([-s2m, c2], 1)], 0).astype(BF16)
    tb = 8
    y = pl.pallas_call(
        functools.partial(_fft_stage2_kernel, tb=tb, scale=(T * HEAD) ** -0.5),
        grid=(B, n1 // tb),
        in_specs=[pl.BlockSpec((1, 2, tb, n2, W), lambda b, j: (b, 0, j, 0, 0)),
                  pl.BlockSpec(f2.shape, lambda b, j: (0, 0)),
                  pl.BlockSpec(cs.shape, lambda b, j: (0, 0))],
        out_specs=pl.BlockSpec((1, tb, n2, W), lambda b, j: (b, j, 0, 0)),
        out_shape=jax.ShapeDtypeStruct((B, n1, n2, W), BF16),
        compiler_params=_cparams(2),
        name="fft_stage2",
    )(z, f2, cs)
    return jnp.transpose(y, (0, 2, 1, 3)).reshape(B, T, W)


def _mlstm_prep_kernel(cur_ref, prev_ref, next_ref, cw_ref, cb_ref, gat_ref, gatt_ref, br_ref, bc_ref,
                       q_ref, kt_ref, col_ref, row_ref, *, n_chunks, n_ctx_chunks, per_step):
    for sub in range(per_step):
        _mlstm_prep_chunk(sub, cur_ref, prev_ref, next_ref, cw_ref, cb_ref, gat_ref, gatt_ref, br_ref, bc_ref,
                          q_ref, kt_ref, col_ref, row_ref, n_chunks=n_chunks, n_ctx_chunks=n_ctx_chunks,
                          per_step=per_step)


def _mlstm_prep_chunk(sub, cur_ref, prev_ref, next_ref, cw_ref, cb_ref, gat_ref, gatt_ref, br_ref, bc_ref,
                      q_ref, kt_ref, col_ref, row_ref, *, n_chunks, n_ctx_chunks, per_step):
    rs = slice(sub * CHUNK, (sub + 1) * CHUNK)
    c = pl.program_id(1) * per_step + sub
    cur = cur_ref[0, rs, :].astype(F32)
    first = jnp.logical_or(c == 0, c == n_ctx_chunks)
    last = jnp.logical_or(c == n_ctx_chunks - 1, c == n_chunks - 1)
    if sub == 0:
        prev_row = prev_ref[0].astype(F32)[15:16, :]
    else:
        prev_row = cur_ref[0, sub * CHUNK - 16:sub * CHUNK, :].astype(F32)[15:16, :]
    if sub == per_step - 1:
        next_row = next_ref[0].astype(F32)[0:1, :]
    else:
        next_row = cur_ref[0, (sub + 1) * CHUNK:(sub + 1) * CHUNK + 16, :].astype(F32)[0:1, :]
    prev_row = jnp.where(first, 0.0, prev_row)
    next_row = jnp.where(last, 0.0, next_row)
    rows = lax.broadcasted_iota(I32, (CHUNK, 1), 0)
    up = jnp.where(rows == 0, prev_row, pltpu.roll(cur, 1, axis=0))
    dn = jnp.where(rows == CHUNK - 1, next_row, pltpu.roll(cur, CHUNK - 1, axis=0))
    y = cw_ref[0:1, :] * up + cw_ref[1:2, :] * cur + cw_ref[2:3, :] * dn + cb_ref[...]
    y = y * _sigmoid(y)
    q_ref[0, rs, :] = (y[:, :M_WIDTH] * HEAD ** -0.5).astype(BF16)
    kt_ref[0, :, rs] = y[:, M_WIDTH:].T.astype(BF16)

    def logsig(v):
        return jnp.minimum(v, 0.0) - jnp.log(1.0 + jnp.exp(-jnp.abs(v)))

    r_i = lax.broadcasted_iota(I32, (CHUNK, CHUNK), 0)
    c_i = lax.broadcasted_iota(I32, (CHUNK, CHUNK), 1)
    lower = jnp.where(c_i <= r_i, 1.0, 0.0).astype(BF16)
    upper = jnp.where(c_i >= r_i, 1.0, 0.0).astype(BF16)

    g = gat_ref[0, rs, :] + br_ref[...]
    lane = lax.broadcasted_iota(I32, (1, 16), 1)
    lg = jnp.where(lane % 8 >= 4, logsig(g), g)
    hi, lo = _split_bf16(lg)
    cum_f = _dot(lower, hi) + _dot(lower, lo)
    cum_b = _dot(upper, hi) + _dot(upper, lo)
    cum_col = jnp.where(lane < 8, cum_f, cum_b)

    gt = gatt_ref[0, :, rs] + bc_ref[...]
    row = lax.broadcasted_iota(I32, (16, 1), 0)
    lgt = jnp.where(row % 8 >= 4, logsig(gt), gt)
    hit, lot = _split_bf16(lgt)
    cum_ft = _dot(hit, upper) + _dot(lot, upper)
    cum_bt = _dot(hit, lower) + _dot(lot, lower)
    cumt = jnp.where(row < 8, cum_ft, cum_bt)
    cumt_i = pltpu.roll(cumt, 12, axis=0)
    rterm = lgt - cumt_i
    btot = jnp.broadcast_to(jnp.sum(lgt, axis=1, keepdims=True), (16, CHUNK))
    row_ref[0, 0:16, rs] = jnp.where(row % 8 < 4, rterm, btot)
    dmax = jnp.broadcast_to(jnp.max(rterm, axis=1, keepdims=True), (16, CHUNK))
    row_ref[0, 16:32, rs] = pltpu.roll(btot, 12, axis=0) + dmax
    seen_max = jnp.zeros((CHUNK, 16), F32)
    for d in range(2):
        seen = (c_i <= r_i) if d == 0 else (c_i >= r_i)
        for h in range(M_HEADS):
            idx = 8 * d + h
            cmx = jnp.max(jnp.where(seen, rterm[idx:idx + 1, :], -jnp.inf), axis=1, keepdims=True)
            seen_max = jnp.where(lane == idx, cmx, seen_max)
    col_ref[0, rs, :] = jnp.where(lane % 8 >= 4, cum_col, seen_max)


def _mlstm_prep(qk, gat, gatt, conv_w, conv_b, bias_row, bias_col, *, n_ctx_chunks):
    B, S, W = qk.shape
    nc = S // CHUNK
    n16 = S // 16
    hw = M_WIDTH
    per = 2 if nc % 2 == 0 else 1
    tb = per * CHUNK
    kern = functools.partial(_mlstm_prep_kernel, n_chunks=nc, n_ctx_chunks=n_ctx_chunks, per_step=per)
    full = lambda a: pl.BlockSpec(a.shape, lambda b, c: (0,) * a.ndim)
    return pl.pallas_call(
        kern,
        grid=(B, nc // per),
        in_specs=[pl.BlockSpec((1, tb, W), lambda b, c: (b, c, 0)),
                  pl.BlockSpec((1, 16, W), lambda b, c: (b, jnp.maximum(c * 8 * per - 1, 0), 0)),
                  pl.BlockSpec((1, 16, W), lambda b, c: (b, jnp.minimum((c + 1) * 8 * per, n16 - 1), 0)),
                  full(conv_w), full(conv_b),
                  pl.BlockSpec((1, tb, 16), lambda b, c: (b, c, 0)),
                  pl.BlockSpec((1, 16, tb), lambda b, c: (b, 0, c)),
                  full(bias_row), full(bias_col)],
        out_specs=[pl.BlockSpec((1, tb, hw), lambda b, c: (b, c, 0)),
                   pl.BlockSpec((1, hw, tb), lambda b, c: (b, 0, c)),
                   pl.BlockSpec((1, tb, 16), lambda b, c: (b, c, 0)),
                   pl.BlockSpec((1, 32, tb), lambda b, c: (b, 0, c))],
        out_shape=[jax.ShapeDtypeStruct((B, S, hw), BF16),
                   jax.ShapeDtypeStruct((B, hw, S), BF16),
                   jax.ShapeDtypeStruct((B, S, 16), F32),
                   jax.ShapeDtypeStruct((B, 32, S), F32)],
        compiler_params=_cparams(2),
        name="mlstm_prep",
    )(qk, qk, qk, conv_w, conv_b, gat, gatt, bias_row, bias_col)


def _mlstm_scan_kernel(*refs, n_batch):
    ins, (of_ref, ob_ref, c_scr, m_scr) = refs[:10], refs[10:]
    j = pl.program_id(1)

    @pl.when(j == 0)
    def _():
        c_scr[...] = jnp.zeros_like(c_scr)
        m_scr[...] = jnp.zeros_like(m_scr)

    t_i = lax.broadcasted_iota(I32, (CHUNK, CHUNK), 0)
    s_i = lax.broadcasted_iota(I32, (CHUNK, CHUNK), 1)
    lane_i = lax.broadcasted_iota(I32, (1, LANE), 1)
    row_i = lax.broadcasted_iota(I32, (LANE, 1), 0)
    for bb in range(n_batch):
        for d, o_ref in enumerate((of_ref, ob_ref)):
            q_ref, kt_ref, v_ref, col_ref, row_ref = ins[5 * d:5 * d + 5]
            colp = col_ref[bb]
            rowp = row_ref[bb]
            mask = (s_i <= t_i) if d == 0 else (s_i >= t_i)
            outs = []
            for pair_i in range(M_HEADS // 2):
                lanes = slice(pair_i * LANE, (pair_i + 1) * LANE)
                q_blk = q_ref[bb, :, lanes]
                kt_blk = kt_ref[bb, lanes, :]
                v_blk = v_ref[bb, :, lanes]
                tots = []
                for p in range(2):
                    h = 2 * pair_i + p
                    st = (bb * 2 + d) * M_HEADS + h
                    own = (lane_i < HEAD) if p == 0 else (lane_i >= HEAD)
                    own_r = (row_i < HEAD) if p == 0 else (row_i >= HEAD)
                    one_lane = HEAD if p == 0 else 0
                    full = (CHUNK, CHUNK)
                    bcol = jnp.broadcast_to(colp[:, 8 * d + 4 + h:8 * d + 5 + h], full)
                    rmax = bcol + jnp.broadcast_to(colp[:, 8 * d + h:8 * d + h + 1], full)
                    rrow = rowp[8 * d + h:8 * d + h + 1, :]
                    btot = rowp[8 * d + 4 + h:8 * d + 5 + h, :]
                    dmax = rowp[16 + 8 * d + h:17 + 8 * d + h, :]
                    m = m_scr[st, 0:1, :]
                    inter = bcol + m
                    m_t = jnp.maximum(inter, rmax)
                    w = jnp.exp(jnp.where(mask, bcol + rrow, -jnp.inf) - m_t)
                    a = jnp.exp(inter - m_t)
                    qh = jnp.where(own, q_blk, 0)
                    vh = jnp.where(own, v_blk, jnp.where(lane_i == one_lane, 1.0, 0.0).astype(BF16))
                    s = _dot(qh, kt_blk) * w
                    cst = c_scr[st]
                    tot = a * _dot(qh, cst.astype(BF16)) + _dot(s.astype(BF16), vh)
                    den = jnp.broadcast_to(tot[:, one_lane:one_lane + 1], full)
                    tots.append(tot / jnp.maximum(jnp.abs(den), jnp.exp(-m_t)))
                    bm = btot + m
                    m_new = jnp.maximum(bm, dmax)
                    ws = jnp.exp(btot + rrow - m_new)
                    kw = (jnp.where(own_r, kt_blk, 0).astype(F32) * ws).astype(BF16)
                    c_scr[st] = jnp.exp(bm - m_new) * cst + _dot(kw, vh)
                    m_scr[st] = jnp.broadcast_to(m_new, (8, LANE))
                outs.append(jnp.where(lane_i < HEAD, tots[0], tots[1]))
            o_ref[bb] = jnp.concatenate(outs, axis=1)


def _mlstm_scan(q, kt, v, colp, rowp, *, n_ctx_chunks):
    B, S, hw = q.shape
    nc = S // CHUNK
    nb = next(n for n in (4, 2, 1) if B % n == 0)

    def rev(j):
        return jnp.where(j < n_ctx_chunks, n_ctx_chunks - 1 - j, nc + n_ctx_chunks - 1 - j)

    def specs(cid):
        return [pl.BlockSpec((nb, CHUNK, hw), lambda b, j: (b, cid(j), 0)),
                pl.BlockSpec((nb, hw, CHUNK), lambda b, j: (b, 0, cid(j))),
                pl.BlockSpec((nb, CHUNK, hw), lambda b, j: (b, cid(j), 0)),
                pl.BlockSpec((nb, CHUNK, 16), lambda b, j: (b, cid(j), 0)),
                pl.BlockSpec((nb, 32, CHUNK), lambda b, j: (b, 0, cid(j)))]

    fwd = lambda j: j
    n_chains = nb * 2 * M_HEADS
    return pl.pallas_call(
        functools.partial(_mlstm_scan_kernel, n_batch=nb),
        grid=(B // nb, nc),
        in_specs=specs(fwd) + specs(rev),
        out_specs=[pl.BlockSpec((nb, CHUNK, hw), lambda b, j: (b, j, 0)),
                   pl.BlockSpec((nb, CHUNK, hw), lambda b, j: (b, rev(j), 0))],
        out_shape=[jax.ShapeDtypeStruct((B, S, hw), F32), jax.ShapeDtypeStruct((B, S, hw), F32)],
        scratch_shapes=[pltpu.VMEM((n_chains, LANE, LANE), F32), pltpu.VMEM((n_chains, 8, LANE), F32)],
        compiler_params=_cparams(2),
        name="mlstm_scan",
    )(q, kt, v, colp, rowp, q, kt, v, colp, rowp)


def _attn_kernel(qt_ref, k_ref, vt_ref, o_ref, s0_scr, s1_scr, p0_scr, p1_scr, al_scr, cm_scr, acc_scr,
                 m_scr, *,
                 n_q, n_kc, sub, qtiles):
    tk = sub * TOK_TILE
    tq = qtiles * TOK_TILE
    n_items = n_q * n_kc
    heads = [slice(hh * LANE, (hh + 1) * LANE) for hh in range(2)]
    vheads = [slice(hh * V_ROWS, (hh + 1) * V_ROWS) for hh in range(2)]
    s_bufs = (s0_scr, s1_scr)
    p_bufs = (p0_scr, p1_scr)

    def split(n):
        if n_q == 1:
            return 0, n
        qi = n // n_kc
        return qi, n - qi * n_kc

    def scores(n, par):
        qi, c = split(n)
        for hh in range(2):
            kc = k_ref[0, pl.ds(pl.multiple_of(c * tk, tk), tk), heads[hh]]
            qt = jnp.concatenate([qt_ref[0, qi * qtiles + j, heads[hh], :] for j in range(qtiles)], axis=1)
            st = _dot(kc, qt)
            s_bufs[par][hh] = st
            cm_scr[par, hh] = jnp.max(st, axis=0, keepdims=True)

    def softmax(n, par):
        _, c = split(n)
        for hh in range(2):
            st = s_bufs[par][hh]
            m = jnp.where(c == 0, -jnp.inf, m_scr[hh])
            m_new = jnp.maximum(m, cm_scr[par, hh])
            p_bufs[par][hh] = jnp.exp2(st - m_new).astype(BF16)
            al_scr[par, hh] = jnp.exp2(m - m_new)
            m_scr[hh] = m_new

    def values(n, par):
        qi, c = split(n)
        for hh in range(2):
            pv = _dot(vt_ref[0, c * sub, vheads[hh], :], p_bufs[par][hh, 0:TOK_TILE])
            for j in range(1, sub):
                pv += _dot(vt_ref[0, c * sub + j, vheads[hh], :],
                           p_bufs[par][hh, j * TOK_TILE:(j + 1) * TOK_TILE])
            acc_scr[hh] = al_scr[par, hh] * acc_scr[hh] + pv

        def finalize():
            o = jnp.concatenate([acc_scr[hh, :HEAD] / acc_scr[hh, HEAD:HEAD + 1] for hh in range(2)], axis=0)
            o_ref[0, pl.ds(pl.multiple_of(qi * tq, tq), tq), :] = o.T.astype(BF16)

        if isinstance(n, int) and n % n_kc == n_kc - 1:
            finalize()

    m_scr[...] = jnp.zeros_like(m_scr)
    acc_scr[...] = jnp.zeros_like(acc_scr)
    scores(0, 0)
    if n_items > 1:
        scores(1, 1)
    softmax(0, 0)

    def step(t, par):
        scores(t + 2, par)
        softmax(t + 1, 1 - par)
        values(t, par)

    def run(t0, t1):
        if t0 < t1 and t0 % 2:
            step(t0, 1)
            t0 += 1
        if t1 - t0 >= 2:
            def pair(i, carry):
                step(t0 + 2 * i, 0)
                step(t0 + 2 * i + 1, 1)
                return carry
            lax.fori_loop(0, (t1 - t0) // 2, pair, 0)
        if (t1 - t0) % 2:
            step(t1 - 1, (t1 - 1) % 2)

    n_full = max(n_items - 2, 0)
    t = 0
    for last in range(n_kc - 1, n_full, n_kc):
        run(t, last)
        step(last, last % 2)
        t = last + 1
    run(t, n_full)
    if n_items > 1:
        softmax(n_items - 1, (n_items - 1) % 2)
        values(n_items - 2, (n_items - 2) % 2)
    values(n_items - 1, (n_items - 1) % 2)


def _attention(qt4, k, vt4, *, n_keys, qtiles, sub, q_per_step=1):
    B, n_qb, kw, _ = qt4.shape
    n_kb = n_keys // TOK_TILE
    tk = sub * TOK_TILE
    tq = qtiles * TOK_TILE
    qb_step = q_per_step * qtiles
    kern = functools.partial(_attn_kernel, n_q=q_per_step, n_kc=n_kb // sub, sub=sub, qtiles=qtiles)
    return pl.pallas_call(
        kern,
        grid=(B, A_HEADS // 2, n_qb // qb_step),
        in_specs=[pl.BlockSpec((1, qb_step, 2 * LANE, TOK_TILE), lambda b, hp, i: (b, i, hp, 0)),
                  pl.BlockSpec((1, n_keys, 2 * LANE), lambda b, hp, i: (b, 0, hp)),
                  pl.BlockSpec((1, n_kb, 2 * V_ROWS, TOK_TILE), lambda b, hp, i: (b, 0, hp, 0))],
        out_specs=pl.BlockSpec((1, qb_step * TOK_TILE, LANE), lambda b, hp, i: (b, i, hp)),
        out_shape=jax.ShapeDtypeStruct((B, n_qb * TOK_TILE, A_HEADS * HEAD), BF16),
        scratch_shapes=[pltpu.VMEM((2, tk, tq), F32), pltpu.VMEM((2, tk, tq), F32),
                        pltpu.VMEM((2, tk, tq), BF16), pltpu.VMEM((2, tk, tq), BF16),
                        pltpu.VMEM((2, 2, 1, tq), F32), pltpu.VMEM((2, 2, 1, tq), F32),
                        pltpu.VMEM((2, V_ROWS, tq), F32), pltpu.VMEM((2, 1, tq), F32)],
        compiler_params=_cparams(3, VMEM_LIMIT),
        name="attention",
    )(qt4, k, vt4)


def _outproj_kernel(yac_ref, yal_ref, hf_ref, hb_ref, o_ref, ycc_ref, ycl_ref, x_ref, mod_ref, wa_ref, wb_ref,
                    wc_ref, mg_ref, blk_ref, n2_ref, rw_ref, xo_ref, hx_ref, aff_ref, afft_ref, *, d_model,
                    n_ctx_tiles):
    D = d_model
    i = pl.program_id(1)
    mod = mod_ref[0]
    g1, sh2, sc2 = mod[:, 2 * D:3 * D], mod[:, 3 * D:4 * D], mod[:, 4 * D:5 * D]
    ya = jnp.where(i < n_ctx_tiles, yac_ref[0], yal_ref[0])
    h = hf_ref[0] + hb_ref[0]
    hi, lo = _split_bf16(h * h)
    ms = _dot(hi, blk_ref[...]) + _dot(lo, blk_ref[...])
    hn = h * lax.rsqrt(ms + EPS) * mg_ref[...]
    yb = (hn * _sigmoid(o_ref[0].astype(F32))).astype(BF16)
    yc = jnp.where(i < n_ctx_tiles, ycc_ref[0], ycl_ref[0])
    mix = _dot(ya, wa_ref[...]) + _dot(yb, wb_ref[...]) + _dot(yc, wc_ref[...])
    x = x_ref[0] + g1 * mix
    xo_ref[0] = x
    hx = _rms(x, n2_ref[...]) * (1.0 + sc2) + sh2
    hx_ref[0] = hx.astype(BF16)
    h_hi, h_lo = _split_bf16(hx)
    r_hi, r_lo = _split_bf16(rw_ref[...])
    lane = lax.broadcasted_iota(I32, (1, LANE), 1)
    logits = jnp.where(lane < N_EXPERTS, _dot(h_hi, r_hi) + _dot(h_lo, r_hi) + _dot(h_hi, r_lo), -jnp.inf)
    e = jnp.exp(logits - jnp.max(logits, axis=1, keepdims=True))
    aff = e / jnp.sum(e, axis=1, keepdims=True)
    aff_ref[0] = aff[:, :N_EXPERTS]
    afft_ref[0] = aff.T[:N_EXPERTS, :]


def _outproj(ya_ctx, ya_lat, hf, hb, o, yc_ctx, yc_lat, xs, mods_l, wa, wb, wc, mg, blk, n2, rw, *,
             n_ctx_tiles):
    B, S, D = xs.shape
    nt = S // TOK_TILE
    tok = lambda w: pl.BlockSpec((1, TOK_TILE, w), lambda b, i: (b, i, 0))
    full = lambda a: pl.BlockSpec(a.shape, lambda b, i: (0,) * a.ndim)
    mod_spec = pl.BlockSpec((1, 1, 6 * D), lambda b, i: (jnp.where(i < n_ctx_tiles, B, b), 0, 0))
    hw = M_WIDTH
    n_lat_tiles = nt - n_ctx_tiles
    ctx_tok = lambda w: pl.BlockSpec((1, TOK_TILE, w), lambda b, i: (b, jnp.minimum(i, n_ctx_tiles - 1), 0))
    lat_tok = lambda w: pl.BlockSpec(
        (1, TOK_TILE, w), lambda b, i: (b, jnp.clip(i - n_ctx_tiles, 0, n_lat_tiles - 1), 0))
    kern = functools.partial(_outproj_kernel, d_model=D, n_ctx_tiles=n_ctx_tiles)
    return pl.pallas_call(
        kern,
        grid=(B, nt),
        in_specs=[ctx_tok(256), lat_tok(256),
                  tok(hw), tok(hw),
                  tok(hw), ctx_tok(A_HEADS * HEAD), lat_tok(A_HEADS * HEAD), tok(D), mod_spec,
                  full(wa), full(wb), full(wc), full(mg), full(blk), full(n2), full(rw)],
        out_specs=[tok(D), tok(D), tok(N_EXPERTS),
                   pl.BlockSpec((1, N_EXPERTS, TOK_TILE), lambda b, i: (b, 0, i))],
        out_shape=[jax.ShapeDtypeStruct((B, S, D), F32), jax.ShapeDtypeStruct((B, S, D), BF16),
                   jax.ShapeDtypeStruct((B, S, N_EXPERTS), F32), jax.ShapeDtypeStruct((B, N_EXPERTS, S), F32)],
        compiler_params=_cparams(2, VMEM_LIMIT),
        name="outproj",
    )(ya_ctx, ya_lat, hf, hb, o, yc_ctx, yc_lat, xs, mods_l, wa, wb, wc, mg, blk, n2, rw)


def _route_part(a, cap, slot0, tri, pos_ref, posc_ref, base_ref, lane0):
    n = a.shape[1]
    capf = float(cap)
    bits = pltpu.bitcast(a, I32)
    v = jnp.zeros((N_EXPERTS, 1), I32)
    for bit in range(30, -1, -1):
        cand = v | (1 << bit)
        cnt = jnp.sum(jnp.where(bits >= cand, 1.0, 0.0), axis=1, keepdims=True)
        v = jnp.where(cnt >= capf, cand, v)
    gt = bits > v
    eq = bits == v
    need = capf - jnp.sum(jnp.where(gt, 1.0, 0.0), axis=1, keepdims=True)
    idx = lax.broadcasted_iota(I32, (N_EXPERTS, n), 1)
    x = jnp.zeros((N_EXPERTS, 1), I32)
    for bit in range(max(n - 1, 1).bit_length() - 1, -1, -1):
        cand = x | (1 << bit)
        cnt = jnp.sum(jnp.where(eq, jnp.where(idx < cand, 1.0, 0.0), 0.0), axis=1, keepdims=True)
        x = jnp.where(cnt < need, cand, x)
    sel = jnp.where(gt, 1.0, jnp.where(eq, jnp.where(idx <= x, 1.0, 0.0), 0.0))
    running = jnp.zeros((N_EXPERTS, 1), F32)
    fill = jnp.zeros((N_EXPERTS, 1), F32)
    for c in range(n // CHUNK):
        blk = sel[:, c * CHUNK:(c + 1) * CHUNK]
        incl = _dot(blk.astype(BF16), tri)
        pos = running + incl - blk + float(slot0)
        cg = lane0 // CHUNK + c
        slot_id = jnp.where(blk > 0.5, pos, -1.0)
        pos_ref[0, :, lane0 + c * CHUNK:lane0 + (c + 1) * CHUNK] = slot_id.astype(I32)
        padded = jnp.concatenate([slot_id, jnp.zeros((CHUNK - N_EXPERTS, CHUNK), F32)], axis=0)
        posc_ref[0, lane0 + c * CHUNK:lane0 + (c + 1) * CHUNK, :] = padded.T[:, :N_EXPERTS].astype(I32)
        base_ref[0, :, cg:cg + 1] = (running + float(slot0)).astype(I32)
        running = running + incl[:, CHUNK - 1:CHUNK]
        fill = jnp.maximum(fill, incl[:, CHUNK - 1:CHUNK])
    return fill


def _route_kernel(aff_ref, pos_ref, posc_ref, base_ref, *, n_ctx, cap_ctx, cap_lat):
    r_i = lax.broadcasted_iota(I32, (CHUNK, CHUNK), 0)
    c_i = lax.broadcasted_iota(I32, (CHUNK, CHUNK), 1)
    tri = jnp.where(r_i <= c_i, 1.0, 0.0).astype(BF16)
    base_ref[...] = jnp.zeros_like(base_ref)
    a = aff_ref[0]
    fill_c = _route_part(a[:, :n_ctx], cap_ctx, 0, tri, pos_ref, posc_ref, base_ref, 0)
    fill_l = _route_part(a[:, n_ctx:], cap_lat, cap_ctx, tri, pos_ref, posc_ref, base_ref, n_ctx)
    base_ref[0, :, LANE - 1:LANE] = jnp.maximum(fill_c, fill_l).astype(I32)


def _route(aff_t, *, n_ctx, cap_ctx, cap_lat):
    B, E, S = aff_t.shape
    assert S // CHUNK < LANE - 1
    kern = functools.partial(_route_kernel, n_ctx=n_ctx, cap_ctx=cap_ctx, cap_lat=cap_lat)
    return pl.pallas_call(
        kern,
        grid=(B,),
        in_specs=[pl.BlockSpec((1, E, S), lambda b: (b, 0, 0))],
        out_specs=[pl.BlockSpec((1, E, S), lambda b: (b, 0, 0)),
                   pl.BlockSpec((1, S, E), lambda b: (b, 0, 0)),
                   pl.BlockSpec((1, E, LANE), lambda b: (b, 0, 0))],
        out_shape=[jax.ShapeDtypeStruct((B, E, S), I32), jax.ShapeDtypeStruct((B, S, E), I32),
                   jax.ShapeDtypeStruct((B, E, LANE), I32)],
        compiler_params=_cparams(1),
        name="route",
    )(aff_t)


def _window_start(base, align, rows, win):
    w0 = lax.shift_left(lax.shift_right_logical(base, int(math.log2(align))), int(math.log2(align)))
    return pl.multiple_of(jnp.minimum(w0, rows - win), align)


def _gather_kernel(base_ref, h_ref, pos_ref, xe_ref, *, n_chunks, rows, win, unroll):
    b = pl.program_id(0)
    e = pl.program_id(2)
    xe_ref[0, 0] = jnp.zeros(xe_ref.shape[2:], BF16)

    def gather_all(win_rows):
        def group(g, carry):
            picked = []
            for u in range(unroll):
                c = g * unroll + u
                w0 = _window_start(base_ref[b, e, c], GATHER_ALIGN_BF16, rows, win_rows)
                posr = pos_ref[0, 0, pl.ds(c, 1), :]
                slot = lax.broadcasted_iota(I32, (win_rows, CHUNK), 0) + w0
                onehot = jnp.where(posr == slot, 1.0, 0.0).astype(BF16)
                hc = h_ref[0, pl.ds(pl.multiple_of(c * CHUNK, CHUNK), CHUNK), :]
                picked.append((w0, _dot(onehot, hc).astype(BF16)))
            for w0, rows_c in picked:
                xe_ref[0, 0, pl.ds(w0, win_rows), :] += rows_c
            return carry

        lax.fori_loop(0, n_chunks // unroll, group, 0)

    most = base_ref[b, e, LANE - 1]
    small = min(SMALL_FILL + GATHER_ALIGN_BF16, rows)
    if small < win:
        pl.when(most <= SMALL_FILL)(functools.partial(gather_all, small))
        pl.when(most > SMALL_FILL)(functools.partial(gather_all, win))
    else:
        gather_all(win)


def _gather(hx, pos4, bases, *, rows):
    B, S, D = hx.shape
    nc = S // CHUNK
    dh = D // 2
    win = min(CHUNK + GATHER_ALIGN_BF16, rows)
    unroll = next(u for u in (11, 8, 6, 4, 3, 2, 1) if nc % u == 0)
    kern = functools.partial(_gather_kernel, n_chunks=nc, rows=rows, win=win, unroll=unroll)
    return pl.pallas_call(
        kern,
        grid_spec=pltpu.PrefetchScalarGridSpec(
            num_scalar_prefetch=1,
            grid=(B, 2, N_EXPERTS),
            in_specs=[pl.BlockSpec((1, S, dh), lambda b, j, e, bs: (b, 0, j)),
                      pl.BlockSpec((1, 1, nc, CHUNK), lambda b, j, e, bs: (b, e, 0, 0))],
            out_specs=pl.BlockSpec((1, 1, rows, dh), lambda b, j, e, bs: (b, e, 0, j))),
        out_shape=jax.ShapeDtypeStruct((B, N_EXPERTS, rows, D), BF16),
        compiler_params=_cparams(3, VMEM_LIMIT),
        name="moe_gather",
    )(bases, hx, pos4)


def _ffn_kernel(x_ref, wg_ref, wu_ref, wd_ref, y_ref, wg_s, wu_s, wd_s, *, row_tile, n_tiles):
    @pl.when(pl.program_id(1) == 0)
    def _():
        wg_s[...] = wg_ref[0, 0].astype(BF16)
        wu_s[...] = wu_ref[0, 0].astype(BF16)
        wd_s[...] = wd_ref[0, 0].astype(BF16)

    def tile(i, carry):
        r0 = pl.multiple_of(i * row_tile, 16)
        x = x_ref[0, 0, pl.ds(r0, row_tile), :]
        g = _dot(x, wg_s[...])
        u = _dot(x, wu_s[...])
        hid = (g * _sigmoid(g) * u).astype(BF16)
        y_ref[0, 0, pl.ds(r0, row_tile), :] = _dot(hid, wd_s[...]).astype(BF16)
        return carry

    lax.fori_loop(0, n_tiles, tile, 0)


def _ffn_row_tile(rows):
    for t in (352, 256, 176, 128, 96, 64, 32, 16):
        if rows % t == 0:
            return t
    raise ValueError(f"expert slot rows {rows} must be a multiple of 16")


def _ffn(xe, wg, wu, wd, *, layer):
    B, E, R, D = xe.shape
    F = wg.shape[-1]
    rt = _ffn_row_tile(R)
    kern = functools.partial(_ffn_kernel, row_tile=rt, n_tiles=R // rt)
    return pl.pallas_call(
        kern,
        grid=(E, B),
        in_specs=[pl.BlockSpec((1, 1, R, D), lambda e, b: (b, e, 0, 0)),
                  pl.BlockSpec((1, 1, D, F), lambda e, b: (layer, e, 0, 0)),
                  pl.BlockSpec((1, 1, D, F), lambda e, b: (layer, e, 0, 0)),
                  pl.BlockSpec((1, 1, F, D), lambda e, b: (layer, e, 0, 0))],
        out_specs=pl.BlockSpec((1, 1, R, D), lambda e, b: (b, e, 0, 0)),
        out_shape=jax.ShapeDtypeStruct((B, E, R, D), BF16),
        scratch_shapes=[pltpu.VMEM((D, F), BF16), pltpu.VMEM((D, F), BF16), pltpu.VMEM((F, D), BF16)],
        compiler_params=_cparams(2, VMEM_LIMIT),
        name="moe_ffn",
    )(xe, wg, wu, wd)


def _combine_kernel(base_ref, x_ref, ye_ref, posc_ref, aff_ref, modc_ref, modx_ref, o_ref, *, d_model,
                    blk_chunks, n_ctx_chunks, rows, win):
    D = d_model
    b = pl.program_id(0)
    tb = pl.program_id(1)
    ep = pl.program_id(2)
    group = ye_ref.shape[1]
    experts = [group * ep + k for k in range(group)]

    @pl.when(ep == 0)
    def _():
        o_ref[0] = jnp.zeros(o_ref.shape[1:], F32)

    lane = lax.broadcasted_iota(I32, (1, N_EXPERTS), 1)
    small = min(SMALL_FILL + GATHER_ALIGN_BF16, rows)

    def columns(rs, e):
        pcol = jnp.sum(jnp.where(lane == e, posc_ref[0, rs, :].astype(F32), 0.0), axis=1, keepdims=True)
        gcol = jnp.sum(jnp.where(lane == e, aff_ref[0, rs, :], 0.0), axis=1, keepdims=True)
        return pcol, gcol

    def scatter_narrow():
        lanes = lax.broadcasted_iota(I32, (CHUNK, group * 2 * small), 1)
        from_k = [lanes >= 2 * small * k for k in range(1, group)]
        within = lanes
        for m in from_k:
            within = jnp.where(m, within - 2 * small, within)
        low = within >= small
        lane1 = jnp.where(low, within - small, within)
        for cc in range(blk_chunks):
            rs = slice(cc * CHUNK, (cc + 1) * CHUNK)
            chunk = tb * blk_chunks + cc
            cols = [columns(rs, e) for e in experts]
            starts = [_window_start(base_ref[b, e, chunk], GATHER_ALIGN_BF16, rows, small) for e in experts]
            w, p, g = starts[0], cols[0][0], cols[0][1]
            for k, m in enumerate(from_k, start=1):
                w, p, g = jnp.where(m, starts[k], w), jnp.where(m, cols[k][0], p), jnp.where(m, cols[k][1], g)
            g_hi = g.astype(BF16).astype(F32)
            gate = jnp.where(low, g - g_hi, g_hi)
            sel = jnp.where(p == (lane1 + w).astype(F32), gate, 0.0).astype(BF16)
            ys = [ye_ref[0, k, pl.ds(starts[k], small), :] for k in range(group)]
            o_ref[0, rs, :] += _dot(sel, jnp.concatenate([y for y in ys for _ in range(2)], axis=0))

    def scatter_wide():
        for cc in range(blk_chunks):
            rs = slice(cc * CHUNK, (cc + 1) * CHUNK)
            chunk = tb * blk_chunks + cc
            for k, e in enumerate(experts):
                pcol, gcol = columns(rs, e)
                w0 = _window_start(base_ref[b, e, chunk], GATHER_ALIGN_BF16, rows, win)
                slot = (lax.broadcasted_iota(I32, (CHUNK, win), 1) + w0).astype(F32)
                onehot = jnp.where(pcol == slot, 1.0, 0.0).astype(BF16)
                o_ref[0, rs, :] += gcol * _dot(onehot, ye_ref[0, k, pl.ds(w0, win), :])

    most = functools.reduce(jnp.maximum, [base_ref[b, e, LANE - 1] for e in experts])
    if small < win:
        pl.when(most <= SMALL_FILL)(scatter_narrow)
        pl.when(most > SMALL_FILL)(scatter_wide)
    else:
        scatter_wide()

    @pl.when(ep == N_EXPERTS // group - 1)
    def _():
        g2c = modc_ref[0][:, 5 * D:6 * D]
        g2x = modx_ref[0][:, 5 * D:6 * D]
        for cc in range(blk_chunks):
            rs = slice(cc * CHUNK, (cc + 1) * CHUNK)
            g2 = jnp.where(tb * blk_chunks + cc < n_ctx_chunks, g2c, g2x)
            o_ref[0, rs, :] = x_ref[0, rs, :] + g2 * o_ref[0, rs, :]


def _combine_blk_chunks(nc):
    for k in (11, 8, 6, 4, 3, 2, 1):
        if nc % k == 0:
            return k
    return 1


def _combine(x, ye, posc, aff, mods_l, bases, *, n_ctx):
    B, S, D = x.shape
    R = ye.shape[2]
    nc = S // CHUNK
    k = _combine_blk_chunks(nc)
    tb = k * CHUNK
    win = min(CHUNK + GATHER_ALIGN_BF16, R)
    kern = functools.partial(_combine_kernel, d_model=D, blk_chunks=k, n_ctx_chunks=n_ctx // CHUNK,
                             rows=R, win=win)
    modc_spec = pl.BlockSpec((1, 1, 6 * D), lambda b, t, e, bs: (B, 0, 0))
    modx_spec = pl.BlockSpec((1, 1, 6 * D), lambda b, t, e, bs: (b, 0, 0))
    return pl.pallas_call(
        kern,
        grid_spec=pltpu.PrefetchScalarGridSpec(
            num_scalar_prefetch=1,
            grid=(B, S // tb, N_EXPERTS // COMBINE_GROUP),
            in_specs=[pl.BlockSpec((1, tb, D), lambda b, t, e, bs: (b, t, 0)),
                      pl.BlockSpec((1, COMBINE_GROUP, R, D), lambda b, t, e, bs: (b, e, 0, 0)),
                      pl.BlockSpec((1, tb, N_EXPERTS), lambda b, t, e, bs: (b, t, 0)),
                      pl.BlockSpec((1, tb, N_EXPERTS), lambda b, t, e, bs: (b, t, 0)),
                      modc_spec, modx_spec],
            out_specs=pl.BlockSpec((1, tb, D), lambda b, t, e, bs: (b, t, 0))),
        out_shape=jax.ShapeDtypeStruct((B, S, D), F32),
        compiler_params=_cparams(3, VMEM_LIMIT),
        name="moe_combine",
    )(bases, x, ye, posc, aff, mods_l, mods_l)


def _final_kernel(x_ref, g_ref, o_ref):
    o_ref[0] = _rms(x_ref[0], g_ref[...])


def _final_norm(xs, g, *, n_ctx):
    B, S, D = xs.shape
    T = S - n_ctx
    off = n_ctx // TOK_TILE
    return pl.pallas_call(
        _final_kernel,
        grid=(B, T // TOK_TILE),
        in_specs=[pl.BlockSpec((1, TOK_TILE, D), lambda b, i: (b, i + off, 0)),
                  pl.BlockSpec((1, D), lambda b, i: (0, 0))],
        out_specs=pl.BlockSpec((1, TOK_TILE, D), lambda b, i: (b, i, 0)),
        out_shape=jax.ShapeDtypeStruct((B, T, D), F32),
        compiler_params=_cparams(2),
        name="final_norm",
    )(xs, g)


def _rot_swap(w):
    half = A_ROPE // 2
    return jnp.concatenate([-w[..., half:], w[..., :half]], -1)


def _rope_tables(n_ctx, n_lat):
    rows = n_lat // GRID_W
    row_id = jnp.repeat(jnp.arange(rows, dtype=F32), GRID_W)
    col_id = jnp.tile(jnp.arange(GRID_W, dtype=F32), rows)
    n_freq = A_ROPE // 4
    inv = ROPE_BASE ** (-jnp.arange(n_freq, dtype=F32) / n_freq)
    ang = jnp.concatenate([row_id[:, None] * inv, col_id[:, None] * inv], -1)
    ang = jnp.concatenate([jnp.zeros((n_ctx, A_ROPE // 2), F32), ang], 0)
    S = n_ctx + n_lat
    cos, sin = jnp.cos(ang), jnp.sin(ang)
    pad = jnp.zeros((S, LANE - HEAD - A_ROPE), F32)
    cos128 = jnp.concatenate([jnp.ones((S, HEAD), F32), cos, cos, pad], -1)
    sin128 = jnp.concatenate([jnp.zeros((S, HEAD), F32), sin, sin, pad], -1)
    return cos128, sin128


def kernel(x, c, ctx, c_ctx, ada_w, ada_b, norm1_g, norm2_g, w_in, m_conv_w, m_conv_b, m_ib, m_fb, m_norm_g, a_qnorm_g, a_wq_up, a_kvnorm_g, a_wkv_up, w_out, router_w, e_w_gate, e_w_up, e_w_down, final_g):
    B, T, D = x.shape
    n_ctx = ctx.shape[1]
    L = ada_w.shape[0]
    S = n_ctx + T
    assert D == 16 * HEAD and n_ctx % TOK_TILE == 0 and T % TOK_TILE == 0 and T % (GRID_W * 8) == 0
    n_ctx_tiles = n_ctx // TOK_TILE
    n_ctx_chunks = n_ctx // CHUNK
    cap_ctx = EC_FACTOR * n_ctx // N_EXPERTS
    cap_lat = EC_FACTOR * T // N_EXPERTS
    slot_rows = cap_ctx + cap_lat

    o_f, o_qk, o_v, o_o, o_g, o_cq, o_ckv, o_kr = (0, 256, 768, 1024, 1280, 1296, 1680, 1936)
    w_kr = w_in[:, :, o_kr:o_kr + A_ROPE]
    slot = lambda w: jnp.pad(w, ((0, 0), (0, 0), (HEAD, LANE - HEAD - A_ROPE)))
    w_g = w_in[:, :, o_g:o_g + 16]
    w1 = jnp.concatenate([
        w_in[:, :, o_f:o_f + 256],
        w_in[:, :, o_qk:o_qk + 2 * M_WIDTH],
        w_in[:, :, o_v:o_v + M_WIDTH],
        w_in[:, :, o_o:o_o + M_WIDTH],
        w_in[:, :, o_cq:o_cq + 384],
        w_in[:, :, o_ckv:o_ckv + 256],
        slot(w_kr),
        slot(_rot_swap(w_kr)),
        jnp.pad(w_g, ((0, 0), (0, 0), (0, LANE - 16))),
    ], -1).astype(BF16)
    assert w1.shape[-1] == _N_COLS
    wgt = jnp.swapaxes(w_g, 1, 2).astype(BF16)
    conv_w = m_conv_w
    conv_b = m_conv_b[:, None, :]
    gate_bias = jnp.stack([m_ib, m_fb], 2).reshape(L, 16)
    wq3 = a_wq_up.reshape(L, -1, A_HEADS, HEAD + A_ROPE)
    zq = jnp.zeros(wq3.shape[:-1] + (LANE - HEAD - A_ROPE,), F32)
    wq = jnp.concatenate([wq3, zq], -1).reshape(L, -1, A_HEADS * LANE)
    wqt = jnp.swapaxes(wq, 1, 2).astype(BF16)
    wkv3 = a_wkv_up.reshape(L, -1, A_HEADS, 2 * HEAD)
    zk = jnp.zeros(wkv3.shape[:-1] + (HEAD,), F32)
    wk = jnp.concatenate([wkv3[..., :HEAD], zk], -1).reshape(L, -1, A_HEADS * LANE).astype(BF16)
    zv = jnp.zeros(wkv3.shape[:-1] + (V_ROWS - HEAD,), F32)
    wvt = jnp.swapaxes(jnp.concatenate([wkv3[..., HEAD:], zv], -1).reshape(L, -1, A_HEADS * V_ROWS),
                       1, 2).astype(BF16)
    wa = w_out[:, 0:256].astype(BF16)
    wb = w_out[:, 256:512].astype(BF16)
    wc = w_out[:, 512:1024].astype(BF16)
    mg = m_norm_g[:, None, :]
    lane = jnp.arange(M_WIDTH)
    blk = jnp.where(lane[:, None] // HEAD == lane[None, :] // HEAD, 1.0 / HEAD, 0.0).astype(BF16)
    rw_pad = jnp.pad(router_w, ((0, 0), (0, 0), (0, LANE - N_EXPERTS)))
    cos128, sin128 = _rope_tables(n_ctx, T)
    row = jnp.arange(LANE)[:, None]
    cost128 = cos128.T
    sint128 = jnp.where(row < HEAD + A_ROPE // 2, -sin128.T, sin128.T)
    cs = _fft_chan_mats()
    attn_sub = 3 if (S // TOK_TILE) % 3 == 0 else 1
    attn_qtiles = 4 if T % (4 * TOK_TILE) == 0 else 1
    attn_q_per_step = next(n for n in (4, 2, 1) if T % (n * attn_qtiles * TOK_TILE) == 0)

    rows16 = 16
    cvec = jnp.zeros((rows16, D), F32).at[:B].set(c).at[B].set(c_ctx)
    mods = _ada_mods(cvec, ada_w, ada_b).reshape(L, rows16, 1, 6 * D)

    xs = jnp.concatenate([ctx, x], axis=1)
    for l in range(L):
        mods_l = mods[l]
        f, qk, vm, og, gat, gatt, qt_ctx, qt_lat, k, vt4 = _inproj(
            xs, mods_l, norm1_g[l][None], w1[l], wgt[l], a_qnorm_g[l][None], a_kvnorm_g[l][None],
            wqt[l], wk[l], wvt[l], cos128, sin128, cost128, sint128,
            n_ctx_tiles=n_ctx_tiles)
        ya_ctx = _fft_ctx(f, n_ctx, cs)
        ya_lat = _fft_latent(f[:, n_ctx:], cs)
        qm, ktm, colp, rowp = _mlstm_prep(qk, gat, gatt, conv_w[l], conv_b[l], gate_bias[l][None, :],
                                          gate_bias[l][:, None], n_ctx_chunks=n_ctx_chunks)
        hf, hb = _mlstm_scan(qm, ktm, vm, colp, rowp, n_ctx_chunks=n_ctx_chunks)
        yc_ctx = _attention(qt_ctx, k, vt4, n_keys=n_ctx, qtiles=1, sub=1)
        yc_lat = _attention(qt_lat, k, vt4, n_keys=S, qtiles=attn_qtiles, sub=attn_sub,
                            q_per_step=attn_q_per_step)
        xs, hx, aff, aff_t = _outproj(ya_ctx, ya_lat, hf, hb, og, yc_ctx, yc_lat, xs, mods_l, wa[l], wb[l], wc[l],
                               mg[l], blk, norm2_g[l][None], rw_pad[l], n_ctx_tiles=n_ctx_tiles)
        posm, posc, bases = _route(aff_t, n_ctx=n_ctx, cap_ctx=cap_ctx, cap_lat=cap_lat)
        xe = _gather(hx, posm.reshape(B, N_EXPERTS, S // CHUNK, CHUNK), bases, rows=slot_rows)
        ye = _ffn(xe, e_w_gate, e_w_up, e_w_down, layer=l)
        xs = _combine(xs, ye, posc, aff, mods_l, bases, n_ctx=n_ctx)
    return _final_norm(xs, final_g[None], n_ctx=n_ctx)
```
